```python
import functools
import jax, jax.numpy as jnp
from jax import lax
import numpy as np

D_MODEL = 1024
BATCH = 4
SEQ = 8192
DEPTH = 2
DEC_BATCH = 32
DEC_SEQ = 1
PAST_LEN = 16384
PAGE_SIZE = 128

N_EVEN = (DEPTH + 1) // 2
N_ODD = DEPTH // 2
MIX_WIDTH = D_MODEL
A_WIDTH = MIX_WIDTH // 2
A_GROUPS = 4
A_GROUP_DIM = A_WIDTH // A_GROUPS
CHUNK = 128
B_WIDTH = MIX_WIDTH - A_WIDTH
B_HEAD_DIM = 64
B_HEADS = B_WIDTH // B_HEAD_DIM
B_LORA_W = 64
B_LORA_A = 64
B_COLS = 4 * B_WIDTH + B_LORA_W + B_LORA_A
B_SPLITS = (B_WIDTH, 2 * B_WIDTH, 3 * B_WIDTH, 4 * B_WIDTH, 4 * B_WIDTH + B_LORA_W)
AB_COLS = 2 * A_WIDTH + B_COLS
C_HEAD_DIM = 64
C_HEADS = MIX_WIDTH // C_HEAD_DIM
C_COLS = 4 * MIX_WIDTH + C_HEADS
C_SPLITS = (MIX_WIDTH, 2 * MIX_WIDTH, 3 * MIX_WIDTH, 4 * MIX_WIDTH)
Q_BLOCK = 128
N_MEM = 256
MEM_HEADS = 4
MEM_HEAD_DIM = D_MODEL // MEM_HEADS
D_FF = -(-8 * D_MODEL // (3 * 256)) * 256
RMS_EPS = 1e-6
GN_EPS = 64e-5
L2_EPS = 1e-12

kernel_name = 'hybrid_gmlp_rwkv7_fox_memory_decoder_step'


def rmsnorm(x, g):
    xf = x.astype(jnp.float32)
    y = xf * lax.rsqrt(jnp.mean(xf * xf, axis=-1, keepdims=True) + RMS_EPS)
    return (y * g.astype(jnp.float32)).astype(x.dtype)


def chunk_spatial_gate(u, v, w_s, b_s):
    B_, L = u.shape[:2]
    Lp = -(-L // CHUNK) * CHUNK
    vc = jnp.pad(v, ((0, 0), (0, Lp - L), (0, 0), (0, 0))).reshape(B_, Lp // CHUNK, CHUNK, A_GROUPS, A_GROUP_DIM)
    causal = jnp.tril(jnp.ones((CHUNK, CHUNK), dtype=bool))
    w = jnp.where(causal[None], w_s, 0)
    s = jnp.einsum('gts,bcsgd->bctgd', w, vc) + b_s.T[None, None, :, :, None]
    s = s.reshape(B_, Lp, A_GROUPS, A_GROUP_DIM)[:, :L]
    return u * s


def rwkv7_mix(z, z_prev0, S0, mu, w0, w_up, a0, a_up, k_k, k_a, r_k, ln_w, ln_b):
    f32 = jnp.float32
    B_, L = z.shape[:2]
    zf = z.astype(f32)
    zprev = jnp.concatenate([z_prev0.astype(f32)[:, None], zf[:, :-1]], axis=1)
    zm = zf + (zprev - zf) * mu.astype(f32)
    r, k, v, gp, wd, ad = jnp.split(zm, B_SPLITS, axis=-1)
    w = -jax.nn.softplus(-(w0.astype(f32) + jnp.tanh(wd) @ w_up.astype(f32))) - 0.5
    decay = jnp.exp(-jnp.exp(w))
    a = jax.nn.sigmoid(a0.astype(f32) + ad @ a_up.astype(f32))
    heads = lambda t: t.reshape(B_, L, B_HEADS, B_HEAD_DIM)
    kk = heads(k * k_k.astype(f32))
    kk = kk * lax.rsqrt(jnp.sum(kk * kk, axis=-1, keepdims=True) + L2_EPS)
    k = k * (1.0 + (a - 1.0) * k_a.astype(f32))
    r, k, v, a, decay = heads(r), heads(k), heads(v), heads(a), heads(decay)

    def step(S, inp):
        r_t, k_t, v_t, kk_t, a_t, d_t = inp
        sa = jnp.einsum('bhvk,bhk->bhv', S, kk_t)
        S = S * d_t[:, :, None, :] - sa[..., None] * (kk_t * a_t)[:, :, None, :] + v_t[..., None] * k_t[:, :, None, :]
        return S, jnp.einsum('bhvk,bhk->bhv', S, r_t)

    xs = tuple(jnp.swapaxes(t, 0, 1) for t in (r, k, v, kk, a, decay))
    S_fin, y = lax.scan(step, S0.astype(f32), xs)
    y = jnp.swapaxes(y, 0, 1)
    mean = jnp.mean(y, axis=-1, keepdims=True)
    var = jnp.mean(jnp.square(y - mean), axis=-1, keepdims=True)
    y = (y - mean) * lax.rsqrt(var + GN_EPS) * ln_w.astype(f32).reshape(B_HEADS, B_HEAD_DIM) + ln_b.astype(f32).reshape(B_HEADS, B_HEAD_DIM)
    y = y + jnp.sum(r * k * r_k.astype(f32), axis=-1, keepdims=True) * v
    y = y.reshape(B_, L, B_WIDTH) * jax.nn.sigmoid(gp)
    return y.astype(z.dtype), S_fin.astype(S0.dtype), z[:, -1]


def forgetting_attention(q, c_q, offset, segs):
    f32 = jnp.float32
    B_, Lq, H, d = q.shape
    qb_len = min(Q_BLOCK, Lq)
    nb = -(-Lq // qb_len)
    pad = nb * qb_len - Lq
    qb = jnp.pad(q, ((0, 0), (0, pad), (0, 0), (0, 0))).reshape(B_, nb, qb_len, H, d).swapaxes(0, 1)
    cb = jnp.pad(c_q, ((0, 0), (0, pad), (0, 0))).reshape(B_, nb, qb_len, H).swapaxes(0, 1)
    pb = (offset + jnp.arange(nb * qb_len)).reshape(nb, qb_len)
    cks = [jnp.swapaxes(c, 1, 2) for (_, _, c, _) in segs]
    scale = d ** -0.5

    def block(args):
        qi, ci, pi = args
        ci = jnp.swapaxes(ci, 1, 2)[..., None]
        scores = []
        for (k, _, _, kp), ck in zip(segs, cks):
            s = jnp.einsum('bqhd,bkhd->bhqk', qi, k).astype(f32) * scale + ci - ck[:, :, None, :]
            scores.append(jnp.where(kp[None, None, None, :] <= pi[None, None, :, None], s, -jnp.inf))
        prob = jax.nn.softmax(jnp.concatenate(scores, axis=-1), axis=-1)
        outs, start = [], 0
        for (_, v, _, _) in segs:
            n = v.shape[1]
            outs.append(jnp.einsum('bhqk,bkhd->bqhd', prob[..., start:start + n].astype(v.dtype), v))
            start += n
        return functools.reduce(jnp.add, outs)

    o = lax.map(block, (qb, cb, pb))
    return o.swapaxes(0, 1).reshape(B_, nb * qb_len, H, d)[:, :Lq]


def fox_mixer(h, w_in, b_f, q_gain, k_gain, w_out, past):
    f32 = jnp.float32
    B_, L = h.shape[:2]
    q, k, v, g, f = jnp.split(h @ w_in, C_SPLITS, axis=-1)
    heads = lambda t: t.reshape(B_, L, C_HEADS, C_HEAD_DIM)
    q = rmsnorm(heads(q), q_gain)
    k = rmsnorm(heads(k), k_gain)
    v = heads(v)
    logf = jax.nn.log_sigmoid(f.astype(f32) + b_f.astype(f32))
    if past is None:
        c = jnp.cumsum(logf, axis=1)
        segs = [(k, v, c, jnp.arange(L))]
        offset, c_q = 0, c
    else:
        kp, vp, lp = past
        P = lp.shape[1]
        c = jnp.cumsum(jnp.concatenate([lp.astype(f32), logf], axis=1), axis=1)
        segs = [(kp.astype(k.dtype), vp.astype(v.dtype), c[:, :P], jnp.arange(P)),
                (k, v, c[:, P:], P + jnp.arange(L))]
        offset, c_q = P, c[:, P:]
    o = forgetting_attention(q, c_q, offset, segs).reshape(B_, L, MIX_WIDTH)
    o = o * jax.nn.sigmoid(g)
    return o @ w_out, k, v, logf.astype(h.dtype)


def mem_kv(mem, g, wk, wv, k_gain):
    B_ = mem.shape[0]
    m = rmsnorm(mem, g)
    k = rmsnorm((m @ wk).reshape(B_, N_MEM, MEM_HEADS, MEM_HEAD_DIM), k_gain)
    v = (m @ wv).reshape(B_, N_MEM, MEM_HEADS, MEM_HEAD_DIM)
    return k, v


def mem_attend(h, wq, q_gain, k, v, wo):
    B_, L = h.shape[:2]
    q = rmsnorm((h @ wq).reshape(B_, L, MEM_HEADS, MEM_HEAD_DIM), q_gain)
    s = jnp.einsum('bqhd,bkhd->bhqk', q, k).astype(jnp.float32) * (MEM_HEAD_DIM ** -0.5)
    prob = jax.nn.softmax(s, axis=-1)
    o = jnp.einsum('bhqk,bkhd->bqhd', prob.astype(v.dtype), v).reshape(B_, L, D_MODEL)
    return o @ wo


def swiglu(h, w_in, w_out):
    gate, up = jnp.split(h @ w_in, 2, axis=-1)
    return (jax.nn.silu(gate) * up) @ w_out


def trunk(x, mem_k, mem_v, rwkv_S0, rwkv_shift0, fox_past, p):
    B_, L = x.shape[:2]
    chunk_v, S_out, shift_out, fk, fv, fl = [], [], [], [], [], []
    ie, io = 0, 0
    for l in range(DEPTH):
        h = rmsnorm(x, p['norm_mix'][l])
        if l % 2 == 0:
            z = h @ p['ab_w_in'][ie]
            za, zb = z[..., :2 * A_WIDTH], z[..., 2 * A_WIDTH:]
            ua, va = jnp.split(jax.nn.gelu(za), 2, axis=-1)
            ua = ua.reshape(B_, L, A_GROUPS, A_GROUP_DIM)
            va = rmsnorm(va.reshape(B_, L, A_GROUPS, A_GROUP_DIM), p['a_v_norm'][ie])
            ya = chunk_spatial_gate(ua, va, p['a_w_s'][ie], p['a_b_s'][ie]).reshape(B_, L, A_WIDTH)
            yb, S_new, shift_new = rwkv7_mix(zb, rwkv_shift0[ie], rwkv_S0[ie], p['b_mu'][ie], p['b_w0'][ie], p['b_w_up'][ie],
                                             p['b_a0'][ie], p['b_a_up'][ie], p['b_k_k'][ie], p['b_k_a'][ie], p['b_r_k'][ie],
                                             p['b_ln_w'][ie], p['b_ln_b'][ie])
            out = jnp.concatenate([ya, yb], axis=-1) @ p['ab_w_out'][ie]
            chunk_v.append(va.reshape(B_, L, A_WIDTH))
            S_out.append(S_new)
            shift_out.append(shift_new)
            ie += 1
        else:
            past = None
            if fox_past is not None:
                ck, cv, cl, pt = fox_past
                past = (ck[io, pt].reshape(B_, -1, C_HEADS, C_HEAD_DIM),
                        cv[io, pt].reshape(B_, -1, C_HEADS, C_HEAD_DIM),
                        cl[io, pt].reshape(B_, -1, C_HEADS))
            out, k_new, v_new, lf_new = fox_mixer(h, p['c_w_in'][io], p['c_b_f'][io], p['c_q_norm'][io],
                                                  p['c_k_norm'][io], p['c_w_out'][io], past)
            fk.append(k_new)
            fv.append(v_new)
            fl.append(lf_new)
            io += 1
        x = x + out
        x = x + mem_attend(rmsnorm(x, p['norm_mem'][l]), p['m_wq'][l], p['m_q_norm'][l], mem_k[l], mem_v[l], p['m_wo'][l])
        x = x + swiglu(rmsnorm(x, p['norm_ffn'][l]), p['f_w_in'][l], p['f_w_out'][l])
    return x, jnp.stack(chunk_v), jnp.stack(S_out), jnp.stack(shift_out), jnp.stack(fk), jnp.stack(fv), jnp.stack(fl)


def setup_inputs(seed: int = 0) -> dict:
    key = jax.random.key(seed)
    ks = iter(jax.random.split(key, 64))
    f32 = jnp.float32
    nrm = lambda shape, s=1.0: s * jax.random.normal(next(ks), shape, f32)
    gain = lambda shape: 1.0 + nrm(shape, 0.05)
    n_pages = PAST_LEN // PAGE_SIZE
    n_pool = (DEC_BATCH * n_pages * 5) // 4
    d = {}
    d['x_prompt'] = nrm((BATCH, SEQ, D_MODEL))
    d['x_sample'] = nrm((DEC_BATCH, DEC_SEQ, D_MODEL))
    d['mem_prompt'] = nrm((BATCH, N_MEM, D_MODEL))
    d['cache_mem_k'] = nrm((DEPTH, DEC_BATCH, N_MEM, MEM_HEADS, MEM_HEAD_DIM))
    d['cache_mem_v'] = nrm((DEPTH, DEC_BATCH, N_MEM, MEM_HEADS, MEM_HEAD_DIM))
    d['state_rwkv_S'] = nrm((N_EVEN, DEC_BATCH, B_HEADS, B_HEAD_DIM, B_HEAD_DIM), 0.3)
    d['state_rwkv_shift'] = nrm((N_EVEN, DEC_BATCH, B_COLS))
    d['cache_fox_k'] = nrm((N_ODD, n_pool, PAGE_SIZE, C_HEADS, C_HEAD_DIM))
    d['cache_fox_v'] = nrm((N_ODD, n_pool, PAGE_SIZE, C_HEADS, C_HEAD_DIM))
    d['cache_fox_logf'] = jax.nn.log_sigmoid(3.0 + nrm((N_ODD, n_pool, PAGE_SIZE, C_HEADS)))
    perm = jax.random.permutation(next(ks), n_pool)[:DEC_BATCH * n_pages]
    d['page_table'] = perm.reshape(DEC_BATCH, n_pages).astype(jnp.int32)
    d['norm_mix'] = gain((DEPTH, D_MODEL))
    d['norm_mem'] = gain((DEPTH, D_MODEL))
    d['norm_ffn'] = gain((DEPTH, D_MODEL))
    d['ab_w_in'] = nrm((N_EVEN, D_MODEL, AB_COLS), D_MODEL ** -0.5)
    d['ab_w_out'] = nrm((N_EVEN, MIX_WIDTH, D_MODEL), MIX_WIDTH ** -0.5)
    d['a_v_norm'] = gain((N_EVEN, A_GROUPS, A_GROUP_DIM))
    d['a_w_s'] = nrm((N_EVEN, A_GROUPS, CHUNK, CHUNK), CHUNK ** -0.5)
    d['a_b_s'] = 1.0 + nrm((N_EVEN, A_GROUPS, CHUNK), 0.1)
    d['b_mu'] = jax.random.uniform(next(ks), (N_EVEN, B_COLS), f32)
    d['b_w0'] = -2.0 + nrm((N_EVEN, B_WIDTH), 0.5)
    d['b_w_up'] = nrm((N_EVEN, B_LORA_W, B_WIDTH), 0.5 * B_LORA_W ** -0.5)
    d['b_a0'] = nrm((N_EVEN, B_WIDTH), 0.1)
    d['b_a_up'] = nrm((N_EVEN, B_LORA_A, B_WIDTH), 0.5 * B_LORA_A ** -0.5)
    d['b_k_k'] = 0.85 + nrm((N_EVEN, B_WIDTH), 0.05)
    d['b_k_a'] = gain((N_EVEN, B_WIDTH))
    d['b_r_k'] = nrm((N_EVEN, B_HEADS, B_HEAD_DIM), 0.1)
    d['b_ln_w'] = gain((N_EVEN, B_WIDTH))
    d['b_ln_b'] = nrm((N_EVEN, B_WIDTH), 0.02)
    d['c_w_in'] = nrm((N_ODD, D_MODEL, C_COLS), D_MODEL ** -0.5)
    d['c_b_f'] = 3.0 + nrm((N_ODD, C_HEADS), 0.1)
    d['c_q_norm'] = gain((N_ODD, C_HEAD_DIM))
    d['c_k_norm'] = gain((N_ODD, C_HEAD_DIM))
    d['c_w_out'] = nrm((N_ODD, MIX_WIDTH, D_MODEL), MIX_WIDTH ** -0.5)
    d['m_mem_norm'] = gain((DEPTH, D_MODEL))
    d['m_wq'] = nrm((DEPTH, D_MODEL, D_MODEL), D_MODEL ** -0.5)
    d['m_wk'] = nrm((DEPTH, D_MODEL, D_MODEL), D_MODEL ** -0.5)
    d['m_wv'] = nrm((DEPTH, D_MODEL, D_MODEL), D_MODEL ** -0.5)
    d['m_q_norm'] = gain((DEPTH, MEM_HEAD_DIM))
    d['m_k_norm'] = gain((DEPTH, MEM_HEAD_DIM))
    d['m_wo'] = nrm((DEPTH, D_MODEL, D_MODEL), D_MODEL ** -0.5)
    d['f_w_in'] = nrm((DEPTH, D_MODEL, 2 * D_FF), D_MODEL ** -0.5)
    d['f_w_out'] = nrm((DEPTH, D_FF, D_MODEL), D_FF ** -0.5)
    return d


def reference(x_prompt, x_sample, mem_prompt, cache_mem_k, cache_mem_v, state_rwkv_S, state_rwkv_shift,
              cache_fox_k, cache_fox_v, cache_fox_logf, page_table,
              norm_mix, norm_mem, norm_ffn, ab_w_in, ab_w_out, a_v_norm, a_w_s, a_b_s,
              b_mu, b_w0, b_w_up, b_a0, b_a_up, b_k_k, b_k_a, b_r_k, b_ln_w, b_ln_b,
              c_w_in, c_b_f, c_q_norm, c_k_norm, c_w_out,
              m_mem_norm, m_wq, m_wk, m_wv, m_q_norm, m_k_norm, m_wo, f_w_in, f_w_out):
    p = dict(norm_mix=norm_mix, norm_mem=norm_mem, norm_ffn=norm_ffn, ab_w_in=ab_w_in, ab_w_out=ab_w_out,
             a_v_norm=a_v_norm, a_w_s=a_w_s, a_b_s=a_b_s, b_mu=b_mu, b_w0=b_w0, b_w_up=b_w_up, b_a0=b_a0,
             b_a_up=b_a_up, b_k_k=b_k_k, b_k_a=b_k_a, b_r_k=b_r_k, b_ln_w=b_ln_w, b_ln_b=b_ln_b,
             c_w_in=c_w_in, c_b_f=c_b_f, c_q_norm=c_q_norm, c_k_norm=c_k_norm, c_w_out=c_w_out,
             m_wq=m_wq, m_q_norm=m_q_norm, m_wo=m_wo, f_w_in=f_w_in, f_w_out=f_w_out)
    pm = [mem_kv(mem_prompt, m_mem_norm[l], m_wk[l], m_wv[l], m_k_norm[l]) for l in range(DEPTH)]
    p_mem_k = jnp.stack([kv[0] for kv in pm])
    p_mem_v = jnp.stack([kv[1] for kv in pm])
    nb = x_prompt.shape[0]
    S0 = jnp.zeros((N_EVEN, nb, B_HEADS, B_HEAD_DIM, B_HEAD_DIM), x_prompt.dtype)
    shift0 = jnp.zeros((N_EVEN, nb, B_COLS), x_prompt.dtype)
    y_prompt, _, p_rwkv_S, p_rwkv_shift, p_fox_k, p_fox_v, p_fox_logf = trunk(
        x_prompt, p_mem_k, p_mem_v, S0, shift0, None, p)
    y_sample, s_chunk_v, s_rwkv_S, s_rwkv_shift, s_fox_k, s_fox_v, s_fox_logf = trunk(
        x_sample, cache_mem_k, cache_mem_v, state_rwkv_S, state_rwkv_shift,
        (cache_fox_k, cache_fox_v, cache_fox_logf, page_table), p)
    return (y_prompt, y_sample, p_mem_k, p_mem_v, p_rwkv_S, p_rwkv_shift, p_fox_k, p_fox_v, p_fox_logf,
            s_chunk_v, s_rwkv_S, s_rwkv_shift, s_fox_k, s_fox_v, s_fox_logf)
```

```python
import functools
import math

import jax
import jax.numpy as jnp
from jax import lax
from jax.experimental import pallas as pl
from jax.experimental.pallas import tpu as pltpu

f32 = jnp.float32
bf16 = jnp.bfloat16

RMS_EPS = 1e-6
GN_EPS = 64e-5
L2_EPS = 1e-12
LANES = 128
HEAD = 64
A_GROUP = 128
MEM_HEAD = 256
VMEM_LIMIT = 56 * 1024 * 1024


def _params(sem):
    return pltpu.CompilerParams(dimension_semantics=sem, vmem_limit_bytes=VMEM_LIMIT)


def _bdot(a, b):
    return jnp.dot(a.astype(bf16), b.astype(bf16), preferred_element_type=f32)


def _bdot_nt(a, b):
    return lax.dot_general(a.astype(bf16), b.astype(bf16), (((1,), (1,)), ((), ())),
                           preferred_element_type=f32)


def _rms_rows(x, g):
    return x * lax.rsqrt(jnp.mean(x * x, axis=-1, keepdims=True) + RMS_EPS) * g


def _sigmoid(x):
    return 1.0 / (1.0 + jnp.exp(-x))


def _lane_iota(shape):
    return lax.broadcasted_iota(jnp.int32, shape, len(shape) - 1)


def _half_sums(x):
    lo = _lane_iota(x.shape) < HEAD
    s_lo = jnp.sum(jnp.where(lo, x, 0.0), axis=-1, keepdims=True)
    s_hi = jnp.sum(jnp.where(lo, 0.0, x), axis=-1, keepdims=True)
    return jnp.where(lo, s_lo, s_hi)


def _per_head_sums(x):
    n = x.shape[-1] // LANES
    return jnp.concatenate([_half_sums(x[:, i * LANES:(i + 1) * LANES]) for i in range(n)], axis=-1)


def _norm_matmul_kernel(x_ref, g_ref, w_ref, o_ref):
    h = _rms_rows(x_ref[...], g_ref[...])
    o_ref[...] = _bdot(h, w_ref[...])


def norm_matmul(x, g, w, tm):
    m, d = x.shape
    n = w.shape[1]
    return pl.pallas_call(
        _norm_matmul_kernel,
        grid=(m // tm,),
        in_specs=[pl.BlockSpec((tm, d), lambda i: (i, 0)),
                  pl.BlockSpec((1, d), lambda i: (0, 0)),
                  pl.BlockSpec((d, n), lambda i: (0, 0))],
        out_specs=pl.BlockSpec((tm, n), lambda i: (i, 0)),
        out_shape=jax.ShapeDtypeStruct((m, n), f32),
        compiler_params=_params(("parallel",)),
        name="norm_matmul",
    )(x, g.reshape(1, d), w)


def _res_matmul_kernel(*refs):
    x_ref, o_ref = refs[0], refs[-1]
    acc = x_ref[...]
    for a_ref, w_ref in zip(refs[1:-1:2], refs[2:-1:2]):
        acc = acc + _bdot(a_ref[...], w_ref[...])
    o_ref[...] = acc


def res_matmul(x, pairs, tm):
    m, d = x.shape
    in_specs = [pl.BlockSpec((tm, d), lambda i: (i, 0))]
    args = [x]
    for a, w in pairs:
        in_specs += [pl.BlockSpec((tm, a.shape[1]), lambda i: (i, 0)),
                     pl.BlockSpec(w.shape, lambda i: (0, 0))]
        args += [a, w]
    return pl.pallas_call(
        _res_matmul_kernel,
        grid=(m // tm,),
        in_specs=in_specs,
        out_specs=pl.BlockSpec((tm, d), lambda i: (i, 0)),
        out_shape=jax.ShapeDtypeStruct((m, d), f32),
        compiler_params=_params(("parallel",)),
        name="res_matmul",
    )(*args)


def _ffn_kernel(x_ref, g_ref, wi_ref, wo_ref, o_ref, *, d_ff, tf):
    x = x_ref[...]
    h = _rms_rows(x, g_ref[...]).astype(bf16)
    acc = x
    for c in range(d_ff // tf):
        gate = jnp.dot(h, wi_ref[:, c * tf:(c + 1) * tf], preferred_element_type=f32)
        up = jnp.dot(h, wi_ref[:, d_ff + c * tf:d_ff + (c + 1) * tf], preferred_element_type=f32)
        act = gate * _sigmoid(gate) * up
        acc = acc + _bdot(act, wo_ref[c * tf:(c + 1) * tf, :])
    o_ref[...] = acc


def ffn(x, g, w_in, w_out, tm):
    m, d = x.shape
    d_ff = w_out.shape[0]
    tf = 256
    return pl.pallas_call(
        functools.partial(_ffn_kernel, d_ff=d_ff, tf=tf),
        grid=(m // tm,),
        in_specs=[pl.BlockSpec((tm, d), lambda i: (i, 0)),
                  pl.BlockSpec((1, d), lambda i: (0, 0)),
                  pl.BlockSpec(w_in.shape, lambda i: (0, 0)),
                  pl.BlockSpec(w_out.shape, lambda i: (0, 0))],
        out_specs=pl.BlockSpec((tm, d), lambda i: (i, 0)),
        out_shape=jax.ShapeDtypeStruct((m, d), f32),
        compiler_params=_params(("parallel",)),
        name="ffn",
    )(x, g.reshape(1, d), w_in, w_out)


def _mem_kv_kernel(m_ref, g_ref, wk_ref, wv_ref, kg_ref, k_ref, v_ref):
    h = _rms_rows(m_ref[...], g_ref[...]).astype(bf16)
    k = jnp.dot(h, wk_ref[...], preferred_element_type=f32)
    kg = kg_ref[...]
    for hd in range(k.shape[1] // MEM_HEAD):
        sl = slice(hd * MEM_HEAD, (hd + 1) * MEM_HEAD)
        k_ref[:, sl] = _rms_rows(k[:, sl], kg)
    v_ref[...] = jnp.dot(h, wv_ref[...], preferred_element_type=f32)


def mem_kv(mem, g, wk, wv, k_gain, tm):
    m, d = mem.shape
    return pl.pallas_call(
        _mem_kv_kernel,
        grid=(m // tm,),
        in_specs=[pl.BlockSpec((tm, d), lambda i: (i, 0)),
                  pl.BlockSpec((1, d), lambda i: (0, 0)),
                  pl.BlockSpec((d, d), lambda i: (0, 0)),
                  pl.BlockSpec((d, d), lambda i: (0, 0)),
                  pl.BlockSpec((1, MEM_HEAD), lambda i: (0, 0))],
        out_specs=[pl.BlockSpec((tm, d), lambda i: (i, 0))] * 2,
        out_shape=[jax.ShapeDtypeStruct((m, d), f32)] * 2,
        compiler_params=_params(("parallel",)),
        name="mem_kv",
    )(mem, g.reshape(1, d), wk, wv, k_gain.reshape(1, MEM_HEAD))


def _mem_attn_kernel(x_ref, g_ref, wq_ref, qg_ref, k_ref, v_ref, wo_ref, o_ref):
    x = x_ref[...]
    h = _rms_rows(x, g_ref[...])
    q = _bdot(h, wq_ref[...])
    qg = qg_ref[...] * (MEM_HEAD ** -0.5)
    outs = []
    for hd in range(q.shape[1] // MEM_HEAD):
        sl = slice(hd * MEM_HEAD, (hd + 1) * MEM_HEAD)
        qn = _rms_rows(q[:, sl], qg)
        s = _bdot_nt(qn, k_ref[0, :, sl])
        p = jnp.exp(s - jnp.max(s, axis=-1, keepdims=True))
        l = jnp.sum(p, axis=-1, keepdims=True)
        outs.append(_bdot(p, v_ref[0, :, sl]) / l)
    o = jnp.concatenate(outs, axis=-1)
    o_ref[...] = x + _bdot(o, wo_ref[...])


def mem_attn(x, g, wq, q_gain, k, v, wo, rows_per_seq, tm):
    m, d = x.shape
    n_mem = k.shape[1]
    per = rows_per_seq // tm
    return pl.pallas_call(
        _mem_attn_kernel,
        grid=(m // tm,),
        in_specs=[pl.BlockSpec((tm, d), lambda i: (i, 0)),
                  pl.BlockSpec((1, d), lambda i: (0, 0)),
                  pl.BlockSpec((d, d), lambda i: (0, 0)),
                  pl.BlockSpec((1, MEM_HEAD), lambda i: (0, 0)),
                  pl.BlockSpec((1, n_mem, d), lambda i: (i // per, 0, 0)),
                  pl.BlockSpec((1, n_mem, d), lambda i: (i // per, 0, 0)),
                  pl.BlockSpec((d, d), lambda i: (0, 0))],
        out_specs=pl.BlockSpec((tm, d), lambda i: (i, 0)),
        out_shape=jax.ShapeDtypeStruct((m, d), f32),
        compiler_params=_params(("parallel",)),
        name="mem_attn",
    )(x, g.reshape(1, d), wq, q_gain.reshape(1, MEM_HEAD), k, v, wo)


def _gelu(x):
    return 0.5 * x * (1.0 + jnp.tanh(0.7978845608028654 * (x + 0.044715 * x * x * x)))


def _mixer_a_kernel(z_ref, vg_ref, w_ref, b_ref, ya_ref, *, width):
    ge = _gelu(z_ref[...])
    tm = ge.shape[0]
    row = lax.broadcasted_iota(jnp.int32, (A_GROUP, A_GROUP), 0)
    col = lax.broadcasted_iota(jnp.int32, (A_GROUP, A_GROUP), 1)
    for g in range(width // A_GROUP):
        u = ge[:, g * A_GROUP:(g + 1) * A_GROUP]
        v = ge[:, width + g * A_GROUP:width + (g + 1) * A_GROUP]
        va = _rms_rows(v, vg_ref[g:g + 1, :]).astype(bf16)
        w = jnp.where(row >= col, w_ref[g], 0.0).astype(bf16)
        for c in range(tm // A_GROUP):
            rs = slice(c * A_GROUP, (c + 1) * A_GROUP)
            s = jnp.dot(w, va[rs], preferred_element_type=f32) + b_ref[g]
            ya_ref[rs, g * A_GROUP:(g + 1) * A_GROUP] = (u[rs] * s).astype(ya_ref.dtype)


def mixer_a(z, v_gain, w_s, b_s, width, tm):
    m = z.shape[0]
    ng = width // A_GROUP
    b_rows = jnp.broadcast_to(b_s[:, :, None], (ng, A_GROUP, A_GROUP))
    return pl.pallas_call(
        functools.partial(_mixer_a_kernel, width=width),
        grid=(m // tm,),
        in_specs=[pl.BlockSpec((tm, 2 * width), lambda i: (i, 0)),
                  pl.BlockSpec((ng, A_GROUP), lambda i: (0, 0)),
                  pl.BlockSpec((ng, A_GROUP, A_GROUP), lambda i: (0, 0, 0)),
                  pl.BlockSpec((ng, A_GROUP, A_GROUP), lambda i: (0, 0, 0))],
        out_specs=pl.BlockSpec((tm, width), lambda i: (i, 0)),
        out_shape=jax.ShapeDtypeStruct((m, width), bf16),
        compiler_params=_params(("parallel",)),
        name="mixer_a",
    )(z, v_gain, w_s, b_rows)


def _mixer_a_step_kernel(z_ref, vg_ref, w0_ref, b0_ref, ya_ref, va_ref, *, width):
    ge = _gelu(z_ref[...])
    for g in range(width // A_GROUP):
        sl = slice(g * A_GROUP, (g + 1) * A_GROUP)
        u = ge[:, sl]
        v = ge[:, width + g * A_GROUP:width + (g + 1) * A_GROUP]
        va = _rms_rows(v, vg_ref[g:g + 1, :])
        va_ref[:, sl] = va
        ya_ref[:, sl] = (u * (w0_ref[:, sl] * va + b0_ref[:, sl])).astype(ya_ref.dtype)


def mixer_a_step(z, v_gain, w_s, b_s, width):
    m = z.shape[0]
    w0 = jnp.repeat(w_s[:, 0, 0], A_GROUP).reshape(1, width)
    b0 = jnp.repeat(b_s[:, 0], A_GROUP).reshape(1, width)
    return pl.pallas_call(
        functools.partial(_mixer_a_step_kernel, width=width),
        grid=(1,),
        in_specs=[pl.BlockSpec((m, 2 * width), lambda i: (0, 0)),
                  pl.BlockSpec(v_gain.shape, lambda i: (0, 0)),
                  pl.BlockSpec((1, width), lambda i: (0, 0)),
                  pl.BlockSpec((1, width), lambda i: (0, 0))],
        out_specs=[pl.BlockSpec((m, width), lambda i: (0, 0))] * 2,
        out_shape=[jax.ShapeDtypeStruct((m, width), bf16), jax.ShapeDtypeStruct((m, width), f32)],
        compiler_params=_params(("arbitrary",)),
        name="mixer_a_step",
    )(z, v_gain, w0, b0)


def _split3(x):
    hi = x.astype(bf16)
    r1 = x - hi.astype(f32)
    mid = r1.astype(bf16)
    lo = (r1 - mid.astype(f32)).astype(bf16)
    return hi, mid, lo


def _dot01_left(sel, x):
    hi, mid, lo = _split3(x)
    d = lambda p: jnp.dot(sel, p, preferred_element_type=f32)
    return d(hi) + d(mid) + d(lo)


def _dot01_right(x, sel):
    hi, mid, lo = _split3(x)
    d = lambda p: jnp.dot(p, sel, preferred_element_type=f32)
    return d(hi) + d(mid) + d(lo)


def _log_sigmoid(x):
    return jnp.minimum(x, 0.0) - jnp.log(1.0 + jnp.exp(-jnp.abs(x)))


def _fox_in_kernel(x_ref, g_ref, w_ref, wf_ref, bf_ref, qg_ref, kg_ref,
                   q_ref, k_ref, kb_ref, v_ref, vb_ref, sg_ref, lf_ref, *rest,
                   tiles_per_seq, width, n_heads, with_cumsum):
    i = pl.program_id(0)
    h = _rms_rows(x_ref[...], g_ref[...]).astype(bf16)
    z = jnp.dot(h, w_ref[...], preferred_element_type=f32)
    q, k = z[:, :width], z[:, width:2 * width]
    v, g = z[:, 2 * width:3 * width], z[:, 3 * width:4 * width]
    qn = q * lax.rsqrt(_per_head_sums(q * q) * (1.0 / HEAD) + RMS_EPS) * (qg_ref[...] * HEAD ** -0.5)
    kn = k * lax.rsqrt(_per_head_sums(k * k) * (1.0 / HEAD) + RMS_EPS) * kg_ref[...]
    q_ref[...] = qn.astype(bf16)
    k_ref[...] = kn
    kb_ref[...] = kn.astype(bf16)
    v_ref[...] = v
    vb_ref[...] = v.astype(bf16)
    sg_ref[...] = _sigmoid(g)
    lf = _log_sigmoid(jnp.dot(h, wf_ref[...], preferred_element_type=f32) + bf_ref[...])
    lf_ref[...] = lf[:, :n_heads]
    if not with_cumsum:
        return
    c_ref, ct_ref, carry_ref = rest

    @pl.when(i % tiles_per_seq == 0)
    def _():
        carry_ref[...] = jnp.zeros_like(carry_ref)

    tm = lf.shape[0]
    row = lax.broadcasted_iota(jnp.int32, (tm, tm), 0)
    col = lax.broadcasted_iota(jnp.int32, (tm, tm), 1)
    tri = jnp.where(row >= col, 1.0, 0.0).astype(bf16)
    c = _dot01_left(tri, lf) + carry_ref[...]
    carry_ref[...] = c[tm - 1:tm, :]
    c_ref[...] = c[:, :n_heads]
    ct = c.T
    for p in range(n_heads // 2):
        ct_ref[0, p] = ct[2 * p:2 * p + 2, :]


def fox_in(x, g, w, wf, b_f, q_gain, k_gain, n_seq, tm, with_cumsum):
    m, d = x.shape
    width = w.shape[1] // 4
    n_heads = width // HEAD
    seq = m // n_seq
    tps = seq // tm
    wf_pad = jnp.zeros((d, LANES), bf16).at[:, :n_heads].set(wf)
    bf_pad = jnp.zeros((1, LANES), f32).at[0, :n_heads].set(b_f)
    qg = jnp.tile(q_gain, n_heads).reshape(1, width)
    kg = jnp.tile(k_gain, n_heads).reshape(1, width)
    row = lambda i: (i, 0)
    const = lambda i: (0, 0)
    tok = lambda n, dt: jax.ShapeDtypeStruct((m, n), dt)
    out_specs = [pl.BlockSpec((tm, width), row)] * 6 + [pl.BlockSpec((tm, n_heads), row)]
    out_shape = [tok(width, bf16), tok(width, f32), tok(width, bf16), tok(width, f32),
                 tok(width, bf16), tok(width, f32), tok(n_heads, f32)]
    scratch = []
    if with_cumsum:
        out_specs += [pl.BlockSpec((tm, n_heads), row),
                      pl.BlockSpec((1, n_heads // 2, 2, tm), lambda i: (i // tps, 0, 0, i % tps))]
        out_shape += [tok(n_heads, f32), jax.ShapeDtypeStruct((n_seq, n_heads // 2, 2, seq), f32)]
        scratch = [pltpu.VMEM((1, LANES), f32)]
    return pl.pallas_call(
        functools.partial(_fox_in_kernel, tiles_per_seq=tps, width=width, n_heads=n_heads,
                          with_cumsum=with_cumsum),
        grid=(m // tm,),
        in_specs=[pl.BlockSpec((tm, d), row), pl.BlockSpec((1, d), const),
                  pl.BlockSpec(w.shape, const), pl.BlockSpec((d, LANES), const),
                  pl.BlockSpec((1, LANES), const), pl.BlockSpec((1, width), const),
                  pl.BlockSpec((1, width), const)],
        out_specs=out_specs,
        out_shape=out_shape,
        scratch_shapes=scratch,
        compiler_params=_params(("arbitrary",)),
        name="fox_in",
    )(x, g.reshape(1, d), w, wf_pad, bf_pad, qg, kg)


def _fox_flash_kernel(q_ref, k_ref, v_ref, c_ref, ct_ref, sg_ref, o_ref, *, tq):
    p = pl.program_id(1)
    qi = pl.program_id(2)
    q2 = q_ref[0]
    lo = _lane_iota((tq, LANES)) < HEAD
    zero = jnp.zeros_like(q2)
    qs = (jnp.where(lo, q2, zero), jnp.where(lo, zero, q2))
    c = c_ref[0]
    hl = _lane_iota(c.shape)
    cq = tuple(jnp.sum(jnp.where(hl == 2 * p + j, c, 0.0), axis=-1, keepdims=True) for j in (0, 1))
    row = lax.broadcasted_iota(jnp.int32, (tq, tq), 0)
    col = lax.broadcasted_iota(jnp.int32, (tq, tq), 1)

    def block(kb, carry, masked):
        m0, l0, m1, l1, acc = carry
        start = pl.multiple_of(kb * tq, tq)
        k2 = k_ref[0, pl.ds(start, tq), :]
        v2 = v_ref[0, pl.ds(start, tq), :]
        new, alphas, pvs = [], [], []
        for j, (mj, lj) in enumerate(((m0, l0), (m1, l1))):
            t = _bdot_nt(qs[j], k2) - ct_ref[0, 0, j:j + 1, pl.ds(start, tq)]
            if masked:
                t = jnp.where(col <= row, t, -jnp.inf)
            m_new = jnp.maximum(mj, jnp.max(t, axis=-1, keepdims=True) + cq[j])
            pe = jnp.exp(t - (m_new - cq[j]))
            alpha = jnp.exp(mj - m_new)
            new += [m_new, alpha * lj + jnp.sum(pe, axis=-1, keepdims=True)]
            alphas.append(alpha)
            pvs.append(jnp.dot(pe.astype(bf16), v2, preferred_element_type=f32))
        acc = acc * jnp.where(lo, alphas[0], alphas[1]) + jnp.where(lo, pvs[0], pvs[1])
        return new[0], new[1], new[2], new[3], acc

    neg = jnp.full((tq, 1), -jnp.inf, f32)
    zl = jnp.zeros((tq, 1), f32)
    carry = (neg, zl, neg, zl, jnp.zeros((tq, LANES), f32))
    carry = lax.fori_loop(0, qi, lambda kb, cr: block(kb, cr, False), carry)
    m0, l0, m1, l1, acc = block(qi, carry, True)
    o_ref[0] = (acc / jnp.where(lo, l0, l1) * sg_ref[0]).astype(o_ref.dtype)


def fox_flash(q, kb, vb, c, ct, sg, n_seq, tq):
    m, width = q.shape
    seq = m // n_seq
    n_heads = c.shape[1]
    r3 = lambda a: a.reshape(n_seq, seq, a.shape[1])
    tile = pl.BlockSpec((1, tq, LANES), lambda b, p, i: (b, i, p))
    whole = pl.BlockSpec((1, seq, LANES), lambda b, p, i: (b, 0, p))
    out = pl.pallas_call(
        functools.partial(_fox_flash_kernel, tq=tq),
        grid=(n_seq, width // LANES, seq // tq),
        in_specs=[tile, whole, whole,
                  pl.BlockSpec((1, tq, n_heads), lambda b, p, i: (b, i, 0)),
                  pl.BlockSpec((1, 1, 2, seq), lambda b, p, i: (b, p, 0, 0)),
                  tile],
        out_specs=tile,
        out_shape=jax.ShapeDtypeStruct((n_seq, seq, width), bf16),
        compiler_params=_params(("parallel", "parallel", "arbitrary")),
        name="fox_flash",
    )(r3(q), r3(kb), r3(vb), r3(c), ct, r3(sg))
    return out.reshape(m, width)


def _fox_step_kernel(pt_ref, qc_ref, knt_ref, vnt_ref, lfn_ref, sg_ref, *rest, n_heads, pps):
    kt_refs, vt_refs, lf_refs = rest[:pps], rest[pps:2 * pps], rest[2 * pps:3 * pps]
    o_ref, m_ref, l_ref, acc_ref, carry_ref = rest[3 * pps:]
    pg = pl.program_id(1)
    page = lf_refs[0].shape[2]

    def attend(kt_ref, vt_ref, bias):
        rows = [jnp.sum(kt_ref[0, h] * qc_ref[0, h], axis=0, keepdims=True) for h in range(n_heads)]
        t = jnp.concatenate(rows, axis=0) + bias
        m_old = m_ref[...]
        m_new = jnp.maximum(m_old, jnp.max(t, axis=-1, keepdims=True))
        alpha = jnp.exp(m_old - m_new)
        pe = jnp.exp(t - m_new)
        l_ref[...] = alpha * l_ref[...] + jnp.sum(pe, axis=-1, keepdims=True)
        m_ref[...] = m_new
        for h in range(n_heads):
            acc_ref[h] = acc_ref[h] * alpha[h:h + 1, :] + pe[h:h + 1, :] * vt_ref[0, h]

    @pl.when(pg == 0)
    def _():
        m_ref[...] = jnp.full(m_ref.shape, -jnp.inf, f32)
        l_ref[...] = jnp.zeros_like(l_ref)
        acc_ref[...] = jnp.zeros_like(acc_ref)
        carry_ref[...] = lfn_ref[0]
        attend(knt_ref, vnt_ref, jnp.where(_lane_iota((n_heads, page)) == 0, 0.0, -jnp.inf))

    r = lax.broadcasted_iota(jnp.int32, (page, page), 0)
    cidx = lax.broadcasted_iota(jnp.int32, (page, page), 1)
    later = jnp.where(r > cidx, 1.0, 0.0).astype(bf16)
    for j in range(pps):
        lf = lf_refs[j][0]
        attend(kt_refs[j], vt_refs[j], _dot01_right(lf, later) + carry_ref[...])
        carry_ref[...] = carry_ref[...] + jnp.sum(lf, axis=-1, keepdims=True)

    @pl.when(pg == pl.num_programs(1) - 1)
    def _():
        l = l_ref[...]
        for h in range(n_heads):
            o_ref[0, h] = jnp.sum(acc_ref[h], axis=-1, keepdims=True) / l[h:h + 1, :] * sg_ref[0, h]


FOX_PAGES_PER_STEP = 4


def fox_step(q, k_new, v_new, lf_new, sg, cache_k, cache_v, cache_lf, page_table):
    b, width = q.shape
    pool, page, n_heads = cache_lf.shape
    n_pages = page_table.shape[1]
    pps = FOX_PAGES_PER_STEP
    kt = jnp.transpose(cache_k, (0, 2, 3, 1))
    vt = jnp.transpose(cache_v, (0, 2, 3, 1))
    lft = jnp.transpose(cache_lf, (0, 2, 1))
    col = lambda a: a.reshape(b, n_heads, HEAD, 1)
    own_page = lambda a: jnp.zeros((b, n_heads, HEAD, page), f32).at[..., 0].set(a.reshape(b, n_heads, HEAD))
    qc = jnp.broadcast_to(col(q), (b, n_heads, HEAD, page))
    per_seq = lambda shape: pl.BlockSpec((1,) + shape, lambda i, g, pt: (i,) + (0,) * len(shape))
    page_spec = lambda shape, j: pl.BlockSpec(
        (1,) + shape, lambda i, g, pt: (pt[i, n_pages - 1 - (g * pps + j)],) + (0,) * len(shape))
    grid_spec = pltpu.PrefetchScalarGridSpec(
        num_scalar_prefetch=1,
        grid=(b, n_pages // pps),
        in_specs=[per_seq((n_heads, HEAD, page))] * 3 + [per_seq((n_heads, 1)), per_seq((n_heads, HEAD, 1))]
                 + [page_spec((n_heads, HEAD, page), j) for j in range(pps)] * 2
                 + [page_spec((n_heads, page), j) for j in range(pps)],
        out_specs=per_seq((n_heads, HEAD, 1)),
        scratch_shapes=[pltpu.VMEM((n_heads, 1), f32), pltpu.VMEM((n_heads, 1), f32),
                        pltpu.VMEM((n_heads, HEAD, page), f32), pltpu.VMEM((n_heads, 1), f32)],
    )
    out = pl.pallas_call(
        functools.partial(_fox_step_kernel, n_heads=n_heads, pps=pps),
        grid_spec=grid_spec,
        out_shape=jax.ShapeDtypeStruct((b, n_heads, HEAD, 1), f32),
        compiler_params=_params(("parallel", "arbitrary")),
        name="fox_step",
    )(page_table, qc, own_page(k_new), own_page(v_new), lf_new.reshape(b, n_heads, 1), col(sg),
      *([kt] * pps), *([vt] * pps), *([lft] * pps))
    return out.reshape(b, width).astype(bf16)


def _mem_attn_step_kernel(q_ref, qg_ref, k_ref, v_ref, o_ref):
    q = jnp.broadcast_to(q_ref[0], (8, q_ref.shape[2]))
    qg = qg_ref[...] * (MEM_HEAD ** -0.5)
    for hd in range(q.shape[1] // MEM_HEAD):
        sl = slice(hd * MEM_HEAD, (hd + 1) * MEM_HEAD)
        qn = _rms_rows(q[:, sl], qg)
        s = _bdot_nt(qn, k_ref[0, :, sl])
        p = jnp.exp(s - jnp.max(s, axis=-1, keepdims=True))
        l = jnp.sum(p, axis=-1, keepdims=True)
        o_ref[0, :, sl] = (_bdot(p, v_ref[0, :, sl]) / l)[0:1].astype(o_ref.dtype)


def mem_attn_step(q, q_gain, k, v):
    b, d = q.shape
    n_mem = k.shape[1]
    out = pl.pallas_call(
        _mem_attn_step_kernel,
        grid=(b,),
        in_specs=[pl.BlockSpec((1, 1, d), lambda i: (i, 0, 0)),
                  pl.BlockSpec((1, MEM_HEAD), lambda i: (0, 0)),
                  pl.BlockSpec((1, n_mem, d), lambda i: (i, 0, 0)),
                  pl.BlockSpec((1, n_mem, d), lambda i: (i, 0, 0))],
        out_specs=pl.BlockSpec((1, 1, d), lambda i: (i, 0, 0)),
        out_shape=jax.ShapeDtypeStruct((b, 1, d), bf16),
        compiler_params=_params(("parallel",)),
        name="mem_attn_step",
    )(q.reshape(b, 1, d), q_gain.reshape(1, MEM_HEAD), k, v)
    return out.reshape(b, d)


def _softplus(x):
    return jnp.maximum(x, 0.0) + jnp.log(1.0 + jnp.exp(-jnp.abs(x)))


def _rwkv_kernel(zr_ref, zk_ref, zv_ref, zg_ref, zwa_ref, sh0_ref, s0_ref,
                 mu_ref, w0_ref, a0_ref, kk_ref, ka_ref, rk_ref, lnw_ref, lnb_ref, wup_ref, aup_ref,
                 y_ref, sout_ref, shout_ref,
                 s_ref, carry_ref, r_s, k_s, v_s, al_s, be_s, ld_s, yo_s,
                 *, chunk, n_valid, width):
    i = pl.program_id(1)
    nblk = pl.num_programs(1)
    cg = zr_ref.shape[0]
    n_pairs = width // LANES

    @pl.when(i == 0)
    def _():
        s_ref[...] = s0_ref[0]
        carry_ref[...] = sh0_ref[0]

    rows = lax.broadcasted_iota(jnp.int32, (cg, 1), 0)
    valid = (rows + i * cg) < n_valid

    def shifted(z, lane0, n):
        prev = jnp.where(rows == 0, carry_ref[:, lane0:lane0 + n], pltpu.roll(z, 1, 0))
        return z + (prev - z) * mu_ref[:, lane0:lane0 + n]

    zr, zk, zv, zg, zwa = zr_ref[...], zk_ref[...], zv_ref[...], zg_ref[...], zwa_ref[...]
    r = shifted(zr, 0, width)
    k = shifted(zk, width, width)
    v = shifted(zv, 2 * width, width)
    gp = shifted(zg, 3 * width, width)
    wa = shifted(zwa, 4 * width, LANES)

    last = jnp.minimum(n_valid - 1 - i * cg, cg - 1)
    pieces = ((zr, 0, width), (zk, width, width), (zv, 2 * width, width), (zg, 3 * width, width),
              (zwa, 4 * width, LANES))

    @pl.when(last >= 0)
    def _():
        for z, lane0, n in pieces:
            carry_ref[:, lane0:lane0 + n] = jnp.sum(jnp.where(rows == last, z, 0.0), axis=0, keepdims=True)

    w = -_softplus(-(w0_ref[...] + _bdot(jnp.tanh(wa), wup_ref[...]))) - 0.5
    logd = -jnp.exp(w)
    a = _sigmoid(a0_ref[...] + _bdot(wa, aup_ref[...]))
    kk = k * kk_ref[...]
    kk = kk * lax.rsqrt(_per_head_sums(kk * kk) + L2_EPS)
    k2 = k * (1.0 + (a - 1.0) * ka_ref[...])
    bonus = _per_head_sums(r * k2 * rk_ref[...]) * v
    r_s[...] = r
    k_s[...] = jnp.where(valid, k2, 0.0)
    v_s[...] = jnp.where(valid, v, 0.0)
    al_s[...] = -kk
    be_s[...] = jnp.where(valid, kk * a, 0.0)
    ld_s[...] = jnp.where(valid, logd, 0.0)

    ri = lax.broadcasted_iota(jnp.int32, (chunk, chunk), 0)
    ci = lax.broadcasted_iota(jnp.int32, (chunk, chunk), 1)
    tri = jnp.where(ri >= ci, 1.0, 0.0).astype(bf16)
    lower, strict = ri >= ci, ri > ci
    eye = jnp.where(ri == ci, 1.0, 0.0)
    lo = _lane_iota((1, LANES)) < HEAD
    sr = lax.broadcasted_iota(jnp.int32, (LANES, LANES), 0) < HEAD
    sc = lax.broadcasted_iota(jnp.int32, (LANES, LANES), 1) < HEAD
    same_head = sr == sc
    n_sq = max(int(math.log2(chunk)) - 1, 0)

    def chunk_body(c, _):
        rs = pl.ds(pl.multiple_of(c * chunk, chunk), chunk)
        ld = ld_s[rs, :]
        lcum = _dot01_left(tri, ld)
        lend = lcum[chunk - 1:chunk, :]
        e_in, e_ex, e_neg, e_end = jnp.exp(lcum), jnp.exp(lcum - ld), jnp.exp(-lcum), jnp.exp(lend - lcum)
        xa_all, xr_all = al_s[rs, :] * e_ex, r_s[rs, :] * e_in
        be, kc, vv = be_s[rs, :], k_s[rs, :], v_s[rs, :]
        yb_all, yk_all = be * e_neg, kc * e_neg
        ybe_all, yke_all = be * e_end, kc * e_end
        d_end = jnp.exp(lend)
        for p in range(n_pairs):
            ls = slice(p * LANES, (p + 1) * LANES)
            xa, xr, yb, yk, vp = xa_all[:, ls], xr_all[:, ls], yb_all[:, ls], yk_all[:, ls], vv[:, ls]
            s_bd = s_ref[p]
            xs = jnp.concatenate([xa, xr], axis=0)
            xh = _bdot_nt(xs, s_bd)
            us, ys = [], []
            for j in (0, 1):
                hm = lo if j == 0 else jnp.logical_not(lo)
                xm = jnp.where(hm, xs, 0.0)
                gb = _bdot_nt(xm, yb)
                gk = _bdot_nt(xm, yk)
                a_ab = jnp.where(strict, gb[:chunk], 0.0)
                a_ak = jnp.where(strict, gk[:chunk], 0.0)
                a_rb = jnp.where(lower, gb[chunk:], 0.0)
                a_rk = jnp.where(lower, gk[chunk:], 0.0)
                tinv = eye + a_ab
                pw = a_ab
                for _ in range(n_sq):
                    pw = _bdot(pw, pw)
                    tinv = tinv + _bdot(tinv, pw)
                u = _bdot(tinv, xh[:chunk] + _bdot(a_ak, vp))
                us.append(u)
                ys.append((a_rb, a_rk))
            u = jnp.where(lo, us[0], us[1])
            y = xh[chunk:]
            y = y + jnp.where(lo, _bdot(ys[0][0], u) + _bdot(ys[0][1], vp),
                              _bdot(ys[1][0], u) + _bdot(ys[1][1], vp))
            yo_s[rs, ls] = y
            uv_t = jnp.concatenate([u, vp], axis=0).T
            upd = _bdot(uv_t, jnp.concatenate([ybe_all[:, ls], yke_all[:, ls]], axis=0))
            s_ref[p] = s_bd * d_end[:, ls] + jnp.where(same_head, upd, 0.0)
        return 0

    lax.fori_loop(0, cg // chunk, chunk_body, 0)

    y = yo_s[...]
    mean = _per_head_sums(y) * (1.0 / HEAD)
    yc = y - mean
    var = _per_head_sums(yc * yc) * (1.0 / HEAD)
    y = yc * lax.rsqrt(var + GN_EPS) * lnw_ref[...] + lnb_ref[...]
    y_ref[...] = ((y + bonus) * _sigmoid(gp)).astype(y_ref.dtype)

    @pl.when(i == nblk - 1)
    def _():
        sout_ref[0] = s_ref[...]
        shout_ref[0] = carry_ref[...]


def rwkv7(z, col0, shift0, s0, prm, n_seq, n_valid, chunk, block_rows):
    m = z.shape[0]
    rows = m // n_seq
    nblk = rows // block_rows
    width = prm["b_w0"].shape[0]
    n_heads = width // HEAD
    n_pairs = width // LANES
    bcols = 4 * width + LANES
    s0p = s0.reshape(n_seq, n_pairs, 2, HEAD, HEAD)
    zeros = jnp.zeros_like(s0p[:, :, 0])
    s0bd = jnp.concatenate([jnp.concatenate([s0p[:, :, 0], zeros], axis=-1),
                            jnp.concatenate([zeros, s0p[:, :, 1]], axis=-1)], axis=-2)
    lora = lambda w_, off: jnp.zeros((LANES, width), bf16).at[off:off + w_.shape[0]].set(w_.astype(bf16))
    row1 = lambda a: a.reshape(1, -1).astype(f32)
    cb = col0 // width
    zspec = lambda j: pl.BlockSpec((block_rows, width), lambda b, i: (b * nblk + i, cb + j))
    const = lambda shape: pl.BlockSpec(shape, lambda b, i: (0,) * len(shape))
    y, sbd, shift = pl.pallas_call(
        functools.partial(_rwkv_kernel, chunk=chunk, n_valid=n_valid, width=width),
        grid=(n_seq, nblk),
        in_specs=[zspec(0), zspec(1), zspec(2), zspec(3),
                  pl.BlockSpec((block_rows, LANES), lambda b, i: (b * nblk + i, (col0 + 4 * width) // LANES)),
                  pl.BlockSpec((1, 1, bcols), lambda b, i: (b, 0, 0)),
                  pl.BlockSpec((1, n_pairs, LANES, LANES), lambda b, i: (b, 0, 0, 0)),
                  const((1, bcols))] + [const((1, width))] * 7 + [const((LANES, width))] * 2,
        out_specs=[pl.BlockSpec((block_rows, width), lambda b, i: (b * nblk + i, 0)),
                   pl.BlockSpec((1, n_pairs, LANES, LANES), lambda b, i: (b, 0, 0, 0)),
                   pl.BlockSpec((1, 1, bcols), lambda b, i: (b, 0, 0))],
        out_shape=[jax.ShapeDtypeStruct((m, width), bf16),
                   jax.ShapeDtypeStruct((n_seq, n_pairs, LANES, LANES), f32),
                   jax.ShapeDtypeStruct((n_seq, 1, bcols), f32)],
        scratch_shapes=[pltpu.VMEM((n_pairs, LANES, LANES), f32), pltpu.VMEM((1, bcols), f32)]
                       + [pltpu.VMEM((block_rows, width), f32)] * 7,
        compiler_params=_params(("parallel", "arbitrary")),
        name="rwkv7",
    )(z, z, z, z, z, shift0.reshape(n_seq, 1, bcols), s0bd,
      row1(prm["b_mu"]), row1(prm["b_w0"]), row1(prm["b_a0"]), row1(prm["b_k_k"]), row1(prm["b_k_a"]),
      row1(prm["b_r_k"]), row1(prm["b_ln_w"]), row1(prm["b_ln_b"]),
      lora(prm["b_w_up"], 0), lora(prm["b_a_up"], HEAD))
    sp = sbd.reshape(n_seq, n_pairs, 2, HEAD, 2, HEAD)
    s_fin = jnp.stack([sp[:, :, 0, :, 0, :], sp[:, :, 1, :, 1, :]], axis=2).reshape(n_seq, n_heads, HEAD, HEAD)
    return y, s_fin, shift.reshape(n_seq, bcols)


A_WIDTH = 512
STEP_ROWS = 64
RWKV_CHUNK = 64
RWKV_BLOCK = 256
FLASH_TILE = 256


def _trunk(x3, mem_k, mem_v, rwkv_s0, rwkv_shift0, fox_past, p):
    n_seq, seq, d = x3.shape
    m = n_seq * seq
    x = x3.reshape(m, d)
    step = seq == 1
    tm = m if step else 256
    depth = p["norm_mix"].shape[0]
    chunk_v, s_out, shift_out, fk, fv, fl = [], [], [], [], [], []
    ie = io = 0
    for l in range(depth):
        if l % 2 == 0:
            z = norm_matmul(x, p["norm_mix"][l], p["ab_w_in"][ie], tm)
            prm = {k_: p[k_][ie] for k_ in ("b_mu", "b_w0", "b_w_up", "b_a0", "b_a_up", "b_k_k", "b_k_a",
                                           "b_r_k", "b_ln_w", "b_ln_b")}
            if step:
                ya, va = mixer_a_step(z, p["a_v_norm"][ie], p["a_w_s"][ie], p["a_b_s"][ie], A_WIDTH)
                zp = jnp.zeros((n_seq, STEP_ROWS, z.shape[1]), f32).at[:, 0].set(z)
                yb, s_new, sh_new = rwkv7(zp.reshape(n_seq * STEP_ROWS, -1), 2 * A_WIDTH, rwkv_shift0[ie],
                                          rwkv_s0[ie], prm, n_seq, 1, RWKV_CHUNK, STEP_ROWS)
                yb = yb.reshape(n_seq, STEP_ROWS, -1)[:, 0]
                chunk_v.append(va.reshape(n_seq, seq, A_WIDTH))
            else:
                ya = mixer_a(z, p["a_v_norm"][ie], p["a_w_s"][ie], p["a_b_s"][ie], A_WIDTH, tm)
                yb, s_new, sh_new = rwkv7(z, 2 * A_WIDTH, rwkv_shift0[ie], rwkv_s0[ie], prm, n_seq, seq,
                                          RWKV_CHUNK, RWKV_BLOCK)
            w_out = p["ab_w_out"][ie]
            x = res_matmul(x, [(ya, w_out[:A_WIDTH]), (yb, w_out[A_WIDTH:])], tm)
            s_out.append(s_new)
            shift_out.append(sh_new)
            ie += 1
        else:
            w_in = p["c_w_in"][io]
            width = (w_in.shape[1] // 4) // HEAD * HEAD
            outs = fox_in(x, p["norm_mix"][l], w_in[:, :4 * width], w_in[:, 4 * width:], p["c_b_f"][io],
                          p["c_q_norm"][io], p["c_k_norm"][io], 1 if step else n_seq, tm, not step)
            q, k, kb, v, vb, sg, lf = outs[:7]
            if step:
                ck, cv, cl, pt = fox_past
                og = fox_step(q.astype(f32), k, v, lf, sg, ck[io], cv[io], cl[io], pt)
            else:
                og = fox_flash(q, kb, vb, outs[7], outs[8], sg, n_seq, FLASH_TILE)
            x = res_matmul(x, [(og, p["c_w_out"][io])], tm)
            n_heads = width // HEAD
            fk.append(k.reshape(n_seq, seq, n_heads, HEAD))
            fv.append(v.reshape(n_seq, seq, n_heads, HEAD))
            fl.append(lf.reshape(n_seq, seq, n_heads))
            io += 1
        if step:
            q_raw = norm_matmul(x, p["norm_mem"][l], p["m_wq"][l], tm)
            o = mem_attn_step(q_raw, p["m_q_norm"][l], mem_k[l], mem_v[l])
            x = res_matmul(x, [(o, p["m_wo"][l])], tm)
        else:
            x = mem_attn(x, p["norm_mem"][l], p["m_wq"][l], p["m_q_norm"][l], mem_k[l].astype(bf16),
                         mem_v[l].astype(bf16), p["m_wo"][l], seq, tm)
        x = ffn(x, p["norm_ffn"][l], p["f_w_in"][l], p["f_w_out"][l], tm)
    stack = lambda xs: jnp.stack(xs)
    return (x.reshape(n_seq, seq, d), (stack(chunk_v) if chunk_v else None), stack(s_out), stack(shift_out),
            stack(fk), stack(fv), stack(fl))


def kernel(x_prompt, x_sample, mem_prompt, cache_mem_k, cache_mem_v, state_rwkv_S, state_rwkv_shift, cache_fox_k, cache_fox_v, cache_fox_logf, page_table, norm_mix, norm_mem, norm_ffn, ab_w_in, ab_w_out, a_v_norm, a_w_s, a_b_s, b_mu, b_w0, b_w_up, b_a0, b_a_up, b_k_k, b_k_a, b_r_k, b_ln_w, b_ln_b, c_w_in, c_b_f, c_q_norm, c_k_norm, c_w_out, m_mem_norm, m_wq, m_wk, m_wv, m_q_norm, m_k_norm, m_wo, f_w_in, f_w_out):
    cast = lambda w: w.astype(bf16)
    p = dict(norm_mix=norm_mix, norm_mem=norm_mem, norm_ffn=norm_ffn, ab_w_in=cast(ab_w_in),
             ab_w_out=cast(ab_w_out), a_v_norm=a_v_norm, a_w_s=a_w_s, a_b_s=a_b_s, b_mu=b_mu, b_w0=b_w0,
             b_w_up=b_w_up, b_a0=b_a0, b_a_up=b_a_up, b_k_k=b_k_k, b_k_a=b_k_a, b_r_k=b_r_k, b_ln_w=b_ln_w,
             b_ln_b=b_ln_b, c_w_in=cast(c_w_in), c_b_f=c_b_f, c_q_norm=c_q_norm, c_k_norm=c_k_norm,
             c_w_out=cast(c_w_out), m_wq=cast(m_wq), m_q_norm=m_q_norm, m_wo=cast(m_wo),
             f_w_in=cast(f_w_in), f_w_out=cast(f_w_out))
    depth, d = norm_mix.shape
    nb, n_mem = mem_prompt.shape[:2]
    n_even, _, n_heads_b, hd, _ = state_rwkv_S.shape
    mem_heads = d // MEM_HEAD

    mem2 = mem_prompt.reshape(nb * n_mem, d)
    kvs = [mem_kv(mem2, m_mem_norm[l], cast(m_wk[l]), cast(m_wv[l]), m_k_norm[l], 256) for l in range(depth)]
    p_mem_k = jnp.stack([kv[0] for kv in kvs]).reshape(depth, nb, n_mem, d)
    p_mem_v = jnp.stack([kv[1] for kv in kvs]).reshape(depth, nb, n_mem, d)
    s0 = jnp.zeros((n_even, nb, n_heads_b, hd, hd), f32)
    shift0 = jnp.zeros((n_even, nb, state_rwkv_shift.shape[2]), f32)
    y_prompt, _, p_rwkv_s, p_rwkv_shift, p_fox_k, p_fox_v, p_fox_logf = _trunk(
        x_prompt, p_mem_k, p_mem_v, s0, shift0, None, p)

    nd = x_sample.shape[0]
    y_sample, s_chunk_v, s_rwkv_s, s_rwkv_shift, s_fox_k, s_fox_v, s_fox_logf = _trunk(
        x_sample, cache_mem_k.reshape(depth, nd, n_mem, d), cache_mem_v.reshape(depth, nd, n_mem, d),
        state_rwkv_S, state_rwkv_shift, (cache_fox_k, cache_fox_v, cache_fox_logf, page_table), p)
    heads5 = lambda a: a.reshape(depth, nb, n_mem, mem_heads, MEM_HEAD)
    return (y_prompt, y_sample, heads5(p_mem_k), heads5(p_mem_v), p_rwkv_s, p_rwkv_shift, p_fox_k, p_fox_v,
            p_fox_logf, s_chunk_v, s_rwkv_s, s_rwkv_shift, s_fox_k, s_fox_v, s_fox_logf)
```

```python
import functools
import math

import jax
import jax.numpy as jnp
from jax import lax
from jax.experimental import pallas as pl
from jax.experimental.pallas import tpu as pltpu

f32 = jnp.float32
bf16 = jnp.bfloat16

RMS_EPS = 1e-6
GN_EPS = 64e-5
L2_EPS = 1e-12
LOG2E = 1.4426950408889634
LANES = 128
HEAD = 64
A_GROUP = 128
MEM_HEAD = 256
VMEM_LIMIT = 56 * 1024 * 1024


def _params(sem):
    return pltpu.CompilerParams(dimension_semantics=sem, vmem_limit_bytes=VMEM_LIMIT)


def _bdot(a, b):
    return jnp.dot(a.astype(bf16), b.astype(bf16), preferred_element_type=f32)


def _bdot_nt(a, b):
    return lax.dot_general(a.astype(bf16), b.astype(bf16), (((1,), (1,)), ((), ())),
                           preferred_element_type=f32)


def _rms_rows(x, g):
    return x * lax.rsqrt(jnp.mean(x * x, axis=-1, keepdims=True) + RMS_EPS) * g


def _sigmoid(x):
    return 1.0 / (1.0 + jnp.exp(-x))


def _lane_iota(shape):
    return lax.broadcasted_iota(jnp.int32, shape, len(shape) - 1)


def _half_sums(x):
    lo = _lane_iota(x.shape) < HEAD
    s_lo = jnp.sum(jnp.where(lo, x, 0.0), axis=-1, keepdims=True)
    s_hi = jnp.sum(jnp.where(lo, 0.0, x), axis=-1, keepdims=True)
    return jnp.where(lo, s_lo, s_hi)


def _per_head_sums(x):
    n = x.shape[-1] // LANES
    return jnp.concatenate([_half_sums(x[:, i * LANES:(i + 1) * LANES]) for i in range(n)], axis=-1)


def _norm_matmul_kernel(x_ref, g_ref, w_ref, o_ref):
    h = _rms_rows(x_ref[...], g_ref[...])
    o_ref[...] = _bdot(h, w_ref[...])


def norm_matmul(x, g, w, tm):
    m, d = x.shape
    n = w.shape[1]
    return pl.pallas_call(
        _norm_matmul_kernel,
        grid=(m // tm,),
        in_specs=[pl.BlockSpec((tm, d), lambda i: (i, 0)),
                  pl.BlockSpec((1, d), lambda i: (0, 0)),
                  pl.BlockSpec((d, n), lambda i: (0, 0))],
        out_specs=pl.BlockSpec((tm, n), lambda i: (i, 0)),
        out_shape=jax.ShapeDtypeStruct((m, n), f32),
        compiler_params=_params(("parallel",)),
        name="norm_matmul",
    )(x, g.reshape(1, d), w)


def _res_matmul_kernel(*refs):
    x_ref, o_ref = refs[0], refs[-1]
    acc = x_ref[...]
    for a_ref, w_ref in zip(refs[1:-1:2], refs[2:-1:2]):
        acc = acc + _bdot(a_ref[...], w_ref[...])
    o_ref[...] = acc


def res_matmul(x, pairs, tm):
    m, d = x.shape
    in_specs = [pl.BlockSpec((tm, d), lambda i: (i, 0))]
    args = [x]
    for a, w in pairs:
        in_specs += [pl.BlockSpec((tm, a.shape[1]), lambda i: (i, 0)),
                     pl.BlockSpec(w.shape, lambda i: (0, 0))]
        args += [a, w]
    return pl.pallas_call(
        _res_matmul_kernel,
        grid=(m // tm,),
        in_specs=in_specs,
        out_specs=pl.BlockSpec((tm, d), lambda i: (i, 0)),
        out_shape=jax.ShapeDtypeStruct((m, d), f32),
        compiler_params=_params(("parallel",)),
        name="res_matmul",
    )(*args)


def _ffn_kernel(x_ref, g_ref, wi_ref, wo_ref, o_ref, *, d_ff, tf):
    x = x_ref[...]
    h = _rms_rows(x, g_ref[...]).astype(bf16)
    acc = x
    for c in range(d_ff // tf):
        gate = jnp.dot(h, wi_ref[:, c * tf:(c + 1) * tf], preferred_element_type=f32)
        up = jnp.dot(h, wi_ref[:, d_ff + c * tf:d_ff + (c + 1) * tf], preferred_element_type=f32)
        act = gate * _sigmoid(gate) * up
        acc = acc + _bdot(act, wo_ref[c * tf:(c + 1) * tf, :])
    o_ref[...] = acc


def ffn(x, g, w_in, w_out, tm):
    m, d = x.shape
    d_ff = w_out.shape[0]
    tf = 256
    return pl.pallas_call(
        functools.partial(_ffn_kernel, d_ff=d_ff, tf=tf),
        grid=(m // tm,),
        in_specs=[pl.BlockSpec((tm, d), lambda i: (i, 0)),
                  pl.BlockSpec((1, d), lambda i: (0, 0)),
                  pl.BlockSpec(w_in.shape, lambda i: (0, 0)),
                  pl.BlockSpec(w_out.shape, lambda i: (0, 0))],
        out_specs=pl.BlockSpec((tm, d), lambda i: (i, 0)),
        out_shape=jax.ShapeDtypeStruct((m, d), f32),
        compiler_params=_params(("parallel",)),
        name="ffn",
    )(x, g.reshape(1, d), w_in, w_out)


def _mem_kv_kernel(m_ref, g_ref, wk_ref, wv_ref, kg_ref, k_ref, v_ref):
    h = _rms_rows(m_ref[...], g_ref[...]).astype(bf16)
    k = jnp.dot(h, wk_ref[...], preferred_element_type=f32)
    kg = kg_ref[...]
    for hd in range(k.shape[1] // MEM_HEAD):
        sl = slice(hd * MEM_HEAD, (hd + 1) * MEM_HEAD)
        k_ref[:, sl] = _rms_rows(k[:, sl], kg)
    v_ref[...] = jnp.dot(h, wv_ref[...], preferred_element_type=f32)


def mem_kv(mem, g, wk, wv, k_gain, tm):
    m, d = mem.shape
    return pl.pallas_call(
        _mem_kv_kernel,
        grid=(m // tm,),
        in_specs=[pl.BlockSpec((tm, d), lambda i: (i, 0)),
                  pl.BlockSpec((1, d), lambda i: (0, 0)),
                  pl.BlockSpec((d, d), lambda i: (0, 0)),
                  pl.BlockSpec((d, d), lambda i: (0, 0)),
                  pl.BlockSpec((1, MEM_HEAD), lambda i: (0, 0))],
        out_specs=[pl.BlockSpec((tm, d), lambda i: (i, 0))] * 2,
        out_shape=[jax.ShapeDtypeStruct((m, d), f32)] * 2,
        compiler_params=_params(("parallel",)),
        name="mem_kv",
    )(mem, g.reshape(1, d), wk, wv, k_gain.reshape(1, MEM_HEAD))


def _mem_attn_kernel(x_ref, g_ref, wq_ref, qg_ref, k_ref, v_ref, wo_ref, o_ref):
    x = x_ref[...]
    h = _rms_rows(x, g_ref[...])
    q = _bdot(h, wq_ref[...])
    qg = qg_ref[...] * (MEM_HEAD ** -0.5)
    outs = []
    for hd in range(q.shape[1] // MEM_HEAD):
        sl = slice(hd * MEM_HEAD, (hd + 1) * MEM_HEAD)
        qn = _rms_rows(q[:, sl], qg)
        s = _bdot_nt(qn, k_ref[0, :, sl])
        p = jnp.exp(s - jnp.max(s, axis=-1, keepdims=True))
        l = jnp.sum(p, axis=-1, keepdims=True)
        outs.append(_bdot(p, v_ref[0, :, sl]) / l)
    o = jnp.concatenate(outs, axis=-1)
    o_ref[...] = x + _bdot(o, wo_ref[...])


def mem_attn(x, g, wq, q_gain, k, v, wo, rows_per_seq, tm):
    m, d = x.shape
    n_mem = k.shape[1]
    per = rows_per_seq // tm
    return pl.pallas_call(
        _mem_attn_kernel,
        grid=(m // tm,),
        in_specs=[pl.BlockSpec((tm, d), lambda i: (i, 0)),
                  pl.BlockSpec((1, d), lambda i: (0, 0)),
                  pl.BlockSpec((d, d), lambda i: (0, 0)),
                  pl.BlockSpec((1, MEM_HEAD), lambda i: (0, 0)),
                  pl.BlockSpec((1, n_mem, d), lambda i: (i // per, 0, 0)),
                  pl.BlockSpec((1, n_mem, d), lambda i: (i // per, 0, 0)),
                  pl.BlockSpec((d, d), lambda i: (0, 0))],
        out_specs=pl.BlockSpec((tm, d), lambda i: (i, 0)),
        out_shape=jax.ShapeDtypeStruct((m, d), f32),
        compiler_params=_params(("parallel",)),
        name="mem_attn",
    )(x, g.reshape(1, d), wq, q_gain.reshape(1, MEM_HEAD), k, v, wo)


def _gelu(x):
    return 0.5 * x * (1.0 + jnp.tanh(0.7978845608028654 * (x + 0.044715 * x * x * x)))


def _mixer_a_kernel(z_ref, vg_ref, w_ref, b_ref, ya_ref, *, width):
    ge = _gelu(z_ref[...])
    tm = ge.shape[0]
    row = lax.broadcasted_iota(jnp.int32, (A_GROUP, A_GROUP), 0)
    col = lax.broadcasted_iota(jnp.int32, (A_GROUP, A_GROUP), 1)
    for g in range(width // A_GROUP):
        u = ge[:, g * A_GROUP:(g + 1) * A_GROUP]
        v = ge[:, width + g * A_GROUP:width + (g + 1) * A_GROUP]
        va = _rms_rows(v, vg_ref[g:g + 1, :]).astype(bf16)
        w = jnp.where(row >= col, w_ref[g], 0.0).astype(bf16)
        for c in range(tm // A_GROUP):
            rs = slice(c * A_GROUP, (c + 1) * A_GROUP)
            s = jnp.dot(w, va[rs], preferred_element_type=f32) + b_ref[g]
            ya_ref[rs, g * A_GROUP:(g + 1) * A_GROUP] = (u[rs] * s).astype(ya_ref.dtype)


def mixer_a(z, v_gain, w_s, b_s, width, tm):
    m = z.shape[0]
    ng = width // A_GROUP
    b_rows = jnp.broadcast_to(b_s[:, :, None], (ng, A_GROUP, A_GROUP))
    return pl.pallas_call(
        functools.partial(_mixer_a_kernel, width=width),
        grid=(m // tm,),
        in_specs=[pl.BlockSpec((tm, 2 * width), lambda i: (i, 0)),
                  pl.BlockSpec((ng, A_GROUP), lambda i: (0, 0)),
                  pl.BlockSpec((ng, A_GROUP, A_GROUP), lambda i: (0, 0, 0)),
                  pl.BlockSpec((ng, A_GROUP, A_GROUP), lambda i: (0, 0, 0))],
        out_specs=pl.BlockSpec((tm, width), lambda i: (i, 0)),
        out_shape=jax.ShapeDtypeStruct((m, width), bf16),
        compiler_params=_params(("parallel",)),
        name="mixer_a",
    )(z, v_gain, w_s, b_rows)


def _mixer_a_step_kernel(z_ref, vg_ref, w0_ref, b0_ref, ya_ref, va_ref, *, width):
    ge = _gelu(z_ref[...])
    for g in range(width // A_GROUP):
        sl = slice(g * A_GROUP, (g + 1) * A_GROUP)
        u = ge[:, sl]
        v = ge[:, width + g * A_GROUP:width + (g + 1) * A_GROUP]
        va = _rms_rows(v, vg_ref[g:g + 1, :])
        va_ref[:, sl] = va
        ya_ref[:, sl] = (u * (w0_ref[:, sl] * va + b0_ref[:, sl])).astype(ya_ref.dtype)


def mixer_a_step(z, v_gain, w_s, b_s, width):
    m = z.shape[0]
    w0 = jnp.repeat(w_s[:, 0, 0], A_GROUP).reshape(1, width)
    b0 = jnp.repeat(b_s[:, 0], A_GROUP).reshape(1, width)
    return pl.pallas_call(
        functools.partial(_mixer_a_step_kernel, width=width),
        grid=(1,),
        in_specs=[pl.BlockSpec((m, 2 * width), lambda i: (0, 0)),
                  pl.BlockSpec(v_gain.shape, lambda i: (0, 0)),
                  pl.BlockSpec((1, width), lambda i: (0, 0)),
                  pl.BlockSpec((1, width), lambda i: (0, 0))],
        out_specs=[pl.BlockSpec((m, width), lambda i: (0, 0))] * 2,
        out_shape=[jax.ShapeDtypeStruct((m, width), bf16), jax.ShapeDtypeStruct((m, width), f32)],
        compiler_params=_params(("arbitrary",)),
        name="mixer_a_step",
    )(z, v_gain, w0, b0)


def _split3(x):
    hi = x.astype(bf16)
    r1 = x - hi.astype(f32)
    mid = r1.astype(bf16)
    lo = (r1 - mid.astype(f32)).astype(bf16)
    return hi, mid, lo


def _dot01_left(sel, x):
    hi, mid, lo = _split3(x)
    d = lambda p: jnp.dot(sel, p, preferred_element_type=f32)
    return d(hi) + d(mid) + d(lo)


def _dot01_right(x, sel):
    hi, mid, lo = _split3(x)
    d = lambda p: jnp.dot(p, sel, preferred_element_type=f32)
    return d(hi) + d(mid) + d(lo)


def _log_sigmoid(x):
    return jnp.minimum(x, 0.0) - jnp.log(1.0 + jnp.exp(-jnp.abs(x)))


def _fox_in_kernel(x_ref, g_ref, w_ref, wf_ref, bf_ref, qg_ref, kg_ref,
                   q_ref, k_ref, kb_ref, v_ref, vb_ref, sg_ref, lf_ref, *rest,
                   tiles_per_seq, width, n_heads, with_cumsum):
    i = pl.program_id(0)
    h = _rms_rows(x_ref[...], g_ref[...]).astype(bf16)
    z = jnp.dot(h, w_ref[...], preferred_element_type=f32)
    q, k = z[:, :width], z[:, width:2 * width]
    v, g = z[:, 2 * width:3 * width], z[:, 3 * width:4 * width]
    unit = LOG2E if with_cumsum else 1.0
    qn = q * lax.rsqrt(_per_head_sums(q * q) * (1.0 / HEAD) + RMS_EPS) * (qg_ref[...] * (HEAD ** -0.5 * unit))
    kn = k * lax.rsqrt(_per_head_sums(k * k) * (1.0 / HEAD) + RMS_EPS) * kg_ref[...]
    q_ref[...] = qn.astype(bf16)
    k_ref[...] = kn
    kb_ref[...] = kn.astype(bf16)
    v_ref[...] = v
    vb_ref[...] = v.astype(bf16)
    sg_ref[...] = _sigmoid(g)
    lf = _log_sigmoid(jnp.dot(h, wf_ref[...], preferred_element_type=f32) + bf_ref[...])
    lf_ref[...] = lf[:, :n_heads]
    if not with_cumsum:
        return
    c_ref, ct_ref, carry_ref = rest

    @pl.when(i % tiles_per_seq == 0)
    def _():
        carry_ref[...] = jnp.zeros_like(carry_ref)

    tm = lf.shape[0]
    row = lax.broadcasted_iota(jnp.int32, (tm, tm), 0)
    col = lax.broadcasted_iota(jnp.int32, (tm, tm), 1)
    tri = jnp.where(row >= col, 1.0, 0.0).astype(bf16)
    c = _dot01_left(tri, lf) + carry_ref[...]
    carry_ref[...] = c[tm - 1:tm, :]
    c = c * LOG2E
    c_ref[...] = c[:, :n_heads]
    ct = c.T
    for p in range(n_heads // 2):
        ct_ref[0, p] = ct[2 * p:2 * p + 2, :]


def fox_in(x, g, w, wf, b_f, q_gain, k_gain, n_seq, tm, with_cumsum):
    m, d = x.shape
    width = w.shape[1] // 4
    n_heads = width // HEAD
    seq = m // n_seq
    tps = seq // tm
    wf_pad = jnp.zeros((d, LANES), bf16).at[:, :n_heads].set(wf)
    bf_pad = jnp.zeros((1, LANES), f32).at[0, :n_heads].set(b_f)
    qg = jnp.tile(q_gain, n_heads).reshape(1, width)
    kg = jnp.tile(k_gain, n_heads).reshape(1, width)
    row = lambda i: (i, 0)
    const = lambda i: (0, 0)
    tok = lambda n, dt: jax.ShapeDtypeStruct((m, n), dt)
    out_specs = [pl.BlockSpec((tm, width), row)] * 6 + [pl.BlockSpec((tm, n_heads), row)]
    out_shape = [tok(width, bf16), tok(width, f32), tok(width, bf16), tok(width, f32),
                 tok(width, bf16), tok(width, f32), tok(n_heads, f32)]
    scratch = []
    if with_cumsum:
        out_specs += [pl.BlockSpec((tm, n_heads), row),
                      pl.BlockSpec((1, n_heads // 2, 2, tm), lambda i: (i // tps, 0, 0, i % tps))]
        out_shape += [tok(n_heads, f32), jax.ShapeDtypeStruct((n_seq, n_heads // 2, 2, seq), f32)]
        scratch = [pltpu.VMEM((1, LANES), f32)]
    return pl.pallas_call(
        functools.partial(_fox_in_kernel, tiles_per_seq=tps, width=width, n_heads=n_heads,
                          with_cumsum=with_cumsum),
        grid=(m // tm,),
        in_specs=[pl.BlockSpec((tm, d), row), pl.BlockSpec((1, d), const),
                  pl.BlockSpec(w.shape, const), pl.BlockSpec((d, LANES), const),
                  pl.BlockSpec((1, LANES), const), pl.BlockSpec((1, width), const),
                  pl.BlockSpec((1, width), const)],
        out_specs=out_specs,
        out_shape=out_shape,
        scratch_shapes=scratch,
        compiler_params=_params(("arbitrary",)),
        name="fox_in",
    )(x, g.reshape(1, d), w, wf_pad, bf_pad, qg, kg)


def _fox_flash_kernel(q_ref, k_ref, v_ref, c_ref, ct_ref, sg_ref, o_ref, *, tq):
    p = pl.program_id(1)
    qi = pl.program_id(2)
    q2 = q_ref[0]
    lo = _lane_iota((tq, LANES)) < HEAD
    zero = jnp.zeros_like(q2)
    qs = (jnp.where(lo, q2, zero), jnp.where(lo, zero, q2))
    c = c_ref[0]
    hl = _lane_iota(c.shape)
    cq = tuple(jnp.sum(jnp.where(hl == 2 * p + j, c, 0.0), axis=-1, keepdims=True) for j in (0, 1))
    row = lax.broadcasted_iota(jnp.int32, (tq, tq), 0)
    col = lax.broadcasted_iota(jnp.int32, (tq, tq), 1)

    def block(kb, carry, masked):
        m0, l0, m1, l1, acc = carry
        start = pl.multiple_of(kb * tq, tq)
        k2 = k_ref[0, pl.ds(start, tq), :]
        v2 = v_ref[0, pl.ds(start, tq), :]
        new, alphas, pvs = [], [], []
        for j, (mj, lj) in enumerate(((m0, l0), (m1, l1))):
            t = _bdot_nt(qs[j], k2) - ct_ref[0, 0, j:j + 1, pl.ds(start, tq)]
            if masked:
                t = jnp.where(col <= row, t, -jnp.inf)
            m_new = jnp.maximum(mj, jnp.max(t, axis=-1, keepdims=True) + cq[j])
            pe = jnp.exp2(t - (m_new - cq[j]))
            alpha = jnp.exp2(mj - m_new)
            new += [m_new, alpha * lj + jnp.sum(pe, axis=-1, keepdims=True)]
            alphas.append(alpha)
            pvs.append(jnp.dot(pe.astype(bf16), v2, preferred_element_type=f32))
        acc = acc * jnp.where(lo, alphas[0], alphas[1]) + jnp.where(lo, pvs[0], pvs[1])
        return new[0], new[1], new[2], new[3], acc

    neg = jnp.full((tq, 1), -jnp.inf, f32)
    zl = jnp.zeros((tq, 1), f32)
    carry = (neg, zl, neg, zl, jnp.zeros((tq, LANES), f32))
    carry = lax.fori_loop(0, qi, lambda kb, cr: block(kb, cr, False), carry)
    m0, l0, m1, l1, acc = block(qi, carry, True)
    o_ref[0] = (acc / jnp.where(lo, l0, l1) * sg_ref[0]).astype(o_ref.dtype)


def fox_flash(q, kb, vb, c, ct, sg, n_seq, tq):
    m, width = q.shape
    seq = m // n_seq
    n_heads = c.shape[1]
    r3 = lambda a: a.reshape(n_seq, seq, a.shape[1])
    tile = pl.BlockSpec((1, tq, LANES), lambda b, p, i: (b, i, p))
    whole = pl.BlockSpec((1, seq, LANES), lambda b, p, i: (b, 0, p))
    out = pl.pallas_call(
        functools.partial(_fox_flash_kernel, tq=tq),
        grid=(n_seq, width // LANES, seq // tq),
        in_specs=[tile, whole, whole,
                  pl.BlockSpec((1, tq, n_heads), lambda b, p, i: (b, i, 0)),
                  pl.BlockSpec((1, 1, 2, seq), lambda b, p, i: (b, p, 0, 0)),
                  tile],
        out_specs=tile,
        out_shape=jax.ShapeDtypeStruct((n_seq, seq, width), bf16),
        compiler_params=_params(("parallel", "parallel", "arbitrary")),
        name="fox_flash",
    )(r3(q), r3(kb), r3(vb), r3(c), ct, r3(sg))
    return out.reshape(m, width)


def _fox_step_kernel(pt_ref, qc_ref, knt_ref, vnt_ref, lfn_ref, sg_ref, *rest, n_heads, pps):
    kt_refs, vt_refs, lf_refs = rest[:pps], rest[pps:2 * pps], rest[2 * pps:3 * pps]
    o_ref, m_ref, l_ref, acc_ref, carry_ref = rest[3 * pps:]
    pg = pl.program_id(1)
    page = lf_refs[0].shape[2]

    def attend(kt_ref, vt_ref, bias):
        rows = [jnp.sum(kt_ref[0, h] * qc_ref[0, h], axis=0, keepdims=True) for h in range(n_heads)]
        t = jnp.concatenate(rows, axis=0) + bias
        m_old = m_ref[...]
        m_new = jnp.maximum(m_old, jnp.max(t, axis=-1, keepdims=True))
        alpha = jnp.exp(m_old - m_new)
        pe = jnp.exp(t - m_new)
        l_ref[...] = alpha * l_ref[...] + jnp.sum(pe, axis=-1, keepdims=True)
        m_ref[...] = m_new
        for h in range(n_heads):
            acc_ref[h] = acc_ref[h] * alpha[h:h + 1, :] + pe[h:h + 1, :] * vt_ref[0, h]

    @pl.when(pg == 0)
    def _():
        m_ref[...] = jnp.full(m_ref.shape, -jnp.inf, f32)
        l_ref[...] = jnp.zeros_like(l_ref)
        acc_ref[...] = jnp.zeros_like(acc_ref)
        carry_ref[...] = lfn_ref[0]
        attend(knt_ref, vnt_ref, jnp.where(_lane_iota((n_heads, page)) == 0, 0.0, -jnp.inf))

    r = lax.broadcasted_iota(jnp.int32, (page, page), 0)
    cidx = lax.broadcasted_iota(jnp.int32, (page, page), 1)
    later = jnp.where(r > cidx, 1.0, 0.0).astype(bf16)
    for j in range(pps):
        lf = lf_refs[j][0]
        attend(kt_refs[j], vt_refs[j], _dot01_right(lf, later) + carry_ref[...])
        carry_ref[...] = carry_ref[...] + jnp.sum(lf, axis=-1, keepdims=True)

    @pl.when(pg == pl.num_programs(1) - 1)
    def _():
        l = l_ref[...]
        for h in range(n_heads):
            o_ref[0, h] = jnp.sum(acc_ref[h], axis=-1, keepdims=True) / l[h:h + 1, :] * sg_ref[0, h]


FOX_PAGES_PER_STEP = 4


def fox_step(q, k_new, v_new, lf_new, sg, cache_k, cache_v, cache_lf, page_table):
    b, width = q.shape
    pool, page, n_heads = cache_lf.shape
    n_pages = page_table.shape[1]
    pps = FOX_PAGES_PER_STEP
    kt = jnp.transpose(cache_k, (0, 2, 3, 1))
    vt = jnp.transpose(cache_v, (0, 2, 3, 1))
    lft = jnp.transpose(cache_lf, (0, 2, 1))
    col = lambda a: a.reshape(b, n_heads, HEAD, 1)
    own_page = lambda a: jnp.zeros((b, n_heads, HEAD, page), f32).at[..., 0].set(a.reshape(b, n_heads, HEAD))
    qc = jnp.broadcast_to(col(q), (b, n_heads, HEAD, page))
    per_seq = lambda shape: pl.BlockSpec((1,) + shape, lambda i, g, pt: (i,) + (0,) * len(shape))
    page_spec = lambda shape, j: pl.BlockSpec(
        (1,) + shape, lambda i, g, pt: (pt[i, n_pages - 1 - (g * pps + j)],) + (0,) * len(shape))
    grid_spec = pltpu.PrefetchScalarGridSpec(
        num_scalar_prefetch=1,
        grid=(b, n_pages // pps),
        in_specs=[per_seq((n_heads, HEAD, page))] * 3 + [per_seq((n_heads, 1)), per_seq((n_heads, HEAD, 1))]
                 + [page_spec((n_heads, HEAD, page), j) for j in range(pps)] * 2
                 + [page_spec((n_heads, page), j) for j in range(pps)],
        out_specs=per_seq((n_heads, HEAD, 1)),
        scratch_shapes=[pltpu.VMEM((n_heads, 1), f32), pltpu.VMEM((n_heads, 1), f32),
                        pltpu.VMEM((n_heads, HEAD, page), f32), pltpu.VMEM((n_heads, 1), f32)],
    )
    out = pl.pallas_call(
        functools.partial(_fox_step_kernel, n_heads=n_heads, pps=pps),
        grid_spec=grid_spec,
        out_shape=jax.ShapeDtypeStruct((b, n_heads, HEAD, 1), f32),
        compiler_params=_params(("parallel", "arbitrary")),
        name="fox_step",
    )(page_table, qc, own_page(k_new), own_page(v_new), lf_new.reshape(b, n_heads, 1), col(sg),
      *([kt] * pps), *([vt] * pps), *([lft] * pps))
    return out.reshape(b, width).astype(bf16)


def _mem_attn_step_kernel(q_ref, qg_ref, k_ref, v_ref, o_ref):
    q = jnp.broadcast_to(q_ref[0], (8, q_ref.shape[2]))
    qg = qg_ref[...] * (MEM_HEAD ** -0.5)
    for hd in range(q.shape[1] // MEM_HEAD):
        sl = slice(hd * MEM_HEAD, (hd + 1) * MEM_HEAD)
        qn = _rms_rows(q[:, sl], qg)
        s = _bdot_nt(qn, k_ref[0, :, sl])
        p = jnp.exp(s - jnp.max(s, axis=-1, keepdims=True))
        l = jnp.sum(p, axis=-1, keepdims=True)
        o_ref[0, :, sl] = (_bdot(p, v_ref[0, :, sl]) / l)[0:1].astype(o_ref.dtype)


def mem_attn_step(q, q_gain, k, v):
    b, d = q.shape
    n_mem = k.shape[1]
    out = pl.pallas_call(
        _mem_attn_step_kernel,
        grid=(b,),
        in_specs=[pl.BlockSpec((1, 1, d), lambda i: (i, 0, 0)),
                  pl.BlockSpec((1, MEM_HEAD), lambda i: (0, 0)),
                  pl.BlockSpec((1, n_mem, d), lambda i: (i, 0, 0)),
                  pl.BlockSpec((1, n_mem, d), lambda i: (i, 0, 0))],
        out_specs=pl.BlockSpec((1, 1, d), lambda i: (i, 0, 0)),
        out_shape=jax.ShapeDtypeStruct((b, 1, d), bf16),
        compiler_params=_params(("parallel",)),
        name="mem_attn_step",
    )(q.reshape(b, 1, d), q_gain.reshape(1, MEM_HEAD), k, v)
    return out.reshape(b, d)


def _softplus(x):
    return jnp.maximum(x, 0.0) + jnp.log(1.0 + jnp.exp(-jnp.abs(x)))


def _rwkv_kernel(zr_ref, zk_ref, zv_ref, zg_ref, zwa_ref, sh0_ref, s0_ref,
                 mu_ref, w0_ref, a0_ref, kk_ref, ka_ref, rk_ref, lnw_ref, lnb_ref, wup_ref, aup_ref,
                 y_ref, sout_ref, shout_ref,
                 s_ref, carry_ref, r_s, k_s, v_s, al_s, be_s, ld_s, yo_s,
                 *, chunk, n_valid, width):
    i = pl.program_id(1)
    nblk = pl.num_programs(1)
    cg = zr_ref.shape[0]
    n_pairs = width // LANES

    @pl.when(i == 0)
    def _():
        s_ref[...] = s0_ref[0]
        carry_ref[...] = sh0_ref[0]

    rows = lax.broadcasted_iota(jnp.int32, (cg, 1), 0)
    valid = (rows + i * cg) < n_valid

    def shifted(z, lane0, n):
        prev = jnp.where(rows == 0, carry_ref[:, lane0:lane0 + n], pltpu.roll(z, 1, 0))
        return z + (prev - z) * mu_ref[:, lane0:lane0 + n]

    zr, zk, zv, zg, zwa = zr_ref[...], zk_ref[...], zv_ref[...], zg_ref[...], zwa_ref[...]
    r = shifted(zr, 0, width)
    k = shifted(zk, width, width)
    v = shifted(zv, 2 * width, width)
    gp = shifted(zg, 3 * width, width)
    wa = shifted(zwa, 4 * width, LANES)

    last = jnp.minimum(n_valid - 1 - i * cg, cg - 1)
    pieces = ((zr, 0, width), (zk, width, width), (zv, 2 * width, width), (zg, 3 * width, width),
              (zwa, 4 * width, LANES))

    @pl.when(last >= 0)
    def _():
        for z, lane0, n in pieces:
            carry_ref[:, lane0:lane0 + n] = jnp.sum(jnp.where(rows == last, z, 0.0), axis=0, keepdims=True)

    w = -_softplus(-(w0_ref[...] + _bdot(jnp.tanh(wa), wup_ref[...]))) - 0.5
    logd = -jnp.exp(w)
    a = _sigmoid(a0_ref[...] + _bdot(wa, aup_ref[...]))
    kk = k * kk_ref[...]
    kk = kk * lax.rsqrt(_per_head_sums(kk * kk) + L2_EPS)
    k2 = k * (1.0 + (a - 1.0) * ka_ref[...])
    bonus = _per_head_sums(r * k2 * rk_ref[...]) * v
    r_s[...] = r
    k_s[...] = jnp.where(valid, k2, 0.0)
    v_s[...] = jnp.where(valid, v, 0.0)
    al_s[...] = -kk
    be_s[...] = jnp.where(valid, kk * a, 0.0)
    ld_s[...] = jnp.where(valid, logd, 0.0)

    ri = lax.broadcasted_iota(jnp.int32, (chunk, chunk), 0)
    ci = lax.broadcasted_iota(jnp.int32, (chunk, chunk), 1)
    tri = jnp.where(ri >= ci, 1.0, 0.0).astype(bf16)
    lower, strict = ri >= ci, ri > ci
    eye = jnp.where(ri == ci, 1.0, 0.0)
    lo = _lane_iota((1, LANES)) < HEAD
    sr = lax.broadcasted_iota(jnp.int32, (LANES, LANES), 0) < HEAD
    sc = lax.broadcasted_iota(jnp.int32, (LANES, LANES), 1) < HEAD
    same_head = sr == sc
    n_sq = max(int(math.log2(chunk)) - 1, 0)

    def chunk_body(c, _):
        rs = pl.ds(pl.multiple_of(c * chunk, chunk), chunk)
        ld = ld_s[rs, :]
        lcum = _dot01_left(tri, ld)
        lend = lcum[chunk - 1:chunk, :]
        e_in, e_ex, e_neg, e_end = jnp.exp(lcum), jnp.exp(lcum - ld), jnp.exp(-lcum), jnp.exp(lend - lcum)
        xa_all, xr_all = al_s[rs, :] * e_ex, r_s[rs, :] * e_in
        be, kc, vv = be_s[rs, :], k_s[rs, :], v_s[rs, :]
        yb_all, yk_all = be * e_neg, kc * e_neg
        ybe_all, yke_all = be * e_end, kc * e_end
        d_end = jnp.exp(lend)
        pairs = range(n_pairs)
        heads = [(p, j) for p in pairs for j in (0, 1)]
        lsl = [slice(p * LANES, (p + 1) * LANES) for p in pairs]
        s_bd = [s_ref[p] for p in pairs]
        xs = [jnp.concatenate([xa_all[:, lsl[p]], xr_all[:, lsl[p]]], axis=0) for p in pairs]
        vp = [vv[:, lsl[p]].astype(bf16) for p in pairs]
        a_ab, a_ak, a_rb, a_rk = {}, {}, {}, {}
        for p, j in heads:
            xm = jnp.where(lo if j == 0 else jnp.logical_not(lo), xs[p], 0.0).astype(bf16)
            gb = _bdot_nt(xm, yb_all[:, lsl[p]])
            gk = _bdot_nt(xm, yk_all[:, lsl[p]])
            a_ab[p, j] = jnp.where(strict, gb[:chunk], 0.0)
            a_ak[p, j] = jnp.where(strict, gk[:chunk], 0.0)
            a_rb[p, j] = jnp.where(lower, gb[chunk:], 0.0)
            a_rk[p, j] = jnp.where(lower, gk[chunk:], 0.0)
        tinv = {h: eye + a_ab[h] for h in heads}
        pw = dict(a_ab)
        for _ in range(n_sq):
            for h in heads:
                pw[h] = _bdot(pw[h], pw[h])
            for h in heads:
                tinv[h] = tinv[h] + _bdot(tinv[h], pw[h])
        xh = [_bdot_nt(xs[p], s_bd[p]) for p in pairs]
        av = {(p, j): _bdot(a_ak[p, j], vp[p]) for p, j in heads}
        uh = {(p, j): _bdot(tinv[p, j], xh[p][:chunk] + av[p, j]) for p, j in heads}
        u = [jnp.where(lo, uh[p, 0], uh[p, 1]) for p in pairs]
        yh = {(p, j): _bdot(a_rb[p, j], u[p]) + _bdot(a_rk[p, j], vp[p]) for p, j in heads}
        for p in pairs:
            yo_s[rs, lsl[p]] = xh[p][chunk:] + jnp.where(lo, yh[p, 0], yh[p, 1])
            uv_t = jnp.concatenate([u[p], vv[:, lsl[p]]], axis=0).T
            upd = _bdot(uv_t, jnp.concatenate([ybe_all[:, lsl[p]], yke_all[:, lsl[p]]], axis=0))
            s_ref[p] = s_bd[p] * d_end[:, lsl[p]] + jnp.where(same_head, upd, 0.0)
        return 0

    lax.fori_loop(0, cg // chunk, chunk_body, 0)

    y = yo_s[...]
    mean = _per_head_sums(y) * (1.0 / HEAD)
    yc = y - mean
    var = _per_head_sums(yc * yc) * (1.0 / HEAD)
    y = yc * lax.rsqrt(var + GN_EPS) * lnw_ref[...] + lnb_ref[...]
    y_ref[...] = ((y + bonus) * _sigmoid(gp)).astype(y_ref.dtype)

    @pl.when(i == nblk - 1)
    def _():
        sout_ref[0] = s_ref[...]
        shout_ref[0] = carry_ref[...]


def rwkv7(z, col0, shift0, s0, prm, n_seq, n_valid, chunk, block_rows):
    m = z.shape[0]
    rows = m // n_seq
    nblk = rows // block_rows
    width = prm["b_w0"].shape[0]
    n_heads = width // HEAD
    n_pairs = width // LANES
    bcols = 4 * width + LANES
    s0p = s0.reshape(n_seq, n_pairs, 2, HEAD, HEAD)
    zeros = jnp.zeros_like(s0p[:, :, 0])
    s0bd = jnp.concatenate([jnp.concatenate([s0p[:, :, 0], zeros], axis=-1),
                            jnp.concatenate([zeros, s0p[:, :, 1]], axis=-1)], axis=-2)
    lora = lambda w_, off: jnp.zeros((LANES, width), bf16).at[off:off + w_.shape[0]].set(w_.astype(bf16))
    row1 = lambda a: a.reshape(1, -1).astype(f32)
    cb = col0 // width
    zspec = lambda j: pl.BlockSpec((block_rows, width), lambda b, i: (b * nblk + i, cb + j))
    const = lambda shape: pl.BlockSpec(shape, lambda b, i: (0,) * len(shape))
    y, sbd, shift = pl.pallas_call(
        functools.partial(_rwkv_kernel, chunk=chunk, n_valid=n_valid, width=width),
        grid=(n_seq, nblk),
        in_specs=[zspec(0), zspec(1), zspec(2), zspec(3),
                  pl.BlockSpec((block_rows, LANES), lambda b, i: (b * nblk + i, (col0 + 4 * width) // LANES)),
                  pl.BlockSpec((1, 1, bcols), lambda b, i: (b, 0, 0)),
                  pl.BlockSpec((1, n_pairs, LANES, LANES), lambda b, i: (b, 0, 0, 0)),
                  const((1, bcols))] + [const((1, width))] * 7 + [const((LANES, width))] * 2,
        out_specs=[pl.BlockSpec((block_rows, width), lambda b, i: (b * nblk + i, 0)),
                   pl.BlockSpec((1, n_pairs, LANES, LANES), lambda b, i: (b, 0, 0, 0)),
                   pl.BlockSpec((1, 1, bcols), lambda b, i: (b, 0, 0))],
        out_shape=[jax.ShapeDtypeStruct((m, width), bf16),
                   jax.ShapeDtypeStruct((n_seq, n_pairs, LANES, LANES), f32),
                   jax.ShapeDtypeStruct((n_seq, 1, bcols), f32)],
        scratch_shapes=[pltpu.VMEM((n_pairs, LANES, LANES), f32), pltpu.VMEM((1, bcols), f32)]
                       + [pltpu.VMEM((block_rows, width), f32)] * 7,
        compiler_params=_params(("parallel", "arbitrary")),
        name="rwkv7",
    )(z, z, z, z, z, shift0.reshape(n_seq, 1, bcols), s0bd,
      row1(prm["b_mu"]), row1(prm["b_w0"]), row1(prm["b_a0"]), row1(prm["b_k_k"]), row1(prm["b_k_a"]),
      row1(prm["b_r_k"]), row1(prm["b_ln_w"]), row1(prm["b_ln_b"]),
      lora(prm["b_w_up"], 0), lora(prm["b_a_up"], HEAD))
    sp = sbd.reshape(n_seq, n_pairs, 2, HEAD, 2, HEAD)
    s_fin = jnp.stack([sp[:, :, 0, :, 0, :], sp[:, :, 1, :, 1, :]], axis=2).reshape(n_seq, n_heads, HEAD, HEAD)
    return y, s_fin, shift.reshape(n_seq, bcols)


A_WIDTH = 512
STEP_ROWS = 64
RWKV_CHUNK = 64
RWKV_BLOCK = 256
FLASH_TILE = 512


def _trunk(x3, mem_k, mem_v, rwkv_s0, rwkv_shift0, fox_past, p):
    n_seq, seq, d = x3.shape
    m = n_seq * seq
    x = x3.reshape(m, d)
    step = seq == 1
    tm = m if step else 256
    depth = p["norm_mix"].shape[0]
    chunk_v, s_out, shift_out, fk, fv, fl = [], [], [], [], [], []
    ie = io = 0
    for l in range(depth):
        if l % 2 == 0:
            z = norm_matmul(x, p["norm_mix"][l], p["ab_w_in"][ie], tm)
            prm = {k_: p[k_][ie] for k_ in ("b_mu", "b_w0", "b_w_up", "b_a0", "b_a_up", "b_k_k", "b_k_a",
                                           "b_r_k", "b_ln_w", "b_ln_b")}
            if step:
                ya, va = mixer_a_step(z, p["a_v_norm"][ie], p["a_w_s"][ie], p["a_b_s"][ie], A_WIDTH)
                zp = jnp.zeros((n_seq, STEP_ROWS, z.shape[1]), f32).at[:, 0].set(z)
                yb, s_new, sh_new = rwkv7(zp.reshape(n_seq * STEP_ROWS, -1), 2 * A_WIDTH, rwkv_shift0[ie],
                                          rwkv_s0[ie], prm, n_seq, 1, RWKV_CHUNK, STEP_ROWS)
                yb = yb.reshape(n_seq, STEP_ROWS, -1)[:, 0]
                chunk_v.append(va.reshape(n_seq, seq, A_WIDTH))
            else:
                ya = mixer_a(z, p["a_v_norm"][ie], p["a_w_s"][ie], p["a_b_s"][ie], A_WIDTH, tm)
                yb, s_new, sh_new = rwkv7(z, 2 * A_WIDTH, rwkv_shift0[ie], rwkv_s0[ie], prm, n_seq, seq,
                                          RWKV_CHUNK, RWKV_BLOCK)
            w_out = p["ab_w_out"][ie]
            x = res_matmul(x, [(ya, w_out[:A_WIDTH]), (yb, w_out[A_WIDTH:])], tm)
            s_out.append(s_new)
            shift_out.append(sh_new)
            ie += 1
        else:
            w_in = p["c_w_in"][io]
            width = (w_in.shape[1] // 4) // HEAD * HEAD
            outs = fox_in(x, p["norm_mix"][l], w_in[:, :4 * width], w_in[:, 4 * width:], p["c_b_f"][io],
                          p["c_q_norm"][io], p["c_k_norm"][io], 1 if step else n_seq, tm, not step)
            q, k, kb, v, vb, sg, lf = outs[:7]
            if step:
                ck, cv, cl, pt = fox_past
                og = fox_step(q.astype(f32), k, v, lf, sg, ck[io], cv[io], cl[io], pt)
            else:
                og = fox_flash(q, kb, vb, outs[7], outs[8], sg, n_seq, FLASH_TILE)
            x = res_matmul(x, [(og, p["c_w_out"][io])], tm)
            n_heads = width // HEAD
            fk.append(k.reshape(n_seq, seq, n_heads, HEAD))
            fv.append(v.reshape(n_seq, seq, n_heads, HEAD))
            fl.append(lf.reshape(n_seq, seq, n_heads))
            io += 1
        if step:
            q_raw = norm_matmul(x, p["norm_mem"][l], p["m_wq"][l], tm)
            o = mem_attn_step(q_raw, p["m_q_norm"][l], mem_k[l], mem_v[l])
            x = res_matmul(x, [(o, p["m_wo"][l])], tm)
        else:
            x = mem_attn(x, p["norm_mem"][l], p["m_wq"][l], p["m_q_norm"][l], mem_k[l].astype(bf16),
                         mem_v[l].astype(bf16), p["m_wo"][l], seq, tm)
        x = ffn(x, p["norm_ffn"][l], p["f_w_in"][l], p["f_w_out"][l], tm)
    stack = lambda xs: jnp.stack(xs)
    return (x.reshape(n_seq, seq, d), (stack(chunk_v) if chunk_v else None), stack(s_out), stack(shift_out),
            stack(fk), stack(fv), stack(fl))


def kernel(x_prompt, x_sample, mem_prompt, cache_mem_k, cache_mem_v, state_rwkv_S, state_rwkv_shift, cache_fox_k, cache_fox_v, cache_fox_logf, page_table, norm_mix, norm_mem, norm_ffn, ab_w_in, ab_w_out, a_v_norm, a_w_s, a_b_s, b_mu, b_w0, b_w_up, b_a0, b_a_up, b_k_k, b_k_a, b_r_k, b_ln_w, b_ln_b, c_w_in, c_b_f, c_q_norm, c_k_norm, c_w_out, m_mem_norm, m_wq, m_wk, m_wv, m_q_norm, m_k_norm, m_wo, f_w_in, f_w_out):
    cast = lambda w: w.astype(bf16)
    p = dict(norm_mix=norm_mix, norm_mem=norm_mem, norm_ffn=norm_ffn, ab_w_in=cast(ab_w_in),
             ab_w_out=cast(ab_w_out), a_v_norm=a_v_norm, a_w_s=a_w_s, a_b_s=a_b_s, b_mu=b_mu, b_w0=b_w0,
             b_w_up=b_w_up, b_a0=b_a0, b_a_up=b_a_up, b_k_k=b_k_k, b_k_a=b_k_a, b_r_k=b_r_k, b_ln_w=b_ln_w,
             b_ln_b=b_ln_b, c_w_in=cast(c_w_in), c_b_f=c_b_f, c_q_norm=c_q_norm, c_k_norm=c_k_norm,
             c_w_out=cast(c_w_out), m_wq=cast(m_wq), m_q_norm=m_q_norm, m_wo=cast(m_wo),
             f_w_in=cast(f_w_in), f_w_out=cast(f_w_out))
    depth, d = norm_mix.shape
    nb, n_mem = mem_prompt.shape[:2]
    n_even, _, n_heads_b, hd, _ = state_rwkv_S.shape
    mem_heads = d // MEM_HEAD

    mem2 = mem_prompt.reshape(nb * n_mem, d)
    kvs = [mem_kv(mem2, m_mem_norm[l], cast(m_wk[l]), cast(m_wv[l]), m_k_norm[l], 256) for l in range(depth)]
    p_mem_k = jnp.stack([kv[0] for kv in kvs]).reshape(depth, nb, n_mem, d)
    p_mem_v = jnp.stack([kv[1] for kv in kvs]).reshape(depth, nb, n_mem, d)
    s0 = jnp.zeros((n_even, nb, n_heads_b, hd, hd), f32)
    shift0 = jnp.zeros((n_even, nb, state_rwkv_shift.shape[2]), f32)
    y_prompt, _, p_rwkv_s, p_rwkv_shift, p_fox_k, p_fox_v, p_fox_logf = _trunk(
        x_prompt, p_mem_k, p_mem_v, s0, shift0, None, p)

    nd = x_sample.shape[0]
    y_sample, s_chunk_v, s_rwkv_s, s_rwkv_shift, s_fox_k, s_fox_v, s_fox_logf = _trunk(
        x_sample, cache_mem_k.reshape(depth, nd, n_mem, d), cache_mem_v.reshape(depth, nd, n_mem, d),
        state_rwkv_S, state_rwkv_shift, (cache_fox_k, cache_fox_v, cache_fox_logf, page_table), p)
    heads5 = lambda a: a.reshape(depth, nb, n_mem, mem_heads, MEM_HEAD)
    return (y_prompt, y_sample, heads5(p_mem_k), heads5(p_mem_v), p_rwkv_s, p_rwkv_shift, p_fox_k, p_fox_v,
            p_fox_logf, s_chunk_v, s_rwkv_s, s_rwkv_shift, s_fox_k, s_fox_v, s_fox_logf)
```

```python
import functools
import math

import jax
import jax.numpy as jnp
from jax import lax
from jax.experimental import pallas as pl
from jax.experimental.pallas import tpu as pltpu

f32 = jnp.float32
bf16 = jnp.bfloat16

RMS_EPS = 1e-6
GN_EPS = 64e-5
L2_EPS = 1e-12
LOG2E = 1.4426950408889634
LANES = 128
HEAD = 64
A_GROUP = 128
MEM_HEAD = 256
VMEM_LIMIT = 56 * 1024 * 1024


def _params(sem):
    return pltpu.CompilerParams(dimension_semantics=sem, vmem_limit_bytes=VMEM_LIMIT)


def _bdot(a, b):
    return jnp.dot(a.astype(bf16), b.astype(bf16), preferred_element_type=f32)


def _bdot_nt(a, b):
    return lax.dot_general(a.astype(bf16), b.astype(bf16), (((1,), (1,)), ((), ())),
                           preferred_element_type=f32)


def _rms_rows(x, g):
    return x * lax.rsqrt(jnp.mean(x * x, axis=-1, keepdims=True) + RMS_EPS) * g


def _sigmoid(x):
    return 1.0 / (1.0 + jnp.exp(-x))


def _lane_iota(shape):
    return lax.broadcasted_iota(jnp.int32, shape, len(shape) - 1)


def _half_sums(x):
    lo = _lane_iota(x.shape) < HEAD
    s_lo = jnp.sum(jnp.where(lo, x, 0.0), axis=-1, keepdims=True)
    s_hi = jnp.sum(jnp.where(lo, 0.0, x), axis=-1, keepdims=True)
    return jnp.where(lo, s_lo, s_hi)


def _per_head_sums(x):
    n = x.shape[-1] // LANES
    return jnp.concatenate([_half_sums(x[:, i * LANES:(i + 1) * LANES]) for i in range(n)], axis=-1)


def _norm_matmul_kernel(x_ref, g_ref, w_ref, o_ref):
    h = _rms_rows(x_ref[...], g_ref[...])
    o_ref[...] = _bdot(h, w_ref[...])


def norm_matmul(x, g, w, tm):
    m, d = x.shape
    n = w.shape[1]
    return pl.pallas_call(
        _norm_matmul_kernel,
        grid=(m // tm,),
        in_specs=[pl.BlockSpec((tm, d), lambda i: (i, 0)),
                  pl.BlockSpec((1, d), lambda i: (0, 0)),
                  pl.BlockSpec((d, n), lambda i: (0, 0))],
        out_specs=pl.BlockSpec((tm, n), lambda i: (i, 0)),
        out_shape=jax.ShapeDtypeStruct((m, n), f32),
        compiler_params=_params(("parallel",)),
        name="norm_matmul",
    )(x, g.reshape(1, d), w)


def _res_matmul_kernel(*refs):
    x_ref, o_ref = refs[0], refs[-1]
    acc = x_ref[...]
    for a_ref, w_ref in zip(refs[1:-1:2], refs[2:-1:2]):
        acc = acc + _bdot(a_ref[...], w_ref[...])
    o_ref[...] = acc


def res_matmul(x, pairs, tm):
    m, d = x.shape
    in_specs = [pl.BlockSpec((tm, d), lambda i: (i, 0))]
    args = [x]
    for a, w in pairs:
        in_specs += [pl.BlockSpec((tm, a.shape[1]), lambda i: (i, 0)),
                     pl.BlockSpec(w.shape, lambda i: (0, 0))]
        args += [a, w]
    return pl.pallas_call(
        _res_matmul_kernel,
        grid=(m // tm,),
        in_specs=in_specs,
        out_specs=pl.BlockSpec((tm, d), lambda i: (i, 0)),
        out_shape=jax.ShapeDtypeStruct((m, d), f32),
        compiler_params=_params(("parallel",)),
        name="res_matmul",
    )(*args)


def _ffn_kernel(x_ref, g_ref, wi_ref, wo_ref, o_ref, *, d_ff, tf):
    x = x_ref[...]
    h = _rms_rows(x, g_ref[...]).astype(bf16)
    acc = x
    for c in range(d_ff // tf):
        gate = jnp.dot(h, wi_ref[:, c * tf:(c + 1) * tf], preferred_element_type=f32)
        up = jnp.dot(h, wi_ref[:, d_ff + c * tf:d_ff + (c + 1) * tf], preferred_element_type=f32)
        act = gate * _sigmoid(gate) * up
        acc = acc + _bdot(act, wo_ref[c * tf:(c + 1) * tf, :])
    o_ref[...] = acc


def ffn(x, g, w_in, w_out, tm):
    m, d = x.shape
    d_ff = w_out.shape[0]
    tf = 256
    return pl.pallas_call(
        functools.partial(_ffn_kernel, d_ff=d_ff, tf=tf),
        grid=(m // tm,),
        in_specs=[pl.BlockSpec((tm, d), lambda i: (i, 0)),
                  pl.BlockSpec((1, d), lambda i: (0, 0)),
                  pl.BlockSpec(w_in.shape, lambda i: (0, 0)),
                  pl.BlockSpec(w_out.shape, lambda i: (0, 0))],
        out_specs=pl.BlockSpec((tm, d), lambda i: (i, 0)),
        out_shape=jax.ShapeDtypeStruct((m, d), f32),
        compiler_params=_params(("parallel",)),
        name="ffn",
    )(x, g.reshape(1, d), w_in, w_out)


def _mem_kv_kernel(m_ref, g_ref, wk_ref, wv_ref, kg_ref, k_ref, v_ref):
    h = _rms_rows(m_ref[...], g_ref[...]).astype(bf16)
    k = jnp.dot(h, wk_ref[...], preferred_element_type=f32)
    kg = kg_ref[...]
    for hd in range(k.shape[1] // MEM_HEAD):
        sl = slice(hd * MEM_HEAD, (hd + 1) * MEM_HEAD)
        k_ref[:, sl] = _rms_rows(k[:, sl], kg)
    v_ref[...] = jnp.dot(h, wv_ref[...], preferred_element_type=f32)


def mem_kv(mem, g, wk, wv, k_gain, tm):
    m, d = mem.shape
    return pl.pallas_call(
        _mem_kv_kernel,
        grid=(m // tm,),
        in_specs=[pl.BlockSpec((tm, d), lambda i: (i, 0)),
                  pl.BlockSpec((1, d), lambda i: (0, 0)),
                  pl.BlockSpec((d, d), lambda i: (0, 0)),
                  pl.BlockSpec((d, d), lambda i: (0, 0)),
                  pl.BlockSpec((1, MEM_HEAD), lambda i: (0, 0))],
        out_specs=[pl.BlockSpec((tm, d), lambda i: (i, 0))] * 2,
        out_shape=[jax.ShapeDtypeStruct((m, d), f32)] * 2,
        compiler_params=_params(("parallel",)),
        name="mem_kv",
    )(mem, g.reshape(1, d), wk, wv, k_gain.reshape(1, MEM_HEAD))


def _mem_attn_kernel(x_ref, g_ref, wq_ref, qg_ref, k_ref, v_ref, wo_ref, o_ref):
    x = x_ref[...]
    h = _rms_rows(x, g_ref[...])
    q = _bdot(h, wq_ref[...])
    qg = qg_ref[...] * (MEM_HEAD ** -0.5)
    outs = []
    for hd in range(q.shape[1] // MEM_HEAD):
        sl = slice(hd * MEM_HEAD, (hd + 1) * MEM_HEAD)
        qn = _rms_rows(q[:, sl], qg)
        s = _bdot_nt(qn, k_ref[0, :, sl])
        p = jnp.exp(s - jnp.max(s, axis=-1, keepdims=True))
        l = jnp.sum(p, axis=-1, keepdims=True)
        outs.append(_bdot(p, v_ref[0, :, sl]) / l)
    o = jnp.concatenate(outs, axis=-1)
    o_ref[...] = x + _bdot(o, wo_ref[...])


def mem_attn(x, g, wq, q_gain, k, v, wo, rows_per_seq, tm):
    m, d = x.shape
    n_mem = k.shape[1]
    per = rows_per_seq // tm
    return pl.pallas_call(
        _mem_attn_kernel,
        grid=(m // tm,),
        in_specs=[pl.BlockSpec((tm, d), lambda i: (i, 0)),
                  pl.BlockSpec((1, d), lambda i: (0, 0)),
                  pl.BlockSpec((d, d), lambda i: (0, 0)),
                  pl.BlockSpec((1, MEM_HEAD), lambda i: (0, 0)),
                  pl.BlockSpec((1, n_mem, d), lambda i: (i // per, 0, 0)),
                  pl.BlockSpec((1, n_mem, d), lambda i: (i // per, 0, 0)),
                  pl.BlockSpec((d, d), lambda i: (0, 0))],
        out_specs=pl.BlockSpec((tm, d), lambda i: (i, 0)),
        out_shape=jax.ShapeDtypeStruct((m, d), f32),
        compiler_params=_params(("parallel",)),
        name="mem_attn",
    )(x, g.reshape(1, d), wq, q_gain.reshape(1, MEM_HEAD), k, v, wo)


def _gelu(x):
    return 0.5 * x * (1.0 + jnp.tanh(0.7978845608028654 * (x + 0.044715 * x * x * x)))


def _mixer_a_kernel(z_ref, vg_ref, w_ref, b_ref, ya_ref, *, width):
    ge = _gelu(z_ref[...])
    tm = ge.shape[0]
    row = lax.broadcasted_iota(jnp.int32, (A_GROUP, A_GROUP), 0)
    col = lax.broadcasted_iota(jnp.int32, (A_GROUP, A_GROUP), 1)
    for g in range(width // A_GROUP):
        u = ge[:, g * A_GROUP:(g + 1) * A_GROUP]
        v = ge[:, width + g * A_GROUP:width + (g + 1) * A_GROUP]
        va = _rms_rows(v, vg_ref[g:g + 1, :]).astype(bf16)
        w = jnp.where(row >= col, w_ref[g], 0.0).astype(bf16)
        for c in range(tm // A_GROUP):
            rs = slice(c * A_GROUP, (c + 1) * A_GROUP)
            s = jnp.dot(w, va[rs], preferred_element_type=f32) + b_ref[g]
            ya_ref[rs, g * A_GROUP:(g + 1) * A_GROUP] = (u[rs] * s).astype(ya_ref.dtype)


def mixer_a(z, v_gain, w_s, b_s, width, tm):
    m = z.shape[0]
    ng = width // A_GROUP
    b_rows = jnp.broadcast_to(b_s[:, :, None], (ng, A_GROUP, A_GROUP))
    return pl.pallas_call(
        functools.partial(_mixer_a_kernel, width=width),
        grid=(m // tm,),
        in_specs=[pl.BlockSpec((tm, 2 * width), lambda i: (i, 0)),
                  pl.BlockSpec((ng, A_GROUP), lambda i: (0, 0)),
                  pl.BlockSpec((ng, A_GROUP, A_GROUP), lambda i: (0, 0, 0)),
                  pl.BlockSpec((ng, A_GROUP, A_GROUP), lambda i: (0, 0, 0))],
        out_specs=pl.BlockSpec((tm, width), lambda i: (i, 0)),
        out_shape=jax.ShapeDtypeStruct((m, width), bf16),
        compiler_params=_params(("parallel",)),
        name="mixer_a",
    )(z, v_gain, w_s, b_rows)


def _mixer_a_step_kernel(z_ref, vg_ref, w0_ref, b0_ref, ya_ref, va_ref, *, width):
    ge = _gelu(z_ref[...])
    for g in range(width // A_GROUP):
        sl = slice(g * A_GROUP, (g + 1) * A_GROUP)
        u = ge[:, sl]
        v = ge[:, width + g * A_GROUP:width + (g + 1) * A_GROUP]
        va = _rms_rows(v, vg_ref[g:g + 1, :])
        va_ref[:, sl] = va
        ya_ref[:, sl] = (u * (w0_ref[:, sl] * va + b0_ref[:, sl])).astype(ya_ref.dtype)


def mixer_a_step(z, v_gain, w_s, b_s, width):
    m = z.shape[0]
    w0 = jnp.repeat(w_s[:, 0, 0], A_GROUP).reshape(1, width)
    b0 = jnp.repeat(b_s[:, 0], A_GROUP).reshape(1, width)
    return pl.pallas_call(
        functools.partial(_mixer_a_step_kernel, width=width),
        grid=(1,),
        in_specs=[pl.BlockSpec((m, 2 * width), lambda i: (0, 0)),
                  pl.BlockSpec(v_gain.shape, lambda i: (0, 0)),
                  pl.BlockSpec((1, width), lambda i: (0, 0)),
                  pl.BlockSpec((1, width), lambda i: (0, 0))],
        out_specs=[pl.BlockSpec((m, width), lambda i: (0, 0))] * 2,
        out_shape=[jax.ShapeDtypeStruct((m, width), bf16), jax.ShapeDtypeStruct((m, width), f32)],
        compiler_params=_params(("arbitrary",)),
        name="mixer_a_step",
    )(z, v_gain, w0, b0)


def _split3(x):
    hi = x.astype(bf16)
    r1 = x - hi.astype(f32)
    mid = r1.astype(bf16)
    lo = (r1 - mid.astype(f32)).astype(bf16)
    return hi, mid, lo


def _dot01_left(sel, x):
    hi, mid, lo = _split3(x)
    d = lambda p: jnp.dot(sel, p, preferred_element_type=f32)
    return d(hi) + d(mid) + d(lo)


def _dot01_right(x, sel):
    hi, mid, lo = _split3(x)
    d = lambda p: jnp.dot(p, sel, preferred_element_type=f32)
    return d(hi) + d(mid) + d(lo)


def _log_sigmoid(x):
    return jnp.minimum(x, 0.0) - jnp.log(1.0 + jnp.exp(-jnp.abs(x)))


AUG = 2 * LANES


def _fox_project(x_ref, g_ref, w_ref, wf_ref, bf_ref, qg_ref, kg_ref, width, unit):
    h = _rms_rows(x_ref[...], g_ref[...]).astype(bf16)
    z = jnp.dot(h, w_ref[...], preferred_element_type=f32)
    q, k = z[:, :width], z[:, width:2 * width]
    v, g = z[:, 2 * width:3 * width], z[:, 3 * width:4 * width]
    qn = q * lax.rsqrt(_per_head_sums(q * q) * (1.0 / HEAD) + RMS_EPS) * (qg_ref[...] * (HEAD ** -0.5 * unit))
    kn = k * lax.rsqrt(_per_head_sums(k * k) * (1.0 / HEAD) + RMS_EPS) * kg_ref[...]
    lf = _log_sigmoid(jnp.dot(h, wf_ref[...], preferred_element_type=f32) + bf_ref[...])
    return qn, kn, v, g, lf


def _fox_in_args(x, g, w, wf, b_f, q_gain, k_gain):
    d = x.shape[1]
    width = w.shape[1] // 4
    n_heads = width // HEAD
    wf_pad = jnp.zeros((d, LANES), bf16).at[:, :n_heads].set(wf)
    bf_pad = jnp.zeros((1, LANES), f32).at[0, :n_heads].set(b_f)
    qg = jnp.tile(q_gain, n_heads).reshape(1, width)
    kg = jnp.tile(k_gain, n_heads).reshape(1, width)
    const = lambda i: (0, 0)
    specs = [pl.BlockSpec((1, d), const), pl.BlockSpec(w.shape, const), pl.BlockSpec((d, LANES), const),
             pl.BlockSpec((1, LANES), const), pl.BlockSpec((1, width), const), pl.BlockSpec((1, width), const)]
    return specs, (g.reshape(1, d), w, wf_pad, bf_pad, qg, kg)


def _fox_step_in_kernel(x_ref, g_ref, w_ref, wf_ref, bf_ref, qg_ref, kg_ref,
                        q_ref, k_ref, v_ref, sg_ref, lf_ref, *, width, n_heads):
    qn, kn, v, g, lf = _fox_project(x_ref, g_ref, w_ref, wf_ref, bf_ref, qg_ref, kg_ref, width, 1.0)
    q_ref[...] = qn.astype(bf16).astype(f32)
    k_ref[...] = kn
    v_ref[...] = v
    sg_ref[...] = _sigmoid(g)
    lf_ref[...] = lf[:, :n_heads]


def fox_step_in(x, g, w, wf, b_f, q_gain, k_gain):
    m, d = x.shape
    width = w.shape[1] // 4
    n_heads = width // HEAD
    specs, args = _fox_in_args(x, g, w, wf, b_f, q_gain, k_gain)
    whole = lambda n: pl.BlockSpec((m, n), lambda i: (0, 0))
    return pl.pallas_call(
        functools.partial(_fox_step_in_kernel, width=width, n_heads=n_heads),
        grid=(1,),
        in_specs=[whole(d)] + specs,
        out_specs=[whole(width)] * 4 + [whole(n_heads)],
        out_shape=[jax.ShapeDtypeStruct((m, width), f32)] * 4 + [jax.ShapeDtypeStruct((m, n_heads), f32)],
        compiler_params=_params(("arbitrary",)),
        name="fox_step_in",
    )(x, *args)


def _fox_seq_in_kernel(x_ref, g_ref, w_ref, wf_ref, bf_ref, qg_ref, kg_ref,
                       q_ref, kt_ref, ka_ref, vt_ref, vtb_ref, sg_ref, lft_ref, c_ref, carry_ref,
                       *, tiles_per_seq, width, n_heads):
    i = pl.program_id(0)
    qn, kn, v, g, lf = _fox_project(x_ref, g_ref, w_ref, wf_ref, bf_ref, qg_ref, kg_ref, width, LOG2E)
    tm = lf.shape[0]
    q_ref[...] = qn.astype(bf16)
    sg_ref[...] = _sigmoid(g)
    for blk in range(width // LANES):
        sl = slice(blk * LANES, (blk + 1) * LANES)
        kt_ref[0, sl, :] = kn[:, sl].T
        vt = v[:, sl].T
        vt_ref[0, sl, :] = vt
        vtb_ref[0, 0, sl, :] = vt.astype(bf16)
    lft_ref[0] = lf.T[:n_heads, :]

    @pl.when(i % tiles_per_seq == 0)
    def _():
        carry_ref[...] = jnp.zeros_like(carry_ref)

    row = lax.broadcasted_iota(jnp.int32, (tm, tm), 0)
    col = lax.broadcasted_iota(jnp.int32, (tm, tm), 1)
    tri = jnp.where(row >= col, 1.0, 0.0).astype(bf16)
    c = _dot01_left(tri, lf) + carry_ref[...]
    carry_ref[...] = c[tm - 1:tm, :]
    c = c * LOG2E
    c_ref[...] = c
    pieces = jnp.concatenate(_split3(c), axis=-1)
    rr = lax.broadcasted_iota(jnp.int32, (3 * LANES, LANES), 0)
    cc = lax.broadcasted_iota(jnp.int32, (3 * LANES, LANES), 1)
    piece, head = rr // LANES, rr % LANES
    ones = jnp.where(_lane_iota((1, LANES)) < 3, 1.0, 0.0)
    knb = kn.astype(bf16)
    for p in range(n_heads // 2):
        pick = ((head == 2 * p) & (cc == 3 + piece)) | ((head == 2 * p + 1) & (cc == 6 + piece))
        aug = jnp.dot(pieces, jnp.where(pick, 1.0, 0.0).astype(bf16), preferred_element_type=f32) + ones
        ka_ref[p] = jnp.concatenate([knb[:, p * LANES:(p + 1) * LANES], aug.astype(bf16)], axis=-1)


def fox_seq_in(x, g, w, wf, b_f, q_gain, k_gain, n_seq, tm):
    m, d = x.shape
    width = w.shape[1] // 4
    n_heads = width // HEAD
    n_pairs = n_heads // 2
    seq = m // n_seq
    tps = seq // tm
    specs, args = _fox_in_args(x, g, w, wf, b_f, q_gain, k_gain)
    row = lambda i: (i, 0)
    chan = lambda i: (i // tps, 0, i % tps)
    return pl.pallas_call(
        functools.partial(_fox_seq_in_kernel, tiles_per_seq=tps, width=width, n_heads=n_heads),
        grid=(m // tm,),
        in_specs=[pl.BlockSpec((tm, d), row)] + specs,
        out_specs=[pl.BlockSpec((tm, width), row),
                   pl.BlockSpec((1, width, tm), chan),
                   pl.BlockSpec((n_pairs, tm, AUG), lambda i: (0, i, 0)),
                   pl.BlockSpec((1, width, tm), chan),
                   pl.BlockSpec((1, 1, width, tm), lambda i: (i // tps, i % tps, 0, 0)),
                   pl.BlockSpec((tm, width), row),
                   pl.BlockSpec((1, n_heads, tm), chan),
                   pl.BlockSpec((tm, LANES), row)],
        out_shape=[jax.ShapeDtypeStruct((m, width), bf16),
                   jax.ShapeDtypeStruct((n_seq, width, seq), f32),
                   jax.ShapeDtypeStruct((n_pairs, m, AUG), bf16),
                   jax.ShapeDtypeStruct((n_seq, width, seq), f32),
                   jax.ShapeDtypeStruct((n_seq, tps, width, tm), bf16),
                   jax.ShapeDtypeStruct((m, width), f32),
                   jax.ShapeDtypeStruct((n_seq, n_heads, seq), f32),
                   jax.ShapeDtypeStruct((m, LANES), f32)],
        scratch_shapes=[pltpu.VMEM((1, LANES), f32)],
        compiler_params=_params(("arbitrary",)),
        name="fox_seq_in",
    )(x, *args)


def _fox_flash_kernel(lo_ref, q_ref, c_ref, ka_ref, vt_ref, sg_ref, o_ref, *, tq, tv):
    b, p, qi = pl.program_id(0), pl.program_id(1), pl.program_id(2)
    nsub = tq // tv
    q2 = q_ref[0]
    lo = _lane_iota((tq, LANES)) < HEAD
    zero = jnp.zeros_like(q2)
    c_hi, c_mid, c_lo = _split3(c_ref[0])
    rr = lax.broadcasted_iota(jnp.int32, (LANES, LANES), 0)
    cc = lax.broadcasted_iota(jnp.int32, (LANES, LANES), 1)
    lane = _lane_iota((1, LANES))
    qa = []
    for j in (0, 1):
        pick = lambda s: jnp.where((rr == 2 * p + j) & (cc == s), 1.0, 0.0).astype(bf16)
        bias = sum(jnp.dot(piece, pick(s), preferred_element_type=f32)
                   for s, piece in enumerate((c_hi, c_mid, c_lo)))
        bias = bias + jnp.where((lane >= 3 + 3 * j) & (lane < 6 + 3 * j), -1.0, 0.0)
        qh = jnp.where(lo, q2, zero) if j == 0 else jnp.where(lo, zero, q2)
        qa.append(jnp.concatenate([qh, bias.astype(bf16)], axis=-1))
    row = lax.broadcasted_iota(jnp.int32, (tq, tq), 0)
    col = lax.broadcasted_iota(jnp.int32, (tq, tq), 1)

    def scores(kb, j):
        ka = ka_ref[0, 0, pl.ds(pl.multiple_of(kb * tq, tq), tq), :]
        return _bdot_nt(ka, qa[j])

    def softmax_pv(kb, j, t, stats, masked):
        m_old, l_old, acc = stats
        if masked:
            t = jnp.where(row <= col, t, -jnp.inf)
        m_new = jnp.maximum(m_old, jnp.max(t, axis=0, keepdims=True))
        pe = jnp.exp2(t - m_new).astype(bf16)
        alpha = jnp.exp2(m_old - m_new)
        l_new = alpha * l_old + jnp.sum(pe.astype(f32), axis=0, keepdims=True)
        pv = sum(jnp.dot(vt_ref[0, kb * nsub + s, j * HEAD:(j + 1) * HEAD, :], pe[s * tv:(s + 1) * tv],
                         preferred_element_type=f32) for s in range(nsub))
        return m_new, l_new, acc * alpha + pv

    def block(kb, t0, st0, st1, masked, last):
        t1 = scores(kb, 1)
        st0 = softmax_pv(kb, 0, t0, st0, masked)
        t0_next = t0 if last else scores(kb + 1, 0)
        st1 = softmax_pv(kb, 1, t1, st1, masked)
        return t0_next, st0, st1

    init = (jnp.full((1, tq), -jnp.inf, f32), jnp.zeros((1, tq), f32), jnp.zeros((HEAD, tq), f32))
    first = lo_ref[b, p, qi]
    state = lax.fori_loop(first, qi, lambda kb, s: block(kb, *s, False, False),
                          (scores(first, 0), init, init))
    _, (m0, l0, a0), (m1, l1, a1) = block(qi, *state, True, True)
    o = jnp.concatenate([a0 / l0, a1 / l1], axis=0).T
    o_ref[0] = (o * sg_ref[0]).astype(o_ref.dtype)


def fox_flash(q, c, ka, vtb, sg, first_block, n_seq, tq):
    m, width = q.shape
    seq = m // n_seq
    n_pairs = width // LANES
    tv = vtb.shape[3]
    r3 = lambda a: a.reshape(n_seq, seq, a.shape[1])
    tile = pl.BlockSpec((1, tq, LANES), lambda b, p, i, lo: (b, i, p))
    grid_spec = pltpu.PrefetchScalarGridSpec(
        num_scalar_prefetch=1,
        grid=(n_seq, n_pairs, seq // tq),
        in_specs=[tile,
                  pl.BlockSpec((1, tq, LANES), lambda b, p, i, lo: (b, i, 0)),
                  pl.BlockSpec((1, 1, seq, AUG), lambda b, p, i, lo: (p, b, 0, 0)),
                  pl.BlockSpec((1, seq // tv, LANES, tv), lambda b, p, i, lo: (b, 0, p, 0)),
                  tile],
        out_specs=tile,
    )
    out = pl.pallas_call(
        functools.partial(_fox_flash_kernel, tq=tq, tv=tv),
        grid_spec=grid_spec,
        out_shape=jax.ShapeDtypeStruct((n_seq, seq, width), bf16),
        compiler_params=_params(("parallel", "parallel", "arbitrary")),
        name="fox_flash",
    )(first_block, r3(q), r3(c), ka.reshape(n_pairs, n_seq, seq, AUG), vtb, r3(sg))
    return out.reshape(m, width)


def _fox_step_kernel(pt_ref, qc_ref, knt_ref, vnt_ref, lfn_ref, sg_ref, *rest, n_heads, pps):
    kt_refs, vt_refs, lf_refs = rest[:pps], rest[pps:2 * pps], rest[2 * pps:3 * pps]
    o_ref, m_ref, l_ref, acc_ref, carry_ref = rest[3 * pps:]
    pg = pl.program_id(1)
    page = lf_refs[0].shape[2]

    def attend(kt_ref, vt_ref, bias):
        rows = [jnp.sum(kt_ref[0, h] * qc_ref[0, h], axis=0, keepdims=True) for h in range(n_heads)]
        t = jnp.concatenate(rows, axis=0) + bias
        m_old = m_ref[...]
        m_new = jnp.maximum(m_old, jnp.max(t, axis=-1, keepdims=True))
        alpha = jnp.exp(m_old - m_new)
        pe = jnp.exp(t - m_new)
        l_ref[...] = alpha * l_ref[...] + jnp.sum(pe, axis=-1, keepdims=True)
        m_ref[...] = m_new
        for h in range(n_heads):
            acc_ref[h] = acc_ref[h] * alpha[h:h + 1, :] + pe[h:h + 1, :] * vt_ref[0, h]

    @pl.when(pg == 0)
    def _():
        m_ref[...] = jnp.full(m_ref.shape, -jnp.inf, f32)
        l_ref[...] = jnp.zeros_like(l_ref)
        acc_ref[...] = jnp.zeros_like(acc_ref)
        carry_ref[...] = lfn_ref[0]
        attend(knt_ref, vnt_ref, jnp.where(_lane_iota((n_heads, page)) == 0, 0.0, -jnp.inf))

    r = lax.broadcasted_iota(jnp.int32, (page, page), 0)
    cidx = lax.broadcasted_iota(jnp.int32, (page, page), 1)
    later = jnp.where(r > cidx, 1.0, 0.0).astype(bf16)
    for j in range(pps):
        lf = lf_refs[j][0]
        attend(kt_refs[j], vt_refs[j], _dot01_right(lf, later) + carry_ref[...])
        carry_ref[...] = carry_ref[...] + jnp.sum(lf, axis=-1, keepdims=True)

    @pl.when(pg == pl.num_programs(1) - 1)
    def _():
        l = l_ref[...]
        for h in range(n_heads):
            o_ref[0, h] = jnp.sum(acc_ref[h], axis=-1, keepdims=True) / l[h:h + 1, :] * sg_ref[0, h]


FOX_PAGES_PER_STEP = 8


def fox_step(q, k_new, v_new, lf_new, sg, cache_k, cache_v, cache_lf, page_table):
    b, width = q.shape
    pool, page, n_heads = cache_lf.shape
    n_pages = page_table.shape[1]
    pps = FOX_PAGES_PER_STEP
    kt = jnp.transpose(cache_k, (0, 2, 3, 1))
    vt = jnp.transpose(cache_v, (0, 2, 3, 1))
    lft = jnp.transpose(cache_lf, (0, 2, 1))
    col = lambda a: a.reshape(b, n_heads, HEAD, 1)
    own_page = lambda a: jnp.zeros((b, n_heads, HEAD, page), f32).at[..., 0].set(a.reshape(b, n_heads, HEAD))
    qc = jnp.broadcast_to(col(q), (b, n_heads, HEAD, page))
    per_seq = lambda shape: pl.BlockSpec((1,) + shape, lambda i, g, pt: (i,) + (0,) * len(shape))
    page_spec = lambda shape, j: pl.BlockSpec(
        (1,) + shape, lambda i, g, pt: (pt[i, n_pages - 1 - (g * pps + j)],) + (0,) * len(shape))
    grid_spec = pltpu.PrefetchScalarGridSpec(
        num_scalar_prefetch=1,
        grid=(b, n_pages // pps),
        in_specs=[per_seq((n_heads, HEAD, page))] * 3 + [per_seq((n_heads, 1)), per_seq((n_heads, HEAD, 1))]
                 + [page_spec((n_heads, HEAD, page), j) for j in range(pps)] * 2
                 + [page_spec((n_heads, page), j) for j in range(pps)],
        out_specs=per_seq((n_heads, HEAD, 1)),
        scratch_shapes=[pltpu.VMEM((n_heads, 1), f32), pltpu.VMEM((n_heads, 1), f32),
                        pltpu.VMEM((n_heads, HEAD, page), f32), pltpu.VMEM((n_heads, 1), f32)],
    )
    out = pl.pallas_call(
        functools.partial(_fox_step_kernel, n_heads=n_heads, pps=pps),
        grid_spec=grid_spec,
        out_shape=jax.ShapeDtypeStruct((b, n_heads, HEAD, 1), f32),
        compiler_params=_params(("parallel", "arbitrary")),
        name="fox_step",
    )(page_table, qc, own_page(k_new), own_page(v_new), lf_new.reshape(b, n_heads, 1), col(sg),
      *([kt] * pps), *([vt] * pps), *([lft] * pps))
    return out.reshape(b, width).astype(bf16)


def _mem_attn_step_kernel(q_ref, qg_ref, k_ref, v_ref, o_ref):
    q = jnp.broadcast_to(q_ref[0], (8, q_ref.shape[2]))
    qg = qg_ref[...] * (MEM_HEAD ** -0.5)
    for hd in range(q.shape[1] // MEM_HEAD):
        sl = slice(hd * MEM_HEAD, (hd + 1) * MEM_HEAD)
        qn = _rms_rows(q[:, sl], qg)
        s = _bdot_nt(qn, k_ref[0, :, sl])
        p = jnp.exp(s - jnp.max(s, axis=-1, keepdims=True))
        l = jnp.sum(p, axis=-1, keepdims=True)
        o_ref[0, :, sl] = (_bdot(p, v_ref[0, :, sl]) / l)[0:1].astype(o_ref.dtype)


def mem_attn_step(q, q_gain, k, v):
    b, d = q.shape
    n_mem = k.shape[1]
    out = pl.pallas_call(
        _mem_attn_step_kernel,
        grid=(b,),
        in_specs=[pl.BlockSpec((1, 1, d), lambda i: (i, 0, 0)),
                  pl.BlockSpec((1, MEM_HEAD), lambda i: (0, 0)),
                  pl.BlockSpec((1, n_mem, d), lambda i: (i, 0, 0)),
                  pl.BlockSpec((1, n_mem, d), lambda i: (i, 0, 0))],
        out_specs=pl.BlockSpec((1, 1, d), lambda i: (i, 0, 0)),
        out_shape=jax.ShapeDtypeStruct((b, 1, d), bf16),
        compiler_params=_params(("parallel",)),
        name="mem_attn_step",
    )(q.reshape(b, 1, d), q_gain.reshape(1, MEM_HEAD), k, v)
    return out.reshape(b, d)


def _softplus(x):
    return jnp.maximum(x, 0.0) + jnp.log(1.0 + jnp.exp(-jnp.abs(x)))


def _rwkv_kernel(zr_ref, zk_ref, zv_ref, zg_ref, zwa_ref, sh0_ref, s0_ref,
                 mu_ref, w0_ref, a0_ref, kk_ref, ka_ref, rk_ref, lnw_ref, lnb_ref, wup_ref, aup_ref,
                 y_ref, sout_ref, shout_ref,
                 s_ref, carry_ref, r_s, k_s, v_s, al_s, be_s, ld_s, yo_s,
                 *, chunk, n_valid, width):
    i = pl.program_id(1)
    nblk = pl.num_programs(1)
    cg = zr_ref.shape[0]
    n_pairs = width // LANES

    @pl.when(i == 0)
    def _():
        s_ref[...] = s0_ref[0]
        carry_ref[...] = sh0_ref[0]

    rows = lax.broadcasted_iota(jnp.int32, (cg, 1), 0)
    valid = (rows + i * cg) < n_valid

    def shifted(z, lane0, n):
        prev = jnp.where(rows == 0, carry_ref[:, lane0:lane0 + n], pltpu.roll(z, 1, 0))
        return z + (prev - z) * mu_ref[:, lane0:lane0 + n]

    zr, zk, zv, zg, zwa = zr_ref[...], zk_ref[...], zv_ref[...], zg_ref[...], zwa_ref[...]
    r = shifted(zr, 0, width)
    k = shifted(zk, width, width)
    v = shifted(zv, 2 * width, width)
    gp = shifted(zg, 3 * width, width)
    wa = shifted(zwa, 4 * width, LANES)

    last = jnp.minimum(n_valid - 1 - i * cg, cg - 1)
    pieces = ((zr, 0, width), (zk, width, width), (zv, 2 * width, width), (zg, 3 * width, width),
              (zwa, 4 * width, LANES))

    @pl.when(last >= 0)
    def _():
        for z, lane0, n in pieces:
            carry_ref[:, lane0:lane0 + n] = jnp.sum(jnp.where(rows == last, z, 0.0), axis=0, keepdims=True)

    w = -_softplus(-(w0_ref[...] + _bdot(jnp.tanh(wa), wup_ref[...]))) - 0.5
    logd = -jnp.exp(w)
    a = _sigmoid(a0_ref[...] + _bdot(wa, aup_ref[...]))
    kk = k * kk_ref[...]
    kk = kk * lax.rsqrt(_per_head_sums(kk * kk) + L2_EPS)
    k2 = k * (1.0 + (a - 1.0) * ka_ref[...])
    bonus = _per_head_sums(r * k2 * rk_ref[...]) * v
    r_s[...] = r
    k_s[...] = jnp.where(valid, k2, 0.0)
    v_s[...] = jnp.where(valid, v, 0.0)
    al_s[...] = -kk
    be_s[...] = jnp.where(valid, kk * a, 0.0)
    ld_s[...] = jnp.where(valid, logd, 0.0)

    ri = lax.broadcasted_iota(jnp.int32, (chunk, chunk), 0)
    ci = lax.broadcasted_iota(jnp.int32, (chunk, chunk), 1)
    tri = jnp.where(ri >= ci, 1.0, 0.0).astype(bf16)
    lower, strict = ri >= ci, ri > ci
    eye = jnp.where(ri == ci, 1.0, 0.0)
    lo = _lane_iota((1, LANES)) < HEAD
    sr = lax.broadcasted_iota(jnp.int32, (LANES, LANES), 0) < HEAD
    sc = lax.broadcasted_iota(jnp.int32, (LANES, LANES), 1) < HEAD
    same_head = sr == sc
    n_sq = max(int(math.log2(chunk)) - 1, 0)

    def chunk_body(c, _):
        rs = pl.ds(pl.multiple_of(c * chunk, chunk), chunk)
        ld = ld_s[rs, :]
        lcum = _dot01_left(tri, ld)
        lend = lcum[chunk - 1:chunk, :]
        e_in, e_ex, e_neg, e_end = jnp.exp(lcum), jnp.exp(lcum - ld), jnp.exp(-lcum), jnp.exp(lend - lcum)
        xa_all, xr_all = al_s[rs, :] * e_ex, r_s[rs, :] * e_in
        be, kc, vv = be_s[rs, :], k_s[rs, :], v_s[rs, :]
        yb_all, yk_all = be * e_neg, kc * e_neg
        ybe_all, yke_all = be * e_end, kc * e_end
        d_end = jnp.exp(lend)
        pairs = range(n_pairs)
        heads = [(p, j) for p in pairs for j in (0, 1)]
        lsl = [slice(p * LANES, (p + 1) * LANES) for p in pairs]
        s_bd = [s_ref[p] for p in pairs]
        xs = [jnp.concatenate([xa_all[:, lsl[p]], xr_all[:, lsl[p]]], axis=0) for p in pairs]
        vp = [vv[:, lsl[p]].astype(bf16) for p in pairs]
        a_ab, a_ak, a_rb, a_rk = {}, {}, {}, {}
        for p, j in heads:
            xm = jnp.where(lo if j == 0 else jnp.logical_not(lo), xs[p], 0.0).astype(bf16)
            gb = _bdot_nt(xm, yb_all[:, lsl[p]])
            gk = _bdot_nt(xm, yk_all[:, lsl[p]])
            a_ab[p, j] = jnp.where(strict, gb[:chunk], 0.0)
            a_ak[p, j] = jnp.where(strict, gk[:chunk], 0.0)
            a_rb[p, j] = jnp.where(lower, gb[chunk:], 0.0)
            a_rk[p, j] = jnp.where(lower, gk[chunk:], 0.0)
        tinv = {h: eye + a_ab[h] for h in heads}
        pw = dict(a_ab)
        for _ in range(n_sq):
            for h in heads:
                pw[h] = _bdot(pw[h], pw[h])
            for h in heads:
                tinv[h] = tinv[h] + _bdot(tinv[h], pw[h])
        xh = [_bdot_nt(xs[p], s_bd[p]) for p in pairs]
        av = {(p, j): _bdot(a_ak[p, j], vp[p]) for p, j in heads}
        uh = {(p, j): _bdot(tinv[p, j], xh[p][:chunk] + av[p, j]) for p, j in heads}
        u = [jnp.where(lo, uh[p, 0], uh[p, 1]) for p in pairs]
        yh = {(p, j): _bdot(a_rb[p, j], u[p]) + _bdot(a_rk[p, j], vp[p]) for p, j in heads}
        for p in pairs:
            yo_s[rs, lsl[p]] = xh[p][chunk:] + jnp.where(lo, yh[p, 0], yh[p, 1])
            uv_t = jnp.concatenate([u[p], vv[:, lsl[p]]], axis=0).T
            upd = _bdot(uv_t, jnp.concatenate([ybe_all[:, lsl[p]], yke_all[:, lsl[p]]], axis=0))
            s_ref[p] = s_bd[p] * d_end[:, lsl[p]] + jnp.where(same_head, upd, 0.0)
        return 0

    lax.fori_loop(0, cg // chunk, chunk_body, 0)

    y = yo_s[...]
    mean = _per_head_sums(y) * (1.0 / HEAD)
    yc = y - mean
    var = _per_head_sums(yc * yc) * (1.0 / HEAD)
    y = yc * lax.rsqrt(var + GN_EPS) * lnw_ref[...] + lnb_ref[...]
    y_ref[...] = ((y + bonus) * _sigmoid(gp)).astype(y_ref.dtype)

    @pl.when(i == nblk - 1)
    def _():
        sout_ref[0] = s_ref[...]
        shout_ref[0] = carry_ref[...]


def rwkv7(z, col0, shift0, s0, prm, n_seq, n_valid, chunk, block_rows):
    m = z.shape[0]
    rows = m // n_seq
    nblk = rows // block_rows
    width = prm["b_w0"].shape[0]
    n_heads = width // HEAD
    n_pairs = width // LANES
    bcols = 4 * width + LANES
    s0p = s0.reshape(n_seq, n_pairs, 2, HEAD, HEAD)
    zeros = jnp.zeros_like(s0p[:, :, 0])
    s0bd = jnp.concatenate([jnp.concatenate([s0p[:, :, 0], zeros], axis=-1),
                            jnp.concatenate([zeros, s0p[:, :, 1]], axis=-1)], axis=-2)
    lora = lambda w_, off: jnp.zeros((LANES, width), bf16).at[off:off + w_.shape[0]].set(w_.astype(bf16))
    row1 = lambda a: a.reshape(1, -1).astype(f32)
    cb = col0 // width
    zspec = lambda j: pl.BlockSpec((block_rows, width), lambda b, i: (b * nblk + i, cb + j))
    const = lambda shape: pl.BlockSpec(shape, lambda b, i: (0,) * len(shape))
    y, sbd, shift = pl.pallas_call(
        functools.partial(_rwkv_kernel, chunk=chunk, n_valid=n_valid, width=width),
        grid=(n_seq, nblk),
        in_specs=[zspec(0), zspec(1), zspec(2), zspec(3),
                  pl.BlockSpec((block_rows, LANES), lambda b, i: (b * nblk + i, (col0 + 4 * width) // LANES)),
                  pl.BlockSpec((1, 1, bcols), lambda b, i: (b, 0, 0)),
                  pl.BlockSpec((1, n_pairs, LANES, LANES), lambda b, i: (b, 0, 0, 0)),
                  const((1, bcols))] + [const((1, width))] * 7 + [const((LANES, width))] * 2,
        out_specs=[pl.BlockSpec((block_rows, width), lambda b, i: (b * nblk + i, 0)),
                   pl.BlockSpec((1, n_pairs, LANES, LANES), lambda b, i: (b, 0, 0, 0)),
                   pl.BlockSpec((1, 1, bcols), lambda b, i: (b, 0, 0))],
        out_shape=[jax.ShapeDtypeStruct((m, width), bf16),
                   jax.ShapeDtypeStruct((n_seq, n_pairs, LANES, LANES), f32),
                   jax.ShapeDtypeStruct((n_seq, 1, bcols), f32)],
        scratch_shapes=[pltpu.VMEM((n_pairs, LANES, LANES), f32), pltpu.VMEM((1, bcols), f32)]
                       + [pltpu.VMEM((block_rows, width), f32)] * 7,
        compiler_params=_params(("parallel", "arbitrary")),
        name="rwkv7",
    )(z, z, z, z, z, shift0.reshape(n_seq, 1, bcols), s0bd,
      row1(prm["b_mu"]), row1(prm["b_w0"]), row1(prm["b_a0"]), row1(prm["b_k_k"]), row1(prm["b_k_a"]),
      row1(prm["b_r_k"]), row1(prm["b_ln_w"]), row1(prm["b_ln_b"]),
      lora(prm["b_w_up"], 0), lora(prm["b_a_up"], HEAD))
    sp = sbd.reshape(n_seq, n_pairs, 2, HEAD, 2, HEAD)
    s_fin = jnp.stack([sp[:, :, 0, :, 0, :], sp[:, :, 1, :, 1, :]], axis=2).reshape(n_seq, n_heads, HEAD, HEAD)
    return y, s_fin, shift.reshape(n_seq, bcols)


A_WIDTH = 512
STEP_ROWS = 64
RWKV_CHUNK = 64
RWKV_BLOCK = 256
FLASH_TILE = 512


UNDERFLOW_LOG2 = 160.0


def _first_key_block(c, q_gain, k_gain, n_seq, n_heads, tile):
    qk = 1.02 * HEAD ** 0.5 * LOG2E * jnp.max(jnp.abs(q_gain)) * jnp.max(jnp.abs(k_gain))
    c3 = c.reshape(n_seq, -1, c.shape[1])[:, :, :n_heads]
    c_first = c3[:, ::tile]
    c_last = c3[:, tile - 1::tile]
    n = c_first.shape[1]
    dead = (2.0 * qk + c_first[:, :, None, :] - c_last[:, None, :, :]) <= -UNDERFLOW_LOG2
    dead = dead & (jnp.arange(n)[None, None, :, None] < jnp.arange(n)[None, :, None, None])
    dead = dead.reshape(n_seq, n, n, n_heads // 2, 2).all(axis=-1)
    return jnp.transpose(jnp.sum(dead, axis=2), (0, 2, 1)).astype(jnp.int32)


def _trunk(x3, mem_k, mem_v, rwkv_s0, rwkv_shift0, fox_past, p):
    n_seq, seq, d = x3.shape
    m = n_seq * seq
    x = x3.reshape(m, d)
    step = seq == 1
    tm = m if step else 256
    depth = p["norm_mix"].shape[0]
    chunk_v, s_out, shift_out, fk, fv, fl = [], [], [], [], [], []
    ie = io = 0
    for l in range(depth):
        if l % 2 == 0:
            z = norm_matmul(x, p["norm_mix"][l], p["ab_w_in"][ie], tm)
            prm = {k_: p[k_][ie] for k_ in ("b_mu", "b_w0", "b_w_up", "b_a0", "b_a_up", "b_k_k", "b_k_a",
                                           "b_r_k", "b_ln_w", "b_ln_b")}
            if step:
                ya, va = mixer_a_step(z, p["a_v_norm"][ie], p["a_w_s"][ie], p["a_b_s"][ie], A_WIDTH)
                zp = jnp.zeros((n_seq, STEP_ROWS, z.shape[1]), f32).at[:, 0].set(z)
                yb, s_new, sh_new = rwkv7(zp.reshape(n_seq * STEP_ROWS, -1), 2 * A_WIDTH, rwkv_shift0[ie],
                                          rwkv_s0[ie], prm, n_seq, 1, RWKV_CHUNK, STEP_ROWS)
                yb = yb.reshape(n_seq, STEP_ROWS, -1)[:, 0]
                chunk_v.append(va.reshape(n_seq, seq, A_WIDTH))
            else:
                ya = mixer_a(z, p["a_v_norm"][ie], p["a_w_s"][ie], p["a_b_s"][ie], A_WIDTH, tm)
                yb, s_new, sh_new = rwkv7(z, 2 * A_WIDTH, rwkv_shift0[ie], rwkv_s0[ie], prm, n_seq, seq,
                                          RWKV_CHUNK, RWKV_BLOCK)
            w_out = p["ab_w_out"][ie]
            x = res_matmul(x, [(ya, w_out[:A_WIDTH]), (yb, w_out[A_WIDTH:])], tm)
            s_out.append(s_new)
            shift_out.append(sh_new)
            ie += 1
        else:
            w_in = p["c_w_in"][io]
            width = (w_in.shape[1] // 4) // HEAD * HEAD
            n_heads = width // HEAD
            front = (x, p["norm_mix"][l], w_in[:, :4 * width], w_in[:, 4 * width:], p["c_b_f"][io],
                     p["c_q_norm"][io], p["c_k_norm"][io])
            if step:
                q, k, v, sg, lf = fox_step_in(*front)
                ck, cv, cl, pt = fox_past
                og = fox_step(q, k, v, lf, sg, ck[io], cv[io], cl[io], pt)
                fk.append(k.reshape(n_seq, seq, n_heads, HEAD))
                fv.append(v.reshape(n_seq, seq, n_heads, HEAD))
                fl.append(lf.reshape(n_seq, seq, n_heads))
            else:
                q, kt, ka, vt, vtb, sg, lft, c = fox_seq_in(*front, n_seq, tm)
                first = _first_key_block(c, p["c_q_norm"][io], p["c_k_norm"][io], n_seq, n_heads, FLASH_TILE)
                og = fox_flash(q, c, ka, vtb, sg, first, n_seq, FLASH_TILE)
                tokens_first = lambda t: jnp.transpose(t.reshape(n_seq, n_heads, HEAD, seq), (0, 3, 1, 2))
                fk.append(tokens_first(kt))
                fv.append(tokens_first(vt))
                fl.append(jnp.transpose(lft, (0, 2, 1)))
            x = res_matmul(x, [(og, p["c_w_out"][io])], tm)
            io += 1
        if step:
            q_raw = norm_matmul(x, p["norm_mem"][l], p["m_wq"][l], tm)
            o = mem_attn_step(q_raw, p["m_q_norm"][l], mem_k[l], mem_v[l])
            x = res_matmul(x, [(o, p["m_wo"][l])], tm)
        else:
            x = mem_attn(x, p["norm_mem"][l], p["m_wq"][l], p["m_q_norm"][l], mem_k[l].astype(bf16),
                         mem_v[l].astype(bf16), p["m_wo"][l], seq, tm)
        x = ffn(x, p["norm_ffn"][l], p["f_w_in"][l], p["f_w_out"][l], tm)
    stack = lambda xs: jnp.stack(xs)
    return (x.reshape(n_seq, seq, d), (stack(chunk_v) if chunk_v else None), stack(s_out), stack(shift_out),
            stack(fk), stack(fv), stack(fl))


def kernel(x_prompt, x_sample, mem_prompt, cache_mem_k, cache_mem_v, state_rwkv_S, state_rwkv_shift, cache_fox_k, cache_fox_v, cache_fox_logf, page_table, norm_mix, norm_mem, norm_ffn, ab_w_in, ab_w_out, a_v_norm, a_w_s, a_b_s, b_mu, b_w0, b_w_up, b_a0, b_a_up, b_k_k, b_k_a, b_r_k, b_ln_w, b_ln_b, c_w_in, c_b_f, c_q_norm, c_k_norm, c_w_out, m_mem_norm, m_wq, m_wk, m_wv, m_q_norm, m_k_norm, m_wo, f_w_in, f_w_out):
    cast = lambda w: w.astype(bf16)
    p = dict(norm_mix=norm_mix, norm_mem=norm_mem, norm_ffn=norm_ffn, ab_w_in=cast(ab_w_in),
             ab_w_out=cast(ab_w_out), a_v_norm=a_v_norm, a_w_s=a_w_s, a_b_s=a_b_s, b_mu=b_mu, b_w0=b_w0,
             b_w_up=b_w_up, b_a0=b_a0, b_a_up=b_a_up, b_k_k=b_k_k, b_k_a=b_k_a, b_r_k=b_r_k, b_ln_w=b_ln_w,
             b_ln_b=b_ln_b, c_w_in=cast(c_w_in), c_b_f=c_b_f, c_q_norm=c_q_norm, c_k_norm=c_k_norm,
             c_w_out=cast(c_w_out), m_wq=cast(m_wq), m_q_norm=m_q_norm, m_wo=cast(m_wo),
             f_w_in=cast(f_w_in), f_w_out=cast(f_w_out))
    depth, d = norm_mix.shape
    nb, n_mem = mem_prompt.shape[:2]
    n_even, _, n_heads_b, hd, _ = state_rwkv_S.shape
    mem_heads = d // MEM_HEAD

    mem2 = mem_prompt.reshape(nb * n_mem, d)
    kvs = [mem_kv(mem2, m_mem_norm[l], cast(m_wk[l]), cast(m_wv[l]), m_k_norm[l], 256) for l in range(depth)]
    p_mem_k = jnp.stack([kv[0] for kv in kvs]).reshape(depth, nb, n_mem, d)
    p_mem_v = jnp.stack([kv[1] for kv in kvs]).reshape(depth, nb, n_mem, d)
    s0 = jnp.zeros((n_even, nb, n_heads_b, hd, hd), f32)
    shift0 = jnp.zeros((n_even, nb, state_rwkv_shift.shape[2]), f32)
    y_prompt, _, p_rwkv_s, p_rwkv_shift, p_fox_k, p_fox_v, p_fox_logf = _trunk(
        x_prompt, p_mem_k, p_mem_v, s0, shift0, None, p)

    nd = x_sample.shape[0]
    y_sample, s_chunk_v, s_rwkv_s, s_rwkv_shift, s_fox_k, s_fox_v, s_fox_logf = _trunk(
        x_sample, cache_mem_k.reshape(depth, nd, n_mem, d), cache_mem_v.reshape(depth, nd, n_mem, d),
        state_rwkv_S, state_rwkv_shift, (cache_fox_k, cache_fox_v, cache_fox_logf, page_table), p)
    heads5 = lambda a: a.reshape(depth, nb, n_mem, mem_heads, MEM_HEAD)
    return (y_prompt, y_sample, heads5(p_mem_k), heads5(p_mem_v), p_rwkv_s, p_rwkv_shift, p_fox_k, p_fox_v,
            p_fox_logf, s_chunk_v, s_rwkv_s, s_rwkv_shift, s_fox_k, s_fox_v, s_fox_logf)
```

```python
import functools
import math

import jax
import jax.numpy as jnp
from jax import lax
from jax.experimental import pallas as pl
from jax.experimental.pallas import tpu as pltpu

f32 = jnp.float32
bf16 = jnp.bfloat16

RMS_EPS = 1e-6
GN_EPS = 64e-5
L2_EPS = 1e-12
LOG2E = 1.4426950408889634
LANES = 128
HEAD = 64
A_GROUP = 128
MEM_HEAD = 256
VMEM_LIMIT = 56 * 1024 * 1024


def _params(sem):
    return pltpu.CompilerParams(dimension_semantics=sem, vmem_limit_bytes=VMEM_LIMIT)


def _bdot(a, b):
    return jnp.dot(a.astype(bf16), b.astype(bf16), preferred_element_type=f32)


def _bdot_nt(a, b):
    return lax.dot_general(a.astype(bf16), b.astype(bf16), (((1,), (1,)), ((), ())),
                           preferred_element_type=f32)


def _rms_rows(x, g):
    return x * lax.rsqrt(jnp.mean(x * x, axis=-1, keepdims=True) + RMS_EPS) * g


def _sigmoid(x):
    return 1.0 / (1.0 + jnp.exp(-x))


def _lane_iota(shape):
    return lax.broadcasted_iota(jnp.int32, shape, len(shape) - 1)


def _half_sums(x):
    lo = _lane_iota(x.shape) < HEAD
    s_lo = jnp.sum(jnp.where(lo, x, 0.0), axis=-1, keepdims=True)
    s_hi = jnp.sum(jnp.where(lo, 0.0, x), axis=-1, keepdims=True)
    return jnp.where(lo, s_lo, s_hi)


def _per_head_sums(x):
    n = x.shape[-1] // LANES
    return jnp.concatenate([_half_sums(x[:, i * LANES:(i + 1) * LANES]) for i in range(n)], axis=-1)


def _norm_matmul_kernel(x_ref, g_ref, w_ref, o_ref):
    h = _rms_rows(x_ref[...], g_ref[...])
    o_ref[...] = _bdot(h, w_ref[...])


def norm_matmul(x, g, w, tm):
    m, d = x.shape
    n = w.shape[1]
    return pl.pallas_call(
        _norm_matmul_kernel,
        grid=(m // tm,),
        in_specs=[pl.BlockSpec((tm, d), lambda i: (i, 0)),
                  pl.BlockSpec((1, d), lambda i: (0, 0)),
                  pl.BlockSpec((d, n), lambda i: (0, 0))],
        out_specs=pl.BlockSpec((tm, n), lambda i: (i, 0)),
        out_shape=jax.ShapeDtypeStruct((m, n), f32),
        compiler_params=_params(("parallel",)),
        name="norm_matmul",
    )(x, g.reshape(1, d), w)


def _res_matmul_kernel(*refs):
    x_ref, o_ref = refs[0], refs[-1]
    acc = x_ref[...]
    for a_ref, w_ref in zip(refs[1:-1:2], refs[2:-1:2]):
        acc = acc + _bdot(a_ref[...], w_ref[...])
    o_ref[...] = acc


def res_matmul(x, pairs, tm):
    m, d = x.shape
    in_specs = [pl.BlockSpec((tm, d), lambda i: (i, 0))]
    args = [x]
    for a, w in pairs:
        in_specs += [pl.BlockSpec((tm, a.shape[1]), lambda i: (i, 0)),
                     pl.BlockSpec(w.shape, lambda i: (0, 0))]
        args += [a, w]
    return pl.pallas_call(
        _res_matmul_kernel,
        grid=(m // tm,),
        in_specs=in_specs,
        out_specs=pl.BlockSpec((tm, d), lambda i: (i, 0)),
        out_shape=jax.ShapeDtypeStruct((m, d), f32),
        compiler_params=_params(("parallel",)),
        name="res_matmul",
    )(*args)


def _ffn_kernel(x_ref, g_ref, wi_ref, wo_ref, o_ref, *, d_ff, tf):
    x = x_ref[...]
    h = _rms_rows(x, g_ref[...]).astype(bf16)
    acc = x
    for c in range(d_ff // tf):
        gate = jnp.dot(h, wi_ref[:, c * tf:(c + 1) * tf], preferred_element_type=f32)
        up = jnp.dot(h, wi_ref[:, d_ff + c * tf:d_ff + (c + 1) * tf], preferred_element_type=f32)
        act = gate * _sigmoid(gate) * up
        acc = acc + _bdot(act, wo_ref[c * tf:(c + 1) * tf, :])
    o_ref[...] = acc


def ffn(x, g, w_in, w_out, tm):
    m, d = x.shape
    d_ff = w_out.shape[0]
    tf = 256
    return pl.pallas_call(
        functools.partial(_ffn_kernel, d_ff=d_ff, tf=tf),
        grid=(m // tm,),
        in_specs=[pl.BlockSpec((tm, d), lambda i: (i, 0)),
                  pl.BlockSpec((1, d), lambda i: (0, 0)),
                  pl.BlockSpec(w_in.shape, lambda i: (0, 0)),
                  pl.BlockSpec(w_out.shape, lambda i: (0, 0))],
        out_specs=pl.BlockSpec((tm, d), lambda i: (i, 0)),
        out_shape=jax.ShapeDtypeStruct((m, d), f32),
        compiler_params=_params(("parallel",)),
        name="ffn",
    )(x, g.reshape(1, d), w_in, w_out)


def _mem_kv_kernel(m_ref, g_ref, wk_ref, wv_ref, kg_ref, k_ref, v_ref):
    h = _rms_rows(m_ref[...], g_ref[...]).astype(bf16)
    k = jnp.dot(h, wk_ref[...], preferred_element_type=f32)
    kg = kg_ref[...]
    for hd in range(k.shape[1] // MEM_HEAD):
        sl = slice(hd * MEM_HEAD, (hd + 1) * MEM_HEAD)
        k_ref[:, sl] = _rms_rows(k[:, sl], kg)
    v_ref[...] = jnp.dot(h, wv_ref[...], preferred_element_type=f32)


def mem_kv(mem, g, wk, wv, k_gain, tm):
    m, d = mem.shape
    return pl.pallas_call(
        _mem_kv_kernel,
        grid=(m // tm,),
        in_specs=[pl.BlockSpec((tm, d), lambda i: (i, 0)),
                  pl.BlockSpec((1, d), lambda i: (0, 0)),
                  pl.BlockSpec((d, d), lambda i: (0, 0)),
                  pl.BlockSpec((d, d), lambda i: (0, 0)),
                  pl.BlockSpec((1, MEM_HEAD), lambda i: (0, 0))],
        out_specs=[pl.BlockSpec((tm, d), lambda i: (i, 0))] * 2,
        out_shape=[jax.ShapeDtypeStruct((m, d), f32)] * 2,
        compiler_params=_params(("parallel",)),
        name="mem_kv",
    )(mem, g.reshape(1, d), wk, wv, k_gain.reshape(1, MEM_HEAD))


def _mem_attn_kernel(x_ref, g_ref, wq_ref, qg_ref, k_ref, v_ref, wo_ref, o_ref):
    x = x_ref[...]
    h = _rms_rows(x, g_ref[...])
    q = _bdot(h, wq_ref[...])
    qg = qg_ref[...] * (MEM_HEAD ** -0.5)
    outs = []
    for hd in range(q.shape[1] // MEM_HEAD):
        sl = slice(hd * MEM_HEAD, (hd + 1) * MEM_HEAD)
        qn = _rms_rows(q[:, sl], qg)
        s = _bdot_nt(qn, k_ref[0, :, sl])
        p = jnp.exp(s - jnp.max(s, axis=-1, keepdims=True))
        l = jnp.sum(p, axis=-1, keepdims=True)
        outs.append(_bdot(p, v_ref[0, :, sl]) / l)
    o = jnp.concatenate(outs, axis=-1)
    o_ref[...] = x + _bdot(o, wo_ref[...])


def mem_attn(x, g, wq, q_gain, k, v, wo, rows_per_seq, tm):
    m, d = x.shape
    n_mem = k.shape[1]
    per = rows_per_seq // tm
    return pl.pallas_call(
        _mem_attn_kernel,
        grid=(m // tm,),
        in_specs=[pl.BlockSpec((tm, d), lambda i: (i, 0)),
                  pl.BlockSpec((1, d), lambda i: (0, 0)),
                  pl.BlockSpec((d, d), lambda i: (0, 0)),
                  pl.BlockSpec((1, MEM_HEAD), lambda i: (0, 0)),
                  pl.BlockSpec((1, n_mem, d), lambda i: (i // per, 0, 0)),
                  pl.BlockSpec((1, n_mem, d), lambda i: (i // per, 0, 0)),
                  pl.BlockSpec((d, d), lambda i: (0, 0))],
        out_specs=pl.BlockSpec((tm, d), lambda i: (i, 0)),
        out_shape=jax.ShapeDtypeStruct((m, d), f32),
        compiler_params=_params(("parallel",)),
        name="mem_attn",
    )(x, g.reshape(1, d), wq, q_gain.reshape(1, MEM_HEAD), k, v, wo)


def _gelu(x):
    return 0.5 * x * (1.0 + jnp.tanh(0.7978845608028654 * (x + 0.044715 * x * x * x)))


def _mixer_a_kernel(z_ref, vg_ref, w_ref, b_ref, ya_ref, *, width):
    ge = _gelu(z_ref[...])
    tm = ge.shape[0]
    row = lax.broadcasted_iota(jnp.int32, (A_GROUP, A_GROUP), 0)
    col = lax.broadcasted_iota(jnp.int32, (A_GROUP, A_GROUP), 1)
    for g in range(width // A_GROUP):
        u = ge[:, g * A_GROUP:(g + 1) * A_GROUP]
        v = ge[:, width + g * A_GROUP:width + (g + 1) * A_GROUP]
        va = _rms_rows(v, vg_ref[g:g + 1, :]).astype(bf16)
        w = jnp.where(row >= col, w_ref[g], 0.0).astype(bf16)
        for c in range(tm // A_GROUP):
            rs = slice(c * A_GROUP, (c + 1) * A_GROUP)
            s = jnp.dot(w, va[rs], preferred_element_type=f32) + b_ref[g]
            ya_ref[rs, g * A_GROUP:(g + 1) * A_GROUP] = (u[rs] * s).astype(ya_ref.dtype)


def mixer_a(z, v_gain, w_s, b_s, width, tm):
    m = z.shape[0]
    ng = width // A_GROUP
    b_rows = jnp.broadcast_to(b_s[:, :, None], (ng, A_GROUP, A_GROUP))
    return pl.pallas_call(
        functools.partial(_mixer_a_kernel, width=width),
        grid=(m // tm,),
        in_specs=[pl.BlockSpec((tm, 2 * width), lambda i: (i, 0)),
                  pl.BlockSpec((ng, A_GROUP), lambda i: (0, 0)),
                  pl.BlockSpec((ng, A_GROUP, A_GROUP), lambda i: (0, 0, 0)),
                  pl.BlockSpec((ng, A_GROUP, A_GROUP), lambda i: (0, 0, 0))],
        out_specs=pl.BlockSpec((tm, width), lambda i: (i, 0)),
        out_shape=jax.ShapeDtypeStruct((m, width), bf16),
        compiler_params=_params(("parallel",)),
        name="mixer_a",
    )(z, v_gain, w_s, b_rows)


def _mixer_a_step_kernel(z_ref, vg_ref, w0_ref, b0_ref, ya_ref, va_ref, *, width):
    ge = _gelu(z_ref[...])
    for g in range(width // A_GROUP):
        sl = slice(g * A_GROUP, (g + 1) * A_GROUP)
        u = ge[:, sl]
        v = ge[:, width + g * A_GROUP:width + (g + 1) * A_GROUP]
        va = _rms_rows(v, vg_ref[g:g + 1, :])
        va_ref[:, sl] = va
        ya_ref[:, sl] = (u * (w0_ref[:, sl] * va + b0_ref[:, sl])).astype(ya_ref.dtype)


def mixer_a_step(z, v_gain, w_s, b_s, width):
    m = z.shape[0]
    w0 = jnp.repeat(w_s[:, 0, 0], A_GROUP).reshape(1, width)
    b0 = jnp.repeat(b_s[:, 0], A_GROUP).reshape(1, width)
    return pl.pallas_call(
        functools.partial(_mixer_a_step_kernel, width=width),
        grid=(1,),
        in_specs=[pl.BlockSpec((m, 2 * width), lambda i: (0, 0)),
                  pl.BlockSpec(v_gain.shape, lambda i: (0, 0)),
                  pl.BlockSpec((1, width), lambda i: (0, 0)),
                  pl.BlockSpec((1, width), lambda i: (0, 0))],
        out_specs=[pl.BlockSpec((m, width), lambda i: (0, 0))] * 2,
        out_shape=[jax.ShapeDtypeStruct((m, width), bf16), jax.ShapeDtypeStruct((m, width), f32)],
        compiler_params=_params(("arbitrary",)),
        name="mixer_a_step",
    )(z, v_gain, w0, b0)


def _split3(x):
    hi = x.astype(bf16)
    r1 = x - hi.astype(f32)
    mid = r1.astype(bf16)
    lo = (r1 - mid.astype(f32)).astype(bf16)
    return hi, mid, lo


def _dot01_left(sel, x):
    hi, mid, lo = _split3(x)
    d = lambda p: jnp.dot(sel, p, preferred_element_type=f32)
    return d(hi) + d(mid) + d(lo)


def _dot01_right(x, sel):
    hi, mid, lo = _split3(x)
    d = lambda p: jnp.dot(p, sel, preferred_element_type=f32)
    return d(hi) + d(mid) + d(lo)


def _log_sigmoid(x):
    return jnp.minimum(x, 0.0) - jnp.log(1.0 + jnp.exp(-jnp.abs(x)))


AUG = 2 * LANES


def _fox_project(x_ref, g_ref, w_ref, wf_ref, bf_ref, qg_ref, kg_ref, width, unit):
    h = _rms_rows(x_ref[...], g_ref[...]).astype(bf16)
    z = jnp.dot(h, w_ref[...], preferred_element_type=f32)
    q, k = z[:, :width], z[:, width:2 * width]
    v, g = z[:, 2 * width:3 * width], z[:, 3 * width:4 * width]
    qn = q * lax.rsqrt(_per_head_sums(q * q) * (1.0 / HEAD) + RMS_EPS) * (qg_ref[...] * (HEAD ** -0.5 * unit))
    kn = k * lax.rsqrt(_per_head_sums(k * k) * (1.0 / HEAD) + RMS_EPS) * kg_ref[...]
    lf = _log_sigmoid(jnp.dot(h, wf_ref[...], preferred_element_type=f32) + bf_ref[...])
    return qn, kn, v, g, lf


def _fox_in_args(x, g, w, wf, b_f, q_gain, k_gain):
    d = x.shape[1]
    width = w.shape[1] // 4
    n_heads = width // HEAD
    wf_pad = jnp.zeros((d, LANES), bf16).at[:, :n_heads].set(wf)
    bf_pad = jnp.zeros((1, LANES), f32).at[0, :n_heads].set(b_f)
    qg = jnp.tile(q_gain, n_heads).reshape(1, width)
    kg = jnp.tile(k_gain, n_heads).reshape(1, width)
    const = lambda i: (0, 0)
    specs = [pl.BlockSpec((1, d), const), pl.BlockSpec(w.shape, const), pl.BlockSpec((d, LANES), const),
             pl.BlockSpec((1, LANES), const), pl.BlockSpec((1, width), const), pl.BlockSpec((1, width), const)]
    return specs, (g.reshape(1, d), w, wf_pad, bf_pad, qg, kg)


def _fox_step_in_kernel(x_ref, g_ref, w_ref, wf_ref, bf_ref, qg_ref, kg_ref,
                        q_ref, k_ref, v_ref, sg_ref, lf_ref, *, width, n_heads):
    qn, kn, v, g, lf = _fox_project(x_ref, g_ref, w_ref, wf_ref, bf_ref, qg_ref, kg_ref, width, 1.0)
    q_ref[...] = qn.astype(bf16).astype(f32)
    k_ref[...] = kn
    v_ref[...] = v
    sg_ref[...] = _sigmoid(g)
    lf_ref[...] = lf[:, :n_heads]


def fox_step_in(x, g, w, wf, b_f, q_gain, k_gain):
    m, d = x.shape
    width = w.shape[1] // 4
    n_heads = width // HEAD
    specs, args = _fox_in_args(x, g, w, wf, b_f, q_gain, k_gain)
    whole = lambda n: pl.BlockSpec((m, n), lambda i: (0, 0))
    return pl.pallas_call(
        functools.partial(_fox_step_in_kernel, width=width, n_heads=n_heads),
        grid=(1,),
        in_specs=[whole(d)] + specs,
        out_specs=[whole(width)] * 4 + [whole(n_heads)],
        out_shape=[jax.ShapeDtypeStruct((m, width), f32)] * 4 + [jax.ShapeDtypeStruct((m, n_heads), f32)],
        compiler_params=_params(("arbitrary",)),
        name="fox_step_in",
    )(x, *args)


def _fox_seq_in_kernel(x_ref, g_ref, w_ref, wf_ref, bf_ref, qg_ref, kg_ref,
                       q_ref, kt_ref, ka_ref, vt_ref, vtb_ref, sg_ref, lft_ref, c_ref, carry_ref,
                       *, tiles_per_seq, width, n_heads):
    i = pl.program_id(0)
    qn, kn, v, g, lf = _fox_project(x_ref, g_ref, w_ref, wf_ref, bf_ref, qg_ref, kg_ref, width, LOG2E)
    tm = lf.shape[0]
    q_ref[...] = qn.astype(bf16)
    sg_ref[...] = _sigmoid(g)
    for blk in range(width // LANES):
        sl = slice(blk * LANES, (blk + 1) * LANES)
        kt_ref[0, sl, :] = kn[:, sl].T
        vt = v[:, sl].T
        vt_ref[0, sl, :] = vt
        vtb_ref[0, 0, sl, :] = vt.astype(bf16)
    lft_ref[0] = lf.T[:n_heads, :]

    @pl.when(i % tiles_per_seq == 0)
    def _():
        carry_ref[...] = jnp.zeros_like(carry_ref)

    row = lax.broadcasted_iota(jnp.int32, (tm, tm), 0)
    col = lax.broadcasted_iota(jnp.int32, (tm, tm), 1)
    tri = jnp.where(row >= col, 1.0, 0.0).astype(bf16)
    c = _dot01_left(tri, lf) + carry_ref[...]
    carry_ref[...] = c[tm - 1:tm, :]
    c = c * LOG2E
    c_ref[...] = c
    pieces = jnp.concatenate(_split3(c), axis=-1)
    rr = lax.broadcasted_iota(jnp.int32, (3 * LANES, LANES), 0)
    cc = lax.broadcasted_iota(jnp.int32, (3 * LANES, LANES), 1)
    piece, head = rr // LANES, rr % LANES
    ones = jnp.where(_lane_iota((1, LANES)) < 3, 1.0, 0.0)
    knb = kn.astype(bf16)
    for p in range(n_heads // 2):
        pick = ((head == 2 * p) & (cc == 3 + piece)) | ((head == 2 * p + 1) & (cc == 6 + piece))
        aug = jnp.dot(pieces, jnp.where(pick, 1.0, 0.0).astype(bf16), preferred_element_type=f32) + ones
        ka_ref[p] = jnp.concatenate([knb[:, p * LANES:(p + 1) * LANES], aug.astype(bf16)], axis=-1)


def fox_seq_in(x, g, w, wf, b_f, q_gain, k_gain, n_seq, tm):
    m, d = x.shape
    width = w.shape[1] // 4
    n_heads = width // HEAD
    n_pairs = n_heads // 2
    seq = m // n_seq
    tps = seq // tm
    specs, args = _fox_in_args(x, g, w, wf, b_f, q_gain, k_gain)
    row = lambda i: (i, 0)
    chan = lambda i: (i // tps, 0, i % tps)
    return pl.pallas_call(
        functools.partial(_fox_seq_in_kernel, tiles_per_seq=tps, width=width, n_heads=n_heads),
        grid=(m // tm,),
        in_specs=[pl.BlockSpec((tm, d), row)] + specs,
        out_specs=[pl.BlockSpec((tm, width), row),
                   pl.BlockSpec((1, width, tm), chan),
                   pl.BlockSpec((n_pairs, tm, AUG), lambda i: (0, i, 0)),
                   pl.BlockSpec((1, width, tm), chan),
                   pl.BlockSpec((1, 1, width, tm), lambda i: (i // tps, i % tps, 0, 0)),
                   pl.BlockSpec((tm, width), row),
                   pl.BlockSpec((1, n_heads, tm), chan),
                   pl.BlockSpec((tm, LANES), row)],
        out_shape=[jax.ShapeDtypeStruct((m, width), bf16),
                   jax.ShapeDtypeStruct((n_seq, width, seq), f32),
                   jax.ShapeDtypeStruct((n_pairs, m, AUG), bf16),
                   jax.ShapeDtypeStruct((n_seq, width, seq), f32),
                   jax.ShapeDtypeStruct((n_seq, tps, width, tm), bf16),
                   jax.ShapeDtypeStruct((m, width), f32),
                   jax.ShapeDtypeStruct((n_seq, n_heads, seq), f32),
                   jax.ShapeDtypeStruct((m, LANES), f32)],
        scratch_shapes=[pltpu.VMEM((1, LANES), f32)],
        compiler_params=_params(("arbitrary",)),
        name="fox_seq_in",
    )(x, *args)


def _fox_flash_kernel(lo_ref, q_ref, c_ref, ka_ref, vt_ref, sg_ref, o_ref, *, tq, tv):
    b, p, qi = pl.program_id(0), pl.program_id(1), pl.program_id(2)
    nsub = tq // tv
    q2 = q_ref[0]
    lo = _lane_iota((tq, LANES)) < HEAD
    zero = jnp.zeros_like(q2)
    c_hi, c_mid, c_lo = _split3(c_ref[0])
    rr = lax.broadcasted_iota(jnp.int32, (LANES, LANES), 0)
    cc = lax.broadcasted_iota(jnp.int32, (LANES, LANES), 1)
    lane = _lane_iota((1, LANES))
    qa = []
    for j in (0, 1):
        pick = lambda s: jnp.where((rr == 2 * p + j) & (cc == s), 1.0, 0.0).astype(bf16)
        bias = sum(jnp.dot(piece, pick(s), preferred_element_type=f32)
                   for s, piece in enumerate((c_hi, c_mid, c_lo)))
        bias = bias + jnp.where((lane >= 3 + 3 * j) & (lane < 6 + 3 * j), -1.0, 0.0)
        qh = jnp.where(lo, q2, zero) if j == 0 else jnp.where(lo, zero, q2)
        qa.append(jnp.concatenate([qh, bias.astype(bf16)], axis=-1))
    row = lax.broadcasted_iota(jnp.int32, (tq, tq), 0)
    col = lax.broadcasted_iota(jnp.int32, (tq, tq), 1)

    def scores(kb, j):
        ka = ka_ref[0, 0, pl.ds(pl.multiple_of(kb * tq, tq), tq), :]
        return _bdot_nt(ka, qa[j])

    def softmax_pv(kb, j, t, stats, masked):
        m_old, l_old, acc = stats
        if masked:
            t = jnp.where(row <= col, t, -jnp.inf)
        m_new = jnp.maximum(m_old, jnp.max(t, axis=0, keepdims=True))
        pe = jnp.exp2(t - m_new).astype(bf16)
        alpha = jnp.exp2(m_old - m_new)
        l_new = alpha * l_old + jnp.sum(pe.astype(f32), axis=0, keepdims=True)
        pv = sum(jnp.dot(vt_ref[0, kb * nsub + s, j * HEAD:(j + 1) * HEAD, :], pe[s * tv:(s + 1) * tv],
                         preferred_element_type=f32) for s in range(nsub))
        return m_new, l_new, acc * alpha + pv

    def block(kb, t0, st0, st1, masked, last):
        t1 = scores(kb, 1)
        st0 = softmax_pv(kb, 0, t0, st0, masked)
        t0_next = t0 if last else scores(kb + 1, 0)
        st1 = softmax_pv(kb, 1, t1, st1, masked)
        return t0_next, st0, st1

    init = (jnp.full((1, tq), -jnp.inf, f32), jnp.zeros((1, tq), f32), jnp.zeros((HEAD, tq), f32))
    first = lo_ref[b, p, qi]
    state = lax.fori_loop(first, qi, lambda kb, s: block(kb, *s, False, False),
                          (scores(first, 0), init, init))
    _, (m0, l0, a0), (m1, l1, a1) = block(qi, *state, True, True)
    o = jnp.concatenate([a0 / l0, a1 / l1], axis=0).T
    o_ref[0] = (o * sg_ref[0]).astype(o_ref.dtype)


def fox_flash(q, c, ka, vtb, sg, first_block, n_seq, tq):
    m, width = q.shape
    seq = m // n_seq
    n_pairs = width // LANES
    tv = vtb.shape[3]
    r3 = lambda a: a.reshape(n_seq, seq, a.shape[1])
    tile = pl.BlockSpec((1, tq, LANES), lambda b, p, i, lo: (b, i, p))
    grid_spec = pltpu.PrefetchScalarGridSpec(
        num_scalar_prefetch=1,
        grid=(n_seq, n_pairs, seq // tq),
        in_specs=[tile,
                  pl.BlockSpec((1, tq, LANES), lambda b, p, i, lo: (b, i, 0)),
                  pl.BlockSpec((1, 1, seq, AUG), lambda b, p, i, lo: (p, b, 0, 0)),
                  pl.BlockSpec((1, seq // tv, LANES, tv), lambda b, p, i, lo: (b, 0, p, 0)),
                  tile],
        out_specs=tile,
    )
    out = pl.pallas_call(
        functools.partial(_fox_flash_kernel, tq=tq, tv=tv),
        grid_spec=grid_spec,
        out_shape=jax.ShapeDtypeStruct((n_seq, seq, width), bf16),
        compiler_params=_params(("parallel", "parallel", "arbitrary")),
        name="fox_flash",
    )(first_block, r3(q), r3(c), ka.reshape(n_pairs, n_seq, seq, AUG), vtb, r3(sg))
    return out.reshape(m, width)


def _fox_step_kernel(pt_ref, qc_ref, knt_ref, vnt_ref, lfn_ref, sg_ref, *rest, n_heads, pps):
    kt_refs, vt_refs, lf_refs = rest[:pps], rest[pps:2 * pps], rest[2 * pps:3 * pps]
    o_ref, m_ref, l_ref, acc_ref, carry_ref = rest[3 * pps:]
    pg = pl.program_id(1)
    page = lf_refs[0].shape[2]

    def attend(kt_ref, vt_ref, bias):
        rows = [jnp.sum(kt_ref[0, h] * qc_ref[0, h], axis=0, keepdims=True) for h in range(n_heads)]
        t = jnp.concatenate(rows, axis=0) + bias
        m_old = m_ref[...]
        m_new = jnp.maximum(m_old, jnp.max(t, axis=-1, keepdims=True))
        alpha = jnp.exp(m_old - m_new)
        pe = jnp.exp(t - m_new)
        l_ref[...] = alpha * l_ref[...] + jnp.sum(pe, axis=-1, keepdims=True)
        m_ref[...] = m_new
        for h in range(n_heads):
            acc_ref[h] = acc_ref[h] * alpha[h:h + 1, :] + pe[h:h + 1, :] * vt_ref[0, h]

    @pl.when(pg == 0)
    def _():
        m_ref[...] = jnp.full(m_ref.shape, -jnp.inf, f32)
        l_ref[...] = jnp.zeros_like(l_ref)
        acc_ref[...] = jnp.zeros_like(acc_ref)
        carry_ref[...] = lfn_ref[0]
        attend(knt_ref, vnt_ref, jnp.where(_lane_iota((n_heads, page)) == 0, 0.0, -jnp.inf))

    r = lax.broadcasted_iota(jnp.int32, (page, page), 0)
    cidx = lax.broadcasted_iota(jnp.int32, (page, page), 1)
    later = jnp.where(r > cidx, 1.0, 0.0).astype(bf16)
    for j in range(pps):
        lf = lf_refs[j][0]
        attend(kt_refs[j], vt_refs[j], _dot01_right(lf, later) + carry_ref[...])
        carry_ref[...] = carry_ref[...] + jnp.sum(lf, axis=-1, keepdims=True)

    @pl.when(pg == pl.num_programs(1) - 1)
    def _():
        l = l_ref[...]
        for h in range(n_heads):
            o_ref[0, h] = jnp.sum(acc_ref[h], axis=-1, keepdims=True) / l[h:h + 1, :] * sg_ref[0, h]


FOX_PAGES_PER_STEP = 8


def fox_step(q, k_new, v_new, lf_new, sg, cache_k, cache_v, cache_lf, page_table):
    b, width = q.shape
    pool, page, n_heads = cache_lf.shape
    n_pages = page_table.shape[1]
    pps = FOX_PAGES_PER_STEP
    kt = jnp.transpose(cache_k, (0, 2, 3, 1))
    vt = jnp.transpose(cache_v, (0, 2, 3, 1))
    lft = jnp.transpose(cache_lf, (0, 2, 1))
    col = lambda a: a.reshape(b, n_heads, HEAD, 1)
    own_page = lambda a: jnp.zeros((b, n_heads, HEAD, page), f32).at[..., 0].set(a.reshape(b, n_heads, HEAD))
    qc = jnp.broadcast_to(col(q), (b, n_heads, HEAD, page))
    per_seq = lambda shape: pl.BlockSpec((1,) + shape, lambda i, g, pt: (i,) + (0,) * len(shape))
    page_spec = lambda shape, j: pl.BlockSpec(
        (1,) + shape, lambda i, g, pt: (pt[i, n_pages - 1 - (g * pps + j)],) + (0,) * len(shape))
    grid_spec = pltpu.PrefetchScalarGridSpec(
        num_scalar_prefetch=1,
        grid=(b, n_pages // pps),
        in_specs=[per_seq((n_heads, HEAD, page))] * 3 + [per_seq((n_heads, 1)), per_seq((n_heads, HEAD, 1))]
                 + [page_spec((n_heads, HEAD, page), j) for j in range(pps)] * 2
                 + [page_spec((n_heads, page), j) for j in range(pps)],
        out_specs=per_seq((n_heads, HEAD, 1)),
        scratch_shapes=[pltpu.VMEM((n_heads, 1), f32), pltpu.VMEM((n_heads, 1), f32),
                        pltpu.VMEM((n_heads, HEAD, page), f32), pltpu.VMEM((n_heads, 1), f32)],
    )
    out = pl.pallas_call(
        functools.partial(_fox_step_kernel, n_heads=n_heads, pps=pps),
        grid_spec=grid_spec,
        out_shape=jax.ShapeDtypeStruct((b, n_heads, HEAD, 1), f32),
        compiler_params=_params(("parallel", "arbitrary")),
        name="fox_step",
    )(page_table, qc, own_page(k_new), own_page(v_new), lf_new.reshape(b, n_heads, 1), col(sg),
      *([kt] * pps), *([vt] * pps), *([lft] * pps))
    return out.reshape(b, width).astype(bf16)


def _mem_attn_step_kernel(q_ref, qg_ref, k_ref, v_ref, o_ref):
    q = q_ref[0]
    qn = q * lax.rsqrt(jnp.mean(q * q, axis=-1, keepdims=True) + RMS_EPS) * (qg_ref[...] * MEM_HEAD ** -0.5)
    s = jnp.sum(k_ref[0, 0] * qn, axis=-1, keepdims=True)
    p = jnp.exp(s - jnp.max(s, axis=0, keepdims=True))
    l = jnp.sum(p, axis=0, keepdims=True)
    o_ref[0] = jnp.sum(p * v_ref[0, 0], axis=0, keepdims=True) / l


def mem_attn_step(q, q_gain, cache_k, cache_v, layer):
    b, d = q.shape
    _, _, n_mem, heads, hd = cache_k.shape
    mem_spec = pl.BlockSpec((1, 1, n_mem, heads, hd), lambda i: (layer, i, 0, 0, 0))
    out = pl.pallas_call(
        _mem_attn_step_kernel,
        grid=(b,),
        in_specs=[pl.BlockSpec((1, 1, heads, hd), lambda i: (i, 0, 0, 0)),
                  pl.BlockSpec((1, 1, hd), lambda i: (0, 0, 0)),
                  mem_spec, mem_spec],
        out_specs=pl.BlockSpec((1, 1, heads, hd), lambda i: (i, 0, 0, 0)),
        out_shape=jax.ShapeDtypeStruct((b, 1, heads, hd), f32),
        compiler_params=_params(("parallel",)),
        name="mem_attn_step",
    )(q.reshape(b, 1, heads, hd), q_gain.reshape(1, 1, hd), cache_k, cache_v)
    return out.reshape(b, d)


def _softplus(x):
    return jnp.maximum(x, 0.0) + jnp.log(1.0 + jnp.exp(-jnp.abs(x)))


def _rwkv_kernel(zr_ref, zk_ref, zv_ref, zg_ref, zwa_ref, sh0_ref, s0_ref,
                 mu_ref, w0_ref, a0_ref, kk_ref, ka_ref, rk_ref, lnw_ref, lnb_ref, wup_ref, aup_ref,
                 y_ref, sout_ref, shout_ref,
                 s_ref, carry_ref, r_s, k_s, v_s, al_s, be_s, ld_s, yo_s, bo_s, sg_s,
                 *, chunk, n_valid, width):
    i = pl.program_id(1)
    nblk = pl.num_programs(1)
    nsq, cg = zr_ref.shape[0], zr_ref.shape[1]
    n_pairs = width // LANES

    @pl.when(i == 0)
    def _():
        s_ref[...] = s0_ref[...]
        carry_ref[...] = sh0_ref[...]

    rows = lax.broadcasted_iota(jnp.int32, (cg, 1), 0)
    valid = (rows + i * cg) < n_valid
    last = jnp.minimum(n_valid - 1 - i * cg, cg - 1)

    for s in range(nsq):
        def shifted(z, lane0, n):
            prev = jnp.where(rows == 0, carry_ref[s, :, lane0:lane0 + n], pltpu.roll(z, 1, 0))
            return z + (prev - z) * mu_ref[:, lane0:lane0 + n]

        zr, zk, zv, zg, zwa = zr_ref[s], zk_ref[s], zv_ref[s], zg_ref[s], zwa_ref[s]
        r = shifted(zr, 0, width)
        k = shifted(zk, width, width)
        v = shifted(zv, 2 * width, width)
        gp = shifted(zg, 3 * width, width)
        wa = shifted(zwa, 4 * width, LANES)

        @pl.when(last >= 0)
        def _():
            for z, lane0, n in ((zr, 0, width), (zk, width, width), (zv, 2 * width, width),
                                (zg, 3 * width, width), (zwa, 4 * width, LANES)):
                carry_ref[s, :, lane0:lane0 + n] = jnp.sum(jnp.where(rows == last, z, 0.0), axis=0,
                                                           keepdims=True)

        w = -_softplus(-(w0_ref[...] + _bdot(jnp.tanh(wa), wup_ref[...]))) - 0.5
        logd = -jnp.exp(w)
        a = _sigmoid(a0_ref[...] + _bdot(wa, aup_ref[...]))
        kk = k * kk_ref[...]
        kk = kk * lax.rsqrt(_per_head_sums(kk * kk) + L2_EPS)
        k2 = k * (1.0 + (a - 1.0) * ka_ref[...])
        bo_s[s] = _per_head_sums(r * k2 * rk_ref[...]) * v
        sg_s[s] = _sigmoid(gp)
        r_s[s] = r
        k_s[s] = jnp.where(valid, k2, 0.0)
        v_s[s] = jnp.where(valid, v, 0.0)
        al_s[s] = -kk
        be_s[s] = jnp.where(valid, kk * a, 0.0)
        ld_s[s] = jnp.where(valid, logd, 0.0)

    ri = lax.broadcasted_iota(jnp.int32, (chunk, chunk), 0)
    ci = lax.broadcasted_iota(jnp.int32, (chunk, chunk), 1)
    tri = jnp.where(ri >= ci, 1.0, 0.0).astype(bf16)
    lower, strict = ri >= ci, ri > ci
    eye = jnp.where(ri == ci, 1.0, 0.0)
    lo = _lane_iota((1, LANES)) < HEAD
    sr = lax.broadcasted_iota(jnp.int32, (LANES, LANES), 0) < HEAD
    sc = lax.broadcasted_iota(jnp.int32, (LANES, LANES), 1) < HEAD
    same_head = sr == sc
    n_sq = max(int(math.log2(chunk)) - 1, 0)

    def chunk_body(c, _):
        rs = pl.ds(pl.multiple_of(c * chunk, chunk), chunk)
        pairs = [(s, p) for s in range(nsq) for p in range(n_pairs)]
        heads = [(s, p, j) for s, p in pairs for j in (0, 1)]
        lsl = [slice(p * LANES, (p + 1) * LANES) for p in range(n_pairs)]
        xs, vv, vp, yb, yk, ybe, yke, d_end = {}, {}, {}, {}, {}, {}, {}, {}
        for s in range(nsq):
            ld = ld_s[s, rs, :]
            lcum = _dot01_left(tri, ld)
            lend = lcum[chunk - 1:chunk, :]
            e_neg, e_end = jnp.exp(-lcum), jnp.exp(lend - lcum)
            xa_all, xr_all = al_s[s, rs, :] * jnp.exp(lcum - ld), r_s[s, rs, :] * jnp.exp(lcum)
            be, kc, v_all = be_s[s, rs, :], k_s[s, rs, :], v_s[s, rs, :]
            dec = jnp.exp(lend)
            for p in range(n_pairs):
                xs[s, p] = jnp.concatenate([xa_all[:, lsl[p]], xr_all[:, lsl[p]]], axis=0)
                vv[s, p] = v_all[:, lsl[p]]
                vp[s, p] = vv[s, p].astype(bf16)
                yb[s, p], yk[s, p] = (be * e_neg)[:, lsl[p]], (kc * e_neg)[:, lsl[p]]
                ybe[s, p], yke[s, p] = (be * e_end)[:, lsl[p]], (kc * e_end)[:, lsl[p]]
                d_end[s, p] = dec[:, lsl[p]]
        s_bd = {sp: s_ref[sp[0], sp[1]] for sp in pairs}
        a_ab, a_ak, a_rb, a_rk = {}, {}, {}, {}
        for s, p, j in heads:
            xm = jnp.where(lo if j == 0 else jnp.logical_not(lo), xs[s, p], 0.0).astype(bf16)
            gb = _bdot_nt(xm, yb[s, p])
            gk = _bdot_nt(xm, yk[s, p])
            a_ab[s, p, j] = jnp.where(strict, gb[:chunk], 0.0)
            a_ak[s, p, j] = jnp.where(strict, gk[:chunk], 0.0)
            a_rb[s, p, j] = jnp.where(lower, gb[chunk:], 0.0)
            a_rk[s, p, j] = jnp.where(lower, gk[chunk:], 0.0)
        tinv = {h: eye + a_ab[h] for h in heads}
        pw = dict(a_ab)
        for _ in range(n_sq):
            for h in heads:
                pw[h] = _bdot(pw[h], pw[h])
            for h in heads:
                tinv[h] = tinv[h] + _bdot(tinv[h], pw[h])
        xh = {sp: _bdot_nt(xs[sp], s_bd[sp]) for sp in pairs}
        av = {(s, p, j): _bdot(a_ak[s, p, j], vp[s, p]) for s, p, j in heads}
        uh = {(s, p, j): _bdot(tinv[s, p, j], xh[s, p][:chunk] + av[s, p, j]) for s, p, j in heads}
        u = {(s, p): jnp.where(lo, uh[s, p, 0], uh[s, p, 1]) for s, p in pairs}
        yh = {(s, p, j): _bdot(a_rb[s, p, j], u[s, p]) + _bdot(a_rk[s, p, j], vp[s, p]) for s, p, j in heads}
        for s, p in pairs:
            yo_s[s, rs, lsl[p]] = xh[s, p][chunk:] + jnp.where(lo, yh[s, p, 0], yh[s, p, 1])
            uv_t = jnp.concatenate([u[s, p], vv[s, p]], axis=0).T
            upd = _bdot(uv_t, jnp.concatenate([ybe[s, p], yke[s, p]], axis=0))
            s_ref[s, p] = s_bd[s, p] * d_end[s, p] + jnp.where(same_head, upd, 0.0)
        return 0

    lax.fori_loop(0, cg // chunk, chunk_body, 0)

    for s in range(nsq):
        y = yo_s[s]
        mean = _per_head_sums(y) * (1.0 / HEAD)
        yc = y - mean
        var = _per_head_sums(yc * yc) * (1.0 / HEAD)
        y = yc * lax.rsqrt(var + GN_EPS) * lnw_ref[...] + lnb_ref[...]
        y_ref[s] = ((y + bo_s[s]) * sg_s[s]).astype(y_ref.dtype)

    @pl.when(i == nblk - 1)
    def _():
        sout_ref[...] = s_ref[...]
        shout_ref[...] = carry_ref[...]


def rwkv7(z, col0, shift0, s0, prm, n_seq, n_valid, chunk, block_rows):
    m = z.shape[0]
    rows = m // n_seq
    nblk = rows // block_rows
    width = prm["b_w0"].shape[0]
    n_heads = width // HEAD
    n_pairs = width // LANES
    bcols = 4 * width + LANES
    s0p = s0.reshape(n_seq, n_pairs, 2, HEAD, HEAD)
    zeros = jnp.zeros_like(s0p[:, :, 0])
    s0bd = jnp.concatenate([jnp.concatenate([s0p[:, :, 0], zeros], axis=-1),
                            jnp.concatenate([zeros, s0p[:, :, 1]], axis=-1)], axis=-2)
    lora = lambda w_, off: jnp.zeros((LANES, width), bf16).at[off:off + w_.shape[0]].set(w_.astype(bf16))
    row1 = lambda a: a.reshape(1, -1).astype(f32)
    cb = col0 // width
    nsq = math.gcd(RWKV_SEQS_PER_STEP, n_seq)
    z3 = z.reshape(n_seq, rows, z.shape[1])
    zspec = lambda j: pl.BlockSpec((nsq, block_rows, width), lambda g, i: (g, i, cb + j))
    const = lambda shape: pl.BlockSpec(shape, lambda g, i: (0,) * len(shape))
    per_seq = lambda shape: pl.BlockSpec((nsq,) + shape, lambda g, i: (g,) + (0,) * len(shape))
    y, sbd, shift = pl.pallas_call(
        functools.partial(_rwkv_kernel, chunk=chunk, n_valid=n_valid, width=width),
        grid=(n_seq // nsq, nblk),
        in_specs=[zspec(0), zspec(1), zspec(2), zspec(3),
                  pl.BlockSpec((nsq, block_rows, LANES), lambda g, i: (g, i, (col0 + 4 * width) // LANES)),
                  per_seq((1, bcols)), per_seq((n_pairs, LANES, LANES)),
                  const((1, bcols))] + [const((1, width))] * 7 + [const((LANES, width))] * 2,
        out_specs=[pl.BlockSpec((nsq, block_rows, width), lambda g, i: (g, i, 0)),
                   per_seq((n_pairs, LANES, LANES)), per_seq((1, bcols))],
        out_shape=[jax.ShapeDtypeStruct((n_seq, rows, width), bf16),
                   jax.ShapeDtypeStruct((n_seq, n_pairs, LANES, LANES), f32),
                   jax.ShapeDtypeStruct((n_seq, 1, bcols), f32)],
        scratch_shapes=[pltpu.VMEM((nsq, n_pairs, LANES, LANES), f32), pltpu.VMEM((nsq, 1, bcols), f32)]
                       + [pltpu.VMEM((nsq, block_rows, width), f32)] * 9,
        compiler_params=_params(("parallel", "arbitrary")),
        name="rwkv7",
    )(z3, z3, z3, z3, z3, shift0.reshape(n_seq, 1, bcols), s0bd,
      row1(prm["b_mu"]), row1(prm["b_w0"]), row1(prm["b_a0"]), row1(prm["b_k_k"]), row1(prm["b_k_a"]),
      row1(prm["b_r_k"]), row1(prm["b_ln_w"]), row1(prm["b_ln_b"]),
      lora(prm["b_w_up"], 0), lora(prm["b_a_up"], HEAD))
    sp = sbd.reshape(n_seq, n_pairs, 2, HEAD, 2, HEAD)
    s_fin = jnp.stack([sp[:, :, 0, :, 0, :], sp[:, :, 1, :, 1, :]], axis=2).reshape(n_seq, n_heads, HEAD, HEAD)
    return y.reshape(m, width), s_fin, shift.reshape(n_seq, bcols)


A_WIDTH = 512
STEP_ROWS = 64
RWKV_CHUNK = 64
RWKV_BLOCK = 128
RWKV_SEQS_PER_STEP = 4
FLASH_TILE = 512
DENSE_TILE = 512
FOX_IN_TILE = 256


UNDERFLOW_LOG2 = 160.0


def _first_key_block(c, q_gain, k_gain, n_seq, n_heads, tile):
    qk = 1.02 * HEAD ** 0.5 * LOG2E * jnp.max(jnp.abs(q_gain)) * jnp.max(jnp.abs(k_gain))
    c3 = c.reshape(n_seq, -1, c.shape[1])[:, :, :n_heads]
    c_first = c3[:, ::tile]
    c_last = c3[:, tile - 1::tile]
    n = c_first.shape[1]
    dead = (2.0 * qk + c_first[:, :, None, :] - c_last[:, None, :, :]) <= -UNDERFLOW_LOG2
    dead = dead & (jnp.arange(n)[None, None, :, None] < jnp.arange(n)[None, :, None, None])
    dead = dead.reshape(n_seq, n, n, n_heads // 2, 2).all(axis=-1)
    return jnp.transpose(jnp.sum(dead, axis=2), (0, 2, 1)).astype(jnp.int32)


def _trunk(x3, mem_k, mem_v, rwkv_s0, rwkv_shift0, fox_past, p):
    n_seq, seq, d = x3.shape
    m = n_seq * seq
    x = x3.reshape(m, d)
    step = seq == 1
    tm = m if step else DENSE_TILE
    depth = p["norm_mix"].shape[0]
    chunk_v, s_out, shift_out, fk, fv, fl = [], [], [], [], [], []
    ie = io = 0
    for l in range(depth):
        if l % 2 == 0:
            z = norm_matmul(x, p["norm_mix"][l], p["ab_w_in"][ie], tm)
            prm = {k_: p[k_][ie] for k_ in ("b_mu", "b_w0", "b_w_up", "b_a0", "b_a_up", "b_k_k", "b_k_a",
                                           "b_r_k", "b_ln_w", "b_ln_b")}
            if step:
                ya, va = mixer_a_step(z, p["a_v_norm"][ie], p["a_w_s"][ie], p["a_b_s"][ie], A_WIDTH)
                zp = jnp.zeros((n_seq, STEP_ROWS, z.shape[1]), f32).at[:, 0].set(z)
                yb, s_new, sh_new = rwkv7(zp.reshape(n_seq * STEP_ROWS, -1), 2 * A_WIDTH, rwkv_shift0[ie],
                                          rwkv_s0[ie], prm, n_seq, 1, RWKV_CHUNK, STEP_ROWS)
                yb = yb.reshape(n_seq, STEP_ROWS, -1)[:, 0]
                chunk_v.append(va.reshape(n_seq, seq, A_WIDTH))
            else:
                ya = mixer_a(z, p["a_v_norm"][ie], p["a_w_s"][ie], p["a_b_s"][ie], A_WIDTH, tm)
                yb, s_new, sh_new = rwkv7(z, 2 * A_WIDTH, rwkv_shift0[ie], rwkv_s0[ie], prm, n_seq, seq,
                                          RWKV_CHUNK, RWKV_BLOCK)
            w_out = p["ab_w_out"][ie]
            x = res_matmul(x, [(ya, w_out[:A_WIDTH]), (yb, w_out[A_WIDTH:])], tm)
            s_out.append(s_new)
            shift_out.append(sh_new)
            ie += 1
        else:
            w_in = p["c_w_in"][io]
            width = (w_in.shape[1] // 4) // HEAD * HEAD
            n_heads = width // HEAD
            front = (x, p["norm_mix"][l], w_in[:, :4 * width], w_in[:, 4 * width:], p["c_b_f"][io],
                     p["c_q_norm"][io], p["c_k_norm"][io])
            if step:
                q, k, v, sg, lf = fox_step_in(*front)
                ck, cv, cl, pt = fox_past
                og = fox_step(q, k, v, lf, sg, ck[io], cv[io], cl[io], pt)
                fk.append(k.reshape(n_seq, seq, n_heads, HEAD))
                fv.append(v.reshape(n_seq, seq, n_heads, HEAD))
                fl.append(lf.reshape(n_seq, seq, n_heads))
            else:
                q, kt, ka, vt, vtb, sg, lft, c = fox_seq_in(*front, n_seq, FOX_IN_TILE)
                first = _first_key_block(c, p["c_q_norm"][io], p["c_k_norm"][io], n_seq, n_heads, FLASH_TILE)
                og = fox_flash(q, c, ka, vtb, sg, first, n_seq, FLASH_TILE)
                tokens_first = lambda t: jnp.transpose(t.reshape(n_seq, n_heads, HEAD, seq), (0, 3, 1, 2))
                fk.append(tokens_first(kt))
                fv.append(tokens_first(vt))
                fl.append(jnp.transpose(lft, (0, 2, 1)))
            x = res_matmul(x, [(og, p["c_w_out"][io])], tm)
            io += 1
        if step:
            q_raw = norm_matmul(x, p["norm_mem"][l], p["m_wq"][l], tm)
            o = mem_attn_step(q_raw, p["m_q_norm"][l], mem_k, mem_v, l)
            x = res_matmul(x, [(o, p["m_wo"][l])], tm)
        else:
            x = mem_attn(x, p["norm_mem"][l], p["m_wq"][l], p["m_q_norm"][l], mem_k[l].astype(bf16),
                         mem_v[l].astype(bf16), p["m_wo"][l], seq, tm)
        x = ffn(x, p["norm_ffn"][l], p["f_w_in"][l], p["f_w_out"][l], tm)
    stack = lambda xs: jnp.stack(xs)
    return (x.reshape(n_seq, seq, d), (stack(chunk_v) if chunk_v else None), stack(s_out), stack(shift_out),
            stack(fk), stack(fv), stack(fl))


def kernel(x_prompt, x_sample, mem_prompt, cache_mem_k, cache_mem_v, state_rwkv_S, state_rwkv_shift, cache_fox_k, cache_fox_v, cache_fox_logf, page_table, norm_mix, norm_mem, norm_ffn, ab_w_in, ab_w_out, a_v_norm, a_w_s, a_b_s, b_mu, b_w0, b_w_up, b_a0, b_a_up, b_k_k, b_k_a, b_r_k, b_ln_w, b_ln_b, c_w_in, c_b_f, c_q_norm, c_k_norm, c_w_out, m_mem_norm, m_wq, m_wk, m_wv, m_q_norm, m_k_norm, m_wo, f_w_in, f_w_out):
    cast = lambda w: w.astype(bf16)
    p = dict(norm_mix=norm_mix, norm_mem=norm_mem, norm_ffn=norm_ffn, ab_w_in=cast(ab_w_in),
             ab_w_out=cast(ab_w_out), a_v_norm=a_v_norm, a_w_s=a_w_s, a_b_s=a_b_s, b_mu=b_mu, b_w0=b_w0,
             b_w_up=b_w_up, b_a0=b_a0, b_a_up=b_a_up, b_k_k=b_k_k, b_k_a=b_k_a, b_r_k=b_r_k, b_ln_w=b_ln_w,
             b_ln_b=b_ln_b, c_w_in=cast(c_w_in), c_b_f=c_b_f, c_q_norm=c_q_norm, c_k_norm=c_k_norm,
             c_w_out=cast(c_w_out), m_wq=cast(m_wq), m_q_norm=m_q_norm, m_wo=cast(m_wo),
             f_w_in=cast(f_w_in), f_w_out=cast(f_w_out))
    depth, d = norm_mix.shape
    nb, n_mem = mem_prompt.shape[:2]
    n_even, _, n_heads_b, hd, _ = state_rwkv_S.shape
    mem_heads = d // MEM_HEAD

    mem2 = mem_prompt.reshape(nb * n_mem, d)
    kvs = [mem_kv(mem2, m_mem_norm[l], cast(m_wk[l]), cast(m_wv[l]), m_k_norm[l], 256) for l in range(depth)]
    p_mem_k = jnp.stack([kv[0] for kv in kvs]).reshape(depth, nb, n_mem, d)
    p_mem_v = jnp.stack([kv[1] for kv in kvs]).reshape(depth, nb, n_mem, d)
    s0 = jnp.zeros((n_even, nb, n_heads_b, hd, hd), f32)
    shift0 = jnp.zeros((n_even, nb, state_rwkv_shift.shape[2]), f32)
    y_prompt, _, p_rwkv_s, p_rwkv_shift, p_fox_k, p_fox_v, p_fox_logf = _trunk(
        x_prompt, p_mem_k, p_mem_v, s0, shift0, None, p)

    nd = x_sample.shape[0]
    y_sample, s_chunk_v, s_rwkv_s, s_rwkv_shift, s_fox_k, s_fox_v, s_fox_logf = _trunk(
        x_sample, cache_mem_k, cache_mem_v, state_rwkv_S, state_rwkv_shift, (cache_fox_k, cache_fox_v, cache_fox_logf, page_table), p)
    heads5 = lambda a: a.reshape(depth, nb, n_mem, mem_heads, MEM_HEAD)
    return (y_prompt, y_sample, heads5(p_mem_k), heads5(p_mem_v), p_rwkv_s, p_rwkv_shift, p_fox_k, p_fox_v,
            p_fox_logf, s_chunk_v, s_rwkv_s, s_rwkv_shift, s_fox_k, s_fox_v, s_fox_logf)
```

```python
import functools
import math

import jax
import jax.numpy as jnp
from jax import lax
from jax.experimental import pallas as pl
from jax.experimental.pallas import tpu as pltpu

f32 = jnp.float32
bf16 = jnp.bfloat16

RMS_EPS = 1e-6
GN_EPS = 64e-5
L2_EPS = 1e-12
LOG2E = 1.4426950408889634
LANES = 128
HEAD = 64
A_GROUP = 128
MEM_HEAD = 256
VMEM_LIMIT = 56 * 1024 * 1024


def _params(sem):
    return pltpu.CompilerParams(dimension_semantics=sem, vmem_limit_bytes=VMEM_LIMIT)


def _bdot(a, b):
    return jnp.dot(a.astype(bf16), b.astype(bf16), preferred_element_type=f32)


def _bdot_nt(a, b):
    return lax.dot_general(a.astype(bf16), b.astype(bf16), (((1,), (1,)), ((), ())),
                           preferred_element_type=f32)


def _rms_rows(x, g):
    return x * lax.rsqrt(jnp.mean(x * x, axis=-1, keepdims=True) + RMS_EPS) * g


def _sigmoid(x):
    return 1.0 / (1.0 + jnp.exp(-x))


def _lane_iota(shape):
    return lax.broadcasted_iota(jnp.int32, shape, len(shape) - 1)


def _half_sums(x):
    lo = _lane_iota(x.shape) < HEAD
    s_lo = jnp.sum(jnp.where(lo, x, 0.0), axis=-1, keepdims=True)
    s_hi = jnp.sum(jnp.where(lo, 0.0, x), axis=-1, keepdims=True)
    return jnp.where(lo, s_lo, s_hi)


def _per_head_sums(x):
    n = x.shape[-1] // LANES
    return jnp.concatenate([_half_sums(x[:, i * LANES:(i + 1) * LANES]) for i in range(n)], axis=-1)


def _norm_matmul_kernel(x_ref, g_ref, w_ref, o_ref):
    h = _rms_rows(x_ref[...], g_ref[...])
    o_ref[...] = _bdot(h, w_ref[...])


def norm_matmul(x, g, w, tm):
    m, d = x.shape
    n = w.shape[1]
    return pl.pallas_call(
        _norm_matmul_kernel,
        grid=(m // tm,),
        in_specs=[pl.BlockSpec((tm, d), lambda i: (i, 0)),
                  pl.BlockSpec((1, d), lambda i: (0, 0)),
                  pl.BlockSpec((d, n), lambda i: (0, 0))],
        out_specs=pl.BlockSpec((tm, n), lambda i: (i, 0)),
        out_shape=jax.ShapeDtypeStruct((m, n), f32),
        compiler_params=_params(("parallel",)),
        name="norm_matmul",
    )(x, g.reshape(1, d), w)


def _res_matmul_kernel(*refs):
    x_ref, o_ref = refs[0], refs[-1]
    acc = x_ref[...]
    for a_ref, w_ref in zip(refs[1:-1:2], refs[2:-1:2]):
        acc = acc + _bdot(a_ref[...], w_ref[...])
    o_ref[...] = acc


def res_matmul(x, pairs, tm):
    m, d = x.shape
    in_specs = [pl.BlockSpec((tm, d), lambda i: (i, 0))]
    args = [x]
    for a, w in pairs:
        in_specs += [pl.BlockSpec((tm, a.shape[1]), lambda i: (i, 0)),
                     pl.BlockSpec(w.shape, lambda i: (0, 0))]
        args += [a, w]
    return pl.pallas_call(
        _res_matmul_kernel,
        grid=(m // tm,),
        in_specs=in_specs,
        out_specs=pl.BlockSpec((tm, d), lambda i: (i, 0)),
        out_shape=jax.ShapeDtypeStruct((m, d), f32),
        compiler_params=_params(("parallel",)),
        name="res_matmul",
    )(*args)


def _ffn_kernel(x_ref, g_ref, wi_ref, wo_ref, o_ref, *, d_ff, tf):
    x = x_ref[...]
    h = _rms_rows(x, g_ref[...]).astype(bf16)
    acc = x
    for c in range(d_ff // tf):
        gate = jnp.dot(h, wi_ref[:, c * tf:(c + 1) * tf], preferred_element_type=f32)
        up = jnp.dot(h, wi_ref[:, d_ff + c * tf:d_ff + (c + 1) * tf], preferred_element_type=f32)
        act = gate * _sigmoid(gate) * up
        acc = acc + _bdot(act, wo_ref[c * tf:(c + 1) * tf, :])
    o_ref[...] = acc


def ffn(x, g, w_in, w_out, tm):
    m, d = x.shape
    d_ff = w_out.shape[0]
    tf = 256
    return pl.pallas_call(
        functools.partial(_ffn_kernel, d_ff=d_ff, tf=tf),
        grid=(m // tm,),
        in_specs=[pl.BlockSpec((tm, d), lambda i: (i, 0)),
                  pl.BlockSpec((1, d), lambda i: (0, 0)),
                  pl.BlockSpec(w_in.shape, lambda i: (0, 0)),
                  pl.BlockSpec(w_out.shape, lambda i: (0, 0))],
        out_specs=pl.BlockSpec((tm, d), lambda i: (i, 0)),
        out_shape=jax.ShapeDtypeStruct((m, d), f32),
        compiler_params=_params(("parallel",)),
        name="ffn",
    )(x, g.reshape(1, d), w_in, w_out)


def _mem_kv_kernel(m_ref, g_ref, wk_ref, wv_ref, kg_ref, k_ref, v_ref):
    h = _rms_rows(m_ref[...], g_ref[...]).astype(bf16)
    k = jnp.dot(h, wk_ref[...], preferred_element_type=f32)
    kg = kg_ref[...]
    for hd in range(k.shape[1] // MEM_HEAD):
        sl = slice(hd * MEM_HEAD, (hd + 1) * MEM_HEAD)
        k_ref[:, sl] = _rms_rows(k[:, sl], kg)
    v_ref[...] = jnp.dot(h, wv_ref[...], preferred_element_type=f32)


def mem_kv(mem, g, wk, wv, k_gain, tm):
    m, d = mem.shape
    return pl.pallas_call(
        _mem_kv_kernel,
        grid=(m // tm,),
        in_specs=[pl.BlockSpec((tm, d), lambda i: (i, 0)),
                  pl.BlockSpec((1, d), lambda i: (0, 0)),
                  pl.BlockSpec((d, d), lambda i: (0, 0)),
                  pl.BlockSpec((d, d), lambda i: (0, 0)),
                  pl.BlockSpec((1, MEM_HEAD), lambda i: (0, 0))],
        out_specs=[pl.BlockSpec((tm, d), lambda i: (i, 0))] * 2,
        out_shape=[jax.ShapeDtypeStruct((m, d), f32)] * 2,
        compiler_params=_params(("parallel",)),
        name="mem_kv",
    )(mem, g.reshape(1, d), wk, wv, k_gain.reshape(1, MEM_HEAD))


def _mem_attn_kernel(x_ref, g_ref, wq_ref, qg_ref, k_ref, v_ref, wo_ref, o_ref):
    x = x_ref[...]
    h = _rms_rows(x, g_ref[...])
    q = _bdot(h, wq_ref[...])
    qg = qg_ref[...] * (MEM_HEAD ** -0.5)
    outs = []
    for hd in range(q.shape[1] // MEM_HEAD):
        sl = slice(hd * MEM_HEAD, (hd + 1) * MEM_HEAD)
        qn = _rms_rows(q[:, sl], qg)
        s = _bdot_nt(qn, k_ref[0, :, sl])
        p = jnp.exp(s - jnp.max(s, axis=-1, keepdims=True))
        l = jnp.sum(p, axis=-1, keepdims=True)
        outs.append(_bdot(p, v_ref[0, :, sl]) / l)
    o = jnp.concatenate(outs, axis=-1)
    o_ref[...] = x + _bdot(o, wo_ref[...])


def mem_attn(x, g, wq, q_gain, k, v, wo, rows_per_seq, tm):
    m, d = x.shape
    n_mem = k.shape[1]
    per = rows_per_seq // tm
    return pl.pallas_call(
        _mem_attn_kernel,
        grid=(m // tm,),
        in_specs=[pl.BlockSpec((tm, d), lambda i: (i, 0)),
                  pl.BlockSpec((1, d), lambda i: (0, 0)),
                  pl.BlockSpec((d, d), lambda i: (0, 0)),
                  pl.BlockSpec((1, MEM_HEAD), lambda i: (0, 0)),
                  pl.BlockSpec((1, n_mem, d), lambda i: (i // per, 0, 0)),
                  pl.BlockSpec((1, n_mem, d), lambda i: (i // per, 0, 0)),
                  pl.BlockSpec((d, d), lambda i: (0, 0))],
        out_specs=pl.BlockSpec((tm, d), lambda i: (i, 0)),
        out_shape=jax.ShapeDtypeStruct((m, d), f32),
        compiler_params=_params(("parallel",)),
        name="mem_attn",
    )(x, g.reshape(1, d), wq, q_gain.reshape(1, MEM_HEAD), k, v, wo)


def _gelu(x):
    return 0.5 * x * (1.0 + jnp.tanh(0.7978845608028654 * (x + 0.044715 * x * x * x)))


def _mixer_a_kernel(z_ref, vg_ref, w_ref, b_ref, ya_ref, *, width):
    ge = _gelu(z_ref[...])
    tm = ge.shape[0]
    row = lax.broadcasted_iota(jnp.int32, (A_GROUP, A_GROUP), 0)
    col = lax.broadcasted_iota(jnp.int32, (A_GROUP, A_GROUP), 1)
    for g in range(width // A_GROUP):
        u = ge[:, g * A_GROUP:(g + 1) * A_GROUP]
        v = ge[:, width + g * A_GROUP:width + (g + 1) * A_GROUP]
        va = _rms_rows(v, vg_ref[g:g + 1, :]).astype(bf16)
        w = jnp.where(row >= col, w_ref[g], 0.0).astype(bf16)
        for c in range(tm // A_GROUP):
            rs = slice(c * A_GROUP, (c + 1) * A_GROUP)
            s = jnp.dot(w, va[rs], preferred_element_type=f32) + b_ref[g]
            ya_ref[rs, g * A_GROUP:(g + 1) * A_GROUP] = (u[rs] * s).astype(ya_ref.dtype)


def mixer_a(z, v_gain, w_s, b_s, width, tm):
    m = z.shape[0]
    ng = width // A_GROUP
    b_rows = jnp.broadcast_to(b_s[:, :, None], (ng, A_GROUP, A_GROUP))
    return pl.pallas_call(
        functools.partial(_mixer_a_kernel, width=width),
        grid=(m // tm,),
        in_specs=[pl.BlockSpec((tm, 2 * width), lambda i: (i, 0)),
                  pl.BlockSpec((ng, A_GROUP), lambda i: (0, 0)),
                  pl.BlockSpec((ng, A_GROUP, A_GROUP), lambda i: (0, 0, 0)),
                  pl.BlockSpec((ng, A_GROUP, A_GROUP), lambda i: (0, 0, 0))],
        out_specs=pl.BlockSpec((tm, width), lambda i: (i, 0)),
        out_shape=jax.ShapeDtypeStruct((m, width), bf16),
        compiler_params=_params(("parallel",)),
        name="mixer_a",
    )(z, v_gain, w_s, b_rows)


def _mixer_a_step_kernel(z_ref, vg_ref, w0_ref, b0_ref, ya_ref, va_ref, *, width):
    ge = _gelu(z_ref[...])
    for g in range(width // A_GROUP):
        sl = slice(g * A_GROUP, (g + 1) * A_GROUP)
        u = ge[:, sl]
        v = ge[:, width + g * A_GROUP:width + (g + 1) * A_GROUP]
        va = _rms_rows(v, vg_ref[g:g + 1, :])
        va_ref[:, sl] = va
        ya_ref[:, sl] = (u * (w0_ref[:, sl] * va + b0_ref[:, sl])).astype(ya_ref.dtype)


def mixer_a_step(z, v_gain, w_s, b_s, width):
    m = z.shape[0]
    w0 = jnp.repeat(w_s[:, 0, 0], A_GROUP).reshape(1, width)
    b0 = jnp.repeat(b_s[:, 0], A_GROUP).reshape(1, width)
    return pl.pallas_call(
        functools.partial(_mixer_a_step_kernel, width=width),
        grid=(1,),
        in_specs=[pl.BlockSpec((m, 2 * width), lambda i: (0, 0)),
                  pl.BlockSpec(v_gain.shape, lambda i: (0, 0)),
                  pl.BlockSpec((1, width), lambda i: (0, 0)),
                  pl.BlockSpec((1, width), lambda i: (0, 0))],
        out_specs=[pl.BlockSpec((m, width), lambda i: (0, 0))] * 2,
        out_shape=[jax.ShapeDtypeStruct((m, width), bf16), jax.ShapeDtypeStruct((m, width), f32)],
        compiler_params=_params(("arbitrary",)),
        name="mixer_a_step",
    )(z, v_gain, w0, b0)


def _split3(x):
    hi = x.astype(bf16)
    r1 = x - hi.astype(f32)
    mid = r1.astype(bf16)
    lo = (r1 - mid.astype(f32)).astype(bf16)
    return hi, mid, lo


def _dot01_left(sel, x):
    hi, mid, lo = _split3(x)
    d = lambda p: jnp.dot(sel, p, preferred_element_type=f32)
    return d(hi) + d(mid) + d(lo)


def _dot01_right(x, sel):
    hi, mid, lo = _split3(x)
    d = lambda p: jnp.dot(p, sel, preferred_element_type=f32)
    return d(hi) + d(mid) + d(lo)


def _log_sigmoid(x):
    return jnp.minimum(x, 0.0) - jnp.log(1.0 + jnp.exp(-jnp.abs(x)))


AUG = 2 * LANES


def _fox_project(x_ref, g_ref, w_ref, wf_ref, bf_ref, qg_ref, kg_ref, width, unit):
    h = _rms_rows(x_ref[...], g_ref[...]).astype(bf16)
    z = jnp.dot(h, w_ref[...], preferred_element_type=f32)
    q, k = z[:, :width], z[:, width:2 * width]
    v, g = z[:, 2 * width:3 * width], z[:, 3 * width:4 * width]
    qn = q * lax.rsqrt(_per_head_sums(q * q) * (1.0 / HEAD) + RMS_EPS) * (qg_ref[...] * (HEAD ** -0.5 * unit))
    kn = k * lax.rsqrt(_per_head_sums(k * k) * (1.0 / HEAD) + RMS_EPS) * kg_ref[...]
    lf = _log_sigmoid(jnp.dot(h, wf_ref[...], preferred_element_type=f32) + bf_ref[...])
    return qn, kn, v, g, lf


def _fox_in_args(x, g, w, wf, b_f, q_gain, k_gain):
    d = x.shape[1]
    width = w.shape[1] // 4
    n_heads = width // HEAD
    wf_pad = jnp.zeros((d, LANES), bf16).at[:, :n_heads].set(wf)
    bf_pad = jnp.zeros((1, LANES), f32).at[0, :n_heads].set(b_f)
    qg = jnp.tile(q_gain, n_heads).reshape(1, width)
    kg = jnp.tile(k_gain, n_heads).reshape(1, width)
    const = lambda i: (0, 0)
    specs = [pl.BlockSpec((1, d), const), pl.BlockSpec(w.shape, const), pl.BlockSpec((d, LANES), const),
             pl.BlockSpec((1, LANES), const), pl.BlockSpec((1, width), const), pl.BlockSpec((1, width), const)]
    return specs, (g.reshape(1, d), w, wf_pad, bf_pad, qg, kg)


def _fox_step_in_kernel(x_ref, g_ref, w_ref, wf_ref, bf_ref, qg_ref, kg_ref,
                        q_ref, k_ref, v_ref, sg_ref, lf_ref, *, width, n_heads):
    qn, kn, v, g, lf = _fox_project(x_ref, g_ref, w_ref, wf_ref, bf_ref, qg_ref, kg_ref, width, 1.0)
    q_ref[...] = qn.astype(bf16).astype(f32)
    k_ref[...] = kn
    v_ref[...] = v
    sg_ref[...] = _sigmoid(g)
    lf_ref[...] = lf[:, :n_heads]


def fox_step_in(x, g, w, wf, b_f, q_gain, k_gain):
    m, d = x.shape
    width = w.shape[1] // 4
    n_heads = width // HEAD
    specs, args = _fox_in_args(x, g, w, wf, b_f, q_gain, k_gain)
    whole = lambda n: pl.BlockSpec((m, n), lambda i: (0, 0))
    return pl.pallas_call(
        functools.partial(_fox_step_in_kernel, width=width, n_heads=n_heads),
        grid=(1,),
        in_specs=[whole(d)] + specs,
        out_specs=[whole(width)] * 4 + [whole(n_heads)],
        out_shape=[jax.ShapeDtypeStruct((m, width), f32)] * 4 + [jax.ShapeDtypeStruct((m, n_heads), f32)],
        compiler_params=_params(("arbitrary",)),
        name="fox_step_in",
    )(x, *args)


def _fox_seq_in_kernel(x_ref, g_ref, w_ref, wf_ref, bf_ref, qg_ref, kg_ref,
                       q_ref, kt_ref, ka_ref, vt_ref, vtb_ref, sg_ref, lft_ref, c_ref, carry_ref,
                       *, tiles_per_seq, width, n_heads):
    i = pl.program_id(0)
    qn, kn, v, g, lf = _fox_project(x_ref, g_ref, w_ref, wf_ref, bf_ref, qg_ref, kg_ref, width, LOG2E)
    tm = lf.shape[0]
    q_ref[...] = qn.astype(bf16)
    sg_ref[...] = _sigmoid(g)
    for blk in range(width // LANES):
        sl = slice(blk * LANES, (blk + 1) * LANES)
        kt_ref[0, sl, :] = kn[:, sl].T
        vt = v[:, sl].T
        vt_ref[0, sl, :] = vt
        vtb_ref[0, 0, sl, :] = vt.astype(bf16)
    lft_ref[0] = lf.T[:n_heads, :]

    @pl.when(i % tiles_per_seq == 0)
    def _():
        carry_ref[...] = jnp.zeros_like(carry_ref)

    row = lax.broadcasted_iota(jnp.int32, (tm, tm), 0)
    col = lax.broadcasted_iota(jnp.int32, (tm, tm), 1)
    tri = jnp.where(row >= col, 1.0, 0.0).astype(bf16)
    c = _dot01_left(tri, lf) + carry_ref[...]
    carry_ref[...] = c[tm - 1:tm, :]
    c = c * LOG2E
    c_ref[...] = c
    pieces = jnp.concatenate(_split3(c), axis=-1)
    rr = lax.broadcasted_iota(jnp.int32, (3 * LANES, LANES), 0)
    cc = lax.broadcasted_iota(jnp.int32, (3 * LANES, LANES), 1)
    piece, head = rr // LANES, rr % LANES
    ones = jnp.where(_lane_iota((1, LANES)) < 3, 1.0, 0.0)
    knb = kn.astype(bf16)
    for p in range(n_heads // 2):
        pick = ((head == 2 * p) & (cc == 3 + piece)) | ((head == 2 * p + 1) & (cc == 6 + piece))
        aug = jnp.dot(pieces, jnp.where(pick, 1.0, 0.0).astype(bf16), preferred_element_type=f32) + ones
        ka_ref[p] = jnp.concatenate([knb[:, p * LANES:(p + 1) * LANES], aug.astype(bf16)], axis=-1)


def fox_seq_in(x, g, w, wf, b_f, q_gain, k_gain, n_seq, tm):
    m, d = x.shape
    width = w.shape[1] // 4
    n_heads = width // HEAD
    n_pairs = n_heads // 2
    seq = m // n_seq
    tps = seq // tm
    specs, args = _fox_in_args(x, g, w, wf, b_f, q_gain, k_gain)
    row = lambda i: (i, 0)
    chan = lambda i: (i // tps, 0, i % tps)
    return pl.pallas_call(
        functools.partial(_fox_seq_in_kernel, tiles_per_seq=tps, width=width, n_heads=n_heads),
        grid=(m // tm,),
        in_specs=[pl.BlockSpec((tm, d), row)] + specs,
        out_specs=[pl.BlockSpec((tm, width), row),
                   pl.BlockSpec((1, width, tm), chan),
                   pl.BlockSpec((n_pairs, tm, AUG), lambda i: (0, i, 0)),
                   pl.BlockSpec((1, width, tm), chan),
                   pl.BlockSpec((1, 1, width, tm), lambda i: (i // tps, i % tps, 0, 0)),
                   pl.BlockSpec((tm, width), row),
                   pl.BlockSpec((1, n_heads, tm), chan),
                   pl.BlockSpec((tm, LANES), row)],
        out_shape=[jax.ShapeDtypeStruct((m, width), bf16),
                   jax.ShapeDtypeStruct((n_seq, width, seq), f32),
                   jax.ShapeDtypeStruct((n_pairs, m, AUG), bf16),
                   jax.ShapeDtypeStruct((n_seq, width, seq), f32),
                   jax.ShapeDtypeStruct((n_seq, tps, width, tm), bf16),
                   jax.ShapeDtypeStruct((m, width), f32),
                   jax.ShapeDtypeStruct((n_seq, n_heads, seq), f32),
                   jax.ShapeDtypeStruct((m, LANES), f32)],
        scratch_shapes=[pltpu.VMEM((1, LANES), f32)],
        compiler_params=_params(("arbitrary",)),
        name="fox_seq_in",
    )(x, *args)


def _fox_flash_kernel(lo_ref, qk_ref, q_ref, c_ref, ka_ref, vt_ref, sg_ref, o_ref, *, tq, tv, bounded):
    b, p, qi = pl.program_id(0), pl.program_id(1), pl.program_id(2)
    nsub = tq // tv
    q2 = q_ref[0]
    lo = _lane_iota((tq, LANES)) < HEAD
    zero = jnp.zeros_like(q2)
    cq = c_ref[0] - qk_ref[0] if bounded else c_ref[0]
    c_hi, c_mid, c_lo = _split3(cq)
    rr = lax.broadcasted_iota(jnp.int32, (LANES, LANES), 0)
    cc = lax.broadcasted_iota(jnp.int32, (LANES, LANES), 1)
    lane = _lane_iota((1, LANES))
    qa = []
    for j in (0, 1):
        pick = lambda s: jnp.where((rr == 2 * p + j) & (cc == s), 1.0, 0.0).astype(bf16)
        bias = sum(jnp.dot(piece, pick(s), preferred_element_type=f32)
                   for s, piece in enumerate((c_hi, c_mid, c_lo)))
        bias = bias + jnp.where((lane >= 3 + 3 * j) & (lane < 6 + 3 * j), -1.0, 0.0)
        qh = jnp.where(lo, q2, zero) if j == 0 else jnp.where(lo, zero, q2)
        qa.append(jnp.concatenate([qh, bias.astype(bf16)], axis=-1))
    row = lax.broadcasted_iota(jnp.int32, (tq, tq), 0)
    col = lax.broadcasted_iota(jnp.int32, (tq, tq), 1)

    def scores(kb, j):
        ka = ka_ref[0, 0, pl.ds(pl.multiple_of(kb * tq, tq), tq), :]
        return _bdot_nt(ka, qa[j])

    def softmax_pv(kb, j, t, stats, masked):
        m_old, l_old, acc = stats
        if masked:
            t = jnp.where(row <= col, t, -jnp.inf)
        m_new = jnp.maximum(m_old, jnp.max(t, axis=0, keepdims=True))
        pe = jnp.exp2(t - m_new).astype(bf16)
        alpha = jnp.exp2(m_old - m_new)
        l_new = alpha * l_old + jnp.sum(pe.astype(f32), axis=0, keepdims=True)
        pv = sum(jnp.dot(vt_ref[0, kb * nsub + s, j * HEAD:(j + 1) * HEAD, :], pe[s * tv:(s + 1) * tv],
                         preferred_element_type=f32) for s in range(nsub))
        return m_new, l_new, acc * alpha + pv

    def block(kb, t0, st0, st1, masked, last):
        t1 = scores(kb, 1)
        st0 = softmax_pv(kb, 0, t0, st0, masked)
        t0_next = t0 if last else scores(kb + 1, 0)
        st1 = softmax_pv(kb, 1, t1, st1, masked)
        return t0_next, st0, st1

    def pv(kb, j, pe):
        return sum(jnp.dot(vt_ref[0, kb * nsub + s, j * HEAD:(j + 1) * HEAD, :], pe[s * tv:(s + 1) * tv],
                   preferred_element_type=f32) for s in range(nsub))

    def free_blocks(kbs, sts, masked):
        units = [(kb, j) for kb in kbs for j in (0, 1)]
        l = [sts[0][1], sts[1][1]]
        acc = [sts[0][2], sts[1][2]]
        t_next = scores(*units[0])
        for n, (kb, j) in enumerate(units):
            t = t_next
            if n + 1 < len(units):
                t_next = scores(*units[n + 1])
            if masked:
                t = jnp.where(row <= col, t, -jnp.inf)
            pe = jnp.exp2(t)
            l[j] = l[j] + jnp.sum(pe, axis=0, keepdims=True)
            acc[j] = acc[j] + pv(kb, j, pe.astype(bf16))
        return (sts[0][0], l[0], acc[0]), (sts[1][0], l[1], acc[1])

    def free_block(kb, sts, masked):
        return free_blocks((kb,), sts, masked)

    init = (jnp.full((1, tq), -jnp.inf, f32), jnp.zeros((1, tq), f32), jnp.zeros((HEAD, tq), f32))
    first = lo_ref[b, p, qi]
    if bounded:
        odd = (qi - first) % 2
        sts = lax.cond(odd == 1, lambda s: free_block(first, s, False), lambda s: s, (init, init))
        sts = lax.fori_loop(0, (qi - first) // 2,
                            lambda i, s: free_blocks((first + odd + 2 * i, first + odd + 2 * i + 1), s, False),
                            sts)
        (m0, l0, a0), (m1, l1, a1) = free_block(qi, sts, True)
    else:
        state = lax.fori_loop(first, qi, lambda kb, s: block(kb, *s, False, False),
                              (scores(first, 0), init, init))
        _, (m0, l0, a0), (m1, l1, a1) = block(qi, *state, True, True)
    o = jnp.concatenate([a0 / l0, a1 / l1], axis=0).T
    o_ref[0] = (o * sg_ref[0]).astype(o_ref.dtype)


QK_BOUNDED_MAX = 40.0


def fox_flash(q, c, ka, vtb, sg, first_block, qk, n_seq, tq):
    m, width = q.shape
    seq = m // n_seq
    n_pairs = width // LANES
    tv = vtb.shape[3]
    r3 = lambda a: a.reshape(n_seq, seq, a.shape[1])
    tile = pl.BlockSpec((1, tq, LANES), lambda b, p, i, lo, qk_: (b, i, p))
    grid_spec = pltpu.PrefetchScalarGridSpec(
        num_scalar_prefetch=2,
        grid=(n_seq, n_pairs, seq // tq),
        in_specs=[tile,
                  pl.BlockSpec((1, tq, LANES), lambda b, p, i, lo, qk_: (b, i, 0)),
                  pl.BlockSpec((1, 1, seq, AUG), lambda b, p, i, lo, qk_: (p, b, 0, 0)),
                  pl.BlockSpec((1, seq // tv, LANES, tv), lambda b, p, i, lo, qk_: (b, 0, p, 0)),
                  tile],
        out_specs=tile,
    )
    args = (first_block, qk.reshape(1).astype(f32), r3(q), r3(c), ka.reshape(n_pairs, n_seq, seq, AUG), vtb,
            r3(sg))
    call = lambda bounded: pl.pallas_call(
        functools.partial(_fox_flash_kernel, tq=tq, tv=tv, bounded=bounded),
        grid_spec=grid_spec,
        out_shape=jax.ShapeDtypeStruct((n_seq, seq, width), bf16),
        compiler_params=_params(("parallel", "parallel", "arbitrary")),
        name="fox_flash_bounded" if bounded else "fox_flash",
    )
    out = lax.cond(qk <= QK_BOUNDED_MAX, lambda a: call(True)(*a), lambda a: call(False)(*a), args)
    return out.reshape(m, width)


def _fox_step_kernel(pt_ref, qc_ref, knt_ref, vnt_ref, lfn_ref, sg_ref, *rest, n_heads, pps):
    kt_refs, vt_refs, lf_refs = rest[:pps], rest[pps:2 * pps], rest[2 * pps:3 * pps]
    o_ref, m_ref, l_ref, acc_ref, carry_ref = rest[3 * pps:]
    pg = pl.program_id(1)
    page = lf_refs[0].shape[2]

    def attend(kt_ref, vt_ref, bias):
        rows = [jnp.sum(kt_ref[0, h] * qc_ref[0, h], axis=0, keepdims=True) for h in range(n_heads)]
        t = jnp.concatenate(rows, axis=0) + bias
        m_old = m_ref[...]
        m_new = jnp.maximum(m_old, jnp.max(t, axis=-1, keepdims=True))
        alpha = jnp.exp(m_old - m_new)
        pe = jnp.exp(t - m_new)
        l_ref[...] = alpha * l_ref[...] + jnp.sum(pe, axis=-1, keepdims=True)
        m_ref[...] = m_new
        for h in range(n_heads):
            acc_ref[h] = acc_ref[h] * alpha[h:h + 1, :] + pe[h:h + 1, :] * vt_ref[0, h]

    @pl.when(pg == 0)
    def _():
        m_ref[...] = jnp.full(m_ref.shape, -jnp.inf, f32)
        l_ref[...] = jnp.zeros_like(l_ref)
        acc_ref[...] = jnp.zeros_like(acc_ref)
        carry_ref[...] = lfn_ref[0]
        attend(knt_ref, vnt_ref, jnp.where(_lane_iota((n_heads, page)) == 0, 0.0, -jnp.inf))

    r = lax.broadcasted_iota(jnp.int32, (page, page), 0)
    cidx = lax.broadcasted_iota(jnp.int32, (page, page), 1)
    later = jnp.where(r > cidx, 1.0, 0.0).astype(bf16)
    for j in range(pps):
        lf = lf_refs[j][0]
        attend(kt_refs[j], vt_refs[j], _dot01_right(lf, later) + carry_ref[...])
        carry_ref[...] = carry_ref[...] + jnp.sum(lf, axis=-1, keepdims=True)

    @pl.when(pg == pl.num_programs(1) - 1)
    def _():
        l = l_ref[...]
        for h in range(n_heads):
            o_ref[0, h] = jnp.sum(acc_ref[h], axis=-1, keepdims=True) / l[h:h + 1, :] * sg_ref[0, h]


FOX_PAGES_PER_STEP = 8


def fox_step(q, k_new, v_new, lf_new, sg, cache_k, cache_v, cache_lf, page_table):
    b, width = q.shape
    pool, page, n_heads = cache_lf.shape
    n_pages = page_table.shape[1]
    pps = FOX_PAGES_PER_STEP
    kt = jnp.transpose(cache_k, (0, 2, 3, 1))
    vt = jnp.transpose(cache_v, (0, 2, 3, 1))
    lft = jnp.transpose(cache_lf, (0, 2, 1))
    col = lambda a: a.reshape(b, n_heads, HEAD, 1)
    own_page = lambda a: jnp.zeros((b, n_heads, HEAD, page), f32).at[..., 0].set(a.reshape(b, n_heads, HEAD))
    qc = jnp.broadcast_to(col(q), (b, n_heads, HEAD, page))
    per_seq = lambda shape: pl.BlockSpec((1,) + shape, lambda i, g, pt: (i,) + (0,) * len(shape))
    page_spec = lambda shape, j: pl.BlockSpec(
        (1,) + shape, lambda i, g, pt: (pt[i, n_pages - 1 - (g * pps + j)],) + (0,) * len(shape))
    grid_spec = pltpu.PrefetchScalarGridSpec(
        num_scalar_prefetch=1,
        grid=(b, n_pages // pps),
        in_specs=[per_seq((n_heads, HEAD, page))] * 3 + [per_seq((n_heads, 1)), per_seq((n_heads, HEAD, 1))]
                 + [page_spec((n_heads, HEAD, page), j) for j in range(pps)] * 2
                 + [page_spec((n_heads, page), j) for j in range(pps)],
        out_specs=per_seq((n_heads, HEAD, 1)),
        scratch_shapes=[pltpu.VMEM((n_heads, 1), f32), pltpu.VMEM((n_heads, 1), f32),
                        pltpu.VMEM((n_heads, HEAD, page), f32), pltpu.VMEM((n_heads, 1), f32)],
    )
    out = pl.pallas_call(
        functools.partial(_fox_step_kernel, n_heads=n_heads, pps=pps),
        grid_spec=grid_spec,
        out_shape=jax.ShapeDtypeStruct((b, n_heads, HEAD, 1), f32),
        compiler_params=_params(("parallel", "arbitrary")),
        name="fox_step",
    )(page_table, qc, own_page(k_new), own_page(v_new), lf_new.reshape(b, n_heads, 1), col(sg),
      *([kt] * pps), *([vt] * pps), *([lft] * pps))
    return out.reshape(b, width).astype(bf16)


def _mem_attn_step_kernel(q_ref, qg_ref, k_ref, v_ref, o_ref):
    q = q_ref[0]
    qn = q * lax.rsqrt(jnp.mean(q * q, axis=-1, keepdims=True) + RMS_EPS) * (qg_ref[...] * MEM_HEAD ** -0.5)
    s = jnp.sum(k_ref[0, 0] * qn, axis=-1, keepdims=True)
    p = jnp.exp(s - jnp.max(s, axis=0, keepdims=True))
    l = jnp.sum(p, axis=0, keepdims=True)
    o_ref[0] = jnp.sum(p * v_ref[0, 0], axis=0, keepdims=True) / l


def mem_attn_step(q, q_gain, cache_k, cache_v, layer):
    b, d = q.shape
    _, _, n_mem, heads, hd = cache_k.shape
    mem_spec = pl.BlockSpec((1, 1, n_mem, heads, hd), lambda i: (layer, i, 0, 0, 0))
    out = pl.pallas_call(
        _mem_attn_step_kernel,
        grid=(b,),
        in_specs=[pl.BlockSpec((1, 1, heads, hd), lambda i: (i, 0, 0, 0)),
                  pl.BlockSpec((1, 1, hd), lambda i: (0, 0, 0)),
                  mem_spec, mem_spec],
        out_specs=pl.BlockSpec((1, 1, heads, hd), lambda i: (i, 0, 0, 0)),
        out_shape=jax.ShapeDtypeStruct((b, 1, heads, hd), f32),
        compiler_params=_params(("parallel",)),
        name="mem_attn_step",
    )(q.reshape(b, 1, heads, hd), q_gain.reshape(1, 1, hd), cache_k, cache_v)
    return out.reshape(b, d)


def _softplus(x):
    return jnp.maximum(x, 0.0) + jnp.log(1.0 + jnp.exp(-jnp.abs(x)))


def _rwkv_kernel(zr_ref, zk_ref, zv_ref, zg_ref, zwa_ref, sh0_ref, s0_ref,
                 mu_ref, w0_ref, a0_ref, kk_ref, ka_ref, rk_ref, lnw_ref, lnb_ref, wup_ref, aup_ref,
                 y_ref, sout_ref, shout_ref,
                 s_ref, carry_ref, r_s, k_s, v_s, al_s, be_s, ld_s, yo_s, bo_s, sg_s,
                 *, chunk, n_valid, width):
    i = pl.program_id(1)
    nblk = pl.num_programs(1)
    nsq, cg = zr_ref.shape[0], zr_ref.shape[1]
    n_pairs = width // LANES

    @pl.when(i == 0)
    def _():
        s_ref[...] = s0_ref[...]
        carry_ref[...] = sh0_ref[...]

    rows = lax.broadcasted_iota(jnp.int32, (cg, 1), 0)
    valid = (rows + i * cg) < n_valid
    last = jnp.minimum(n_valid - 1 - i * cg, cg - 1)

    for s in range(nsq):
        def shifted(z, lane0, n):
            prev = jnp.where(rows == 0, carry_ref[s, :, lane0:lane0 + n], pltpu.roll(z, 1, 0))
            return z + (prev - z) * mu_ref[:, lane0:lane0 + n]

        zr, zk, zv, zg, zwa = zr_ref[s], zk_ref[s], zv_ref[s], zg_ref[s], zwa_ref[s]
        r = shifted(zr, 0, width)
        k = shifted(zk, width, width)
        v = shifted(zv, 2 * width, width)
        gp = shifted(zg, 3 * width, width)
        wa = shifted(zwa, 4 * width, LANES)

        @pl.when(last >= 0)
        def _():
            for z, lane0, n in ((zr, 0, width), (zk, width, width), (zv, 2 * width, width),
                                (zg, 3 * width, width), (zwa, 4 * width, LANES)):
                carry_ref[s, :, lane0:lane0 + n] = jnp.sum(jnp.where(rows == last, z, 0.0), axis=0,
                                                           keepdims=True)

        w = -_softplus(-(w0_ref[...] + _bdot(jnp.tanh(wa), wup_ref[...]))) - 0.5
        logd = -jnp.exp(w)
        a = _sigmoid(a0_ref[...] + _bdot(wa, aup_ref[...]))
        kk = k * kk_ref[...]
        kk = kk * lax.rsqrt(_per_head_sums(kk * kk) + L2_EPS)
        k2 = k * (1.0 + (a - 1.0) * ka_ref[...])
        bo_s[s] = _per_head_sums(r * k2 * rk_ref[...]) * v
        sg_s[s] = _sigmoid(gp)
        r_s[s] = r
        k_s[s] = jnp.where(valid, k2, 0.0)
        v_s[s] = jnp.where(valid, v, 0.0)
        al_s[s] = -kk
        be_s[s] = jnp.where(valid, kk * a, 0.0)
        ld_s[s] = jnp.where(valid, logd, 0.0)

    ri = lax.broadcasted_iota(jnp.int32, (chunk, chunk), 0)
    ci = lax.broadcasted_iota(jnp.int32, (chunk, chunk), 1)
    tri = jnp.where(ri >= ci, 1.0, 0.0).astype(bf16)
    lower, strict = ri >= ci, ri > ci
    eye = jnp.where(ri == ci, 1.0, 0.0)
    lo = _lane_iota((1, LANES)) < HEAD
    sr = lax.broadcasted_iota(jnp.int32, (LANES, LANES), 0) < HEAD
    sc = lax.broadcasted_iota(jnp.int32, (LANES, LANES), 1) < HEAD
    same_head = sr == sc
    n_sq = max(int(math.log2(chunk)) - 1, 0)

    def chunk_body(c, _):
        rs = pl.ds(pl.multiple_of(c * chunk, chunk), chunk)
        pairs = [(s, p) for s in range(nsq) for p in range(n_pairs)]
        heads = [(s, p, j) for s, p in pairs for j in (0, 1)]
        lsl = [slice(p * LANES, (p + 1) * LANES) for p in range(n_pairs)]
        xs, vv, vp, yb, yk, ybe, yke, d_end = {}, {}, {}, {}, {}, {}, {}, {}
        for s in range(nsq):
            ld = ld_s[s, rs, :]
            lcum = _dot01_left(tri, ld)
            lend = lcum[chunk - 1:chunk, :]
            e_neg, e_end = jnp.exp(-lcum), jnp.exp(lend - lcum)
            xa_all, xr_all = al_s[s, rs, :] * jnp.exp(lcum - ld), r_s[s, rs, :] * jnp.exp(lcum)
            be, kc, v_all = be_s[s, rs, :], k_s[s, rs, :], v_s[s, rs, :]
            dec = jnp.exp(lend)
            for p in range(n_pairs):
                xs[s, p] = jnp.concatenate([xa_all[:, lsl[p]], xr_all[:, lsl[p]]], axis=0)
                vv[s, p] = v_all[:, lsl[p]]
                vp[s, p] = vv[s, p].astype(bf16)
                yb[s, p], yk[s, p] = (be * e_neg)[:, lsl[p]], (kc * e_neg)[:, lsl[p]]
                ybe[s, p], yke[s, p] = (be * e_end)[:, lsl[p]], (kc * e_end)[:, lsl[p]]
                d_end[s, p] = dec[:, lsl[p]]
        s_bd = {sp: s_ref[sp[0], sp[1]] for sp in pairs}
        a_ab, a_ak, a_rb, a_rk = {}, {}, {}, {}
        for s, p, j in heads:
            xm = jnp.where(lo if j == 0 else jnp.logical_not(lo), xs[s, p], 0.0).astype(bf16)
            gb = _bdot_nt(xm, yb[s, p])
            gk = _bdot_nt(xm, yk[s, p])
            a_ab[s, p, j] = jnp.where(strict, gb[:chunk], 0.0)
            a_ak[s, p, j] = jnp.where(strict, gk[:chunk], 0.0)
            a_rb[s, p, j] = jnp.where(lower, gb[chunk:], 0.0)
            a_rk[s, p, j] = jnp.where(lower, gk[chunk:], 0.0)
        tinv = {h: eye + a_ab[h] for h in heads}
        pw = dict(a_ab)
        for _ in range(n_sq):
            for h in heads:
                pw[h] = _bdot(pw[h], pw[h])
            for h in heads:
                tinv[h] = tinv[h] + _bdot(tinv[h], pw[h])
        xh = {sp: _bdot_nt(xs[sp], s_bd[sp]) for sp in pairs}
        av = {(s, p, j): _bdot(a_ak[s, p, j], vp[s, p]) for s, p, j in heads}
        uh = {(s, p, j): _bdot(tinv[s, p, j], xh[s, p][:chunk] + av[s, p, j]) for s, p, j in heads}
        u = {(s, p): jnp.where(lo, uh[s, p, 0], uh[s, p, 1]) for s, p in pairs}
        yh = {(s, p, j): _bdot(a_rb[s, p, j], u[s, p]) + _bdot(a_rk[s, p, j], vp[s, p]) for s, p, j in heads}
        for s, p in pairs:
            yo_s[s, rs, lsl[p]] = xh[s, p][chunk:] + jnp.where(lo, yh[s, p, 0], yh[s, p, 1])
            uv_t = jnp.concatenate([u[s, p], vv[s, p]], axis=0).T
            upd = _bdot(uv_t, jnp.concatenate([ybe[s, p], yke[s, p]], axis=0))
            s_ref[s, p] = s_bd[s, p] * d_end[s, p] + jnp.where(same_head, upd, 0.0)
        return 0

    lax.fori_loop(0, cg // chunk, chunk_body, 0)

    for s in range(nsq):
        y = yo_s[s]
        mean = _per_head_sums(y) * (1.0 / HEAD)
        yc = y - mean
        var = _per_head_sums(yc * yc) * (1.0 / HEAD)
        y = yc * lax.rsqrt(var + GN_EPS) * lnw_ref[...] + lnb_ref[...]
        y_ref[s] = ((y + bo_s[s]) * sg_s[s]).astype(y_ref.dtype)

    @pl.when(i == nblk - 1)
    def _():
        sout_ref[...] = s_ref[...]
        shout_ref[...] = carry_ref[...]


def rwkv7(z, col0, shift0, s0, prm, n_seq, n_valid, chunk, block_rows):
    m = z.shape[0]
    rows = m // n_seq
    nblk = rows // block_rows
    width = prm["b_w0"].shape[0]
    n_heads = width // HEAD
    n_pairs = width // LANES
    bcols = 4 * width + LANES
    s0p = s0.reshape(n_seq, n_pairs, 2, HEAD, HEAD)
    zeros = jnp.zeros_like(s0p[:, :, 0])
    s0bd = jnp.concatenate([jnp.concatenate([s0p[:, :, 0], zeros], axis=-1),
                            jnp.concatenate([zeros, s0p[:, :, 1]], axis=-1)], axis=-2)
    lora = lambda w_, off: jnp.zeros((LANES, width), bf16).at[off:off + w_.shape[0]].set(w_.astype(bf16))
    row1 = lambda a: a.reshape(1, -1).astype(f32)
    cb = col0 // width
    nsq = math.gcd(RWKV_SEQS_PER_STEP, n_seq)
    z3 = z.reshape(n_seq, rows, z.shape[1])
    zspec = lambda j: pl.BlockSpec((nsq, block_rows, width), lambda g, i: (g, i, cb + j))
    const = lambda shape: pl.BlockSpec(shape, lambda g, i: (0,) * len(shape))
    per_seq = lambda shape: pl.BlockSpec((nsq,) + shape, lambda g, i: (g,) + (0,) * len(shape))
    y, sbd, shift = pl.pallas_call(
        functools.partial(_rwkv_kernel, chunk=chunk, n_valid=n_valid, width=width),
        grid=(n_seq // nsq, nblk),
        in_specs=[zspec(0), zspec(1), zspec(2), zspec(3),
                  pl.BlockSpec((nsq, block_rows, LANES), lambda g, i: (g, i, (col0 + 4 * width) // LANES)),
                  per_seq((1, bcols)), per_seq((n_pairs, LANES, LANES)),
                  const((1, bcols))] + [const((1, width))] * 7 + [const((LANES, width))] * 2,
        out_specs=[pl.BlockSpec((nsq, block_rows, width), lambda g, i: (g, i, 0)),
                   per_seq((n_pairs, LANES, LANES)), per_seq((1, bcols))],
        out_shape=[jax.ShapeDtypeStruct((n_seq, rows, width), bf16),
                   jax.ShapeDtypeStruct((n_seq, n_pairs, LANES, LANES), f32),
                   jax.ShapeDtypeStruct((n_seq, 1, bcols), f32)],
        scratch_shapes=[pltpu.VMEM((nsq, n_pairs, LANES, LANES), f32), pltpu.VMEM((nsq, 1, bcols), f32)]
                       + [pltpu.VMEM((nsq, block_rows, width), f32)] * 9,
        compiler_params=_params(("parallel", "arbitrary")),
        name="rwkv7",
    )(z3, z3, z3, z3, z3, shift0.reshape(n_seq, 1, bcols), s0bd,
      row1(prm["b_mu"]), row1(prm["b_w0"]), row1(prm["b_a0"]), row1(prm["b_k_k"]), row1(prm["b_k_a"]),
      row1(prm["b_r_k"]), row1(prm["b_ln_w"]), row1(prm["b_ln_b"]),
      lora(prm["b_w_up"], 0), lora(prm["b_a_up"], HEAD))
    sp = sbd.reshape(n_seq, n_pairs, 2, HEAD, 2, HEAD)
    s_fin = jnp.stack([sp[:, :, 0, :, 0, :], sp[:, :, 1, :, 1, :]], axis=2).reshape(n_seq, n_heads, HEAD, HEAD)
    return y.reshape(m, width), s_fin, shift.reshape(n_seq, bcols)


A_WIDTH = 512
STEP_ROWS = 64
RWKV_CHUNK = 64
RWKV_BLOCK = 128
RWKV_SEQS_PER_STEP = 4
FLASH_TILE = 512
DENSE_TILE = 512
FOX_IN_TILE = 256


UNDERFLOW_LOG2 = 160.0


def _qk_bound(q_gain, k_gain):
    return 1.02 * HEAD ** 0.5 * LOG2E * jnp.max(jnp.abs(q_gain)) * jnp.max(jnp.abs(k_gain))


def _first_key_block(c, qk, n_seq, n_heads, tile):
    c3 =c.reshape(n_seq, -1, c.shape[1])[:, :, :n_heads]
    c_first = c3[:, ::tile]
    c_last = c3[:, tile - 1::tile]
    n = c_first.shape[1]
    dead = (2.0 * qk + c_first[:, :, None, :] - c_last[:, None, :, :]) <= -UNDERFLOW_LOG2
    dead = dead & (jnp.arange(n)[None, None, :, None] < jnp.arange(n)[None, :, None, None])
    dead = dead.reshape(n_seq, n, n, n_heads // 2, 2).all(axis=-1)
    return jnp.transpose(jnp.sum(dead, axis=2), (0, 2, 1)).astype(jnp.int32)


def _trunk(x3, mem_k, mem_v, rwkv_s0, rwkv_shift0, fox_past, p):
    n_seq, seq, d = x3.shape
    m = n_seq * seq
    x = x3.reshape(m, d)
    step = seq == 1
    tm = m if step else DENSE_TILE
    depth = p["norm_mix"].shape[0]
    chunk_v, s_out, shift_out, fk, fv, fl = [], [], [], [], [], []
    ie = io = 0
    for l in range(depth):
        if l % 2 == 0:
            z = norm_matmul(x, p["norm_mix"][l], p["ab_w_in"][ie], tm)
            prm = {k_: p[k_][ie] for k_ in ("b_mu", "b_w0", "b_w_up", "b_a0", "b_a_up", "b_k_k", "b_k_a",
                                           "b_r_k", "b_ln_w", "b_ln_b")}
            if step:
                ya, va = mixer_a_step(z, p["a_v_norm"][ie], p["a_w_s"][ie], p["a_b_s"][ie], A_WIDTH)
                zp = jnp.zeros((n_seq, STEP_ROWS, z.shape[1]), f32).at[:, 0].set(z)
                yb, s_new, sh_new = rwkv7(zp.reshape(n_seq * STEP_ROWS, -1), 2 * A_WIDTH, rwkv_shift0[ie],
                                          rwkv_s0[ie], prm, n_seq, 1, RWKV_CHUNK, STEP_ROWS)
                yb = yb.reshape(n_seq, STEP_ROWS, -1)[:, 0]
                chunk_v.append(va.reshape(n_seq, seq, A_WIDTH))
            else:
                ya = mixer_a(z, p["a_v_norm"][ie], p["a_w_s"][ie], p["a_b_s"][ie], A_WIDTH, tm)
                yb, s_new, sh_new = rwkv7(z, 2 * A_WIDTH, rwkv_shift0[ie], rwkv_s0[ie], prm, n_seq, seq,
                                          RWKV_CHUNK, RWKV_BLOCK)
            w_out = p["ab_w_out"][ie]
            x = res_matmul(x, [(ya, w_out[:A_WIDTH]), (yb, w_out[A_WIDTH:])], tm)
            s_out.append(s_new)
            shift_out.append(sh_new)
            ie += 1
        else:
            w_in = p["c_w_in"][io]
            width = (w_in.shape[1] // 4) // HEAD * HEAD
            n_heads = width // HEAD
            front = (x, p["norm_mix"][l], w_in[:, :4 * width], w_in[:, 4 * width:], p["c_b_f"][io],
                     p["c_q_norm"][io], p["c_k_norm"][io])
            if step:
                q, k, v, sg, lf = fox_step_in(*front)
                ck, cv, cl, pt = fox_past
                og = fox_step(q, k, v, lf, sg, ck[io], cv[io], cl[io], pt)
                fk.append(k.reshape(n_seq, seq, n_heads, HEAD))
                fv.append(v.reshape(n_seq, seq, n_heads, HEAD))
                fl.append(lf.reshape(n_seq, seq, n_heads))
            else:
                q, kt, ka, vt, vtb, sg, lft, c = fox_seq_in(*front, n_seq, FOX_IN_TILE)
                qk = _qk_bound(p["c_q_norm"][io], p["c_k_norm"][io])
                first = _first_key_block(c, qk, n_seq, n_heads, FLASH_TILE)
                og = fox_flash(q, c, ka, vtb, sg, first, qk, n_seq, FLASH_TILE)
                tokens_first = lambda t: jnp.transpose(t.reshape(n_seq, n_heads, HEAD, seq), (0, 3, 1, 2))
                fk.append(tokens_first(kt))
                fv.append(tokens_first(vt))
                fl.append(jnp.transpose(lft, (0, 2, 1)))
            x = res_matmul(x, [(og, p["c_w_out"][io])], tm)
            io += 1
        if step:
            q_raw = norm_matmul(x, p["norm_mem"][l], p["m_wq"][l], tm)
            o = mem_attn_step(q_raw, p["m_q_norm"][l], mem_k, mem_v, l)
            x = res_matmul(x, [(o, p["m_wo"][l])], tm)
        else:
            x = mem_attn(x, p["norm_mem"][l], p["m_wq"][l], p["m_q_norm"][l], mem_k[l].astype(bf16),
                         mem_v[l].astype(bf16), p["m_wo"][l], seq, tm)
        x = ffn(x, p["norm_ffn"][l], p["f_w_in"][l], p["f_w_out"][l], tm)
    stack = lambda xs: jnp.stack(xs)
    return (x.reshape(n_seq, seq, d), (stack(chunk_v) if chunk_v else None), stack(s_out), stack(shift_out),
            stack(fk), stack(fv), stack(fl))


def kernel(x_prompt, x_sample, mem_prompt, cache_mem_k, cache_mem_v, state_rwkv_S, state_rwkv_shift, cache_fox_k, cache_fox_v, cache_fox_logf, page_table, norm_mix, norm_mem, norm_ffn, ab_w_in, ab_w_out, a_v_norm, a_w_s, a_b_s, b_mu, b_w0, b_w_up, b_a0, b_a_up, b_k_k, b_k_a, b_r_k, b_ln_w, b_ln_b, c_w_in, c_b_f, c_q_norm, c_k_norm, c_w_out, m_mem_norm, m_wq, m_wk, m_wv, m_q_norm, m_k_norm, m_wo, f_w_in, f_w_out):
    cast = lambda w: w.astype(bf16)
    p = dict(norm_mix=norm_mix, norm_mem=norm_mem, norm_ffn=norm_ffn, ab_w_in=cast(ab_w_in),
             ab_w_out=cast(ab_w_out), a_v_norm=a_v_norm, a_w_s=a_w_s, a_b_s=a_b_s, b_mu=b_mu, b_w0=b_w0,
             b_w_up=b_w_up, b_a0=b_a0, b_a_up=b_a_up, b_k_k=b_k_k, b_k_a=b_k_a, b_r_k=b_r_k, b_ln_w=b_ln_w,
             b_ln_b=b_ln_b, c_w_in=cast(c_w_in), c_b_f=c_b_f, c_q_norm=c_q_norm, c_k_norm=c_k_norm,
             c_w_out=cast(c_w_out), m_wq=cast(m_wq), m_q_norm=m_q_norm, m_wo=cast(m_wo),
             f_w_in=cast(f_w_in), f_w_out=cast(f_w_out))
    depth, d = norm_mix.shape
    nb, n_mem = mem_prompt.shape[:2]
    n_even, _, n_heads_b, hd, _ = state_rwkv_S.shape
    mem_heads = d // MEM_HEAD

    mem2 = mem_prompt.reshape(nb * n_mem, d)
    kvs = [mem_kv(mem2, m_mem_norm[l], cast(m_wk[l]), cast(m_wv[l]), m_k_norm[l], 256) for l in range(depth)]
    p_mem_k = jnp.stack([kv[0] for kv in kvs]).reshape(depth, nb, n_mem, d)
    p_mem_v = jnp.stack([kv[1] for kv in kvs]).reshape(depth, nb, n_mem, d)
    s0 = jnp.zeros((n_even, nb, n_heads_b, hd, hd), f32)
    shift0 = jnp.zeros((n_even, nb, state_rwkv_shift.shape[2]), f32)
    y_prompt, _, p_rwkv_s, p_rwkv_shift, p_fox_k, p_fox_v, p_fox_logf = _trunk(
        x_prompt, p_mem_k, p_mem_v, s0, shift0, None, p)

    nd = x_sample.shape[0]
    y_sample, s_chunk_v, s_rwkv_s, s_rwkv_shift, s_fox_k, s_fox_v, s_fox_logf = _trunk(
        x_sample, cache_mem_k, cache_mem_v, state_rwkv_S, state_rwkv_shift, (cache_fox_k, cache_fox_v, cache_fox_logf, page_table), p)
    heads5 = lambda a: a.reshape(depth, nb, n_mem, mem_heads, MEM_HEAD)
    return (y_prompt, y_sample, heads5(p_mem_k), heads5(p_mem_v), p_rwkv_s, p_rwkv_shift, p_fox_k, p_fox_v,
            p_fox_logf, s_chunk_v, s_rwkv_s, s_rwkv_shift, s_fox_k, s_fox_v, s_fox_logf)
```

```python
import functools
import math

import jax
import jax.numpy as jnp
from jax import lax
from jax.experimental import pallas as pl
from jax.experimental.pallas import tpu as pltpu

f32 = jnp.float32
bf16 = jnp.bfloat16

RMS_EPS = 1e-6
GN_EPS = 64e-5
L2_EPS = 1e-12
LOG2E = 1.4426950408889634
LANES = 128
HEAD = 64
A_GROUP = 128
MEM_HEAD = 256
VMEM_LIMIT = 56 * 1024 * 1024


def _params(sem):
    return pltpu.CompilerParams(dimension_semantics=sem, vmem_limit_bytes=VMEM_LIMIT)


def _bdot(a, b):
    return jnp.dot(a.astype(bf16), b.astype(bf16), preferred_element_type=f32)


def _bdot_nt(a, b):
    return lax.dot_general(a.astype(bf16), b.astype(bf16), (((1,), (1,)), ((), ())),
                           preferred_element_type=f32)


def _rms_rows(x, g):
    return x * lax.rsqrt(jnp.mean(x * x, axis=-1, keepdims=True) + RMS_EPS) * g


def _sigmoid(x):
    return 1.0 / (1.0 + jnp.exp(-x))


def _lane_iota(shape):
    return lax.broadcasted_iota(jnp.int32, shape, len(shape) - 1)


def _half_sums(x):
    lo = _lane_iota(x.shape) < HEAD
    s_lo = jnp.sum(jnp.where(lo, x, 0.0), axis=-1, keepdims=True)
    s_hi = jnp.sum(jnp.where(lo, 0.0, x), axis=-1, keepdims=True)
    return jnp.where(lo, s_lo, s_hi)


def _per_head_sums(x):
    n = x.shape[-1] // LANES
    return jnp.concatenate([_half_sums(x[:, i * LANES:(i + 1) * LANES]) for i in range(n)], axis=-1)


def _norm_matmul_kernel(x_ref, g_ref, w_ref, o_ref):
    h = _rms_rows(x_ref[...], g_ref[...])
    o_ref[...] = _bdot(h, w_ref[...])


def norm_matmul(x, g, w, tm):
    m, d = x.shape
    n = w.shape[1]
    return pl.pallas_call(
        _norm_matmul_kernel,
        grid=(m // tm,),
        in_specs=[pl.BlockSpec((tm, d), lambda i: (i, 0)),
                  pl.BlockSpec((1, d), lambda i: (0, 0)),
                  pl.BlockSpec((d, n), lambda i: (0, 0))],
        out_specs=pl.BlockSpec((tm, n), lambda i: (i, 0)),
        out_shape=jax.ShapeDtypeStruct((m, n), f32),
        compiler_params=_params(("parallel",)),
        name="norm_matmul",
    )(x, g.reshape(1, d), w)


def _res_matmul_kernel(*refs):
    x_ref, o_ref = refs[0], refs[-1]
    acc = x_ref[...]
    for a_ref, w_ref in zip(refs[1:-1:2], refs[2:-1:2]):
        acc = acc + _bdot(a_ref[...], w_ref[...])
    o_ref[...] = acc


def res_matmul(x, pairs, tm):
    m, d = x.shape
    in_specs = [pl.BlockSpec((tm, d), lambda i: (i, 0))]
    args = [x]
    for a, w in pairs:
        in_specs += [pl.BlockSpec((tm, a.shape[1]), lambda i: (i, 0)),
                     pl.BlockSpec(w.shape, lambda i: (0, 0))]
        args += [a, w]
    return pl.pallas_call(
        _res_matmul_kernel,
        grid=(m // tm,),
        in_specs=in_specs,
        out_specs=pl.BlockSpec((tm, d), lambda i: (i, 0)),
        out_shape=jax.ShapeDtypeStruct((m, d), f32),
        compiler_params=_params(("parallel",)),
        name="res_matmul",
    )(*args)


def _ffn_kernel(x_ref, g_ref, wi_ref, wo_ref, o_ref, *, d_ff, tf):
    x = x_ref[...]
    h = _rms_rows(x, g_ref[...]).astype(bf16)
    acc = x
    for c in range(d_ff // tf):
        gate = jnp.dot(h, wi_ref[:, c * tf:(c + 1) * tf], preferred_element_type=f32)
        up = jnp.dot(h, wi_ref[:, d_ff + c * tf:d_ff + (c + 1) * tf], preferred_element_type=f32)
        act = gate * _sigmoid(gate) * up
        acc = acc + _bdot(act, wo_ref[c * tf:(c + 1) * tf, :])
    o_ref[...] = acc


def ffn(x, g, w_in, w_out, tm):
    m, d = x.shape
    d_ff = w_out.shape[0]
    tf = 256
    return pl.pallas_call(
        functools.partial(_ffn_kernel, d_ff=d_ff, tf=tf),
        grid=(m // tm,),
        in_specs=[pl.BlockSpec((tm, d), lambda i: (i, 0)),
                  pl.BlockSpec((1, d), lambda i: (0, 0)),
                  pl.BlockSpec(w_in.shape, lambda i: (0, 0)),
                  pl.BlockSpec(w_out.shape, lambda i: (0, 0))],
        out_specs=pl.BlockSpec((tm, d), lambda i: (i, 0)),
        out_shape=jax.ShapeDtypeStruct((m, d), f32),
        compiler_params=_params(("parallel",)),
        name="ffn",
    )(x, g.reshape(1, d), w_in, w_out)


def _mem_kv_kernel(m_ref, g_ref, wk_ref, wv_ref, kg_ref, k_ref, v_ref):
    h = _rms_rows(m_ref[...], g_ref[...]).astype(bf16)
    k = jnp.dot(h, wk_ref[...], preferred_element_type=f32)
    kg = kg_ref[...]
    for hd in range(k.shape[1] // MEM_HEAD):
        sl = slice(hd * MEM_HEAD, (hd + 1) * MEM_HEAD)
        k_ref[:, sl] = _rms_rows(k[:, sl], kg)
    v_ref[...] = jnp.dot(h, wv_ref[...], preferred_element_type=f32)


def mem_kv(mem, g, wk, wv, k_gain, tm):
    m, d = mem.shape
    return pl.pallas_call(
        _mem_kv_kernel,
        grid=(m // tm,),
        in_specs=[pl.BlockSpec((tm, d), lambda i: (i, 0)),
                  pl.BlockSpec((1, d), lambda i: (0, 0)),
                  pl.BlockSpec((d, d), lambda i: (0, 0)),
                  pl.BlockSpec((d, d), lambda i: (0, 0)),
                  pl.BlockSpec((1, MEM_HEAD), lambda i: (0, 0))],
        out_specs=[pl.BlockSpec((tm, d), lambda i: (i, 0))] * 2,
        out_shape=[jax.ShapeDtypeStruct((m, d), f32)] * 2,
        compiler_params=_params(("parallel",)),
        name="mem_kv",
    )(mem, g.reshape(1, d), wk, wv, k_gain.reshape(1, MEM_HEAD))


def _mem_attn_kernel(x_ref, g_ref, wq_ref, qg_ref, k_ref, v_ref, wo_ref, *rest):
    o_ref = rest[-1]
    x = x_ref[...]
    for a_ref, w_ref in zip(rest[:-1:2], rest[1:-1:2]):
        x = x + _bdot(a_ref[...], w_ref[...])
    h = _rms_rows(x, g_ref[...])
    q = _bdot(h, wq_ref[...])
    qg = qg_ref[...] * (MEM_HEAD ** -0.5)
    outs = []
    for hd in range(q.shape[1] // MEM_HEAD):
        sl = slice(hd * MEM_HEAD, (hd + 1) * MEM_HEAD)
        qn = _rms_rows(q[:, sl], qg)
        s = _bdot_nt(qn, k_ref[0, :, sl])
        p = jnp.exp(s - jnp.max(s, axis=-1, keepdims=True))
        l = jnp.sum(p, axis=-1, keepdims=True)
        outs.append(_bdot(p, v_ref[0, :, sl]) / l)
    o = jnp.concatenate(outs, axis=-1)
    o_ref[...] = x + _bdot(o, wo_ref[...])


def mem_attn(x, pairs, g, wq, q_gain, k, v, wo, rows_per_seq, tm):
    m, d = x.shape
    n_mem = k.shape[1]
    per = rows_per_seq // tm
    extra_specs, extra = [], []
    for a, w in pairs:
        extra_specs += [pl.BlockSpec((tm, a.shape[1]), lambda i: (i, 0)), pl.BlockSpec(w.shape, lambda i: (0, 0))]
        extra += [a, w]
    return pl.pallas_call(
        _mem_attn_kernel,
        grid=(m // tm,),
        in_specs=[pl.BlockSpec((tm, d), lambda i: (i, 0)),
                  pl.BlockSpec((1, d), lambda i: (0, 0)),
                  pl.BlockSpec((d, d), lambda i: (0, 0)),
                  pl.BlockSpec((1, MEM_HEAD), lambda i: (0, 0)),
                  pl.BlockSpec((1, n_mem, d), lambda i: (i // per, 0, 0)),
                  pl.BlockSpec((1, n_mem, d), lambda i: (i // per, 0, 0)),
                  pl.BlockSpec((d, d), lambda i: (0, 0))] + extra_specs,
        out_specs=pl.BlockSpec((tm, d), lambda i: (i, 0)),
        out_shape=jax.ShapeDtypeStruct((m, d), f32),
        compiler_params=_params(("parallel",)),
        name="mem_attn",
    )(x, g.reshape(1, d), wq, q_gain.reshape(1, MEM_HEAD), k, v, wo, *extra)


def _gelu(x):
    return 0.5 * x * (1.0 + jnp.tanh(0.7978845608028654 * (x + 0.044715 * x * x * x)))


def _mixer_a_kernel(z_ref, vg_ref, w_ref, b_ref, ya_ref, *, width):
    ge = _gelu(z_ref[...])
    tm = ge.shape[0]
    row = lax.broadcasted_iota(jnp.int32, (A_GROUP, A_GROUP), 0)
    col = lax.broadcasted_iota(jnp.int32, (A_GROUP, A_GROUP), 1)
    for g in range(width // A_GROUP):
        u = ge[:, g * A_GROUP:(g + 1) * A_GROUP]
        v = ge[:, width + g * A_GROUP:width + (g + 1) * A_GROUP]
        va = _rms_rows(v, vg_ref[g:g + 1, :]).astype(bf16)
        w = jnp.where(row >= col, w_ref[g], 0.0).astype(bf16)
        for c in range(tm // A_GROUP):
            rs = slice(c * A_GROUP, (c + 1) * A_GROUP)
            s = jnp.dot(w, va[rs], preferred_element_type=f32) + b_ref[g]
            ya_ref[rs, g * A_GROUP:(g + 1) * A_GROUP] = (u[rs] * s).astype(ya_ref.dtype)


def mixer_a(z, v_gain, w_s, b_s, width, tm):
    m = z.shape[0]
    ng = width // A_GROUP
    b_rows = jnp.broadcast_to(b_s[:, :, None], (ng, A_GROUP, A_GROUP))
    return pl.pallas_call(
        functools.partial(_mixer_a_kernel, width=width),
        grid=(m // tm,),
        in_specs=[pl.BlockSpec((tm, 2 * width), lambda i: (i, 0)),
                  pl.BlockSpec((ng, A_GROUP), lambda i: (0, 0)),
                  pl.BlockSpec((ng, A_GROUP, A_GROUP), lambda i: (0, 0, 0)),
                  pl.BlockSpec((ng, A_GROUP, A_GROUP), lambda i: (0, 0, 0))],
        out_specs=pl.BlockSpec((tm, width), lambda i: (i, 0)),
        out_shape=jax.ShapeDtypeStruct((m, width), bf16),
        compiler_params=_params(("parallel",)),
        name="mixer_a",
    )(z, v_gain, w_s, b_rows)


def _mixer_a_step_kernel(z_ref, vg_ref, w0_ref, b0_ref, ya_ref, va_ref, *, width):
    ge = _gelu(z_ref[...])
    for g in range(width // A_GROUP):
        sl = slice(g * A_GROUP, (g + 1) * A_GROUP)
        u = ge[:, sl]
        v = ge[:, width + g * A_GROUP:width + (g + 1) * A_GROUP]
        va = _rms_rows(v, vg_ref[g:g + 1, :])
        va_ref[:, sl] = va
        ya_ref[:, sl] = (u * (w0_ref[:, sl] * va + b0_ref[:, sl])).astype(ya_ref.dtype)


def mixer_a_step(z, v_gain, w_s, b_s, width):
    m = z.shape[0]
    w0 = jnp.repeat(w_s[:, 0, 0], A_GROUP).reshape(1, width)
    b0 = jnp.repeat(b_s[:, 0], A_GROUP).reshape(1, width)
    return pl.pallas_call(
        functools.partial(_mixer_a_step_kernel, width=width),
        grid=(1,),
        in_specs=[pl.BlockSpec((m, 2 * width), lambda i: (0, 0)),
                  pl.BlockSpec(v_gain.shape, lambda i: (0, 0)),
                  pl.BlockSpec((1, width), lambda i: (0, 0)),
                  pl.BlockSpec((1, width), lambda i: (0, 0))],
        out_specs=[pl.BlockSpec((m, width), lambda i: (0, 0))] * 2,
        out_shape=[jax.ShapeDtypeStruct((m, width), bf16), jax.ShapeDtypeStruct((m, width), f32)],
        compiler_params=_params(("arbitrary",)),
        name="mixer_a_step",
    )(z, v_gain, w0, b0)


def _split3(x):
    hi = x.astype(bf16)
    r1 = x - hi.astype(f32)
    mid = r1.astype(bf16)
    lo = (r1 - mid.astype(f32)).astype(bf16)
    return hi, mid, lo


def _dot01_left(sel, x):
    hi, mid, lo = _split3(x)
    d = lambda p: jnp.dot(sel, p, preferred_element_type=f32)
    return d(hi) + d(mid) + d(lo)


def _dot01_right(x, sel):
    hi, mid, lo = _split3(x)
    d = lambda p: jnp.dot(p, sel, preferred_element_type=f32)
    return d(hi) + d(mid) + d(lo)


def _log_sigmoid(x):
    return jnp.minimum(x, 0.0) - jnp.log(1.0 + jnp.exp(-jnp.abs(x)))


AUG = 2 * LANES
SHIFT_LANE = 12


def _fox_project(x_ref, g_ref, w_ref, wf_ref, bf_ref, qg_ref, kg_ref, width, unit):
    h = _rms_rows(x_ref[...], g_ref[...]).astype(bf16)
    z = jnp.dot(h, w_ref[...], preferred_element_type=f32)
    q, k = z[:, :width], z[:, width:2 * width]
    v, g = z[:, 2 * width:3 * width], z[:, 3 * width:4 * width]
    qn = q * lax.rsqrt(_per_head_sums(q * q) * (1.0 / HEAD) + RMS_EPS) * (qg_ref[...] * (HEAD ** -0.5 * unit))
    kn = k * lax.rsqrt(_per_head_sums(k * k) * (1.0 / HEAD) + RMS_EPS) * kg_ref[...]
    lf = _log_sigmoid(jnp.dot(h, wf_ref[...], preferred_element_type=f32) + bf_ref[...])
    return qn, kn, v, g, lf


def _fox_in_args(x, g, w, wf, b_f, q_gain, k_gain):
    d = x.shape[1]
    width = w.shape[1] // 4
    n_heads = width // HEAD
    wf_pad = jnp.zeros((d, LANES), bf16).at[:, :n_heads].set(wf)
    bf_pad = jnp.zeros((1, LANES), f32).at[0, :n_heads].set(b_f)
    qg = jnp.tile(q_gain, n_heads).reshape(1, width)
    kg = jnp.tile(k_gain, n_heads).reshape(1, width)
    const = lambda i: (0, 0)
    specs = [pl.BlockSpec((1, d), const), pl.BlockSpec(w.shape, const), pl.BlockSpec((d, LANES), const),
             pl.BlockSpec((1, LANES), const), pl.BlockSpec((1, width), const), pl.BlockSpec((1, width), const)]
    return specs, (g.reshape(1, d), w, wf_pad, bf_pad, qg, kg)


def _fox_step_in_kernel(x_ref, g_ref, w_ref, wf_ref, bf_ref, qg_ref, kg_ref,
                        q_ref, k_ref, v_ref, sg_ref, lf_ref, *, width, n_heads):
    qn, kn, v, g, lf = _fox_project(x_ref, g_ref, w_ref, wf_ref, bf_ref, qg_ref, kg_ref, width, 1.0)
    q_ref[...] = qn.astype(bf16).astype(f32)
    k_ref[...] = kn
    v_ref[...] = v
    sg_ref[...] = _sigmoid(g)
    lf_ref[...] = lf[:, :n_heads]


def fox_step_in(x, g, w, wf, b_f, q_gain, k_gain):
    m, d = x.shape
    width = w.shape[1] // 4
    n_heads = width // HEAD
    specs, args = _fox_in_args(x, g, w, wf, b_f, q_gain, k_gain)
    whole = lambda n: pl.BlockSpec((m, n), lambda i: (0, 0))
    return pl.pallas_call(
        functools.partial(_fox_step_in_kernel, width=width, n_heads=n_heads),
        grid=(1,),
        in_specs=[whole(d)] + specs,
        out_specs=[whole(width)] * 4 + [whole(n_heads)],
        out_shape=[jax.ShapeDtypeStruct((m, width), f32)] * 4 + [jax.ShapeDtypeStruct((m, n_heads), f32)],
        compiler_params=_params(("arbitrary",)),
        name="fox_step_in",
    )(x, *args)


def _fox_seq_in_kernel(x_ref, g_ref, w_ref, wf_ref, bf_ref, qg_ref, kg_ref,
                       q_ref, kt_ref, ka_ref, qa_ref, vt_ref, vtb_ref, sg_ref, lft_ref, c_ref, carry_ref,
                       *, tiles_per_seq, width, n_heads):
    i = pl.program_id(0)
    qn, kn, v, g, lf = _fox_project(x_ref, g_ref, w_ref, wf_ref, bf_ref, qg_ref, kg_ref, width, LOG2E)
    tm = lf.shape[0]
    q_ref[...] = qn.astype(bf16)
    sg_ref[...] = _sigmoid(g)
    for blk in range(width // LANES):
        sl = slice(blk * LANES, (blk + 1) * LANES)
        kt_ref[0, sl, :] = kn[:, sl].T
        vt = v[:, sl].T
        vt_ref[0, sl, :] = vt
        vtb_ref[0, 0, sl, :] = vt.astype(bf16)
    lft_ref[0] = lf.T[:n_heads, :]

    @pl.when(i % tiles_per_seq == 0)
    def _():
        carry_ref[...] = jnp.zeros_like(carry_ref)

    row = lax.broadcasted_iota(jnp.int32, (tm, tm), 0)
    col = lax.broadcasted_iota(jnp.int32, (tm, tm), 1)
    tri = jnp.where(row >= col, 1.0, 0.0).astype(bf16)
    c = _dot01_left(tri, lf) + carry_ref[...]
    carry_ref[...] = c[tm - 1:tm, :]
    c = c * LOG2E
    c_ref[...] = c
    pieces = jnp.concatenate(_split3(c), axis=-1)
    rr = lax.broadcasted_iota(jnp.int32, (3 * LANES, LANES), 0)
    cc = lax.broadcasted_iota(jnp.int32, (3 * LANES, LANES), 1)
    piece, head = rr // LANES, rr % LANES
    lane = _lane_iota((1, LANES))
    k_const = jnp.where((lane < 6) | (lane == SHIFT_LANE), 1.0, 0.0)
    q_const = jnp.where((lane >= 6) & (lane < 12), -1.0, 0.0)
    knb = kn.astype(bf16)
    for p in range(n_heads // 2):
        place = lambda base: jnp.where(((head == 2 * p) & (cc == base + piece))
                                       | ((head == 2 * p + 1) & (cc == base + 3 + piece)), 1.0, 0.0).astype(bf16)
        k_aug = jnp.dot(pieces, place(6), preferred_element_type=f32) + k_const
        ka_ref[p] = jnp.concatenate([knb[:, p * LANES:(p + 1) * LANES], k_aug.astype(bf16)], axis=-1)
        qa_ref[p] = (jnp.dot(pieces, place(0), preferred_element_type=f32) + q_const).astype(bf16)


def fox_seq_in(x, g, w, wf, b_f, q_gain, k_gain, n_seq, tm):
    m, d = x.shape
    width = w.shape[1] // 4
    n_heads = width // HEAD
    n_pairs = n_heads // 2
    seq = m // n_seq
    tps = seq // tm
    specs, args = _fox_in_args(x, g, w, wf, b_f, q_gain, k_gain)
    row = lambda i: (i, 0)
    chan = lambda i: (i // tps, 0, i % tps)
    return pl.pallas_call(
        functools.partial(_fox_seq_in_kernel, tiles_per_seq=tps, width=width, n_heads=n_heads),
        grid=(m // tm,),
        in_specs=[pl.BlockSpec((tm, d), row)] + specs,
        out_specs=[pl.BlockSpec((tm, width), row),
                   pl.BlockSpec((1, width, tm), chan),
                   pl.BlockSpec((n_pairs, tm, AUG), lambda i: (0, i, 0)),
                   pl.BlockSpec((n_pairs, tm, LANES), lambda i: (0, i, 0)),
                   pl.BlockSpec((1, width, tm), chan),
                   pl.BlockSpec((1, 1, width, tm), lambda i: (i // tps, i % tps, 0, 0)),
                   pl.BlockSpec((tm, width), row),
                   pl.BlockSpec((1, n_heads, tm), chan),
                   pl.BlockSpec((tm, LANES), row)],
        out_shape=[jax.ShapeDtypeStruct((m, width), bf16),
                   jax.ShapeDtypeStruct((n_seq, width, seq), f32),
                   jax.ShapeDtypeStruct((n_pairs, m, AUG), bf16),
                   jax.ShapeDtypeStruct((n_pairs, m, LANES), bf16),
                   jax.ShapeDtypeStruct((n_seq, width, seq), f32),
                   jax.ShapeDtypeStruct((n_seq, tps, width, tm), bf16),
                   jax.ShapeDtypeStruct((m, width), f32),
                   jax.ShapeDtypeStruct((n_seq, n_heads, seq), f32),
                   jax.ShapeDtypeStruct((m, LANES), f32)],
        scratch_shapes=[pltpu.VMEM((1, LANES), f32)],
        compiler_params=_params(("arbitrary",)),
        name="fox_seq_in",
    )(x, *args)


def _fox_flash_kernel(lo_ref, qk_ref, q_ref, qa_ref, ka_ref, vt_ref, sg_ref, o_ref, *, tq, tv, bounded):
    b, p, qi = pl.program_id(0), pl.program_id(1), pl.program_id(2)
    nsub = tq // tv
    q2 = q_ref[0]
    lo = _lane_iota((tq, LANES)) < HEAD
    zero = jnp.zeros_like(q2)
    bias2 = qa_ref[0]
    lane = _lane_iota((tq, LANES))
    shift = jnp.where(lane == SHIFT_LANE, -qk_ref[0], 0.0).astype(bf16)
    qa = []
    for j in (0, 1):
        own = ((lane >= 3 * j) & (lane < 3 * j + 3)) | ((lane >= 6 + 3 * j) & (lane < 9 + 3 * j))
        bias = jnp.where(own, bias2, zero)
        if bounded:
            bias = jnp.where(lane == SHIFT_LANE, shift, bias)
        qh = jnp.where(lo, q2, zero) if j == 0 else jnp.where(lo, zero, q2)
        qa.append(jnp.concatenate([qh, bias], axis=-1))
    row = lax.broadcasted_iota(jnp.int32, (tq, tq), 0)
    col = lax.broadcasted_iota(jnp.int32, (tq, tq), 1)

    def scores(kb, j):
        ka = ka_ref[0, 0, pl.ds(pl.multiple_of(kb * tq, tq), tq), :]
        return _bdot_nt(ka, qa[j])

    def softmax_pv(kb, j, t, stats, masked):
        m_old, l_old, acc = stats
        if masked:
            t = jnp.where(row <= col, t, -jnp.inf)
        m_new = jnp.maximum(m_old, jnp.max(t, axis=0, keepdims=True))
        pe = jnp.exp2(t - m_new).astype(bf16)
        alpha = jnp.exp2(m_old - m_new)
        l_new = alpha * l_old + jnp.sum(pe.astype(f32), axis=0, keepdims=True)
        pv = sum(jnp.dot(vt_ref[0, kb * nsub + s, j * HEAD:(j + 1) * HEAD, :], pe[s * tv:(s + 1) * tv],
                         preferred_element_type=f32) for s in range(nsub))
        return m_new, l_new, acc * alpha + pv

    def block(kb, t0, st0, st1, masked, last):
        t1 = scores(kb, 1)
        st0 = softmax_pv(kb, 0, t0, st0, masked)
        t0_next = t0 if last else scores(kb + 1, 0)
        st1 = softmax_pv(kb, 1, t1, st1, masked)
        return t0_next, st0, st1

    def pv(kb, j, pe):
        return sum(jnp.dot(vt_ref[0, kb * nsub + s, j * HEAD:(j + 1) * HEAD, :], pe[s * tv:(s + 1) * tv],
                   preferred_element_type=f32) for s in range(nsub))

    def free_blocks(kbs, sts):
        units = [(kb, j) for kb in kbs for j in (0, 1)]
        l = [sts[0][1], sts[1][1]]
        acc = [sts[0][2], sts[1][2]]
        t_next = scores(*units[0])
        for n, (kb, j) in enumerate(units):
            t = t_next
            if n + 1 < len(units):
                t_next = scores(*units[n + 1])
            pe = jnp.exp2(jnp.where((row <= col) | (kb < qi), t, -jnp.inf))
            l[j] = l[j] + jnp.sum(pe, axis=0, keepdims=True)
            acc[j] = acc[j] + pv(kb, j, pe.astype(bf16))
        return (sts[0][0], l[0], acc[0]), (sts[1][0], l[1], acc[1])

    init = (jnp.full((1, tq), -jnp.inf, f32), jnp.zeros((1, tq), f32), jnp.zeros((HEAD, tq), f32))
    first = lo_ref[b, p, qi]
    if bounded:
        odd = (qi - first + 1) % 2
        sts = lax.cond(odd == 1, lambda s: free_blocks((first,), s), lambda s: s, (init, init))
        (m0, l0, a0), (m1, l1, a1) = lax.fori_loop(
            0, (qi - first + 1) // 2,
            lambda i, s: free_blocks((first + odd + 2 * i, first + odd + 2 * i + 1), s), sts)
    else:
        state = lax.fori_loop(first, qi, lambda kb, s: block(kb, *s, False, False),
                              (scores(first, 0), init, init))
        _, (m0, l0, a0), (m1, l1, a1) = block(qi, *state, True, True)
    o = jnp.concatenate([a0 / l0, a1 / l1], axis=0).T
    o_ref[0] = (o * sg_ref[0]).astype(o_ref.dtype)


QK_BOUNDED_MAX = 40.0


def fox_flash(q, qa, ka, vtb, sg, first_block, qk, n_seq, tq):
    m, width = q.shape
    seq = m // n_seq
    n_pairs = width // LANES
    tv = vtb.shape[3]
    r3 = lambda a: a.reshape(n_seq, seq, a.shape[1])
    tile = pl.BlockSpec((1, tq, LANES), lambda b, p, i, lo, qk_: (b, i, p))
    grid_spec = pltpu.PrefetchScalarGridSpec(
        num_scalar_prefetch=2,
        grid=(n_seq, n_pairs, seq // tq),
        in_specs=[tile,
                  pl.BlockSpec((1, tq, LANES), lambda b, p, i, lo, qk_: (p, b * (seq // tq) + i, 0)),
                  pl.BlockSpec((1, 1, seq, AUG), lambda b, p, i, lo, qk_: (p, b, 0, 0)),
                  pl.BlockSpec((1, seq // tv, LANES, tv), lambda b, p, i, lo, qk_: (b, 0, p, 0)),
                  tile],
        out_specs=tile,
    )
    args = (first_block, qk.reshape(1).astype(f32), r3(q), qa, ka.reshape(n_pairs, n_seq, seq, AUG), vtb,
            r3(sg))
    call = lambda bounded: pl.pallas_call(
        functools.partial(_fox_flash_kernel, tq=tq, tv=tv, bounded=bounded),
        grid_spec=grid_spec,
        out_shape=jax.ShapeDtypeStruct((n_seq, seq, width), bf16),
        compiler_params=_params(("parallel", "parallel", "arbitrary")),
        name="fox_flash_bounded" if bounded else "fox_flash",
    )
    out = lax.cond(qk <= QK_BOUNDED_MAX, lambda a: call(True)(*a), lambda a: call(False)(*a), args)
    return out.reshape(m, width)


def _fox_step_kernel(pt_ref, qc_ref, knt_ref, vnt_ref, lfn_ref, sg_ref, *rest, n_heads, pps):
    kt_refs, vt_refs, lf_refs = rest[:pps], rest[pps:2 * pps], rest[2 * pps:3 * pps]
    o_ref, m_ref, l_ref, acc_ref, carry_ref = rest[3 * pps:]
    pg = pl.program_id(1)
    page = lf_refs[0].shape[2]

    def attend(kt_ref, vt_ref, bias):
        rows = [jnp.sum(kt_ref[0, h] * qc_ref[0, h], axis=0, keepdims=True) for h in range(n_heads)]
        t = jnp.concatenate(rows, axis=0) + bias
        m_old = m_ref[...]
        m_new = jnp.maximum(m_old, jnp.max(t, axis=-1, keepdims=True))
        alpha = jnp.exp(m_old - m_new)
        pe = jnp.exp(t - m_new)
        l_ref[...] = alpha * l_ref[...] + jnp.sum(pe, axis=-1, keepdims=True)
        m_ref[...] = m_new
        for h in range(n_heads):
            acc_ref[h] = acc_ref[h] * alpha[h:h + 1, :] + pe[h:h + 1, :] * vt_ref[0, h]

    @pl.when(pg == 0)
    def _():
        m_ref[...] = jnp.full(m_ref.shape, -jnp.inf, f32)
        l_ref[...] = jnp.zeros_like(l_ref)
        acc_ref[...] = jnp.zeros_like(acc_ref)
        carry_ref[...] = lfn_ref[0]
        attend(knt_ref, vnt_ref, jnp.where(_lane_iota((n_heads, page)) == 0, 0.0, -jnp.inf))

    r = lax.broadcasted_iota(jnp.int32, (page, page), 0)
    cidx = lax.broadcasted_iota(jnp.int32, (page, page), 1)
    later = jnp.where(r > cidx, 1.0, 0.0).astype(bf16)
    for j in range(pps):
        lf = lf_refs[j][0]
        attend(kt_refs[j], vt_refs[j], _dot01_right(lf, later) + carry_ref[...])
        carry_ref[...] = carry_ref[...] + jnp.sum(lf, axis=-1, keepdims=True)

    @pl.when(pg == pl.num_programs(1) - 1)
    def _():
        l = l_ref[...]
        for h in range(n_heads):
            o_ref[0, h] = jnp.sum(acc_ref[h], axis=-1, keepdims=True) / l[h:h + 1, :] * sg_ref[0, h]


FOX_PAGES_PER_STEP = 8


def fox_step(q, k_new, v_new, lf_new, sg, cache_k, cache_v, cache_lf, page_table):
    b, width = q.shape
    pool, page, n_heads = cache_lf.shape
    n_pages = page_table.shape[1]
    pps = FOX_PAGES_PER_STEP
    kt = jnp.transpose(cache_k, (0, 2, 3, 1))
    vt = jnp.transpose(cache_v, (0, 2, 3, 1))
    lft = jnp.transpose(cache_lf, (0, 2, 1))
    col = lambda a: a.reshape(b, n_heads, HEAD, 1)
    own_page = lambda a: jnp.zeros((b, n_heads, HEAD, page), f32).at[..., 0].set(a.reshape(b, n_heads, HEAD))
    qc = jnp.broadcast_to(col(q), (b, n_heads, HEAD, page))
    per_seq = lambda shape: pl.BlockSpec((1,) + shape, lambda i, g, pt: (i,) + (0,) * len(shape))
    page_spec = lambda shape, j: pl.BlockSpec(
        (1,) + shape, lambda i, g, pt: (pt[i, n_pages - 1 - (g * pps + j)],) + (0,) * len(shape))
    grid_spec = pltpu.PrefetchScalarGridSpec(
        num_scalar_prefetch=1,
        grid=(b, n_pages // pps),
        in_specs=[per_seq((n_heads, HEAD, page))] * 3 + [per_seq((n_heads, 1)), per_seq((n_heads, HEAD, 1))]
                 + [page_spec((n_heads, HEAD, page), j) for j in range(pps)] * 2
                 + [page_spec((n_heads, page), j) for j in range(pps)],
        out_specs=per_seq((n_heads, HEAD, 1)),
        scratch_shapes=[pltpu.VMEM((n_heads, 1), f32), pltpu.VMEM((n_heads, 1), f32),
                        pltpu.VMEM((n_heads, HEAD, page), f32), pltpu.VMEM((n_heads, 1), f32)],
    )
    out = pl.pallas_call(
        functools.partial(_fox_step_kernel, n_heads=n_heads, pps=pps),
        grid_spec=grid_spec,
        out_shape=jax.ShapeDtypeStruct((b, n_heads, HEAD, 1), f32),
        compiler_params=_params(("parallel", "arbitrary")),
        name="fox_step",
    )(page_table, qc, own_page(k_new), own_page(v_new), lf_new.reshape(b, n_heads, 1), col(sg),
      *([kt] * pps), *([vt] * pps), *([lft] * pps))
    return out.reshape(b, width).astype(bf16)


def _mem_attn_step_kernel(q_ref, qg_ref, k_ref, v_ref, o_ref):
    q = q_ref[0]
    qn = q * lax.rsqrt(jnp.mean(q * q, axis=-1, keepdims=True) + RMS_EPS) * (qg_ref[...] * MEM_HEAD ** -0.5)
    s = jnp.sum(k_ref[0, 0] * qn, axis=-1, keepdims=True)
    p = jnp.exp(s - jnp.max(s, axis=0, keepdims=True))
    l = jnp.sum(p, axis=0, keepdims=True)
    o_ref[0] = jnp.sum(p * v_ref[0, 0], axis=0, keepdims=True) / l


def mem_attn_step(q, q_gain, cache_k, cache_v, layer):
    b, d = q.shape
    _, _, n_mem, heads, hd = cache_k.shape
    mem_spec = pl.BlockSpec((1, 1, n_mem, heads, hd), lambda i: (layer, i, 0, 0, 0))
    out = pl.pallas_call(
        _mem_attn_step_kernel,
        grid=(b,),
        in_specs=[pl.BlockSpec((1, 1, heads, hd), lambda i: (i, 0, 0, 0)),
                  pl.BlockSpec((1, 1, hd), lambda i: (0, 0, 0)),
                  mem_spec, mem_spec],
        out_specs=pl.BlockSpec((1, 1, heads, hd), lambda i: (i, 0, 0, 0)),
        out_shape=jax.ShapeDtypeStruct((b, 1, heads, hd), f32),
        compiler_params=_params(("parallel",)),
        name="mem_attn_step",
    )(q.reshape(b, 1, heads, hd), q_gain.reshape(1, 1, hd), cache_k, cache_v)
    return out.reshape(b, d)


def _softplus(x):
    return jnp.maximum(x, 0.0) + jnp.log(1.0 + jnp.exp(-jnp.abs(x)))


def _rwkv_kernel(zr_ref, zk_ref, zv_ref, zg_ref, zwa_ref, sh0_ref, s0_ref,
                 mu_ref, w0_ref, a0_ref, kk_ref, ka_ref, rk_ref, lnw_ref, lnb_ref, wup_ref, aup_ref,
                 y_ref, sout_ref, shout_ref,
                 s_ref, carry_ref, r_s, k_s, v_s, al_s, be_s, ld_s, yo_s, bo_s, sg_s,
                 *, chunk, n_valid, width):
    i = pl.program_id(1)
    nblk = pl.num_programs(1)
    nsq, cg = zr_ref.shape[0], zr_ref.shape[1]
    n_pairs = width // LANES

    @pl.when(i == 0)
    def _():
        s_ref[...] = s0_ref[...]
        carry_ref[...] = sh0_ref[...]

    rows = lax.broadcasted_iota(jnp.int32, (cg, 1), 0)
    valid = (rows + i * cg) < n_valid
    last = jnp.minimum(n_valid - 1 - i * cg, cg - 1)

    for s in range(nsq):
        def shifted(z, lane0, n):
            prev = jnp.where(rows == 0, carry_ref[s, :, lane0:lane0 + n], pltpu.roll(z, 1, 0))
            return z + (prev - z) * mu_ref[:, lane0:lane0 + n]

        zr, zk, zv, zg, zwa = zr_ref[s], zk_ref[s], zv_ref[s], zg_ref[s], zwa_ref[s]
        r = shifted(zr, 0, width)
        k = shifted(zk, width, width)
        v = shifted(zv, 2 * width, width)
        gp = shifted(zg, 3 * width, width)
        wa = shifted(zwa, 4 * width, LANES)

        @pl.when(last >= 0)
        def _():
            for z, lane0, n in ((zr, 0, width), (zk, width, width), (zv, 2 * width, width),
                                (zg, 3 * width, width), (zwa, 4 * width, LANES)):
                carry_ref[s, :, lane0:lane0 + n] = jnp.sum(jnp.where(rows == last, z, 0.0), axis=0,
                                                           keepdims=True)

        w = -_softplus(-(w0_ref[...] + _bdot(jnp.tanh(wa), wup_ref[...]))) - 0.5
        logd = -jnp.exp(w)
        a = _sigmoid(a0_ref[...] + _bdot(wa, aup_ref[...]))
        kk = k * kk_ref[...]
        kk = kk * lax.rsqrt(_per_head_sums(kk * kk) + L2_EPS)
        k2 = k * (1.0 + (a - 1.0) * ka_ref[...])
        bo_s[s] = _per_head_sums(r * k2 * rk_ref[...]) * v
        sg_s[s] = _sigmoid(gp)
        r_s[s] = r
        k_s[s] = jnp.where(valid, k2, 0.0)
        v_s[s] = jnp.where(valid, v, 0.0)
        al_s[s] = -kk
        be_s[s] = jnp.where(valid, kk * a, 0.0)
        ld_s[s] = jnp.where(valid, logd, 0.0)

    ri = lax.broadcasted_iota(jnp.int32, (chunk, chunk), 0)
    ci = lax.broadcasted_iota(jnp.int32, (chunk, chunk), 1)
    tri = jnp.where(ri >= ci, 1.0, 0.0).astype(bf16)
    lower, strict = ri >= ci, ri > ci
    eye = jnp.where(ri == ci, 1.0, 0.0)
    lo = _lane_iota((1, LANES)) < HEAD
    sr = lax.broadcasted_iota(jnp.int32, (LANES, LANES), 0) < HEAD
    sc = lax.broadcasted_iota(jnp.int32, (LANES, LANES), 1) < HEAD
    same_head = sr == sc
    n_sq = max(int(math.log2(chunk)) - 1, 0)

    def chunk_body(c, _):
        rs = pl.ds(pl.multiple_of(c * chunk, chunk), chunk)
        pairs = [(s, p) for s in range(nsq) for p in range(n_pairs)]
        heads = [(s, p, j) for s, p in pairs for j in (0, 1)]
        lsl = [slice(p * LANES, (p + 1) * LANES) for p in range(n_pairs)]
        xs, vv, vp, yb, yk, ybe, yke, d_end = {}, {}, {}, {}, {}, {}, {}, {}
        for s in range(nsq):
            ld = ld_s[s, rs, :]
            lcum = _dot01_left(tri, ld)
            lend = lcum[chunk - 1:chunk, :]
            e_neg, e_end = jnp.exp(-lcum), jnp.exp(lend - lcum)
            xa_all, xr_all = al_s[s, rs, :] * jnp.exp(lcum - ld), r_s[s, rs, :] * jnp.exp(lcum)
            be, kc, v_all = be_s[s, rs, :], k_s[s, rs, :], v_s[s, rs, :]
            dec = jnp.exp(lend)
            for p in range(n_pairs):
                xs[s, p] = jnp.concatenate([xa_all[:, lsl[p]], xr_all[:, lsl[p]]], axis=0)
                vv[s, p] = v_all[:, lsl[p]]
                vp[s, p] = vv[s, p].astype(bf16)
                yb[s, p], yk[s, p] = (be * e_neg)[:, lsl[p]], (kc * e_neg)[:, lsl[p]]
                ybe[s, p], yke[s, p] = (be * e_end)[:, lsl[p]], (kc * e_end)[:, lsl[p]]
                d_end[s, p] = dec[:, lsl[p]]
        s_bd = {sp: s_ref[sp[0], sp[1]] for sp in pairs}
        a_ab, a_ak, a_rb, a_rk = {}, {}, {}, {}
        for s, p, j in heads:
            xm = jnp.where(lo if j == 0 else jnp.logical_not(lo), xs[s, p], 0.0).astype(bf16)
            gb = _bdot_nt(xm, yb[s, p])
            gk = _bdot_nt(xm, yk[s, p])
            a_ab[s, p, j] = jnp.where(strict, gb[:chunk], 0.0)
            a_ak[s, p, j] = jnp.where(strict, gk[:chunk], 0.0)
            a_rb[s, p, j] = jnp.where(lower, gb[chunk:], 0.0)
            a_rk[s, p, j] = jnp.where(lower, gk[chunk:], 0.0)
        tinv = {h: eye + a_ab[h] for h in heads}
        pw = dict(a_ab)
        for _ in range(n_sq):
            for h in heads:
                pw[h] = _bdot(pw[h], pw[h])
            for h in heads:
                tinv[h] = tinv[h] + _bdot(tinv[h], pw[h])
        xh = {sp: _bdot_nt(xs[sp], s_bd[sp]) for sp in pairs}
        av = {(s, p, j): _bdot(a_ak[s, p, j], vp[s, p]) for s, p, j in heads}
        uh = {(s, p, j): _bdot(tinv[s, p, j], xh[s, p][:chunk] + av[s, p, j]) for s, p, j in heads}
        u = {(s, p): jnp.where(lo, uh[s, p, 0], uh[s, p, 1]) for s, p in pairs}
        yh = {(s, p, j): _bdot(a_rb[s, p, j], u[s, p]) + _bdot(a_rk[s, p, j], vp[s, p]) for s, p, j in heads}
        for s, p in pairs:
            yo_s[s, rs, lsl[p]] = xh[s, p][chunk:] + jnp.where(lo, yh[s, p, 0], yh[s, p, 1])
            uv_t = jnp.concatenate([u[s, p], vv[s, p]], axis=0).T
            upd = _bdot(uv_t, jnp.concatenate([ybe[s, p], yke[s, p]], axis=0))
            s_ref[s, p] = s_bd[s, p] * d_end[s, p] + jnp.where(same_head, upd, 0.0)
        return 0

    lax.fori_loop(0, cg // chunk, chunk_body, 0)

    for s in range(nsq):
        y = yo_s[s]
        mean = _per_head_sums(y) * (1.0 / HEAD)
        yc = y - mean
        var = _per_head_sums(yc * yc) * (1.0 / HEAD)
        y = yc * lax.rsqrt(var + GN_EPS) * lnw_ref[...] + lnb_ref[...]
        y_ref[s] = ((y + bo_s[s]) * sg_s[s]).astype(y_ref.dtype)

    @pl.when(i == nblk - 1)
    def _():
        sout_ref[...] = s_ref[...]
        shout_ref[...] = carry_ref[...]


def rwkv7(z, col0, shift0, s0, prm, n_seq, n_valid, chunk, block_rows):
    m = z.shape[0]
    rows = m // n_seq
    nblk = rows // block_rows
    width = prm["b_w0"].shape[0]
    n_heads = width // HEAD
    n_pairs = width // LANES
    bcols = 4 * width + LANES
    s0p = s0.reshape(n_seq, n_pairs, 2, HEAD, HEAD)
    zeros = jnp.zeros_like(s0p[:, :, 0])
    s0bd = jnp.concatenate([jnp.concatenate([s0p[:, :, 0], zeros], axis=-1),
                            jnp.concatenate([zeros, s0p[:, :, 1]], axis=-1)], axis=-2)
    lora = lambda w_, off: jnp.zeros((LANES, width), bf16).at[off:off + w_.shape[0]].set(w_.astype(bf16))
    row1 = lambda a: a.reshape(1, -1).astype(f32)
    cb = col0 // width
    nsq = math.gcd(RWKV_SEQS_PER_STEP, n_seq)
    z3 = z.reshape(n_seq, rows, z.shape[1])
    zspec = lambda j: pl.BlockSpec((nsq, block_rows, width), lambda g, i: (g, i, cb + j))
    const = lambda shape: pl.BlockSpec(shape, lambda g, i: (0,) * len(shape))
    per_seq = lambda shape: pl.BlockSpec((nsq,) + shape, lambda g, i: (g,) + (0,) * len(shape))
    y, sbd, shift = pl.pallas_call(
        functools.partial(_rwkv_kernel, chunk=chunk, n_valid=n_valid, width=width),
        grid=(n_seq // nsq, nblk),
        in_specs=[zspec(0), zspec(1), zspec(2), zspec(3),
                  pl.BlockSpec((nsq, block_rows, LANES), lambda g, i: (g, i, (col0 + 4 * width) // LANES)),
                  per_seq((1, bcols)), per_seq((n_pairs, LANES, LANES)),
                  const((1, bcols))] + [const((1, width))] * 7 + [const((LANES, width))] * 2,
        out_specs=[pl.BlockSpec((nsq, block_rows, width), lambda g, i: (g, i, 0)),
                   per_seq((n_pairs, LANES, LANES)), per_seq((1, bcols))],
        out_shape=[jax.ShapeDtypeStruct((n_seq, rows, width), bf16),
                   jax.ShapeDtypeStruct((n_seq, n_pairs, LANES, LANES), f32),
                   jax.ShapeDtypeStruct((n_seq, 1, bcols), f32)],
        scratch_shapes=[pltpu.VMEM((nsq, n_pairs, LANES, LANES), f32), pltpu.VMEM((nsq, 1, bcols), f32)]
                       + [pltpu.VMEM((nsq, block_rows, width), f32)] * 9,
        compiler_params=_params(("parallel", "arbitrary")),
        name="rwkv7",
    )(z3, z3, z3, z3, z3, shift0.reshape(n_seq, 1, bcols), s0bd,
      row1(prm["b_mu"]), row1(prm["b_w0"]), row1(prm["b_a0"]), row1(prm["b_k_k"]), row1(prm["b_k_a"]),
      row1(prm["b_r_k"]), row1(prm["b_ln_w"]), row1(prm["b_ln_b"]),
      lora(prm["b_w_up"], 0), lora(prm["b_a_up"], HEAD))
    sp = sbd.reshape(n_seq, n_pairs, 2, HEAD, 2, HEAD)
    s_fin = jnp.stack([sp[:, :, 0, :, 0, :], sp[:, :, 1, :, 1, :]], axis=2).reshape(n_seq, n_heads, HEAD, HEAD)
    return y.reshape(m, width), s_fin, shift.reshape(n_seq, bcols)


A_WIDTH = 512
STEP_ROWS = 64
RWKV_CHUNK = 64
RWKV_BLOCK = 128
RWKV_SEQS_PER_STEP = 4
FLASH_TILE = 512
DENSE_TILE = 512
FOX_IN_TILE = 256


UNDERFLOW_LOG2 = 160.0


def _qk_bound(q_gain, k_gain):
    return 1.02 * HEAD ** 0.5 * LOG2E * jnp.max(jnp.abs(q_gain)) * jnp.max(jnp.abs(k_gain))


def _first_key_block(c, qk, n_seq, n_heads, tile):
    c3 =c.reshape(n_seq, -1, c.shape[1])[:, :, :n_heads]
    c_first = c3[:, ::tile]
    c_last = c3[:, tile - 1::tile]
    n = c_first.shape[1]
    dead = (2.0 * qk + c_first[:, :, None, :] - c_last[:, None, :, :]) <= -UNDERFLOW_LOG2
    dead = dead & (jnp.arange(n)[None, None, :, None] < jnp.arange(n)[None, :, None, None])
    dead = dead.reshape(n_seq, n, n, n_heads // 2, 2).all(axis=-1)
    return jnp.transpose(jnp.sum(dead, axis=2), (0, 2, 1)).astype(jnp.int32)


def _trunk(x3, mem_k, mem_v, rwkv_s0, rwkv_shift0, fox_past, p):
    n_seq, seq, d = x3.shape
    m = n_seq * seq
    x = x3.reshape(m, d)
    step = seq == 1
    tm = m if step else DENSE_TILE
    depth = p["norm_mix"].shape[0]
    chunk_v, s_out, shift_out, fk, fv, fl = [], [], [], [], [], []
    ie = io = 0
    for l in range(depth):
        if l % 2 == 0:
            z = norm_matmul(x, p["norm_mix"][l], p["ab_w_in"][ie], tm)
            prm = {k_: p[k_][ie] for k_ in ("b_mu", "b_w0", "b_w_up", "b_a0", "b_a_up", "b_k_k", "b_k_a",
                                           "b_r_k", "b_ln_w", "b_ln_b")}
            if step:
                ya, va = mixer_a_step(z, p["a_v_norm"][ie], p["a_w_s"][ie], p["a_b_s"][ie], A_WIDTH)
                zp = jnp.zeros((n_seq, STEP_ROWS, z.shape[1]), f32).at[:, 0].set(z)
                yb, s_new, sh_new = rwkv7(zp.reshape(n_seq * STEP_ROWS, -1), 2 * A_WIDTH, rwkv_shift0[ie],
                                          rwkv_s0[ie], prm, n_seq, 1, RWKV_CHUNK, STEP_ROWS)
                yb = yb.reshape(n_seq, STEP_ROWS, -1)[:, 0]
                chunk_v.append(va.reshape(n_seq, seq, A_WIDTH))
            else:
                ya = mixer_a(z, p["a_v_norm"][ie], p["a_w_s"][ie], p["a_b_s"][ie], A_WIDTH, tm)
                yb, s_new, sh_new = rwkv7(z, 2 * A_WIDTH, rwkv_shift0[ie], rwkv_s0[ie], prm, n_seq, seq,
                                          RWKV_CHUNK, RWKV_BLOCK)
            w_out = p["ab_w_out"][ie]
            mixed = [(ya, w_out[:A_WIDTH]), (yb, w_out[A_WIDTH:])]
            s_out.append(s_new)
            shift_out.append(sh_new)
            ie += 1
        else:
            w_in = p["c_w_in"][io]
            width = (w_in.shape[1] // 4) // HEAD * HEAD
            n_heads = width // HEAD
            front = (x, p["norm_mix"][l], w_in[:, :4 * width], w_in[:, 4 * width:], p["c_b_f"][io],
                     p["c_q_norm"][io], p["c_k_norm"][io])
            if step:
                q, k, v, sg, lf = fox_step_in(*front)
                ck, cv, cl, pt = fox_past
                og = fox_step(q, k, v, lf, sg, ck[io], cv[io], cl[io], pt)
                fk.append(k.reshape(n_seq, seq, n_heads, HEAD))
                fv.append(v.reshape(n_seq, seq, n_heads, HEAD))
                fl.append(lf.reshape(n_seq, seq, n_heads))
            else:
                q, kt, ka, qa, vt, vtb, sg, lft, c = fox_seq_in(*front, n_seq, FOX_IN_TILE)
                qk = _qk_bound(p["c_q_norm"][io], p["c_k_norm"][io])
                first = _first_key_block(c, qk, n_seq, n_heads, FLASH_TILE)
                og = fox_flash(q, qa, ka, vtb, sg, first, qk, n_seq, FLASH_TILE)
                tokens_first = lambda t: jnp.transpose(t.reshape(n_seq, n_heads, HEAD, seq), (0, 3, 1, 2))
                fk.append(tokens_first(kt))
                fv.append(tokens_first(vt))
                fl.append(jnp.transpose(lft, (0, 2, 1)))
            mixed = [(og, p["c_w_out"][io])]
            io += 1
        if step:
            x = res_matmul(x, mixed, tm)
            q_raw = norm_matmul(x, p["norm_mem"][l], p["m_wq"][l], tm)
            o = mem_attn_step(q_raw, p["m_q_norm"][l], mem_k, mem_v, l)
            x = res_matmul(x, [(o, p["m_wo"][l])], tm)
        else:
            x = mem_attn(x, mixed, p["norm_mem"][l], p["m_wq"][l], p["m_q_norm"][l], mem_k[l].astype(bf16),
                         mem_v[l].astype(bf16), p["m_wo"][l], seq, tm)
        x = ffn(x, p["norm_ffn"][l], p["f_w_in"][l], p["f_w_out"][l], tm)
    stack = lambda xs: jnp.stack(xs)
    return (x.reshape(n_seq, seq, d), (stack(chunk_v) if chunk_v else None), stack(s_out), stack(shift_out),
            stack(fk), stack(fv), stack(fl))


def kernel(x_prompt, x_sample, mem_prompt, cache_mem_k, cache_mem_v, state_rwkv_S, state_rwkv_shift, cache_fox_k, cache_fox_v, cache_fox_logf, page_table, norm_mix, norm_mem, norm_ffn, ab_w_in, ab_w_out, a_v_norm, a_w_s, a_b_s, b_mu, b_w0, b_w_up, b_a0, b_a_up, b_k_k, b_k_a, b_r_k, b_ln_w, b_ln_b, c_w_in, c_b_f, c_q_norm, c_k_norm, c_w_out, m_mem_norm, m_wq, m_wk, m_wv, m_q_norm, m_k_norm, m_wo, f_w_in, f_w_out):
    cast = lambda w: w.astype(bf16)
    p = dict(norm_mix=norm_mix, norm_mem=norm_mem, norm_ffn=norm_ffn, ab_w_in=cast(ab_w_in),
             ab_w_out=cast(ab_w_out), a_v_norm=a_v_norm, a_w_s=a_w_s, a_b_s=a_b_s, b_mu=b_mu, b_w0=b_w0,
             b_w_up=b_w_up, b_a0=b_a0, b_a_up=b_a_up, b_k_k=b_k_k, b_k_a=b_k_a, b_r_k=b_r_k, b_ln_w=b_ln_w,
             b_ln_b=b_ln_b, c_w_in=cast(c_w_in), c_b_f=c_b_f, c_q_norm=c_q_norm, c_k_norm=c_k_norm,
             c_w_out=cast(c_w_out), m_wq=cast(m_wq), m_q_norm=m_q_norm, m_wo=cast(m_wo),
             f_w_in=cast(f_w_in), f_w_out=cast(f_w_out))
    depth, d = norm_mix.shape
    nb, n_mem = mem_prompt.shape[:2]
    n_even, _, n_heads_b, hd, _ = state_rwkv_S.shape
    mem_heads = d // MEM_HEAD

    mem2 = mem_prompt.reshape(nb * n_mem, d)
    kvs = [mem_kv(mem2, m_mem_norm[l], cast(m_wk[l]), cast(m_wv[l]), m_k_norm[l], 256) for l in range(depth)]
    p_mem_k = jnp.stack([kv[0] for kv in kvs]).reshape(depth, nb, n_mem, d)
    p_mem_v = jnp.stack([kv[1] for kv in kvs]).reshape(depth, nb, n_mem, d)
    s0 = jnp.zeros((n_even, nb, n_heads_b, hd, hd), f32)
    shift0 = jnp.zeros((n_even, nb, state_rwkv_shift.shape[2]), f32)
    y_prompt, _, p_rwkv_s, p_rwkv_shift, p_fox_k, p_fox_v, p_fox_logf = _trunk(
        x_prompt, p_mem_k, p_mem_v, s0, shift0, None, p)

    nd = x_sample.shape[0]
    y_sample, s_chunk_v, s_rwkv_s, s_rwkv_shift, s_fox_k, s_fox_v, s_fox_logf = _trunk(
        x_sample, cache_mem_k, cache_mem_v, state_rwkv_S, state_rwkv_shift, (cache_fox_k, cache_fox_v, cache_fox_logf, page_table), p)
    heads5 = lambda a: a.reshape(depth, nb, n_mem, mem_heads, MEM_HEAD)
    return (y_prompt, y_sample, heads5(p_mem_k), heads5(p_mem_v), p_rwkv_s, p_rwkv_shift, p_fox_k, p_fox_v,
            p_fox_logf, s_chunk_v, s_rwkv_s, s_rwkv_shift, s_fox_k, s_fox_v, s_fox_logf)
```

```python
import functools
import math

import jax
import jax.numpy as jnp
from jax import lax
from jax.experimental import pallas as pl
from jax.experimental.pallas import tpu as pltpu

f32 = jnp.float32
bf16 = jnp.bfloat16

RMS_EPS = 1e-6
GN_EPS = 64e-5
L2_EPS = 1e-12
LOG2E = 1.4426950408889634
LANES = 128
HEAD = 64
A_GROUP = 128
MEM_HEAD = 256
VMEM_LIMIT = 56 * 1024 * 1024


def _params(sem):
    return pltpu.CompilerParams(dimension_semantics=sem, vmem_limit_bytes=VMEM_LIMIT)


def _bdot(a, b):
    return jnp.dot(a.astype(bf16), b.astype(bf16), preferred_element_type=f32)


def _bdot_nt(a, b):
    return lax.dot_general(a.astype(bf16), b.astype(bf16), (((1,), (1,)), ((), ())),
                           preferred_element_type=f32)


def _rms_rows(x, g):
    return x * lax.rsqrt(jnp.mean(x * x, axis=-1, keepdims=True) + RMS_EPS) * g


def _sigmoid(x):
    return 1.0 / (1.0 + jnp.exp(-x))


def _lane_iota(shape):
    return lax.broadcasted_iota(jnp.int32, shape, len(shape) - 1)


def _half_sums(x):
    lo = _lane_iota(x.shape) < HEAD
    s_lo = jnp.sum(jnp.where(lo, x, 0.0), axis=-1, keepdims=True)
    s_hi = jnp.sum(jnp.where(lo, 0.0, x), axis=-1, keepdims=True)
    return jnp.where(lo, s_lo, s_hi)


def _per_head_sums(x):
    n = x.shape[-1] // LANES
    return jnp.concatenate([_half_sums(x[:, i * LANES:(i + 1) * LANES]) for i in range(n)], axis=-1)


def _norm_matmul_kernel(x_ref, g_ref, w_ref, o_ref):
    h = _rms_rows(x_ref[...], g_ref[...])
    o_ref[...] = _bdot(h, w_ref[...])


def norm_matmul(x, g, w, tm):
    m, d = x.shape
    n = w.shape[1]
    return pl.pallas_call(
        _norm_matmul_kernel,
        grid=(m // tm,),
        in_specs=[pl.BlockSpec((tm, d), lambda i: (i, 0)),
                  pl.BlockSpec((1, d), lambda i: (0, 0)),
                  pl.BlockSpec((d, n), lambda i: (0, 0))],
        out_specs=pl.BlockSpec((tm, n), lambda i: (i, 0)),
        out_shape=jax.ShapeDtypeStruct((m, n), f32),
        compiler_params=_params(("parallel",)),
        name="norm_matmul",
    )(x, g.reshape(1, d), w)


def _res_matmul_kernel(*refs):
    x_ref, o_ref = refs[0], refs[-1]
    acc = x_ref[...]
    for a_ref, w_ref in zip(refs[1:-1:2], refs[2:-1:2]):
        acc = acc + _bdot(a_ref[...], w_ref[...])
    o_ref[...] = acc


def res_matmul(x, pairs, tm):
    m, d = x.shape
    in_specs = [pl.BlockSpec((tm, d), lambda i: (i, 0))]
    args = [x]
    for a, w in pairs:
        in_specs += [pl.BlockSpec((tm, a.shape[1]), lambda i: (i, 0)),
                     pl.BlockSpec(w.shape, lambda i: (0, 0))]
        args += [a, w]
    return pl.pallas_call(
        _res_matmul_kernel,
        grid=(m // tm,),
        in_specs=in_specs,
        out_specs=pl.BlockSpec((tm, d), lambda i: (i, 0)),
        out_shape=jax.ShapeDtypeStruct((m, d), f32),
        compiler_params=_params(("parallel",)),
        name="res_matmul",
    )(*args)


def _ffn_kernel(x_ref, g_ref, wi_ref, wo_ref, o_ref, *, d_ff, tf):
    x = x_ref[...]
    h = _rms_rows(x, g_ref[...]).astype(bf16)
    acc = x
    for c in range(d_ff // tf):
        gate = jnp.dot(h, wi_ref[:, c * tf:(c + 1) * tf], preferred_element_type=f32)
        up = jnp.dot(h, wi_ref[:, d_ff + c * tf:d_ff + (c + 1) * tf], preferred_element_type=f32)
        act = gate * _sigmoid(gate) * up
        acc = acc + _bdot(act, wo_ref[c * tf:(c + 1) * tf, :])
    o_ref[...] = acc


def ffn(x, g, w_in, w_out, tm):
    m, d = x.shape
    d_ff = w_out.shape[0]
    tf = 256
    return pl.pallas_call(
        functools.partial(_ffn_kernel, d_ff=d_ff, tf=tf),
        grid=(m // tm,),
        in_specs=[pl.BlockSpec((tm, d), lambda i: (i, 0)),
                  pl.BlockSpec((1, d), lambda i: (0, 0)),
                  pl.BlockSpec(w_in.shape, lambda i: (0, 0)),
                  pl.BlockSpec(w_out.shape, lambda i: (0, 0))],
        out_specs=pl.BlockSpec((tm, d), lambda i: (i, 0)),
        out_shape=jax.ShapeDtypeStruct((m, d), f32),
        compiler_params=_params(("parallel",)),
        name="ffn",
    )(x, g.reshape(1, d), w_in, w_out)


def _mem_kv_kernel(m_ref, g_ref, wk_ref, wv_ref, kg_ref, k_ref, v_ref):
    h = _rms_rows(m_ref[...], g_ref[...]).astype(bf16)
    k = jnp.dot(h, wk_ref[...], preferred_element_type=f32)
    kg = kg_ref[...]
    for hd in range(k.shape[1] // MEM_HEAD):
        sl = slice(hd * MEM_HEAD, (hd + 1) * MEM_HEAD)
        k_ref[:, sl] = _rms_rows(k[:, sl], kg)
    v_ref[...] = jnp.dot(h, wv_ref[...], preferred_element_type=f32)


def mem_kv(mem, g, wk, wv, k_gain, tm):
    m, d = mem.shape
    return pl.pallas_call(
        _mem_kv_kernel,
        grid=(m // tm,),
        in_specs=[pl.BlockSpec((tm, d), lambda i: (i, 0)),
                  pl.BlockSpec((1, d), lambda i: (0, 0)),
                  pl.BlockSpec((d, d), lambda i: (0, 0)),
                  pl.BlockSpec((d, d), lambda i: (0, 0)),
                  pl.BlockSpec((1, MEM_HEAD), lambda i: (0, 0))],
        out_specs=[pl.BlockSpec((tm, d), lambda i: (i, 0))] * 2,
        out_shape=[jax.ShapeDtypeStruct((m, d), f32)] * 2,
        compiler_params=_params(("parallel",)),
        name="mem_kv",
    )(mem, g.reshape(1, d), wk, wv, k_gain.reshape(1, MEM_HEAD))


def _mem_attn_kernel(x_ref, g_ref, wq_ref, qg_ref, k_ref, v_ref, wo_ref, *rest):
    o_ref = rest[-1]
    x = x_ref[...]
    for a_ref, w_ref in zip(rest[:-1:2], rest[1:-1:2]):
        x = x + _bdot(a_ref[...], w_ref[...])
    h = _rms_rows(x, g_ref[...])
    q = _bdot(h, wq_ref[...])
    qg = qg_ref[...] * (MEM_HEAD ** -0.5)
    outs = []
    for hd in range(q.shape[1] // MEM_HEAD):
        sl = slice(hd * MEM_HEAD, (hd + 1) * MEM_HEAD)
        qn = _rms_rows(q[:, sl], qg)
        s = _bdot_nt(qn, k_ref[0, :, sl])
        p = jnp.exp(s - jnp.max(s, axis=-1, keepdims=True))
        l = jnp.sum(p, axis=-1, keepdims=True)
        outs.append(_bdot(p, v_ref[0, :, sl]) / l)
    o = jnp.concatenate(outs, axis=-1)
    o_ref[...] = x + _bdot(o, wo_ref[...])


def mem_attn(x, pairs, g, wq, q_gain, k, v, wo, rows_per_seq, tm):
    m, d = x.shape
    n_mem = k.shape[1]
    per = rows_per_seq // tm
    extra_specs, extra = [], []
    for a, w in pairs:
        extra_specs += [pl.BlockSpec((tm, a.shape[1]), lambda i: (i, 0)), pl.BlockSpec(w.shape, lambda i: (0, 0))]
        extra += [a, w]
    return pl.pallas_call(
        _mem_attn_kernel,
        grid=(m // tm,),
        in_specs=[pl.BlockSpec((tm, d), lambda i: (i, 0)),
                  pl.BlockSpec((1, d), lambda i: (0, 0)),
                  pl.BlockSpec((d, d), lambda i: (0, 0)),
                  pl.BlockSpec((1, MEM_HEAD), lambda i: (0, 0)),
                  pl.BlockSpec((1, n_mem, d), lambda i: (i // per, 0, 0)),
                  pl.BlockSpec((1, n_mem, d), lambda i: (i // per, 0, 0)),
                  pl.BlockSpec((d, d), lambda i: (0, 0))] + extra_specs,
        out_specs=pl.BlockSpec((tm, d), lambda i: (i, 0)),
        out_shape=jax.ShapeDtypeStruct((m, d), f32),
        compiler_params=_params(("parallel",)),
        name="mem_attn",
    )(x, g.reshape(1, d), wq, q_gain.reshape(1, MEM_HEAD), k, v, wo, *extra)


def _gelu(x):
    return 0.5 * x * (1.0 + jnp.tanh(0.7978845608028654 * (x + 0.044715 * x * x * x)))


def _mixer_a_kernel(z_ref, vg_ref, w_ref, b_ref, ya_ref, *, width):
    ge = _gelu(z_ref[...])
    tm = ge.shape[0]
    row = lax.broadcasted_iota(jnp.int32, (A_GROUP, A_GROUP), 0)
    col = lax.broadcasted_iota(jnp.int32, (A_GROUP, A_GROUP), 1)
    for g in range(width // A_GROUP):
        u = ge[:, g * A_GROUP:(g + 1) * A_GROUP]
        v = ge[:, width + g * A_GROUP:width + (g + 1) * A_GROUP]
        va = _rms_rows(v, vg_ref[g:g + 1, :]).astype(bf16)
        w = jnp.where(row >= col, w_ref[g], 0.0).astype(bf16)
        for c in range(tm // A_GROUP):
            rs = slice(c * A_GROUP, (c + 1) * A_GROUP)
            s = jnp.dot(w, va[rs], preferred_element_type=f32) + b_ref[g]
            ya_ref[rs, g * A_GROUP:(g + 1) * A_GROUP] = (u[rs] * s).astype(ya_ref.dtype)


def mixer_a(z, v_gain, w_s, b_s, width, tm):
    m = z.shape[0]
    ng = width // A_GROUP
    b_rows = jnp.broadcast_to(b_s[:, :, None], (ng, A_GROUP, A_GROUP))
    return pl.pallas_call(
        functools.partial(_mixer_a_kernel, width=width),
        grid=(m // tm,),
        in_specs=[pl.BlockSpec((tm, 2 * width), lambda i: (i, 0)),
                  pl.BlockSpec((ng, A_GROUP), lambda i: (0, 0)),
                  pl.BlockSpec((ng, A_GROUP, A_GROUP), lambda i: (0, 0, 0)),
                  pl.BlockSpec((ng, A_GROUP, A_GROUP), lambda i: (0, 0, 0))],
        out_specs=pl.BlockSpec((tm, width), lambda i: (i, 0)),
        out_shape=jax.ShapeDtypeStruct((m, width), bf16),
        compiler_params=_params(("parallel",)),
        name="mixer_a",
    )(z, v_gain, w_s, b_rows)


def _mixer_a_step_kernel(z_ref, vg_ref, w0_ref, b0_ref, ya_ref, va_ref, *, width):
    ge = _gelu(z_ref[...])
    for g in range(width // A_GROUP):
        sl = slice(g * A_GROUP, (g + 1) * A_GROUP)
        u = ge[:, sl]
        v = ge[:, width + g * A_GROUP:width + (g + 1) * A_GROUP]
        va = _rms_rows(v, vg_ref[g:g + 1, :])
        va_ref[:, sl] = va
        ya_ref[:, sl] = (u * (w0_ref[:, sl] * va + b0_ref[:, sl])).astype(ya_ref.dtype)


def mixer_a_step(z, v_gain, w_s, b_s, width):
    m = z.shape[0]
    w0 = jnp.repeat(w_s[:, 0, 0], A_GROUP).reshape(1, width)
    b0 = jnp.repeat(b_s[:, 0], A_GROUP).reshape(1, width)
    return pl.pallas_call(
        functools.partial(_mixer_a_step_kernel, width=width),
        grid=(1,),
        in_specs=[pl.BlockSpec((m, 2 * width), lambda i: (0, 0)),
                  pl.BlockSpec(v_gain.shape, lambda i: (0, 0)),
                  pl.BlockSpec((1, width), lambda i: (0, 0)),
                  pl.BlockSpec((1, width), lambda i: (0, 0))],
        out_specs=[pl.BlockSpec((m, width), lambda i: (0, 0))] * 2,
        out_shape=[jax.ShapeDtypeStruct((m, width), bf16), jax.ShapeDtypeStruct((m, width), f32)],
        compiler_params=_params(("arbitrary",)),
        name="mixer_a_step",
    )(z, v_gain, w0, b0)


def _split3(x):
    hi = x.astype(bf16)
    r1 = x - hi.astype(f32)
    mid = r1.astype(bf16)
    lo = (r1 - mid.astype(f32)).astype(bf16)
    return hi, mid, lo


def _dot01_left(sel, x):
    hi, mid, lo = _split3(x)
    d = lambda p: jnp.dot(sel, p, preferred_element_type=f32)
    return d(hi) + d(mid) + d(lo)


def _dot01_right(x, sel):
    hi, mid, lo = _split3(x)
    d = lambda p: jnp.dot(p, sel, preferred_element_type=f32)
    return d(hi) + d(mid) + d(lo)


def _log_sigmoid(x):
    return jnp.minimum(x, 0.0) - jnp.log(1.0 + jnp.exp(-jnp.abs(x)))


AUG = 2 * LANES
SHIFT_LANE = 12


def _fox_project(x_ref, g_ref, w_ref, wf_ref, bf_ref, qg_ref, kg_ref, width, unit):
    h = _rms_rows(x_ref[...], g_ref[...]).astype(bf16)
    z = jnp.dot(h, w_ref[...], preferred_element_type=f32)
    q, k = z[:, :width], z[:, width:2 * width]
    v, g = z[:, 2 * width:3 * width], z[:, 3 * width:4 * width]
    qn = q * lax.rsqrt(_per_head_sums(q * q) * (1.0 / HEAD) + RMS_EPS) * (qg_ref[...] * (HEAD ** -0.5 * unit))
    kn = k * lax.rsqrt(_per_head_sums(k * k) * (1.0 / HEAD) + RMS_EPS) * kg_ref[...]
    lf = _log_sigmoid(jnp.dot(h, wf_ref[...], preferred_element_type=f32) + bf_ref[...])
    return qn, kn, v, g, lf


def _fox_in_args(x, g, w, wf, b_f, q_gain, k_gain):
    d = x.shape[1]
    width = w.shape[1] // 4
    n_heads = width // HEAD
    wf_pad = jnp.zeros((d, LANES), bf16).at[:, :n_heads].set(wf)
    bf_pad = jnp.zeros((1, LANES), f32).at[0, :n_heads].set(b_f)
    qg = jnp.tile(q_gain, n_heads).reshape(1, width)
    kg = jnp.tile(k_gain, n_heads).reshape(1, width)
    const = lambda i: (0, 0)
    specs = [pl.BlockSpec((1, d), const), pl.BlockSpec(w.shape, const), pl.BlockSpec((d, LANES), const),
             pl.BlockSpec((1, LANES), const), pl.BlockSpec((1, width), const), pl.BlockSpec((1, width), const)]
    return specs, (g.reshape(1, d), w, wf_pad, bf_pad, qg, kg)


def _fox_step_in_kernel(x_ref, g_ref, w_ref, wf_ref, bf_ref, qg_ref, kg_ref,
                        q_ref, k_ref, v_ref, sg_ref, lf_ref, *, width, n_heads):
    qn, kn, v, g, lf = _fox_project(x_ref, g_ref, w_ref, wf_ref, bf_ref, qg_ref, kg_ref, width, 1.0)
    q_ref[...] = qn.astype(bf16).astype(f32)
    k_ref[...] = kn
    v_ref[...] = v
    sg_ref[...] = _sigmoid(g)
    lf_ref[...] = lf[:, :n_heads]


def fox_step_in(x, g, w, wf, b_f, q_gain, k_gain):
    m, d = x.shape
    width = w.shape[1] // 4
    n_heads = width // HEAD
    specs, args = _fox_in_args(x, g, w, wf, b_f, q_gain, k_gain)
    whole = lambda n: pl.BlockSpec((m, n), lambda i: (0, 0))
    return pl.pallas_call(
        functools.partial(_fox_step_in_kernel, width=width, n_heads=n_heads),
        grid=(1,),
        in_specs=[whole(d)] + specs,
        out_specs=[whole(width)] * 4 + [whole(n_heads)],
        out_shape=[jax.ShapeDtypeStruct((m, width), f32)] * 4 + [jax.ShapeDtypeStruct((m, n_heads), f32)],
        compiler_params=_params(("arbitrary",)),
        name="fox_step_in",
    )(x, *args)


def _fox_seq_in_kernel(x_ref, g_ref, w_ref, wf_ref, bf_ref, qg_ref, kg_ref,
                       q_ref, kt_ref, ka_ref, qa_ref, vt_ref, vtb_ref, sg_ref, lft_ref, c_ref, carry_ref,
                       *, tiles_per_seq, width, n_heads):
    i = pl.program_id(0)
    qn, kn, v, g, lf = _fox_project(x_ref, g_ref, w_ref, wf_ref, bf_ref, qg_ref, kg_ref, width, LOG2E)
    tm = lf.shape[0]
    q_ref[...] = qn.astype(bf16)
    sg_ref[...] = _sigmoid(g)
    for blk in range(width // LANES):
        sl = slice(blk * LANES, (blk + 1) * LANES)
        kt_ref[0, sl, :] = kn[:, sl].T
        vt = v[:, sl].T
        vt_ref[0, sl, :] = vt
        vtb_ref[0, 0, sl, :] = vt.astype(bf16)
    lft_ref[0] = lf.T[:n_heads, :]

    @pl.when(i % tiles_per_seq == 0)
    def _():
        carry_ref[...] = jnp.zeros_like(carry_ref)

    row = lax.broadcasted_iota(jnp.int32, (tm, tm), 0)
    col = lax.broadcasted_iota(jnp.int32, (tm, tm), 1)
    tri = jnp.where(row >= col, 1.0, 0.0).astype(bf16)
    c = _dot01_left(tri, lf) + carry_ref[...]
    carry_ref[...] = c[tm - 1:tm, :]
    c = c * LOG2E
    c_ref[...] = c
    pieces = jnp.concatenate(_split3(c), axis=-1)
    rr = lax.broadcasted_iota(jnp.int32, (3 * LANES, LANES), 0)
    cc = lax.broadcasted_iota(jnp.int32, (3 * LANES, LANES), 1)
    piece, head = rr // LANES, rr % LANES
    lane = _lane_iota((1, LANES))
    k_const = jnp.where((lane < 6) | (lane == SHIFT_LANE), 1.0, 0.0)
    q_const = jnp.where((lane >= 6) & (lane < 12), -1.0, 0.0)
    knb = kn.astype(bf16)
    for p in range(n_heads // 2):
        place = lambda base: jnp.where(((head == 2 * p) & (cc == base + piece))
                                       | ((head == 2 * p + 1) & (cc == base + 3 + piece)), 1.0, 0.0).astype(bf16)
        both = jnp.dot(pieces, jnp.concatenate([place(6), place(0)], axis=-1), preferred_element_type=f32)
        k_aug = both[:, :LANES] + k_const
        ka_ref[p] = jnp.concatenate([knb[:, p * LANES:(p + 1) * LANES], k_aug.astype(bf16)], axis=-1)
        qa_ref[p] = (both[:, LANES:] + q_const).astype(bf16)


def fox_seq_in(x, g, w, wf, b_f, q_gain, k_gain, n_seq, tm):
    m, d = x.shape
    width = w.shape[1] // 4
    n_heads = width // HEAD
    n_pairs = n_heads // 2
    seq = m // n_seq
    tps = seq // tm
    specs, args = _fox_in_args(x, g, w, wf, b_f, q_gain, k_gain)
    row = lambda i: (i, 0)
    chan = lambda i: (i // tps, 0, i % tps)
    return pl.pallas_call(
        functools.partial(_fox_seq_in_kernel, tiles_per_seq=tps, width=width, n_heads=n_heads),
        grid=(m // tm,),
        in_specs=[pl.BlockSpec((tm, d), row)] + specs,
        out_specs=[pl.BlockSpec((tm, width), row),
                   pl.BlockSpec((1, width, tm), chan),
                   pl.BlockSpec((n_pairs, tm, AUG), lambda i: (0, i, 0)),
                   pl.BlockSpec((n_pairs, tm, LANES), lambda i: (0, i, 0)),
                   pl.BlockSpec((1, width, tm), chan),
                   pl.BlockSpec((1, 1, width, tm), lambda i: (i // tps, i % tps, 0, 0)),
                   pl.BlockSpec((tm, width), row),
                   pl.BlockSpec((1, n_heads, tm), chan),
                   pl.BlockSpec((tm, LANES), row)],
        out_shape=[jax.ShapeDtypeStruct((m, width), bf16),
                   jax.ShapeDtypeStruct((n_seq, width, seq), f32),
                   jax.ShapeDtypeStruct((n_pairs, m, AUG), bf16),
                   jax.ShapeDtypeStruct((n_pairs, m, LANES), bf16),
                   jax.ShapeDtypeStruct((n_seq, width, seq), f32),
                   jax.ShapeDtypeStruct((n_seq, tps, width, tm), bf16),
                   jax.ShapeDtypeStruct((m, width), f32),
                   jax.ShapeDtypeStruct((n_seq, n_heads, seq), f32),
                   jax.ShapeDtypeStruct((m, LANES), f32)],
        scratch_shapes=[pltpu.VMEM((1, LANES), f32)],
        compiler_params=_params(("arbitrary",)),
        name="fox_seq_in",
    )(x, *args)


def _fox_flash_kernel(lo_ref, qk_ref, q_ref, qa_ref, ka_ref, vt_ref, sg_ref, o_ref, *, tq, tv, bounded):
    b, p, qi = pl.program_id(0), pl.program_id(1), pl.program_id(2)
    nsub = tq // tv
    q2 = q_ref[0]
    lo = _lane_iota((tq, LANES)) < HEAD
    zero = jnp.zeros_like(q2)
    bias2 = qa_ref[0]
    lane = _lane_iota((tq, LANES))
    shift = jnp.where(lane == SHIFT_LANE, -qk_ref[0], 0.0).astype(bf16)
    qa = []
    for j in (0, 1):
        own = ((lane >= 3 * j) & (lane < 3 * j + 3)) | ((lane >= 6 + 3 * j) & (lane < 9 + 3 * j))
        bias = jnp.where(own, bias2, zero)
        if bounded:
            bias = jnp.where(lane == SHIFT_LANE, shift, bias)
        qh = jnp.where(lo, q2, zero) if j == 0 else jnp.where(lo, zero, q2)
        qa.append(jnp.concatenate([qh, bias], axis=-1))
    row = lax.broadcasted_iota(jnp.int32, (tq, tq), 0)
    col = lax.broadcasted_iota(jnp.int32, (tq, tq), 1)

    def scores(kb, j):
        ka = ka_ref[0, 0, pl.ds(pl.multiple_of(kb * tq, tq), tq), :]
        return _bdot_nt(ka, qa[j])

    def softmax_pv(kb, j, t, stats, masked):
        m_old, l_old, acc = stats
        if masked:
            t = jnp.where(row <= col, t, -jnp.inf)
        m_new = jnp.maximum(m_old, jnp.max(t, axis=0, keepdims=True))
        pe = jnp.exp2(t - m_new).astype(bf16)
        alpha = jnp.exp2(m_old - m_new)
        l_new = alpha * l_old + jnp.sum(pe.astype(f32), axis=0, keepdims=True)
        pv = sum(jnp.dot(vt_ref[0, kb * nsub + s, j * HEAD:(j + 1) * HEAD, :], pe[s * tv:(s + 1) * tv],
                         preferred_element_type=f32) for s in range(nsub))
        return m_new, l_new, acc * alpha + pv

    def block(kb, t0, st0, st1, masked, last):
        t1 = scores(kb, 1)
        st0 = softmax_pv(kb, 0, t0, st0, masked)
        t0_next = t0 if last else scores(kb + 1, 0)
        st1 = softmax_pv(kb, 1, t1, st1, masked)
        return t0_next, st0, st1

    def pv(kb, j, pe):
        return sum(jnp.dot(vt_ref[0, kb * nsub + s, j * HEAD:(j + 1) * HEAD, :], pe[s * tv:(s + 1) * tv],
                   preferred_element_type=f32) for s in range(nsub))

    def free_blocks(kbs, sts):
        units = [(kb, j) for kb in kbs for j in (0, 1)]
        l = [sts[0][1], sts[1][1]]
        acc = [sts[0][2], sts[1][2]]
        t_next = scores(*units[0])
        for n, (kb, j) in enumerate(units):
            t = t_next
            if n + 1 < len(units):
                t_next = scores(*units[n + 1])
            pe = jnp.exp2(jnp.where((row <= col) | (kb < qi), t, -jnp.inf))
            l[j] = l[j] + jnp.sum(pe, axis=0, keepdims=True)
            acc[j] = acc[j] + pv(kb, j, pe.astype(bf16))
        return (sts[0][0], l[0], acc[0]), (sts[1][0], l[1], acc[1])

    init = (jnp.full((1, tq), -jnp.inf, f32), jnp.zeros((1, tq), f32), jnp.zeros((HEAD, tq), f32))
    first = lo_ref[b, p, qi]
    if bounded:
        odd = (qi - first + 1) % 2
        sts = lax.cond(odd == 1, lambda s: free_blocks((first,), s), lambda s: s, (init, init))
        (m0, l0, a0), (m1, l1, a1) = lax.fori_loop(
            0, (qi - first + 1) // 2,
            lambda i, s: free_blocks((first + odd + 2 * i, first + odd + 2 * i + 1), s), sts)
    else:
        state = lax.fori_loop(first, qi, lambda kb, s: block(kb, *s, False, False),
                              (scores(first, 0), init, init))
        _, (m0, l0, a0), (m1, l1, a1) = block(qi, *state, True, True)
    o = jnp.concatenate([a0 / l0, a1 / l1], axis=0).T
    o_ref[0] = (o * sg_ref[0]).astype(o_ref.dtype)


QK_BOUNDED_MAX = 40.0


def fox_flash(q, qa, ka, vtb, sg, first_block, qk, n_seq, tq):
    m, width = q.shape
    seq = m // n_seq
    n_pairs = width // LANES
    tv = vtb.shape[3]
    r3 = lambda a: a.reshape(n_seq, seq, a.shape[1])
    tile = pl.BlockSpec((1, tq, LANES), lambda b, p, i, lo, qk_: (b, i, p))
    grid_spec = pltpu.PrefetchScalarGridSpec(
        num_scalar_prefetch=2,
        grid=(n_seq, n_pairs, seq // tq),
        in_specs=[tile,
                  pl.BlockSpec((1, tq, LANES), lambda b, p, i, lo, qk_: (p, b * (seq // tq) + i, 0)),
                  pl.BlockSpec((1, 1, seq, AUG), lambda b, p, i, lo, qk_: (p, b, 0, 0)),
                  pl.BlockSpec((1, seq // tv, LANES, tv), lambda b, p, i, lo, qk_: (b, 0, p, 0)),
                  tile],
        out_specs=tile,
    )
    args = (first_block, qk.reshape(1).astype(f32), r3(q), qa, ka.reshape(n_pairs, n_seq, seq, AUG), vtb,
            r3(sg))
    call = lambda bounded: pl.pallas_call(
        functools.partial(_fox_flash_kernel, tq=tq, tv=tv, bounded=bounded),
        grid_spec=grid_spec,
        out_shape=jax.ShapeDtypeStruct((n_seq, seq, width), bf16),
        compiler_params=_params(("parallel", "parallel", "arbitrary")),
        name="fox_flash_bounded" if bounded else "fox_flash",
    )
    out = lax.cond(qk <= QK_BOUNDED_MAX, lambda a: call(True)(*a), lambda a: call(False)(*a), args)
    return out.reshape(m, width)


FOX_PAGES_PER_STEP = 8
FOX_VALUE_PAGES_PER_STEP = 16
EXP_UNDERFLOW = 104.0


def _fox_scores_kernel(pt_ref, qc_ref, knt_ref, lfn_ref, *rest, n_heads, pps):
    kt_refs, lf_refs = rest[:pps], rest[pps:2 * pps]
    t_ref, town_ref, m_out, l_out, m_ref, l_ref, carry_ref = rest[2 * pps:]
    pg = pl.program_id(1)
    page = lf_refs[0].shape[2]

    def score(kt_ref, bias):
        rows = [jnp.sum(kt_ref[0, h] * qc_ref[0, h], axis=0, keepdims=True) for h in range(n_heads)]
        t = jnp.concatenate(rows, axis=0) + bias
        m_old = m_ref[...]
        m_new = jnp.maximum(m_old, jnp.max(t, axis=-1, keepdims=True))
        l_ref[...] = jnp.exp(m_old - m_new) * l_ref[...] + jnp.sum(jnp.exp(t - m_new), axis=-1, keepdims=True)
        m_ref[...] = m_new
        return t

    @pl.when(pg == 0)
    def _():
        m_ref[...] = jnp.full(m_ref.shape, -jnp.inf, f32)
        l_ref[...] = jnp.zeros_like(l_ref)
        carry_ref[...] = lfn_ref[0]
        town_ref[0] = score(knt_ref, jnp.where(_lane_iota((n_heads, page)) == 0, 0.0, -jnp.inf))

    r = lax.broadcasted_iota(jnp.int32, (page, page), 0)
    cidx = lax.broadcasted_iota(jnp.int32, (page, page), 1)
    later = jnp.where(r > cidx, 1.0, 0.0).astype(bf16)
    for j in range(pps):
        lf = lf_refs[j][0]
        t_ref[0, j] = score(kt_refs[j], _dot01_right(lf, later) + carry_ref[...])
        carry_ref[...] = carry_ref[...] + jnp.sum(lf, axis=-1, keepdims=True)

    @pl.when(pg == pl.num_programs(1) - 1)
    def _():
        m_out[0] = m_ref[...]
        l_out[0] = l_ref[...]


def _fox_values_kernel(pt_ref, idx_ref, cnt_ref, town_ref, vnt_ref, m_ref, l_ref, sg_ref, *rest, n_heads, pps):
    t_refs, vt_refs = rest[:pps], rest[pps:2 * pps]
    o_ref, acc_ref = rest[2 * pps:]
    i, pg = pl.program_id(0), pl.program_id(1)
    m = m_ref[0]

    @pl.when(pg == 0)
    def _():
        pe = jnp.exp(town_ref[0] - m)
        for h in range(n_heads):
            acc_ref[h] = pe[h:h + 1, :] * vnt_ref[0, h]

    for j in range(pps):
        @pl.when(pg * pps + j < cnt_ref[i])
        def _():
            pe = jnp.exp(t_refs[j][0, 0] - m)
            for h in range(n_heads):
                acc_ref[h] = acc_ref[h] + pe[h:h + 1, :] * vt_refs[j][0, h]

    @pl.when(pg == pl.num_programs(1) - 1)
    def _():
        l = l_ref[0]
        for h in range(n_heads):
            o_ref[0, h] = jnp.sum(acc_ref[h], axis=-1, keepdims=True) / l[h:h + 1, :] * sg_ref[0, h]


def fox_step(q, k_new, v_new, lf_new, sg, cache_k, cache_v, cache_lf, page_table):
    b, width = q.shape
    pool, page, n_heads = cache_lf.shape
    n_pages = page_table.shape[1]
    pps = FOX_PAGES_PER_STEP
    kt = jnp.transpose(cache_k, (0, 2, 3, 1))
    vt = jnp.transpose(cache_v, (0, 2, 3, 1))
    lft = jnp.transpose(cache_lf, (0, 2, 1))
    col = lambda a: a.reshape(b, n_heads, HEAD, 1)
    own_page = lambda a: jnp.zeros((b, n_heads, HEAD, page), f32).at[..., 0].set(a.reshape(b, n_heads, HEAD))
    qc = jnp.broadcast_to(col(q), (b, n_heads, HEAD, page))
    per_seq = lambda shape: pl.BlockSpec((1,) + shape, lambda i, g, *_: (i,) + (0,) * len(shape))
    page_spec = lambda shape, j: pl.BlockSpec(
        (1,) + shape, lambda i, g, pt: (pt[i, n_pages - 1 - (g * pps + j)],) + (0,) * len(shape))
    stat = jax.ShapeDtypeStruct((b, n_heads, 1), f32)
    t_all, t_own, m, l = pl.pallas_call(
        functools.partial(_fox_scores_kernel, n_heads=n_heads, pps=pps),
        grid_spec=pltpu.PrefetchScalarGridSpec(
            num_scalar_prefetch=1,
            grid=(b, n_pages // pps),
            in_specs=[per_seq((n_heads, HEAD, page))] * 2 + [per_seq((n_heads, 1))]
                     + [page_spec((n_heads, HEAD, page), j) for j in range(pps)]
                     + [page_spec((n_heads, page), j) for j in range(pps)],
            out_specs=[pl.BlockSpec((1, pps, n_heads, page), lambda i, g, pt: (i, g, 0, 0)),
                       per_seq((n_heads, page)), per_seq((n_heads, 1)), per_seq((n_heads, 1))],
            scratch_shapes=[pltpu.VMEM((n_heads, 1), f32)] * 3),
        out_shape=[jax.ShapeDtypeStruct((b, n_pages, n_heads, page), f32),
                   jax.ShapeDtypeStruct((b, n_heads, page), f32), stat, stat],
        compiler_params=_params(("parallel", "arbitrary")),
        name="fox_step_scores",
    )(page_table, qc, own_page(k_new), lf_new.reshape(b, n_heads, 1), *([kt] * pps), *([lft] * pps))

    live = jnp.any(jnp.max(t_all, axis=-1) - m[:, None, :, 0] > -EXP_UNDERFLOW, axis=-1)
    live_idx = jnp.argsort(jnp.logical_not(live), axis=1, stable=True).astype(jnp.int32)
    count = jnp.sum(live, axis=1).astype(jnp.int32)

    pps = math.gcd(FOX_VALUE_PAGES_PER_STEP, n_pages)

    def slot(i, g, j, idx, cnt):
        return idx[i, jnp.maximum(jnp.minimum(g * pps + j, cnt[i] - 1), 0)]

    t_spec = lambda j: pl.BlockSpec((1, 1, n_heads, page),
                                    lambda i, g, pt, idx, cnt: (i, slot(i, g, j, idx, cnt), 0, 0))
    v_spec = lambda j: pl.BlockSpec((1, n_heads, HEAD, page),
                                    lambda i, g, pt, idx, cnt: (pt[i, n_pages - 1 - slot(i, g, j, idx, cnt)], 0, 0, 0))
    out = pl.pallas_call(
        functools.partial(_fox_values_kernel, n_heads=n_heads, pps=pps),
        grid_spec=pltpu.PrefetchScalarGridSpec(
            num_scalar_prefetch=3,
            grid=(b, n_pages // pps),
            in_specs=[per_seq((n_heads, page)), per_seq((n_heads, HEAD, page)), per_seq((n_heads, 1)),
                      per_seq((n_heads, 1)), per_seq((n_heads, HEAD, 1))]
                     + [t_spec(j) for j in range(pps)] + [v_spec(j) for j in range(pps)],
            out_specs=per_seq((n_heads, HEAD, 1)),
            scratch_shapes=[pltpu.VMEM((n_heads, HEAD, page), f32)]),
        out_shape=jax.ShapeDtypeStruct((b, n_heads, HEAD, 1), f32),
        compiler_params=_params(("parallel", "arbitrary")),
        name="fox_step_values",
    )(page_table, live_idx, count, t_own, own_page(v_new), m, l, col(sg), *([t_all] * pps), *([vt] * pps))
    return out.reshape(b, width).astype(bf16)


def _mem_attn_step_kernel(q_ref, qg_ref, k_ref, v_ref, o_ref):
    q = q_ref[0]
    qn = q * lax.rsqrt(jnp.mean(q * q, axis=-1, keepdims=True) + RMS_EPS) * (qg_ref[...] * MEM_HEAD ** -0.5)
    s = jnp.sum(k_ref[0, 0] * qn, axis=-1, keepdims=True)
    p = jnp.exp(s - jnp.max(s, axis=0, keepdims=True))
    l = jnp.sum(p, axis=0, keepdims=True)
    o_ref[0] = jnp.sum(p * v_ref[0, 0], axis=0, keepdims=True) / l


def mem_attn_step(q, q_gain, cache_k, cache_v, layer):
    b, d = q.shape
    _, _, n_mem, heads, hd = cache_k.shape
    mem_spec = pl.BlockSpec((1, 1, n_mem, heads, hd), lambda i: (layer, i, 0, 0, 0))
    out = pl.pallas_call(
        _mem_attn_step_kernel,
        grid=(b,),
        in_specs=[pl.BlockSpec((1, 1, heads, hd), lambda i: (i, 0, 0, 0)),
                  pl.BlockSpec((1, 1, hd), lambda i: (0, 0, 0)),
                  mem_spec, mem_spec],
        out_specs=pl.BlockSpec((1, 1, heads, hd), lambda i: (i, 0, 0, 0)),
        out_shape=jax.ShapeDtypeStruct((b, 1, heads, hd), f32),
        compiler_params=_params(("parallel",)),
        name="mem_attn_step",
    )(q.reshape(b, 1, heads, hd), q_gain.reshape(1, 1, hd), cache_k, cache_v)
    return out.reshape(b, d)


def _softplus(x):
    return jnp.maximum(x, 0.0) + jnp.log(1.0 + jnp.exp(-jnp.abs(x)))


def _rwkv_kernel(zr_ref, zk_ref, zv_ref, zg_ref, zwa_ref, sh0_ref, s0_ref,
                 mu_ref, w0_ref, a0_ref, kk_ref, ka_ref, rk_ref, lnw_ref, lnb_ref, wup_ref, aup_ref,
                 y_ref, sout_ref, shout_ref,
                 s_ref, carry_ref, r_s, k_s, v_s, al_s, be_s, ld_s, yo_s, bo_s, sg_s,
                 *, chunk, n_valid, width):
    i = pl.program_id(1)
    nblk = pl.num_programs(1)
    nsq, cg = zr_ref.shape[0], zr_ref.shape[1]
    n_pairs = width // LANES

    @pl.when(i == 0)
    def _():
        s_ref[...] = s0_ref[...]
        carry_ref[...] = sh0_ref[...]

    rows = lax.broadcasted_iota(jnp.int32, (cg, 1), 0)
    valid = (rows + i * cg) < n_valid
    last = jnp.minimum(n_valid - 1 - i * cg, cg - 1)

    for s in range(nsq):
        def shifted(z, lane0, n):
            prev = jnp.where(rows == 0, carry_ref[s, :, lane0:lane0 + n], pltpu.roll(z, 1, 0))
            return z + (prev - z) * mu_ref[:, lane0:lane0 + n]

        zr, zk, zv, zg, zwa = zr_ref[s], zk_ref[s], zv_ref[s], zg_ref[s], zwa_ref[s]
        r = shifted(zr, 0, width)
        k = shifted(zk, width, width)
        v = shifted(zv, 2 * width, width)
        gp = shifted(zg, 3 * width, width)
        wa = shifted(zwa, 4 * width, LANES)

        @pl.when(last >= 0)
        def _():
            for z, lane0, n in ((zr, 0, width), (zk, width, width), (zv, 2 * width, width),
                                (zg, 3 * width, width), (zwa, 4 * width, LANES)):
                carry_ref[s, :, lane0:lane0 + n] = jnp.sum(jnp.where(rows == last, z, 0.0), axis=0,
                                                           keepdims=True)

        w = -_softplus(-(w0_ref[...] + _bdot(jnp.tanh(wa), wup_ref[...]))) - 0.5
        logd = -jnp.exp(w)
        a = _sigmoid(a0_ref[...] + _bdot(wa, aup_ref[...]))
        kk = k * kk_ref[...]
        kk = kk * lax.rsqrt(_per_head_sums(kk * kk) + L2_EPS)
        k2 = k * (1.0 + (a - 1.0) * ka_ref[...])
        bo_s[s] = _per_head_sums(r * k2 * rk_ref[...]) * v
        sg_s[s] = _sigmoid(gp)
        r_s[s] = r
        k_s[s] = jnp.where(valid, k2, 0.0)
        v_s[s] = jnp.where(valid, v, 0.0)
        al_s[s] = -kk
        be_s[s] = jnp.where(valid, kk * a, 0.0)
        ld_s[s] = jnp.where(valid, logd, 0.0)

    ri = lax.broadcasted_iota(jnp.int32, (chunk, chunk), 0)
    ci = lax.broadcasted_iota(jnp.int32, (chunk, chunk), 1)
    tri = jnp.where(ri >= ci, 1.0, 0.0).astype(bf16)
    lower, strict = ri >= ci, ri > ci
    eye = jnp.where(ri == ci, 1.0, 0.0)
    lo = _lane_iota((1, LANES)) < HEAD
    sr = lax.broadcasted_iota(jnp.int32, (LANES, LANES), 0) < HEAD
    sc = lax.broadcasted_iota(jnp.int32, (LANES, LANES), 1) < HEAD
    same_head = sr == sc
    n_sq = max(int(math.log2(chunk)) - 1, 0)

    def chunk_body(c, _):
        rs = pl.ds(pl.multiple_of(c * chunk, chunk), chunk)
        pairs = [(s, p) for s in range(nsq) for p in range(n_pairs)]
        heads = [(s, p, j) for s, p in pairs for j in (0, 1)]
        lsl = [slice(p * LANES, (p + 1) * LANES) for p in range(n_pairs)]
        xs, vv, vp, yb, yk, ybe, yke, d_end = {}, {}, {}, {}, {}, {}, {}, {}
        for s in range(nsq):
            ld = ld_s[s, rs, :]
            lcum = _dot01_left(tri, ld)
            lend = lcum[chunk - 1:chunk, :]
            e_neg, e_end = jnp.exp(-lcum), jnp.exp(lend - lcum)
            xa_all, xr_all = al_s[s, rs, :] * jnp.exp(lcum - ld), r_s[s, rs, :] * jnp.exp(lcum)
            be, kc, v_all = be_s[s, rs, :], k_s[s, rs, :], v_s[s, rs, :]
            dec = jnp.exp(lend)
            for p in range(n_pairs):
                xs[s, p] = jnp.concatenate([xa_all[:, lsl[p]], xr_all[:, lsl[p]]], axis=0)
                vv[s, p] = v_all[:, lsl[p]]
                vp[s, p] = vv[s, p].astype(bf16)
                yb[s, p], yk[s, p] = (be * e_neg)[:, lsl[p]], (kc * e_neg)[:, lsl[p]]
                ybe[s, p], yke[s, p] = (be * e_end)[:, lsl[p]], (kc * e_end)[:, lsl[p]]
                d_end[s, p] = dec[:, lsl[p]]
        s_bd = {sp: s_ref[sp[0], sp[1]] for sp in pairs}
        a_ab, a_ak, a_rb, a_rk = {}, {}, {}, {}
        for s, p, j in heads:
            xm = jnp.where(lo if j == 0 else jnp.logical_not(lo), xs[s, p], 0.0).astype(bf16)
            gb = _bdot_nt(xm, yb[s, p])
            gk = _bdot_nt(xm, yk[s, p])
            a_ab[s, p, j] = jnp.where(strict, gb[:chunk], 0.0)
            a_ak[s, p, j] = jnp.where(strict, gk[:chunk], 0.0)
            a_rb[s, p, j] = jnp.where(lower, gb[chunk:], 0.0)
            a_rk[s, p, j] = jnp.where(lower, gk[chunk:], 0.0)
        tinv = {h: eye + a_ab[h] for h in heads}
        pw = dict(a_ab)
        for _ in range(n_sq):
            for h in heads:
                pw[h] = _bdot(pw[h], pw[h])
            for h in heads:
                tinv[h] = tinv[h] + _bdot(tinv[h], pw[h])
        xh = {sp: _bdot_nt(xs[sp], s_bd[sp]) for sp in pairs}
        av = {(s, p, j): _bdot(a_ak[s, p, j], vp[s, p]) for s, p, j in heads}
        uh = {(s, p, j): _bdot(tinv[s, p, j], xh[s, p][:chunk] + av[s, p, j]) for s, p, j in heads}
        u = {(s, p): jnp.where(lo, uh[s, p, 0], uh[s, p, 1]) for s, p in pairs}
        yh = {(s, p, j): _bdot(a_rb[s, p, j], u[s, p]) + _bdot(a_rk[s, p, j], vp[s, p]) for s, p, j in heads}
        for s, p in pairs:
            yo_s[s, rs, lsl[p]] = xh[s, p][chunk:] + jnp.where(lo, yh[s, p, 0], yh[s, p, 1])
            uv_t = jnp.concatenate([u[s, p], vv[s, p]], axis=0).T
            upd = _bdot(uv_t, jnp.concatenate([ybe[s, p], yke[s, p]], axis=0))
            s_ref[s, p] = s_bd[s, p] * d_end[s, p] + jnp.where(same_head, upd, 0.0)
        return 0

    lax.fori_loop(0, cg // chunk, chunk_body, 0)

    for s in range(nsq):
        y = yo_s[s]
        mean = _per_head_sums(y) * (1.0 / HEAD)
        yc = y - mean
        var = _per_head_sums(yc * yc) * (1.0 / HEAD)
        y = yc * lax.rsqrt(var + GN_EPS) * lnw_ref[...] + lnb_ref[...]
        y_ref[s] = ((y + bo_s[s]) * sg_s[s]).astype(y_ref.dtype)

    @pl.when(i == nblk - 1)
    def _():
        sout_ref[...] = s_ref[...]
        shout_ref[...] = carry_ref[...]


def rwkv7(z, col0, shift0, s0, prm, n_seq, n_valid, chunk, block_rows):
    m = z.shape[0]
    rows = m // n_seq
    nblk = rows // block_rows
    width = prm["b_w0"].shape[0]
    n_heads = width // HEAD
    n_pairs = width // LANES
    bcols = 4 * width + LANES
    s0p = s0.reshape(n_seq, n_pairs, 2, HEAD, HEAD)
    zeros = jnp.zeros_like(s0p[:, :, 0])
    s0bd = jnp.concatenate([jnp.concatenate([s0p[:, :, 0], zeros], axis=-1),
                            jnp.concatenate([zeros, s0p[:, :, 1]], axis=-1)], axis=-2)
    lora = lambda w_, off: jnp.zeros((LANES, width), bf16).at[off:off + w_.shape[0]].set(w_.astype(bf16))
    row1 = lambda a: a.reshape(1, -1).astype(f32)
    cb = col0 // width
    nsq = math.gcd(RWKV_SEQS_PER_STEP, n_seq)
    z3 = z.reshape(n_seq, rows, z.shape[1])
    zspec = lambda j: pl.BlockSpec((nsq, block_rows, width), lambda g, i: (g, i, cb + j))
    const = lambda shape: pl.BlockSpec(shape, lambda g, i: (0,) * len(shape))
    per_seq = lambda shape: pl.BlockSpec((nsq,) + shape, lambda g, i: (g,) + (0,) * len(shape))
    y, sbd, shift = pl.pallas_call(
        functools.partial(_rwkv_kernel, chunk=chunk, n_valid=n_valid, width=width),
        grid=(n_seq // nsq, nblk),
        in_specs=[zspec(0), zspec(1), zspec(2), zspec(3),
                  pl.BlockSpec((nsq, block_rows, LANES), lambda g, i: (g, i, (col0 + 4 * width) // LANES)),
                  per_seq((1, bcols)), per_seq((n_pairs, LANES, LANES)),
                  const((1, bcols))] + [const((1, width))] * 7 + [const((LANES, width))] * 2,
        out_specs=[pl.BlockSpec((nsq, block_rows, width), lambda g, i: (g, i, 0)),
                   per_seq((n_pairs, LANES, LANES)), per_seq((1, bcols))],
        out_shape=[jax.ShapeDtypeStruct((n_seq, rows, width), bf16),
                   jax.ShapeDtypeStruct((n_seq, n_pairs, LANES, LANES), f32),
                   jax.ShapeDtypeStruct((n_seq, 1, bcols), f32)],
        scratch_shapes=[pltpu.VMEM((nsq, n_pairs, LANES, LANES), f32), pltpu.VMEM((nsq, 1, bcols), f32)]
                       + [pltpu.VMEM((nsq, block_rows, width), f32)] * 9,
        compiler_params=_params(("parallel", "arbitrary")),
        name="rwkv7",
    )(z3, z3, z3, z3, z3, shift0.reshape(n_seq, 1, bcols), s0bd,
      row1(prm["b_mu"]), row1(prm["b_w0"]), row1(prm["b_a0"]), row1(prm["b_k_k"]), row1(prm["b_k_a"]),
      row1(prm["b_r_k"]), row1(prm["b_ln_w"]), row1(prm["b_ln_b"]),
      lora(prm["b_w_up"], 0), lora(prm["b_a_up"], HEAD))
    sp = sbd.reshape(n_seq, n_pairs, 2, HEAD, 2, HEAD)
    s_fin = jnp.stack([sp[:, :, 0, :, 0, :], sp[:, :, 1, :, 1, :]], axis=2).reshape(n_seq, n_heads, HEAD, HEAD)
    return y.reshape(m, width), s_fin, shift.reshape(n_seq, bcols)


A_WIDTH = 512
STEP_ROWS = 64
RWKV_CHUNK = 64
RWKV_BLOCK = 128
RWKV_SEQS_PER_STEP = 4
FLASH_TILE = 512
DENSE_TILE = 512
FOX_IN_TILE = 256


UNDERFLOW_LOG2 = 160.0


def _qk_bound(q_gain, k_gain):
    return 1.02 * HEAD ** 0.5 * LOG2E * jnp.max(jnp.abs(q_gain)) * jnp.max(jnp.abs(k_gain))


def _first_key_block(c, qk, n_seq, n_heads, tile):
    c3 =c.reshape(n_seq, -1, c.shape[1])[:, :, :n_heads]
    c_first = c3[:, ::tile]
    c_last = c3[:, tile - 1::tile]
    n = c_first.shape[1]
    dead = (2.0 * qk + c_first[:, :, None, :] - c_last[:, None, :, :]) <= -UNDERFLOW_LOG2
    dead = dead & (jnp.arange(n)[None, None, :, None] < jnp.arange(n)[None, :, None, None])
    dead = dead.reshape(n_seq, n, n, n_heads // 2, 2).all(axis=-1)
    return jnp.transpose(jnp.sum(dead, axis=2), (0, 2, 1)).astype(jnp.int32)


def _trunk(x3, mem_k, mem_v, rwkv_s0, rwkv_shift0, fox_past, p):
    n_seq, seq, d = x3.shape
    m = n_seq * seq
    x = x3.reshape(m, d)
    step = seq == 1
    tm = m if step else DENSE_TILE
    depth = p["norm_mix"].shape[0]
    chunk_v, s_out, shift_out, fk, fv, fl = [], [], [], [], [], []
    ie = io = 0
    for l in range(depth):
        if l % 2 == 0:
            z = norm_matmul(x, p["norm_mix"][l], p["ab_w_in"][ie], tm)
            prm = {k_: p[k_][ie] for k_ in ("b_mu", "b_w0", "b_w_up", "b_a0", "b_a_up", "b_k_k", "b_k_a",
                                           "b_r_k", "b_ln_w", "b_ln_b")}
            if step:
                ya, va = mixer_a_step(z, p["a_v_norm"][ie], p["a_w_s"][ie], p["a_b_s"][ie], A_WIDTH)
                zp = jnp.zeros((n_seq, STEP_ROWS, z.shape[1]), f32).at[:, 0].set(z)
                yb, s_new, sh_new = rwkv7(zp.reshape(n_seq * STEP_ROWS, -1), 2 * A_WIDTH, rwkv_shift0[ie],
                                          rwkv_s0[ie], prm, n_seq, 1, RWKV_CHUNK, STEP_ROWS)
                yb = yb.reshape(n_seq, STEP_ROWS, -1)[:, 0]
                chunk_v.append(va.reshape(n_seq, seq, A_WIDTH))
            else:
                ya = mixer_a(z, p["a_v_norm"][ie], p["a_w_s"][ie], p["a_b_s"][ie], A_WIDTH, tm)
                yb, s_new, sh_new = rwkv7(z, 2 * A_WIDTH, rwkv_shift0[ie], rwkv_s0[ie], prm, n_seq, seq,
                                          RWKV_CHUNK, RWKV_BLOCK)
            w_out = p["ab_w_out"][ie]
            mixed = [(ya, w_out[:A_WIDTH]), (yb, w_out[A_WIDTH:])]
            s_out.append(s_new)
            shift_out.append(sh_new)
            ie += 1
        else:
            w_in = p["c_w_in"][io]
            width = (w_in.shape[1] // 4) // HEAD * HEAD
            n_heads = width // HEAD
            front = (x, p["norm_mix"][l], w_in[:, :4 * width], w_in[:, 4 * width:], p["c_b_f"][io],
                     p["c_q_norm"][io], p["c_k_norm"][io])
            if step:
                q, k, v, sg, lf = fox_step_in(*front)
                ck, cv, cl, pt = fox_past
                og = fox_step(q, k, v, lf, sg, ck[io], cv[io], cl[io], pt)
                fk.append(k.reshape(n_seq, seq, n_heads, HEAD))
                fv.append(v.reshape(n_seq, seq, n_heads, HEAD))
                fl.append(lf.reshape(n_seq, seq, n_heads))
            else:
                q, kt, ka, qa, vt, vtb, sg, lft, c = fox_seq_in(*front, n_seq, FOX_IN_TILE)
                qk = _qk_bound(p["c_q_norm"][io], p["c_k_norm"][io])
                first = _first_key_block(c, qk, n_seq, n_heads, FLASH_TILE)
                og = fox_flash(q, qa, ka, vtb, sg, first, qk, n_seq, FLASH_TILE)
                tokens_first = lambda t: jnp.transpose(t.reshape(n_seq, n_heads, HEAD, seq), (0, 3, 1, 2))
                fk.append(tokens_first(kt))
                fv.append(tokens_first(vt))
                fl.append(jnp.transpose(lft, (0, 2, 1)))
            mixed = [(og, p["c_w_out"][io])]
            io += 1
        if step:
            x = res_matmul(x, mixed, tm)
            q_raw = norm_matmul(x, p["norm_mem"][l], p["m_wq"][l], tm)
            o = mem_attn_step(q_raw, p["m_q_norm"][l], mem_k, mem_v, l)
            x = res_matmul(x, [(o, p["m_wo"][l])], tm)
        else:
            x = mem_attn(x, mixed, p["norm_mem"][l], p["m_wq"][l], p["m_q_norm"][l], mem_k[l].astype(bf16),
                         mem_v[l].astype(bf16), p["m_wo"][l], seq, tm)
        x = ffn(x, p["norm_ffn"][l], p["f_w_in"][l], p["f_w_out"][l], tm)
    stack = lambda xs: jnp.stack(xs)
    return (x.reshape(n_seq, seq, d), (stack(chunk_v) if chunk_v else None), stack(s_out), stack(shift_out),
            stack(fk), stack(fv), stack(fl))


def kernel(x_prompt, x_sample, mem_prompt, cache_mem_k, cache_mem_v, state_rwkv_S, state_rwkv_shift, cache_fox_k, cache_fox_v, cache_fox_logf, page_table, norm_mix, norm_mem, norm_ffn, ab_w_in, ab_w_out, a_v_norm, a_w_s, a_b_s, b_mu, b_w0, b_w_up, b_a0, b_a_up, b_k_k, b_k_a, b_r_k, b_ln_w, b_ln_b, c_w_in, c_b_f, c_q_norm, c_k_norm, c_w_out, m_mem_norm, m_wq, m_wk, m_wv, m_q_norm, m_k_norm, m_wo, f_w_in, f_w_out):
    cast = lambda w: w.astype(bf16)
    p = dict(norm_mix=norm_mix, norm_mem=norm_mem, norm_ffn=norm_ffn, ab_w_in=cast(ab_w_in),
             ab_w_out=cast(ab_w_out), a_v_norm=a_v_norm, a_w_s=a_w_s, a_b_s=a_b_s, b_mu=b_mu, b_w0=b_w0,
             b_w_up=b_w_up, b_a0=b_a0, b_a_up=b_a_up, b_k_k=b_k_k, b_k_a=b_k_a, b_r_k=b_r_k, b_ln_w=b_ln_w,
             b_ln_b=b_ln_b, c_w_in=cast(c_w_in), c_b_f=c_b_f, c_q_norm=c_q_norm, c_k_norm=c_k_norm,
             c_w_out=cast(c_w_out), m_wq=cast(m_wq), m_q_norm=m_q_norm, m_wo=cast(m_wo),
             f_w_in=cast(f_w_in), f_w_out=cast(f_w_out))
    depth, d = norm_mix.shape
    nb, n_mem = mem_prompt.shape[:2]
    n_even, _, n_heads_b, hd, _ = state_rwkv_S.shape
    mem_heads = d // MEM_HEAD

    mem2 = mem_prompt.reshape(nb * n_mem, d)
    kvs = [mem_kv(mem2, m_mem_norm[l], cast(m_wk[l]), cast(m_wv[l]), m_k_norm[l], 256) for l in range(depth)]
    p_mem_k = jnp.stack([kv[0] for kv in kvs]).reshape(depth, nb, n_mem, d)
    p_mem_v = jnp.stack([kv[1] for kv in kvs]).reshape(depth, nb, n_mem, d)
    s0 = jnp.zeros((n_even, nb, n_heads_b, hd, hd), f32)
    shift0 = jnp.zeros((n_even, nb, state_rwkv_shift.shape[2]), f32)
    y_prompt, _, p_rwkv_s, p_rwkv_shift, p_fox_k, p_fox_v, p_fox_logf = _trunk(
        x_prompt, p_mem_k, p_mem_v, s0, shift0, None, p)

    nd = x_sample.shape[0]
    y_sample, s_chunk_v, s_rwkv_s, s_rwkv_shift, s_fox_k, s_fox_v, s_fox_logf = _trunk(
        x_sample, cache_mem_k, cache_mem_v, state_rwkv_S, state_rwkv_shift, (cache_fox_k, cache_fox_v, cache_fox_logf, page_table), p)
    heads5 = lambda a: a.reshape(depth, nb, n_mem, mem_heads, MEM_HEAD)
    return (y_prompt, y_sample, heads5(p_mem_k), heads5(p_mem_v), p_rwkv_s, p_rwkv_shift, p_fox_k, p_fox_v,
            p_fox_logf, s_chunk_v, s_rwkv_s, s_rwkv_shift, s_fox_k, s_fox_v, s_fox_logf)
```

```python
import functools
import math

import jax
import jax.numpy as jnp
from jax import lax
from jax.experimental import pallas as pl
from jax.experimental.pallas import tpu as pltpu

f32 = jnp.float32
bf16 = jnp.bfloat16

RMS_EPS = 1e-6
GN_EPS = 64e-5
L2_EPS = 1e-12
LOG2E = 1.4426950408889634
LANES = 128
HEAD = 64
A_GROUP = 128
MEM_HEAD = 256
VMEM_LIMIT = 56 * 1024 * 1024


def _params(sem):
    return pltpu.CompilerParams(dimension_semantics=sem, vmem_limit_bytes=VMEM_LIMIT)


def _bdot(a, b):
    return jnp.dot(a.astype(bf16), b.astype(bf16), preferred_element_type=f32)


def _bdot_nt(a, b):
    return lax.dot_general(a.astype(bf16), b.astype(bf16), (((1,), (1,)), ((), ())),
                           preferred_element_type=f32)


def _rms_rows(x, g):
    return x * lax.rsqrt(jnp.mean(x * x, axis=-1, keepdims=True) + RMS_EPS) * g


def _sigmoid(x):
    return 1.0 / (1.0 + jnp.exp(-x))


def _lane_iota(shape):
    return lax.broadcasted_iota(jnp.int32, shape, len(shape) - 1)


def _half_sums(x):
    lo = _lane_iota(x.shape) < HEAD
    s_lo = jnp.sum(jnp.where(lo, x, 0.0), axis=-1, keepdims=True)
    s_hi = jnp.sum(jnp.where(lo, 0.0, x), axis=-1, keepdims=True)
    return jnp.where(lo, s_lo, s_hi)


def _per_head_sums(x):
    n = x.shape[-1] // LANES
    return jnp.concatenate([_half_sums(x[:, i * LANES:(i + 1) * LANES]) for i in range(n)], axis=-1)


def _norm_matmul_kernel(x_ref, g_ref, w_ref, o_ref):
    h = _rms_rows(x_ref[...], g_ref[...])
    o_ref[...] = _bdot(h, w_ref[...])


def norm_matmul(x, g, w, tm):
    m, d = x.shape
    n = w.shape[1]
    return pl.pallas_call(
        _norm_matmul_kernel,
        grid=(m // tm,),
        in_specs=[pl.BlockSpec((tm, d), lambda i: (i, 0)),
                  pl.BlockSpec((1, d), lambda i: (0, 0)),
                  pl.BlockSpec((d, n), lambda i: (0, 0))],
        out_specs=pl.BlockSpec((tm, n), lambda i: (i, 0)),
        out_shape=jax.ShapeDtypeStruct((m, n), f32),
        compiler_params=_params(("parallel",)),
        name="norm_matmul",
    )(x, g.reshape(1, d), w)


def _res_matmul_kernel(*refs):
    x_ref, o_ref = refs[0], refs[-1]
    acc = x_ref[...]
    for a_ref, w_ref in zip(refs[1:-1:2], refs[2:-1:2]):
        acc = acc + _bdot(a_ref[...], w_ref[...])
    o_ref[...] = acc


def res_matmul(x, pairs, tm):
    m, d = x.shape
    in_specs = [pl.BlockSpec((tm, d), lambda i: (i, 0))]
    args = [x]
    for a, w in pairs:
        in_specs += [pl.BlockSpec((tm, a.shape[1]), lambda i: (i, 0)),
                     pl.BlockSpec(w.shape, lambda i: (0, 0))]
        args += [a, w]
    return pl.pallas_call(
        _res_matmul_kernel,
        grid=(m // tm,),
        in_specs=in_specs,
        out_specs=pl.BlockSpec((tm, d), lambda i: (i, 0)),
        out_shape=jax.ShapeDtypeStruct((m, d), f32),
        compiler_params=_params(("parallel",)),
        name="res_matmul",
    )(*args)


def _ffn_kernel(x_ref, g_ref, wi_ref, wo_ref, o_ref, *, d_ff, tf):
    x = x_ref[...]
    h = _rms_rows(x, g_ref[...]).astype(bf16)
    acc = x
    for c in range(d_ff // tf):
        gate = jnp.dot(h, wi_ref[:, c * tf:(c + 1) * tf], preferred_element_type=f32)
        up = jnp.dot(h, wi_ref[:, d_ff + c * tf:d_ff + (c + 1) * tf], preferred_element_type=f32)
        act = gate * _sigmoid(gate) * up
        acc = acc + _bdot(act, wo_ref[c * tf:(c + 1) * tf, :])
    o_ref[...] = acc


def ffn(x, g, w_in, w_out, tm):
    m, d = x.shape
    d_ff = w_out.shape[0]
    tf = 256
    return pl.pallas_call(
        functools.partial(_ffn_kernel, d_ff=d_ff, tf=tf),
        grid=(m // tm,),
        in_specs=[pl.BlockSpec((tm, d), lambda i: (i, 0)),
                  pl.BlockSpec((1, d), lambda i: (0, 0)),
                  pl.BlockSpec(w_in.shape, lambda i: (0, 0)),
                  pl.BlockSpec(w_out.shape, lambda i: (0, 0))],
        out_specs=pl.BlockSpec((tm, d), lambda i: (i, 0)),
        out_shape=jax.ShapeDtypeStruct((m, d), f32),
        compiler_params=_params(("parallel",)),
        name="ffn",
    )(x, g.reshape(1, d), w_in, w_out)


def _mem_kv_kernel(m_ref, g_ref, wk_ref, wv_ref, kg_ref, k_ref, v_ref):
    h = _rms_rows(m_ref[...], g_ref[...]).astype(bf16)
    k = jnp.dot(h, wk_ref[...], preferred_element_type=f32)
    kg = kg_ref[...]
    for hd in range(k.shape[1] // MEM_HEAD):
        sl = slice(hd * MEM_HEAD, (hd + 1) * MEM_HEAD)
        k_ref[:, sl] = _rms_rows(k[:, sl], kg)
    v_ref[...] = jnp.dot(h, wv_ref[...], preferred_element_type=f32)


def mem_kv(mem, g, wk, wv, k_gain, tm):
    m, d = mem.shape
    return pl.pallas_call(
        _mem_kv_kernel,
        grid=(m // tm,),
        in_specs=[pl.BlockSpec((tm, d), lambda i: (i, 0)),
                  pl.BlockSpec((1, d), lambda i: (0, 0)),
                  pl.BlockSpec((d, d), lambda i: (0, 0)),
                  pl.BlockSpec((d, d), lambda i: (0, 0)),
                  pl.BlockSpec((1, MEM_HEAD), lambda i: (0, 0))],
        out_specs=[pl.BlockSpec((tm, d), lambda i: (i, 0))] * 2,
        out_shape=[jax.ShapeDtypeStruct((m, d), f32)] * 2,
        compiler_params=_params(("parallel",)),
        name="mem_kv",
    )(mem, g.reshape(1, d), wk, wv, k_gain.reshape(1, MEM_HEAD))


def _mem_attn_kernel(x_ref, g_ref, wq_ref, qg_ref, k_ref, v_ref, wo_ref, *rest):
    o_ref = rest[-1]
    x = x_ref[...]
    for a_ref, w_ref in zip(rest[:-1:2], rest[1:-1:2]):
        x = x + _bdot(a_ref[...], w_ref[...])
    h = _rms_rows(x, g_ref[...])
    q = _bdot(h, wq_ref[...])
    qg = qg_ref[...] * (MEM_HEAD ** -0.5)
    outs = []
    for hd in range(q.shape[1] // MEM_HEAD):
        sl = slice(hd * MEM_HEAD, (hd + 1) * MEM_HEAD)
        qn = _rms_rows(q[:, sl], qg)
        s = _bdot_nt(qn, k_ref[0, :, sl])
        p = jnp.exp(s - jnp.max(s, axis=-1, keepdims=True))
        l = jnp.sum(p, axis=-1, keepdims=True)
        outs.append(_bdot(p, v_ref[0, :, sl]) / l)
    o = jnp.concatenate(outs, axis=-1)
    o_ref[...] = x + _bdot(o, wo_ref[...])


def mem_attn(x, pairs, g, wq, q_gain, k, v, wo, rows_per_seq, tm):
    m, d = x.shape
    n_mem = k.shape[1]
    per = rows_per_seq // tm
    extra_specs, extra = [], []
    for a, w in pairs:
        extra_specs += [pl.BlockSpec((tm, a.shape[1]), lambda i: (i, 0)), pl.BlockSpec(w.shape, lambda i: (0, 0))]
        extra += [a, w]
    return pl.pallas_call(
        _mem_attn_kernel,
        grid=(m // tm,),
        in_specs=[pl.BlockSpec((tm, d), lambda i: (i, 0)),
                  pl.BlockSpec((1, d), lambda i: (0, 0)),
                  pl.BlockSpec((d, d), lambda i: (0, 0)),
                  pl.BlockSpec((1, MEM_HEAD), lambda i: (0, 0)),
                  pl.BlockSpec((1, n_mem, d), lambda i: (i // per, 0, 0)),
                  pl.BlockSpec((1, n_mem, d), lambda i: (i // per, 0, 0)),
                  pl.BlockSpec((d, d), lambda i: (0, 0))] + extra_specs,
        out_specs=pl.BlockSpec((tm, d), lambda i: (i, 0)),
        out_shape=jax.ShapeDtypeStruct((m, d), f32),
        compiler_params=_params(("parallel",)),
        name="mem_attn",
    )(x, g.reshape(1, d), wq, q_gain.reshape(1, MEM_HEAD), k, v, wo, *extra)


def _gelu(x):
    return 0.5 * x * (1.0 + jnp.tanh(0.7978845608028654 * (x + 0.044715 * x * x * x)))


def _mixer_a_kernel(z_ref, vg_ref, w_ref, b_ref, ya_ref, *, width):
    ge = _gelu(z_ref[...])
    tm = ge.shape[0]
    row = lax.broadcasted_iota(jnp.int32, (A_GROUP, A_GROUP), 0)
    col = lax.broadcasted_iota(jnp.int32, (A_GROUP, A_GROUP), 1)
    for g in range(width // A_GROUP):
        u = ge[:, g * A_GROUP:(g + 1) * A_GROUP]
        v = ge[:, width + g * A_GROUP:width + (g + 1) * A_GROUP]
        va = _rms_rows(v, vg_ref[g:g + 1, :]).astype(bf16)
        w = jnp.where(row >= col, w_ref[g], 0.0).astype(bf16)
        for c in range(tm // A_GROUP):
            rs = slice(c * A_GROUP, (c + 1) * A_GROUP)
            s = jnp.dot(w, va[rs], preferred_element_type=f32) + b_ref[g]
            ya_ref[rs, g * A_GROUP:(g + 1) * A_GROUP] = (u[rs] * s).astype(ya_ref.dtype)


def mixer_a(z, v_gain, w_s, b_s, width, tm):
    m = z.shape[0]
    ng = width // A_GROUP
    b_rows = jnp.broadcast_to(b_s[:, :, None], (ng, A_GROUP, A_GROUP))
    return pl.pallas_call(
        functools.partial(_mixer_a_kernel, width=width),
        grid=(m // tm,),
        in_specs=[pl.BlockSpec((tm, 2 * width), lambda i: (i, 0)),
                  pl.BlockSpec((ng, A_GROUP), lambda i: (0, 0)),
                  pl.BlockSpec((ng, A_GROUP, A_GROUP), lambda i: (0, 0, 0)),
                  pl.BlockSpec((ng, A_GROUP, A_GROUP), lambda i: (0, 0, 0))],
        out_specs=pl.BlockSpec((tm, width), lambda i: (i, 0)),
        out_shape=jax.ShapeDtypeStruct((m, width), bf16),
        compiler_params=_params(("parallel",)),
        name="mixer_a",
    )(z, v_gain, w_s, b_rows)


def _mixer_a_step_kernel(z_ref, vg_ref, w0_ref, b0_ref, ya_ref, va_ref, *, width):
    ge = _gelu(z_ref[...])
    for g in range(width // A_GROUP):
        sl = slice(g * A_GROUP, (g + 1) * A_GROUP)
        u = ge[:, sl]
        v = ge[:, width + g * A_GROUP:width + (g + 1) * A_GROUP]
        va = _rms_rows(v, vg_ref[g:g + 1, :])
        va_ref[:, sl] = va
        ya_ref[:, sl] = (u * (w0_ref[:, sl] * va + b0_ref[:, sl])).astype(ya_ref.dtype)


def mixer_a_step(z, v_gain, w_s, b_s, width):
    m = z.shape[0]
    w0 = jnp.repeat(w_s[:, 0, 0], A_GROUP).reshape(1, width)
    b0 = jnp.repeat(b_s[:, 0], A_GROUP).reshape(1, width)
    return pl.pallas_call(
        functools.partial(_mixer_a_step_kernel, width=width),
        grid=(1,),
        in_specs=[pl.BlockSpec((m, 2 * width), lambda i: (0, 0)),
                  pl.BlockSpec(v_gain.shape, lambda i: (0, 0)),
                  pl.BlockSpec((1, width), lambda i: (0, 0)),
                  pl.BlockSpec((1, width), lambda i: (0, 0))],
        out_specs=[pl.BlockSpec((m, width), lambda i: (0, 0))] * 2,
        out_shape=[jax.ShapeDtypeStruct((m, width), bf16), jax.ShapeDtypeStruct((m, width), f32)],
        compiler_params=_params(("arbitrary",)),
        name="mixer_a_step",
    )(z, v_gain, w0, b0)


def _split3(x):
    hi = x.astype(bf16)
    r1 = x - hi.astype(f32)
    mid = r1.astype(bf16)
    lo = (r1 - mid.astype(f32)).astype(bf16)
    return hi, mid, lo


def _dot01_left(sel, x):
    hi, mid, lo = _split3(x)
    d = lambda p: jnp.dot(sel, p, preferred_element_type=f32)
    return d(hi) + d(mid) + d(lo)


def _dot01_right(x, sel):
    hi, mid, lo = _split3(x)
    d = lambda p: jnp.dot(p, sel, preferred_element_type=f32)
    return d(hi) + d(mid) + d(lo)


def _log_sigmoid(x):
    return jnp.minimum(x, 0.0) - jnp.log(1.0 + jnp.exp(-jnp.abs(x)))


AUG = 2 * LANES
SHIFT_LANE = 12


def _fox_project(x_ref, g_ref, w_ref, wf_ref, bf_ref, qg_ref, kg_ref, width, unit):
    h = _rms_rows(x_ref[...], g_ref[...]).astype(bf16)
    z = jnp.dot(h, w_ref[...], preferred_element_type=f32)
    q, k = z[:, :width], z[:, width:2 * width]
    v, g = z[:, 2 * width:3 * width], z[:, 3 * width:4 * width]
    qn = q * lax.rsqrt(_per_head_sums(q * q) * (1.0 / HEAD) + RMS_EPS) * (qg_ref[...] * (HEAD ** -0.5 * unit))
    kn = k * lax.rsqrt(_per_head_sums(k * k) * (1.0 / HEAD) + RMS_EPS) * kg_ref[...]
    lf = _log_sigmoid(jnp.dot(h, wf_ref[...], preferred_element_type=f32) + bf_ref[...])
    return qn, kn, v, g, lf


def _fox_in_args(x, g, w, wf, b_f, q_gain, k_gain):
    d = x.shape[1]
    width = w.shape[1] // 4
    n_heads = width // HEAD
    wf_pad = jnp.zeros((d, LANES), bf16).at[:, :n_heads].set(wf)
    bf_pad = jnp.zeros((1, LANES), f32).at[0, :n_heads].set(b_f)
    qg = jnp.tile(q_gain, n_heads).reshape(1, width)
    kg = jnp.tile(k_gain, n_heads).reshape(1, width)
    const = lambda i: (0, 0)
    specs = [pl.BlockSpec((1, d), const), pl.BlockSpec(w.shape, const), pl.BlockSpec((d, LANES), const),
             pl.BlockSpec((1, LANES), const), pl.BlockSpec((1, width), const), pl.BlockSpec((1, width), const)]
    return specs, (g.reshape(1, d), w, wf_pad, bf_pad, qg, kg)


def _fox_step_in_kernel(x_ref, g_ref, w_ref, wf_ref, bf_ref, qg_ref, kg_ref,
                        q_ref, k_ref, v_ref, sg_ref, lf_ref, *, width, n_heads):
    qn, kn, v, g, lf = _fox_project(x_ref, g_ref, w_ref, wf_ref, bf_ref, qg_ref, kg_ref, width, 1.0)
    q_ref[...] = qn.astype(bf16).astype(f32)
    k_ref[...] = kn
    v_ref[...] = v
    sg_ref[...] = _sigmoid(g)
    lf_ref[...] = lf[:, :n_heads]


def fox_step_in(x, g, w, wf, b_f, q_gain, k_gain):
    m, d = x.shape
    width = w.shape[1] // 4
    n_heads = width // HEAD
    specs, args = _fox_in_args(x, g, w, wf, b_f, q_gain, k_gain)
    whole = lambda n: pl.BlockSpec((m, n), lambda i: (0, 0))
    return pl.pallas_call(
        functools.partial(_fox_step_in_kernel, width=width, n_heads=n_heads),
        grid=(1,),
        in_specs=[whole(d)] + specs,
        out_specs=[whole(width)] * 4 + [whole(n_heads)],
        out_shape=[jax.ShapeDtypeStruct((m, width), f32)] * 4 + [jax.ShapeDtypeStruct((m, n_heads), f32)],
        compiler_params=_params(("arbitrary",)),
        name="fox_step_in",
    )(x, *args)


def _fox_seq_in_kernel(x_ref, g_ref, w_ref, wf_ref, bf_ref, qg_ref, kg_ref,
                       q_ref, kt_ref, ka_ref, qa_ref, vt_ref, vtb_ref, sg_ref, lft_ref, c_ref, carry_ref,
                       *, tiles_per_seq, width, n_heads):
    i = pl.program_id(0)
    qn, kn, v, g, lf = _fox_project(x_ref, g_ref, w_ref, wf_ref, bf_ref, qg_ref, kg_ref, width, LOG2E)
    tm = lf.shape[0]
    q_ref[...] = qn.astype(bf16)
    sg_ref[...] = _sigmoid(g)
    for blk in range(width // LANES):
        sl = slice(blk * LANES, (blk + 1) * LANES)
        kt_ref[0, sl, :] = kn[:, sl].T
        vt = v[:, sl].T
        vt_ref[0, sl, :] = vt
        vtb_ref[0, 0, sl, :] = vt.astype(bf16)
    lft_ref[0] = lf.T[:n_heads, :]

    @pl.when(i % tiles_per_seq == 0)
    def _():
        carry_ref[...] = jnp.zeros_like(carry_ref)

    row = lax.broadcasted_iota(jnp.int32, (tm, tm), 0)
    col = lax.broadcasted_iota(jnp.int32, (tm, tm), 1)
    tri = jnp.where(row >= col, 1.0, 0.0).astype(bf16)
    c = _dot01_left(tri, lf) + carry_ref[...]
    carry_ref[...] = c[tm - 1:tm, :]
    c = c * LOG2E
    c_ref[...] = c
    pieces = jnp.concatenate(_split3(c), axis=-1)
    rr = lax.broadcasted_iota(jnp.int32, (3 * LANES, LANES), 0)
    cc = lax.broadcasted_iota(jnp.int32, (3 * LANES, LANES), 1)
    piece, head = rr // LANES, rr % LANES
    lane = _lane_iota((1, LANES))
    k_const = jnp.where((lane < 6) | (lane == SHIFT_LANE), 1.0, 0.0)
    q_const = jnp.where((lane >= 6) & (lane < 12), -1.0, 0.0)
    knb = kn.astype(bf16)
    for p in range(n_heads // 2):
        place = lambda base: jnp.where(((head == 2 * p) & (cc == base + piece))
                                       | ((head == 2 * p + 1) & (cc == base + 3 + piece)), 1.0, 0.0).astype(bf16)
        both = jnp.dot(pieces, jnp.concatenate([place(6), place(0)], axis=-1), preferred_element_type=f32)
        k_aug = both[:, :LANES] + k_const
        ka_ref[p] = jnp.concatenate([knb[:, p * LANES:(p + 1) * LANES], k_aug.astype(bf16)], axis=-1)
        qa_ref[p] = (both[:, LANES:] + q_const).astype(bf16)


def fox_seq_in(x, g, w, wf, b_f, q_gain, k_gain, n_seq, tm):
    m, d = x.shape
    width = w.shape[1] // 4
    n_heads = width // HEAD
    n_pairs = n_heads // 2
    seq = m // n_seq
    tps = seq // tm
    specs, args = _fox_in_args(x, g, w, wf, b_f, q_gain, k_gain)
    row = lambda i: (i, 0)
    chan = lambda i: (i // tps, 0, i % tps)
    return pl.pallas_call(
        functools.partial(_fox_seq_in_kernel, tiles_per_seq=tps, width=width, n_heads=n_heads),
        grid=(m // tm,),
        in_specs=[pl.BlockSpec((tm, d), row)] + specs,
        out_specs=[pl.BlockSpec((tm, width), row),
                   pl.BlockSpec((1, width, tm), chan),
                   pl.BlockSpec((n_pairs, tm, AUG), lambda i: (0, i, 0)),
                   pl.BlockSpec((n_pairs, tm, LANES), lambda i: (0, i, 0)),
                   pl.BlockSpec((1, width, tm), chan),
                   pl.BlockSpec((1, 1, width, tm), lambda i: (i // tps, i % tps, 0, 0)),
                   pl.BlockSpec((tm, width), row),
                   pl.BlockSpec((1, n_heads, tm), chan),
                   pl.BlockSpec((tm, LANES), row)],
        out_shape=[jax.ShapeDtypeStruct((m, width), bf16),
                   jax.ShapeDtypeStruct((n_seq, width, seq), f32),
                   jax.ShapeDtypeStruct((n_pairs, m, AUG), bf16),
                   jax.ShapeDtypeStruct((n_pairs, m, LANES), bf16),
                   jax.ShapeDtypeStruct((n_seq, width, seq), f32),
                   jax.ShapeDtypeStruct((n_seq, tps, width, tm), bf16),
                   jax.ShapeDtypeStruct((m, width), f32),
                   jax.ShapeDtypeStruct((n_seq, n_heads, seq), f32),
                   jax.ShapeDtypeStruct((m, LANES), f32)],
        scratch_shapes=[pltpu.VMEM((1, LANES), f32)],
        compiler_params=_params(("arbitrary",)),
        name="fox_seq_in",
    )(x, *args)


def _fox_flash_kernel(lo_ref, qk_ref, q_ref, qa_ref, ka_ref, vt_ref, sg_ref, o_ref, *, tq, tv, bounded):
    b, p, qi = pl.program_id(0), pl.program_id(1), pl.program_id(2)
    nsub = tq // tv
    q2 = q_ref[0]
    lo = _lane_iota((tq, LANES)) < HEAD
    zero = jnp.zeros_like(q2)
    bias2 = qa_ref[0]
    lane = _lane_iota((tq, LANES))
    shift = jnp.where(lane == SHIFT_LANE, -qk_ref[0], 0.0).astype(bf16)
    qa = []
    for j in (0, 1):
        own = ((lane >= 3 * j) & (lane < 3 * j + 3)) | ((lane >= 6 + 3 * j) & (lane < 9 + 3 * j))
        bias = jnp.where(own, bias2, zero)
        if bounded:
            bias = jnp.where(lane == SHIFT_LANE, shift, bias)
        qh = jnp.where(lo, q2, zero) if j == 0 else jnp.where(lo, zero, q2)
        qa.append(jnp.concatenate([qh, bias], axis=-1))
    row = lax.broadcasted_iota(jnp.int32, (tq, tq), 0)
    col = lax.broadcasted_iota(jnp.int32, (tq, tq), 1)

    def scores(kb, j):
        ka = ka_ref[0, 0, pl.ds(pl.multiple_of(kb * tq, tq), tq), :]
        return _bdot_nt(ka, qa[j])

    def softmax_pv(kb, j, t, stats, masked):
        m_old, l_old, acc = stats
        if masked:
            t = jnp.where(row <= col, t, -jnp.inf)
        m_new = jnp.maximum(m_old, jnp.max(t, axis=0, keepdims=True))
        pe = jnp.exp2(t - m_new).astype(bf16)
        alpha = jnp.exp2(m_old - m_new)
        l_new = alpha * l_old + jnp.sum(pe.astype(f32), axis=0, keepdims=True)
        pv = sum(jnp.dot(vt_ref[0, kb * nsub + s, j * HEAD:(j + 1) * HEAD, :], pe[s * tv:(s + 1) * tv],
                         preferred_element_type=f32) for s in range(nsub))
        return m_new, l_new, acc * alpha + pv

    def block(kb, t0, st0, st1, masked, last):
        t1 = scores(kb, 1)
        st0 = softmax_pv(kb, 0, t0, st0, masked)
        t0_next = t0 if last else scores(kb + 1, 0)
        st1 = softmax_pv(kb, 1, t1, st1, masked)
        return t0_next, st0, st1

    def pv(kb, j, pe):
        return sum(jnp.dot(vt_ref[0, kb * nsub + s, j * HEAD:(j + 1) * HEAD, :], pe[s * tv:(s + 1) * tv],
                   preferred_element_type=f32) for s in range(nsub))

    def free_blocks(kbs, sts):
        units = [(kb, j) for kb in kbs for j in (0, 1)]
        l = [sts[0][1], sts[1][1]]
        acc = [sts[0][2], sts[1][2]]
        t_next = scores(*units[0])
        for n, (kb, j) in enumerate(units):
            t = t_next
            if n + 1 < len(units):
                t_next = scores(*units[n + 1])
            pe = jnp.exp2(jnp.where((row <= col) | (kb < qi), t, -jnp.inf))
            l[j] = l[j] + jnp.sum(pe, axis=0, keepdims=True)
            acc[j] = acc[j] + pv(kb, j, pe.astype(bf16))
        return (sts[0][0], l[0], acc[0]), (sts[1][0], l[1], acc[1])

    init = (jnp.full((1, tq), -jnp.inf, f32), jnp.zeros((1, tq), f32), jnp.zeros((HEAD, tq), f32))
    first = lo_ref[b, p, qi]
    if bounded:
        odd = (qi - first + 1) % 2
        sts = lax.cond(odd == 1, lambda s: free_blocks((first,), s), lambda s: s, (init, init))
        (m0, l0, a0), (m1, l1, a1) = lax.fori_loop(
            0, (qi - first + 1) // 2,
            lambda i, s: free_blocks((first + odd + 2 * i, first + odd + 2 * i + 1), s), sts)
    else:
        state = lax.fori_loop(first, qi, lambda kb, s: block(kb, *s, False, False),
                              (scores(first, 0), init, init))
        _, (m0, l0, a0), (m1, l1, a1) = block(qi, *state, True, True)
    o = jnp.concatenate([a0 / l0, a1 / l1], axis=0).T
    o_ref[0] = (o * sg_ref[0]).astype(o_ref.dtype)


QK_BOUNDED_MAX = 40.0


def fox_flash(q, qa, ka, vtb, sg, first_block, qk, n_seq, tq):
    m, width = q.shape
    seq = m // n_seq
    n_pairs = width // LANES
    tv = vtb.shape[3]
    r3 = lambda a: a.reshape(n_seq, seq, a.shape[1])
    tile = pl.BlockSpec((1, tq, LANES), lambda b, p, i, lo, qk_: (b, i, p))
    grid_spec = pltpu.PrefetchScalarGridSpec(
        num_scalar_prefetch=2,
        grid=(n_seq, n_pairs, seq // tq),
        in_specs=[tile,
                  pl.BlockSpec((1, tq, LANES), lambda b, p, i, lo, qk_: (p, b * (seq // tq) + i, 0)),
                  pl.BlockSpec((1, 1, seq, AUG), lambda b, p, i, lo, qk_: (p, b, 0, 0)),
                  pl.BlockSpec((1, seq // tv, LANES, tv), lambda b, p, i, lo, qk_: (b, 0, p, 0)),
                  tile],
        out_specs=tile,
    )
    args = (first_block, qk.reshape(1).astype(f32), r3(q), qa, ka.reshape(n_pairs, n_seq, seq, AUG), vtb,
            r3(sg))
    call = lambda bounded: pl.pallas_call(
        functools.partial(_fox_flash_kernel, tq=tq, tv=tv, bounded=bounded),
        grid_spec=grid_spec,
        out_shape=jax.ShapeDtypeStruct((n_seq, seq, width), bf16),
        compiler_params=_params(("parallel", "parallel", "arbitrary")),
        name="fox_flash_bounded" if bounded else "fox_flash",
    )
    out = lax.cond(qk <= QK_BOUNDED_MAX, lambda a: call(True)(*a), lambda a: call(False)(*a), args)
    return out.reshape(m, width)


FOX_PAGES_PER_STEP = 16
FOX_VALUE_PAGES_PER_STEP = 32
EXP_UNDERFLOW = 104.0


def _fox_scores_kernel(pt_ref, qc_ref, knt_ref, lfn_ref, *rest, n_heads, pps):
    kt_refs, lf_refs = rest[:pps], rest[pps:2 * pps]
    t_ref, town_ref, m_out, l_out, m_ref, l_ref, carry_ref = rest[2 * pps:]
    pg = pl.program_id(1)
    page = lf_refs[0].shape[2]

    def score(kt_ref, bias):
        rows = [jnp.sum(kt_ref[0, h] * qc_ref[0, h], axis=0, keepdims=True) for h in range(n_heads)]
        t = jnp.concatenate(rows, axis=0) + bias
        m_old = m_ref[...]
        m_new = jnp.maximum(m_old, jnp.max(t, axis=-1, keepdims=True))
        l_ref[...] = jnp.exp(m_old - m_new) * l_ref[...] + jnp.sum(jnp.exp(t - m_new), axis=-1, keepdims=True)
        m_ref[...] = m_new
        return t

    @pl.when(pg == 0)
    def _():
        m_ref[...] = jnp.full(m_ref.shape, -jnp.inf, f32)
        l_ref[...] = jnp.zeros_like(l_ref)
        carry_ref[...] = lfn_ref[0]
        town_ref[0] = score(knt_ref, jnp.where(_lane_iota((n_heads, page)) == 0, 0.0, -jnp.inf))

    r = lax.broadcasted_iota(jnp.int32, (page, page), 0)
    cidx = lax.broadcasted_iota(jnp.int32, (page, page), 1)
    later = jnp.where(r > cidx, 1.0, 0.0).astype(bf16)
    for j in range(pps):
        lf = lf_refs[j][0]
        t_ref[0, j] = score(kt_refs[j], _dot01_right(lf, later) + carry_ref[...])
        carry_ref[...] = carry_ref[...] + jnp.sum(lf, axis=-1, keepdims=True)

    @pl.when(pg == pl.num_programs(1) - 1)
    def _():
        m_out[0] = m_ref[...]
        l_out[0] = l_ref[...]


def _fox_values_kernel(pt_ref, idx_ref, cnt_ref, town_ref, vnt_ref, m_ref, l_ref, sg_ref, *rest, n_heads, pps):
    t_refs, vt_refs = rest[:pps], rest[pps:2 * pps]
    o_ref, acc_ref = rest[2 * pps:]
    i, pg = pl.program_id(0), pl.program_id(1)
    m = m_ref[0]

    @pl.when(pg == 0)
    def _():
        pe = jnp.exp(town_ref[0] - m)
        for h in range(n_heads):
            acc_ref[h] = pe[h:h + 1, :] * vnt_ref[0, h]

    for j in range(pps):
        @pl.when(pg * pps + j < cnt_ref[i])
        def _():
            pe = jnp.exp(t_refs[j][0, 0] - m)
            for h in range(n_heads):
                acc_ref[h] = acc_ref[h] + pe[h:h + 1, :] * vt_refs[j][0, h]

    @pl.when(pg == pl.num_programs(1) - 1)
    def _():
        l = l_ref[0]
        for h in range(n_heads):
            o_ref[0, h] = jnp.sum(acc_ref[h], axis=-1, keepdims=True) / l[h:h + 1, :] * sg_ref[0, h]


def fox_step(q, k_new, v_new, lf_new, sg, cache_k, cache_v, cache_lf, page_table):
    b, width = q.shape
    pool, page, n_heads = cache_lf.shape
    n_pages = page_table.shape[1]
    pps = math.gcd(FOX_PAGES_PER_STEP, n_pages)
    kt = jnp.transpose(cache_k, (0, 2, 3, 1))
    vt = jnp.transpose(cache_v, (0, 2, 3, 1))
    lft = jnp.transpose(cache_lf, (0, 2, 1))
    col = lambda a: a.reshape(b, n_heads, HEAD, 1)
    own_page = lambda a: jnp.zeros((b, n_heads, HEAD, page), f32).at[..., 0].set(a.reshape(b, n_heads, HEAD))
    qc = jnp.broadcast_to(col(q), (b, n_heads, HEAD, page))
    per_seq = lambda shape: pl.BlockSpec((1,) + shape, lambda i, g, *_: (i,) + (0,) * len(shape))
    page_spec = lambda shape, j: pl.BlockSpec(
        (1,) + shape, lambda i, g, pt: (pt[i, n_pages - 1 - (g * pps + j)],) + (0,) * len(shape))
    stat = jax.ShapeDtypeStruct((b, n_heads, 1), f32)
    t_all, t_own, m, l = pl.pallas_call(
        functools.partial(_fox_scores_kernel, n_heads=n_heads, pps=pps),
        grid_spec=pltpu.PrefetchScalarGridSpec(
            num_scalar_prefetch=1,
            grid=(b, n_pages // pps),
            in_specs=[per_seq((n_heads, HEAD, page))] * 2 + [per_seq((n_heads, 1))]
                     + [page_spec((n_heads, HEAD, page), j) for j in range(pps)]
                     + [page_spec((n_heads, page), j) for j in range(pps)],
            out_specs=[pl.BlockSpec((1, pps, n_heads, page), lambda i, g, pt: (i, g, 0, 0)),
                       per_seq((n_heads, page)), per_seq((n_heads, 1)), per_seq((n_heads, 1))],
            scratch_shapes=[pltpu.VMEM((n_heads, 1), f32)] * 3),
        out_shape=[jax.ShapeDtypeStruct((b, n_pages, n_heads, page), f32),
                   jax.ShapeDtypeStruct((b, n_heads, page), f32), stat, stat],
        compiler_params=_params(("parallel", "arbitrary")),
        name="fox_step_scores",
    )(page_table, qc, own_page(k_new), lf_new.reshape(b, n_heads, 1), *([kt] * pps), *([lft] * pps))

    live = jnp.any(jnp.max(t_all, axis=-1) - m[:, None, :, 0] > -EXP_UNDERFLOW, axis=-1)
    live_idx = jnp.argsort(jnp.logical_not(live), axis=1, stable=True).astype(jnp.int32)
    count = jnp.sum(live, axis=1).astype(jnp.int32)

    pps = math.gcd(FOX_VALUE_PAGES_PER_STEP, n_pages)

    def slot(i, g, j, idx, cnt):
        return idx[i, jnp.maximum(jnp.minimum(g * pps + j, cnt[i] - 1), 0)]

    t_spec = lambda j: pl.BlockSpec((1, 1, n_heads, page),
                                    lambda i, g, pt, idx, cnt: (i, slot(i, g, j, idx, cnt), 0, 0))
    v_spec = lambda j: pl.BlockSpec((1, n_heads, HEAD, page),
                                    lambda i, g, pt, idx, cnt: (pt[i, n_pages - 1 - slot(i, g, j, idx, cnt)], 0, 0, 0))
    out = pl.pallas_call(
        functools.partial(_fox_values_kernel, n_heads=n_heads, pps=pps),
        grid_spec=pltpu.PrefetchScalarGridSpec(
            num_scalar_prefetch=3,
            grid=(b, n_pages // pps),
            in_specs=[per_seq((n_heads, page)), per_seq((n_heads, HEAD, page)), per_seq((n_heads, 1)),
                      per_seq((n_heads, 1)), per_seq((n_heads, HEAD, 1))]
                     + [t_spec(j) for j in range(pps)] + [v_spec(j) for j in range(pps)],
            out_specs=per_seq((n_heads, HEAD, 1)),
            scratch_shapes=[pltpu.VMEM((n_heads, HEAD, page), f32)]),
        out_shape=jax.ShapeDtypeStruct((b, n_heads, HEAD, 1), f32),
        compiler_params=_params(("parallel", "arbitrary")),
        name="fox_step_values",
    )(page_table, live_idx, count, t_own, own_page(v_new), m, l, col(sg), *([t_all] * pps), *([vt] * pps))
    return out.reshape(b, width).astype(bf16)


def _mem_attn_step_kernel(q_ref, qg_ref, k_ref, v_ref, o_ref):
    q = q_ref[0]
    qn = q * lax.rsqrt(jnp.mean(q * q, axis=-1, keepdims=True) + RMS_EPS) * (qg_ref[...] * MEM_HEAD ** -0.5)
    s = jnp.sum(k_ref[0, 0] * qn, axis=-1, keepdims=True)
    p = jnp.exp(s - jnp.max(s, axis=0, keepdims=True))
    l = jnp.sum(p, axis=0, keepdims=True)
    o_ref[0] = jnp.sum(p * v_ref[0, 0], axis=0, keepdims=True) / l


def mem_attn_step(q, q_gain, cache_k, cache_v, layer):
    b, d = q.shape
    _, _, n_mem, heads, hd = cache_k.shape
    mem_spec = pl.BlockSpec((1, 1, n_mem, heads, hd), lambda i: (layer, i, 0, 0, 0))
    out = pl.pallas_call(
        _mem_attn_step_kernel,
        grid=(b,),
        in_specs=[pl.BlockSpec((1, 1, heads, hd), lambda i: (i, 0, 0, 0)),
                  pl.BlockSpec((1, 1, hd), lambda i: (0, 0, 0)),
                  mem_spec, mem_spec],
        out_specs=pl.BlockSpec((1, 1, heads, hd), lambda i: (i, 0, 0, 0)),
        out_shape=jax.ShapeDtypeStruct((b, 1, heads, hd), f32),
        compiler_params=_params(("parallel",)),
        name="mem_attn_step",
    )(q.reshape(b, 1, heads, hd), q_gain.reshape(1, 1, hd), cache_k, cache_v)
    return out.reshape(b, d)


def _softplus(x):
    return jnp.maximum(x, 0.0) + jnp.log(1.0 + jnp.exp(-jnp.abs(x)))


def _rwkv_kernel(zr_ref, zk_ref, zv_ref, zg_ref, zwa_ref, sh0_ref, s0_ref,
                 mu_ref, w0_ref, a0_ref, kk_ref, ka_ref, rk_ref, lnw_ref, lnb_ref, wup_ref, aup_ref,
                 y_ref, sout_ref, shout_ref,
                 s_ref, carry_ref, r_s, k_s, v_s, al_s, be_s, ld_s, yo_s, bo_s, sg_s,
                 *, chunk, n_valid, width, padded):
    i = pl.program_id(1)
    nblk = pl.num_programs(1)
    nsq, cg = zr_ref.shape[0], zr_ref.shape[1]
    n_pairs = width // LANES

    @pl.when(i == 0)
    def _():
        s_ref[...] = s0_ref[...]
        carry_ref[...] = sh0_ref[...]

    rows = lax.broadcasted_iota(jnp.int32, (cg, 1), 0)
    valid = (rows + i * cg) < n_valid
    last = jnp.minimum(n_valid - 1 - i * cg, cg - 1)

    for s in range(nsq):
        def shifted(z, lane0, n):
            prev = jnp.where(rows == 0, carry_ref[s, :, lane0:lane0 + n], pltpu.roll(z, 1, 0))
            return z + (prev - z) * mu_ref[:, lane0:lane0 + n]

        zr, zk, zv, zg, zwa = zr_ref[s], zk_ref[s], zv_ref[s], zg_ref[s], zwa_ref[s]
        r = shifted(zr, 0, width)
        k = shifted(zk, width, width)
        v = shifted(zv, 2 * width, width)
        gp = shifted(zg, 3 * width, width)
        wa = shifted(zwa, 4 * width, LANES)

        pieces = ((zr, 0, width), (zk, width, width), (zv, 2 * width, width), (zg, 3 * width, width),
                  (zwa, 4 * width, LANES))
        if padded:
            @pl.when(last >= 0)
            def _():
                for z, lane0, n in pieces:
                    carry_ref[s, :, lane0:lane0 + n] = jnp.sum(jnp.where(rows == last, z, 0.0), axis=0,
                                                               keepdims=True)
        else:
            for z, lane0, n in pieces:
                carry_ref[s, :, lane0:lane0 + n] = z[cg - 1:cg, :]

        w = -_softplus(-(w0_ref[...] + _bdot(jnp.tanh(wa), wup_ref[...]))) - 0.5
        logd = -jnp.exp(w)
        a = _sigmoid(a0_ref[...] + _bdot(wa, aup_ref[...]))
        kk = k * kk_ref[...]
        kk = kk * lax.rsqrt(_per_head_sums(kk * kk) + L2_EPS)
        k2 = k * (1.0 + (a - 1.0) * ka_ref[...])
        bo_s[s] = _per_head_sums(r * k2 * rk_ref[...]) * v
        sg_s[s] = _sigmoid(gp)
        r_s[s] = r
        keep = (lambda t: jnp.where(valid, t, 0.0)) if padded else (lambda t: t)
        k_s[s] = keep(k2)
        v_s[s] = keep(v)
        al_s[s] = -kk
        be_s[s] = keep(kk * a)
        ld_s[s] = keep(logd)

    ri = lax.broadcasted_iota(jnp.int32, (chunk, chunk), 0)
    ci = lax.broadcasted_iota(jnp.int32, (chunk, chunk), 1)
    tri = jnp.where(ri >= ci, 1.0, 0.0).astype(bf16)
    lower, strict = ri >= ci, ri > ci
    eye = jnp.where(ri == ci, 1.0, 0.0)
    lo = _lane_iota((1, LANES)) < HEAD
    sr = lax.broadcasted_iota(jnp.int32, (LANES, LANES), 0) < HEAD
    sc = lax.broadcasted_iota(jnp.int32, (LANES, LANES), 1) < HEAD
    same_head = sr == sc
    n_sq = max(int(math.log2(chunk)) - 1, 0)

    def chunk_body(c, _):
        rs = pl.ds(pl.multiple_of(c * chunk, chunk), chunk)
        pairs = [(s, p) for s in range(nsq) for p in range(n_pairs)]
        heads = [(s, p, j) for s, p in pairs for j in (0, 1)]
        lsl = [slice(p * LANES, (p + 1) * LANES) for p in range(n_pairs)]
        xs, vv, vp, yb, yk, ybe, yke, d_end = {}, {}, {}, {}, {}, {}, {}, {}
        for s in range(nsq):
            ld = ld_s[s, rs, :]
            lcum = _dot01_left(tri, ld)
            lend = lcum[chunk - 1:chunk, :]
            e_neg, e_end = jnp.exp(-lcum), jnp.exp(lend - lcum)
            xa_all, xr_all = al_s[s, rs, :] * jnp.exp(lcum - ld), r_s[s, rs, :] * jnp.exp(lcum)
            be, kc, v_all = be_s[s, rs, :], k_s[s, rs, :], v_s[s, rs, :]
            dec = jnp.exp(lend)
            for p in range(n_pairs):
                xs[s, p] = jnp.concatenate([xa_all[:, lsl[p]], xr_all[:, lsl[p]]], axis=0)
                vv[s, p] = v_all[:, lsl[p]]
                vp[s, p] = vv[s, p].astype(bf16)
                yb[s, p], yk[s, p] = (be * e_neg)[:, lsl[p]], (kc * e_neg)[:, lsl[p]]
                ybe[s, p], yke[s, p] = (be * e_end)[:, lsl[p]], (kc * e_end)[:, lsl[p]]
                d_end[s, p] = dec[:, lsl[p]]
        s_bd = {sp: s_ref[sp[0], sp[1]] for sp in pairs}
        a_ab, a_ak, a_rb, a_rk = {}, {}, {}, {}
        for s, p, j in heads:
            xm = jnp.where(lo if j == 0 else jnp.logical_not(lo), xs[s, p], 0.0).astype(bf16)
            gb = _bdot_nt(xm, yb[s, p])
            gk = _bdot_nt(xm, yk[s, p])
            a_ab[s, p, j] = jnp.where(strict, gb[:chunk], 0.0)
            a_ak[s, p, j] = jnp.where(strict, gk[:chunk], 0.0)
            a_rb[s, p, j] = jnp.where(lower, gb[chunk:], 0.0)
            a_rk[s, p, j] = jnp.where(lower, gk[chunk:], 0.0)
        tinv = {h: eye + a_ab[h] for h in heads}
        pw = dict(a_ab)
        for _ in range(n_sq):
            for h in heads:
                pw[h] = _bdot(pw[h], pw[h])
            for h in heads:
                tinv[h] = tinv[h] + _bdot(tinv[h], pw[h])
        xh = {sp: _bdot_nt(xs[sp], s_bd[sp]) for sp in pairs}
        av = {(s, p, j): _bdot(a_ak[s, p, j], vp[s, p]) for s, p, j in heads}
        uh = {(s, p, j): _bdot(tinv[s, p, j], xh[s, p][:chunk] + av[s, p, j]) for s, p, j in heads}
        u = {(s, p): jnp.where(lo, uh[s, p, 0], uh[s, p, 1]) for s, p in pairs}
        yh = {(s, p, j): _bdot(a_rb[s, p, j], u[s, p]) + _bdot(a_rk[s, p, j], vp[s, p]) for s, p, j in heads}
        for s, p in pairs:
            yo_s[s, rs, lsl[p]] = xh[s, p][chunk:] + jnp.where(lo, yh[s, p, 0], yh[s, p, 1])
            uv_t = jnp.concatenate([u[s, p], vv[s, p]], axis=0).T
            upd = _bdot(uv_t, jnp.concatenate([ybe[s, p], yke[s, p]], axis=0))
            s_ref[s, p] = s_bd[s, p] * d_end[s, p] + jnp.where(same_head, upd, 0.0)
        return 0

    lax.fori_loop(0, cg // chunk, chunk_body, 0)

    for s in range(nsq):
        y = yo_s[s]
        mean = _per_head_sums(y) * (1.0 / HEAD)
        yc = y - mean
        var = _per_head_sums(yc * yc) * (1.0 / HEAD)
        y = yc * lax.rsqrt(var + GN_EPS) * lnw_ref[...] + lnb_ref[...]
        y_ref[s] = ((y + bo_s[s]) * sg_s[s]).astype(y_ref.dtype)

    @pl.when(i == nblk - 1)
    def _():
        sout_ref[...] = s_ref[...]
        shout_ref[...] = carry_ref[...]


def rwkv7(z, col0, shift0, s0, prm, n_seq, n_valid, chunk, block_rows):
    m = z.shape[0]
    rows = m // n_seq
    nblk = rows // block_rows
    width = prm["b_w0"].shape[0]
    n_heads = width // HEAD
    n_pairs = width // LANES
    bcols = 4 * width + LANES
    s0p = s0.reshape(n_seq, n_pairs, 2, HEAD, HEAD)
    zeros = jnp.zeros_like(s0p[:, :, 0])
    s0bd = jnp.concatenate([jnp.concatenate([s0p[:, :, 0], zeros], axis=-1),
                            jnp.concatenate([zeros, s0p[:, :, 1]], axis=-1)], axis=-2)
    lora = lambda w_, off: jnp.zeros((LANES, width), bf16).at[off:off + w_.shape[0]].set(w_.astype(bf16))
    row1 = lambda a: a.reshape(1, -1).astype(f32)
    cb = col0 // width
    nsq = math.gcd(RWKV_SEQS_PER_STEP, n_seq)
    z3 = z.reshape(n_seq, rows, z.shape[1])
    zspec = lambda j: pl.BlockSpec((nsq, block_rows, width), lambda g, i: (g, i, cb + j))
    const = lambda shape: pl.BlockSpec(shape, lambda g, i: (0,) * len(shape))
    per_seq = lambda shape: pl.BlockSpec((nsq,) + shape, lambda g, i: (g,) + (0,) * len(shape))
    y, sbd, shift = pl.pallas_call(
        functools.partial(_rwkv_kernel, chunk=chunk, n_valid=n_valid, width=width, padded=n_valid < rows),
        grid=(n_seq // nsq, nblk),
        in_specs=[zspec(0), zspec(1), zspec(2), zspec(3),
                  pl.BlockSpec((nsq, block_rows, LANES), lambda g, i: (g, i, (col0 + 4 * width) // LANES)),
                  per_seq((1, bcols)), per_seq((n_pairs, LANES, LANES)),
                  const((1, bcols))] + [const((1, width))] * 7 + [const((LANES, width))] * 2,
        out_specs=[pl.BlockSpec((nsq, block_rows, width), lambda g, i: (g, i, 0)),
                   per_seq((n_pairs, LANES, LANES)), per_seq((1, bcols))],
        out_shape=[jax.ShapeDtypeStruct((n_seq, rows, width), bf16),
                   jax.ShapeDtypeStruct((n_seq, n_pairs, LANES, LANES), f32),
                   jax.ShapeDtypeStruct((n_seq, 1, bcols), f32)],
        scratch_shapes=[pltpu.VMEM((nsq, n_pairs, LANES, LANES), f32), pltpu.VMEM((nsq, 1, bcols), f32)]
                       + [pltpu.VMEM((nsq, block_rows, width), f32)] * 9,
        compiler_params=_params(("parallel", "arbitrary")),
        name="rwkv7",
    )(z3, z3, z3, z3, z3, shift0.reshape(n_seq, 1, bcols), s0bd,
      row1(prm["b_mu"]), row1(prm["b_w0"]), row1(prm["b_a0"]), row1(prm["b_k_k"]), row1(prm["b_k_a"]),
      row1(prm["b_r_k"]), row1(prm["b_ln_w"]), row1(prm["b_ln_b"]),
      lora(prm["b_w_up"], 0), lora(prm["b_a_up"], HEAD))
    sp = sbd.reshape(n_seq, n_pairs, 2, HEAD, 2, HEAD)
    s_fin = jnp.stack([sp[:, :, 0, :, 0, :], sp[:, :, 1, :, 1, :]], axis=2).reshape(n_seq, n_heads, HEAD, HEAD)
    return y.reshape(m, width), s_fin, shift.reshape(n_seq, bcols)


A_WIDTH = 512
STEP_ROWS = 64
RWKV_CHUNK = 64
RWKV_BLOCK = 128
RWKV_SEQS_PER_STEP = 4
FLASH_TILE = 512
DENSE_TILE = 512
FOX_IN_TILE = 256


UNDERFLOW_LOG2 = 160.0


def _qk_bound(q_gain, k_gain):
    return 1.02 * HEAD ** 0.5 * LOG2E * jnp.max(jnp.abs(q_gain)) * jnp.max(jnp.abs(k_gain))


def _first_key_block(c, qk, n_seq, n_heads, tile):
    c3 =c.reshape(n_seq, -1, c.shape[1])[:, :, :n_heads]
    c_first = c3[:, ::tile]
    c_last = c3[:, tile - 1::tile]
    n = c_first.shape[1]
    dead = (2.0 * qk + c_first[:, :, None, :] - c_last[:, None, :, :]) <= -UNDERFLOW_LOG2
    dead = dead & (jnp.arange(n)[None, None, :, None] < jnp.arange(n)[None, :, None, None])
    dead = dead.reshape(n_seq, n, n, n_heads // 2, 2).all(axis=-1)
    return jnp.transpose(jnp.sum(dead, axis=2), (0, 2, 1)).astype(jnp.int32)


def _trunk(x3, mem_k, mem_v, rwkv_s0, rwkv_shift0, fox_past, p):
    n_seq, seq, d = x3.shape
    m = n_seq * seq
    x = x3.reshape(m, d)
    step = seq == 1
    tm = m if step else DENSE_TILE
    depth = p["norm_mix"].shape[0]
    chunk_v, s_out, shift_out, fk, fv, fl = [], [], [], [], [], []
    ie = io = 0
    for l in range(depth):
        if l % 2 == 0:
            z = norm_matmul(x, p["norm_mix"][l], p["ab_w_in"][ie], tm)
            prm = {k_: p[k_][ie] for k_ in ("b_mu", "b_w0", "b_w_up", "b_a0", "b_a_up", "b_k_k", "b_k_a",
                                           "b_r_k", "b_ln_w", "b_ln_b")}
            if step:
                ya, va = mixer_a_step(z, p["a_v_norm"][ie], p["a_w_s"][ie], p["a_b_s"][ie], A_WIDTH)
                zp = jnp.zeros((n_seq, STEP_ROWS, z.shape[1]), f32).at[:, 0].set(z)
                yb, s_new, sh_new = rwkv7(zp.reshape(n_seq * STEP_ROWS, -1), 2 * A_WIDTH, rwkv_shift0[ie],
                                          rwkv_s0[ie], prm, n_seq, 1, RWKV_CHUNK, STEP_ROWS)
                yb = yb.reshape(n_seq, STEP_ROWS, -1)[:, 0]
                chunk_v.append(va.reshape(n_seq, seq, A_WIDTH))
            else:
                ya = mixer_a(z, p["a_v_norm"][ie], p["a_w_s"][ie], p["a_b_s"][ie], A_WIDTH, tm)
                yb, s_new, sh_new = rwkv7(z, 2 * A_WIDTH, rwkv_shift0[ie], rwkv_s0[ie], prm, n_seq, seq,
                                          RWKV_CHUNK, RWKV_BLOCK)
            w_out = p["ab_w_out"][ie]
            mixed = [(ya, w_out[:A_WIDTH]), (yb, w_out[A_WIDTH:])]
            s_out.append(s_new)
            shift_out.append(sh_new)
            ie += 1
        else:
            w_in = p["c_w_in"][io]
            width = (w_in.shape[1] // 4) // HEAD * HEAD
            n_heads = width // HEAD
            front = (x, p["norm_mix"][l], w_in[:, :4 * width], w_in[:, 4 * width:], p["c_b_f"][io],
                     p["c_q_norm"][io], p["c_k_norm"][io])
            if step:
                q, k, v, sg, lf = fox_step_in(*front)
                ck, cv, cl, pt = fox_past
                og = fox_step(q, k, v, lf, sg, ck[io], cv[io], cl[io], pt)
                fk.append(k.reshape(n_seq, seq, n_heads, HEAD))
                fv.append(v.reshape(n_seq, seq, n_heads, HEAD))
                fl.append(lf.reshape(n_seq, seq, n_heads))
            else:
                q, kt, ka, qa, vt, vtb, sg, lft, c = fox_seq_in(*front, n_seq, FOX_IN_TILE)
                qk = _qk_bound(p["c_q_norm"][io], p["c_k_norm"][io])
                first = _first_key_block(c, qk, n_seq, n_heads, FLASH_TILE)
                og = fox_flash(q, qa, ka, vtb, sg, first, qk, n_seq, FLASH_TILE)
                tokens_first = lambda t: jnp.transpose(t.reshape(n_seq, n_heads, HEAD, seq), (0, 3, 1, 2))
                fk.append(tokens_first(kt))
                fv.append(tokens_first(vt))
                fl.append(jnp.transpose(lft, (0, 2, 1)))
            mixed = [(og, p["c_w_out"][io])]
            io += 1
        if step:
            x = res_matmul(x, mixed, tm)
            q_raw = norm_matmul(x, p["norm_mem"][l], p["m_wq"][l], tm)
            o = mem_attn_step(q_raw, p["m_q_norm"][l], mem_k, mem_v, l)
            x = res_matmul(x, [(o, p["m_wo"][l])], tm)
        else:
            x = mem_attn(x, mixed, p["norm_mem"][l], p["m_wq"][l], p["m_q_norm"][l], mem_k[l].astype(bf16),
                         mem_v[l].astype(bf16), p["m_wo"][l], seq, tm)
        x = ffn(x, p["norm_ffn"][l], p["f_w_in"][l], p["f_w_out"][l], tm)
    stack = lambda xs: jnp.stack(xs)
    return (x.reshape(n_seq, seq, d), (stack(chunk_v) if chunk_v else None), stack(s_out), stack(shift_out),
            stack(fk), stack(fv), stack(fl))


def kernel(x_prompt, x_sample, mem_prompt, cache_mem_k, cache_mem_v, state_rwkv_S, state_rwkv_shift, cache_fox_k, cache_fox_v, cache_fox_logf, page_table, norm_mix, norm_mem, norm_ffn, ab_w_in, ab_w_out, a_v_norm, a_w_s, a_b_s, b_mu, b_w0, b_w_up, b_a0, b_a_up, b_k_k, b_k_a, b_r_k, b_ln_w, b_ln_b, c_w_in, c_b_f, c_q_norm, c_k_norm, c_w_out, m_mem_norm, m_wq, m_wk, m_wv, m_q_norm, m_k_norm, m_wo, f_w_in, f_w_out):
    cast = lambda w: w.astype(bf16)
    p = dict(norm_mix=norm_mix, norm_mem=norm_mem, norm_ffn=norm_ffn, ab_w_in=cast(ab_w_in),
             ab_w_out=cast(ab_w_out), a_v_norm=a_v_norm, a_w_s=a_w_s, a_b_s=a_b_s, b_mu=b_mu, b_w0=b_w0,
             b_w_up=b_w_up, b_a0=b_a0, b_a_up=b_a_up, b_k_k=b_k_k, b_k_a=b_k_a, b_r_k=b_r_k, b_ln_w=b_ln_w,
             b_ln_b=b_ln_b, c_w_in=cast(c_w_in), c_b_f=c_b_f, c_q_norm=c_q_norm, c_k_norm=c_k_norm,
             c_w_out=cast(c_w_out), m_wq=cast(m_wq), m_q_norm=m_q_norm, m_wo=cast(m_wo),
             f_w_in=cast(f_w_in), f_w_out=cast(f_w_out))
    depth, d = norm_mix.shape
    nb, n_mem = mem_prompt.shape[:2]
    n_even, _, n_heads_b, hd, _ = state_rwkv_S.shape
    mem_heads = d // MEM_HEAD

    mem2 = mem_prompt.reshape(nb * n_mem, d)
    kvs = [mem_kv(mem2, m_mem_norm[l], cast(m_wk[l]), cast(m_wv[l]), m_k_norm[l], 256) for l in range(depth)]
    p_mem_k = jnp.stack([kv[0] for kv in kvs]).reshape(depth, nb, n_mem, d)
    p_mem_v = jnp.stack([kv[1] for kv in kvs]).reshape(depth, nb, n_mem, d)
    s0 = jnp.zeros((n_even, nb, n_heads_b, hd, hd), f32)
    shift0 = jnp.zeros((n_even, nb, state_rwkv_shift.shape[2]), f32)
    y_prompt, _, p_rwkv_s, p_rwkv_shift, p_fox_k, p_fox_v, p_fox_logf = _trunk(
        x_prompt, p_mem_k, p_mem_v, s0, shift0, None, p)

    nd = x_sample.shape[0]
    y_sample, s_chunk_v, s_rwkv_s, s_rwkv_shift, s_fox_k, s_fox_v, s_fox_logf = _trunk(
        x_sample, cache_mem_k, cache_mem_v, state_rwkv_S, state_rwkv_shift, (cache_fox_k, cache_fox_v, cache_fox_logf, page_table), p)
    heads5 = lambda a: a.reshape(depth, nb, n_mem, mem_heads, MEM_HEAD)
    return (y_prompt, y_sample, heads5(p_mem_k), heads5(p_mem_v), p_rwkv_s, p_rwkv_shift, p_fox_k, p_fox_v,
            p_fox_logf, s_chunk_v, s_rwkv_s, s_rwkv_shift, s_fox_k, s_fox_v, s_fox_logf)
```

```python
import functools
import math

import jax
import jax.numpy as jnp
from jax import lax
from jax.experimental import pallas as pl
from jax.experimental.pallas import tpu as pltpu

f32 = jnp.float32
bf16 = jnp.bfloat16

RMS_EPS = 1e-6
GN_EPS = 64e-5
L2_EPS = 1e-12
LOG2E = 1.4426950408889634
LANES = 128
HEAD = 64
A_GROUP = 128
MEM_HEAD = 256
VMEM_LIMIT = 56 * 1024 * 1024


def _params(sem):
    return pltpu.CompilerParams(dimension_semantics=sem, vmem_limit_bytes=VMEM_LIMIT)


def _bdot(a, b):
    return jnp.dot(a.astype(bf16), b.astype(bf16), preferred_element_type=f32)


def _bdot_nt(a, b):
    return lax.dot_general(a.astype(bf16), b.astype(bf16), (((1,), (1,)), ((), ())),
                           preferred_element_type=f32)


def _rms_rows(x, g):
    return x * lax.rsqrt(jnp.mean(x * x, axis=-1, keepdims=True) + RMS_EPS) * g


def _sigmoid(x):
    return 1.0 / (1.0 + jnp.exp(-x))


def _lane_iota(shape):
    return lax.broadcasted_iota(jnp.int32, shape, len(shape) - 1)


def _half_sums(x):
    lo = _lane_iota(x.shape) < HEAD
    s_lo = jnp.sum(jnp.where(lo, x, 0.0), axis=-1, keepdims=True)
    s_hi = jnp.sum(jnp.where(lo, 0.0, x), axis=-1, keepdims=True)
    return jnp.where(lo, s_lo, s_hi)


def _per_head_sums(x):
    n = x.shape[-1] // LANES
    return jnp.concatenate([_half_sums(x[:, i * LANES:(i + 1) * LANES]) for i in range(n)], axis=-1)


def _norm_matmul_kernel(x_ref, g_ref, w_ref, o_ref):
    h = _rms_rows(x_ref[...], g_ref[...])
    o_ref[...] = _bdot(h, w_ref[...])


def norm_matmul(x, g, w, tm):
    m, d = x.shape
    n = w.shape[1]
    return pl.pallas_call(
        _norm_matmul_kernel,
        grid=(m // tm,),
        in_specs=[pl.BlockSpec((tm, d), lambda i: (i, 0)),
                  pl.BlockSpec((1, d), lambda i: (0, 0)),
                  pl.BlockSpec((d, n), lambda i: (0, 0))],
        out_specs=pl.BlockSpec((tm, n), lambda i: (i, 0)),
        out_shape=jax.ShapeDtypeStruct((m, n), f32),
        compiler_params=_params(("parallel",)),
        name="norm_matmul",
    )(x, g.reshape(1, d), w)


def _res_matmul_kernel(*refs):
    x_ref, o_ref = refs[0], refs[-1]
    acc = x_ref[...]
    for a_ref, w_ref in zip(refs[1:-1:2], refs[2:-1:2]):
        acc = acc + _bdot(a_ref[...], w_ref[...])
    o_ref[...] = acc


def res_matmul(x, pairs, tm):
    m, d = x.shape
    in_specs = [pl.BlockSpec((tm, d), lambda i: (i, 0))]
    args = [x]
    for a, w in pairs:
        in_specs += [pl.BlockSpec((tm, a.shape[1]), lambda i: (i, 0)),
                     pl.BlockSpec(w.shape, lambda i: (0, 0))]
        args += [a, w]
    return pl.pallas_call(
        _res_matmul_kernel,
        grid=(m // tm,),
        in_specs=in_specs,
        out_specs=pl.BlockSpec((tm, d), lambda i: (i, 0)),
        out_shape=jax.ShapeDtypeStruct((m, d), f32),
        compiler_params=_params(("parallel",)),
        name="res_matmul",
    )(*args)


def _ffn_kernel(x_ref, g_ref, wi_ref, wo_ref, o_ref, *, d_ff, tf):
    x = x_ref[...]
    h = _rms_rows(x, g_ref[...]).astype(bf16)
    acc = x
    for c in range(d_ff // tf):
        gate = jnp.dot(h, wi_ref[:, c * tf:(c + 1) * tf], preferred_element_type=f32)
        up = jnp.dot(h, wi_ref[:, d_ff + c * tf:d_ff + (c + 1) * tf], preferred_element_type=f32)
        act = gate * _sigmoid(gate) * up
        acc = acc + _bdot(act, wo_ref[c * tf:(c + 1) * tf, :])
    o_ref[...] = acc


def ffn(x, g, w_in, w_out, tm):
    m, d = x.shape
    d_ff = w_out.shape[0]
    tf = 256
    return pl.pallas_call(
        functools.partial(_ffn_kernel, d_ff=d_ff, tf=tf),
        grid=(m // tm,),
        in_specs=[pl.BlockSpec((tm, d), lambda i: (i, 0)),
                  pl.BlockSpec((1, d), lambda i: (0, 0)),
                  pl.BlockSpec(w_in.shape, lambda i: (0, 0)),
                  pl.BlockSpec(w_out.shape, lambda i: (0, 0))],
        out_specs=pl.BlockSpec((tm, d), lambda i: (i, 0)),
        out_shape=jax.ShapeDtypeStruct((m, d), f32),
        compiler_params=_params(("parallel",)),
        name="ffn",
    )(x, g.reshape(1, d), w_in, w_out)


def _mem_kv_kernel(m_ref, g_ref, wk_ref, wv_ref, kg_ref, k_ref, v_ref):
    h = _rms_rows(m_ref[...], g_ref[...]).astype(bf16)
    k = jnp.dot(h, wk_ref[...], preferred_element_type=f32)
    kg = kg_ref[...]
    for hd in range(k.shape[1] // MEM_HEAD):
        sl = slice(hd * MEM_HEAD, (hd + 1) * MEM_HEAD)
        k_ref[:, sl] = _rms_rows(k[:, sl], kg)
    v_ref[...] = jnp.dot(h, wv_ref[...], preferred_element_type=f32)


def mem_kv(mem, g, wk, wv, k_gain, tm):
    m, d = mem.shape
    return pl.pallas_call(
        _mem_kv_kernel,
        grid=(m // tm,),
        in_specs=[pl.BlockSpec((tm, d), lambda i: (i, 0)),
                  pl.BlockSpec((1, d), lambda i: (0, 0)),
                  pl.BlockSpec((d, d), lambda i: (0, 0)),
                  pl.BlockSpec((d, d), lambda i: (0, 0)),
                  pl.BlockSpec((1, MEM_HEAD), lambda i: (0, 0))],
        out_specs=[pl.BlockSpec((tm, d), lambda i: (i, 0))] * 2,
        out_shape=[jax.ShapeDtypeStruct((m, d), f32)] * 2,
        compiler_params=_params(("parallel",)),
        name="mem_kv",
    )(mem, g.reshape(1, d), wk, wv, k_gain.reshape(1, MEM_HEAD))


def _mem_attn_kernel(x_ref, g_ref, wq_ref, qg_ref, k_ref, v_ref, wo_ref, *rest):
    o_ref = rest[-1]
    x = x_ref[...]
    for a_ref, w_ref in zip(rest[:-1:2], rest[1:-1:2]):
        x = x + _bdot(a_ref[...], w_ref[...])
    h = _rms_rows(x, g_ref[...])
    q = _bdot(h, wq_ref[...])
    qg = qg_ref[...] * (MEM_HEAD ** -0.5)
    outs = []
    for hd in range(q.shape[1] // MEM_HEAD):
        sl = slice(hd * MEM_HEAD, (hd + 1) * MEM_HEAD)
        qn = _rms_rows(q[:, sl], qg)
        s = _bdot_nt(qn, k_ref[0, :, sl])
        p = jnp.exp(s - jnp.max(s, axis=-1, keepdims=True))
        l = jnp.sum(p, axis=-1, keepdims=True)
        outs.append(_bdot(p, v_ref[0, :, sl]) / l)
    o = jnp.concatenate(outs, axis=-1)
    o_ref[...] = x + _bdot(o, wo_ref[...])


def mem_attn(x, pairs, g, wq, q_gain, k, v, wo, rows_per_seq, tm):
    m, d = x.shape
    n_mem = k.shape[1]
    per = rows_per_seq // tm
    extra_specs, extra = [], []
    for a, w in pairs:
        extra_specs += [pl.BlockSpec((tm, a.shape[1]), lambda i: (i, 0)), pl.BlockSpec(w.shape, lambda i: (0, 0))]
        extra += [a, w]
    return pl.pallas_call(
        _mem_attn_kernel,
        grid=(m // tm,),
        in_specs=[pl.BlockSpec((tm, d), lambda i: (i, 0)),
                  pl.BlockSpec((1, d), lambda i: (0, 0)),
                  pl.BlockSpec((d, d), lambda i: (0, 0)),
                  pl.BlockSpec((1, MEM_HEAD), lambda i: (0, 0)),
                  pl.BlockSpec((1, n_mem, d), lambda i: (i // per, 0, 0)),
                  pl.BlockSpec((1, n_mem, d), lambda i: (i // per, 0, 0)),
                  pl.BlockSpec((d, d), lambda i: (0, 0))] + extra_specs,
        out_specs=pl.BlockSpec((tm, d), lambda i: (i, 0)),
        out_shape=jax.ShapeDtypeStruct((m, d), f32),
        compiler_params=_params(("parallel",)),
        name="mem_attn",
    )(x, g.reshape(1, d), wq, q_gain.reshape(1, MEM_HEAD), k, v, wo, *extra)


def _gelu(x):
    return 0.5 * x * (1.0 + jnp.tanh(0.7978845608028654 * (x + 0.044715 * x * x * x)))


def _mixer_a_kernel(z_ref, vg_ref, w_ref, b_ref, ya_ref, *, width):
    ge = _gelu(z_ref[...])
    tm = ge.shape[0]
    row = lax.broadcasted_iota(jnp.int32, (A_GROUP, A_GROUP), 0)
    col = lax.broadcasted_iota(jnp.int32, (A_GROUP, A_GROUP), 1)
    for g in range(width // A_GROUP):
        u = ge[:, g * A_GROUP:(g + 1) * A_GROUP]
        v = ge[:, width + g * A_GROUP:width + (g + 1) * A_GROUP]
        va = _rms_rows(v, vg_ref[g:g + 1, :]).astype(bf16)
        w = jnp.where(row >= col, w_ref[g], 0.0).astype(bf16)
        for c in range(tm // A_GROUP):
            rs = slice(c * A_GROUP, (c + 1) * A_GROUP)
            s = jnp.dot(w, va[rs], preferred_element_type=f32) + b_ref[g]
            ya_ref[rs, g * A_GROUP:(g + 1) * A_GROUP] = (u[rs] * s).astype(ya_ref.dtype)


def mixer_a(z, v_gain, w_s, b_s, width, tm):
    m = z.shape[0]
    ng = width // A_GROUP
    b_rows = jnp.broadcast_to(b_s[:, :, None], (ng, A_GROUP, A_GROUP))
    return pl.pallas_call(
        functools.partial(_mixer_a_kernel, width=width),
        grid=(m // tm,),
        in_specs=[pl.BlockSpec((tm, 2 * width), lambda i: (i, 0)),
                  pl.BlockSpec((ng, A_GROUP), lambda i: (0, 0)),
                  pl.BlockSpec((ng, A_GROUP, A_GROUP), lambda i: (0, 0, 0)),
                  pl.BlockSpec((ng, A_GROUP, A_GROUP), lambda i: (0, 0, 0))],
        out_specs=pl.BlockSpec((tm, width), lambda i: (i, 0)),
        out_shape=jax.ShapeDtypeStruct((m, width), bf16),
        compiler_params=_params(("parallel",)),
        name="mixer_a",
    )(z, v_gain, w_s, b_rows)


def _mixer_a_step_kernel(z_ref, vg_ref, w0_ref, b0_ref, ya_ref, va_ref, *, width):
    ge = _gelu(z_ref[...])
    for g in range(width // A_GROUP):
        sl = slice(g * A_GROUP, (g + 1) * A_GROUP)
        u = ge[:, sl]
        v = ge[:, width + g * A_GROUP:width + (g + 1) * A_GROUP]
        va = _rms_rows(v, vg_ref[g:g + 1, :])
        va_ref[:, sl] = va
        ya_ref[:, sl] = (u * (w0_ref[:, sl] * va + b0_ref[:, sl])).astype(ya_ref.dtype)


def mixer_a_step(z, v_gain, w_s, b_s, width):
    m = z.shape[0]
    w0 = jnp.repeat(w_s[:, 0, 0], A_GROUP).reshape(1, width)
    b0 = jnp.repeat(b_s[:, 0], A_GROUP).reshape(1, width)
    return pl.pallas_call(
        functools.partial(_mixer_a_step_kernel, width=width),
        grid=(1,),
        in_specs=[pl.BlockSpec((m, 2 * width), lambda i: (0, 0)),
                  pl.BlockSpec(v_gain.shape, lambda i: (0, 0)),
                  pl.BlockSpec((1, width), lambda i: (0, 0)),
                  pl.BlockSpec((1, width), lambda i: (0, 0))],
        out_specs=[pl.BlockSpec((m, width), lambda i: (0, 0))] * 2,
        out_shape=[jax.ShapeDtypeStruct((m, width), bf16), jax.ShapeDtypeStruct((m, width), f32)],
        compiler_params=_params(("arbitrary",)),
        name="mixer_a_step",
    )(z, v_gain, w0, b0)


def _split3(x):
    hi = x.astype(bf16)
    r1 = x - hi.astype(f32)
    mid = r1.astype(bf16)
    lo = (r1 - mid.astype(f32)).astype(bf16)
    return hi, mid, lo


def _dot01_left(sel, x):
    hi, mid, lo = _split3(x)
    d = lambda p: jnp.dot(sel, p, preferred_element_type=f32)
    return d(hi) + d(mid) + d(lo)


def _dot01_right(x, sel):
    hi, mid, lo = _split3(x)
    d = lambda p: jnp.dot(p, sel, preferred_element_type=f32)
    return d(hi) + d(mid) + d(lo)


def _log_sigmoid(x):
    return jnp.minimum(x, 0.0) - jnp.log(1.0 + jnp.exp(-jnp.abs(x)))


AUG = 2 * LANES
SHIFT_LANE = 12


def _fox_project(x_ref, g_ref, w_ref, wf_ref, bf_ref, qg_ref, kg_ref, width, unit):
    h = _rms_rows(x_ref[...], g_ref[...]).astype(bf16)
    z = jnp.dot(h, w_ref[...], preferred_element_type=f32)
    q, k = z[:, :width], z[:, width:2 * width]
    v, g = z[:, 2 * width:3 * width], z[:, 3 * width:4 * width]
    qn = q * lax.rsqrt(_per_head_sums(q * q) * (1.0 / HEAD) + RMS_EPS) * (qg_ref[...] * (HEAD ** -0.5 * unit))
    kn = k * lax.rsqrt(_per_head_sums(k * k) * (1.0 / HEAD) + RMS_EPS) * kg_ref[...]
    lf = _log_sigmoid(jnp.dot(h, wf_ref[...], preferred_element_type=f32) + bf_ref[...])
    return qn, kn, v, g, lf


def _fox_in_args(x, g, w, wf, b_f, q_gain, k_gain):
    d = x.shape[1]
    width = w.shape[1] // 4
    n_heads = width // HEAD
    wf_pad = jnp.zeros((d, LANES), bf16).at[:, :n_heads].set(wf)
    bf_pad = jnp.zeros((1, LANES), f32).at[0, :n_heads].set(b_f)
    qg = jnp.tile(q_gain, n_heads).reshape(1, width)
    kg = jnp.tile(k_gain, n_heads).reshape(1, width)
    const = lambda i: (0, 0)
    specs = [pl.BlockSpec((1, d), const), pl.BlockSpec(w.shape, const), pl.BlockSpec((d, LANES), const),
             pl.BlockSpec((1, LANES), const), pl.BlockSpec((1, width), const), pl.BlockSpec((1, width), const)]
    return specs, (g.reshape(1, d), w, wf_pad, bf_pad, qg, kg)


def _fox_step_in_kernel(x_ref, g_ref, w_ref, wf_ref, bf_ref, qg_ref, kg_ref,
                        q_ref, k_ref, v_ref, sg_ref, lf_ref, *, width, n_heads):
    qn, kn, v, g, lf = _fox_project(x_ref, g_ref, w_ref, wf_ref, bf_ref, qg_ref, kg_ref, width, 1.0)
    q_ref[...] = qn.astype(bf16).astype(f32)
    k_ref[...] = kn
    v_ref[...] = v
    sg_ref[...] = _sigmoid(g)
    lf_ref[...] = lf[:, :n_heads]


def fox_step_in(x, g, w, wf, b_f, q_gain, k_gain):
    m, d = x.shape
    width = w.shape[1] // 4
    n_heads = width // HEAD
    specs, args = _fox_in_args(x, g, w, wf, b_f, q_gain, k_gain)
    whole = lambda n: pl.BlockSpec((m, n), lambda i: (0, 0))
    return pl.pallas_call(
        functools.partial(_fox_step_in_kernel, width=width, n_heads=n_heads),
        grid=(1,),
        in_specs=[whole(d)] + specs,
        out_specs=[whole(width)] * 4 + [whole(n_heads)],
        out_shape=[jax.ShapeDtypeStruct((m, width), f32)] * 4 + [jax.ShapeDtypeStruct((m, n_heads), f32)],
        compiler_params=_params(("arbitrary",)),
        name="fox_step_in",
    )(x, *args)


def _fox_seq_in_kernel(x_ref, g_ref, w_ref, wf_ref, bf_ref, qg_ref, kg_ref,
                       q_ref, kt_ref, ka_ref, qa_ref, vt_ref, vtb_ref, sg_ref, lft_ref, c_ref, carry_ref,
                       *, tiles_per_seq, width, n_heads):
    i = pl.program_id(0)
    qn, kn, v, g, lf = _fox_project(x_ref, g_ref, w_ref, wf_ref, bf_ref, qg_ref, kg_ref, width, LOG2E)
    tm = lf.shape[0]
    q_ref[...] = qn.astype(bf16)
    sg_ref[...] = _sigmoid(g)
    for blk in range(width // LANES):
        sl = slice(blk * LANES, (blk + 1) * LANES)
        kt_ref[0, sl, :] = kn[:, sl].T
        vt = v[:, sl].T
        vt_ref[0, sl, :] = vt
        vtb_ref[0, 0, sl, :] = vt.astype(bf16)
    lft_ref[0] = lf.T[:n_heads, :]

    @pl.when(i % tiles_per_seq == 0)
    def _():
        carry_ref[...] = jnp.zeros_like(carry_ref)

    row = lax.broadcasted_iota(jnp.int32, (tm, tm), 0)
    col = lax.broadcasted_iota(jnp.int32, (tm, tm), 1)
    tri = jnp.where(row >= col, 1.0, 0.0).astype(bf16)
    c = _dot01_left(tri, lf) + carry_ref[...]
    carry_ref[...] = c[tm - 1:tm, :]
    c = c * LOG2E
    c_ref[...] = c
    pieces = jnp.concatenate(_split3(c), axis=-1)
    rr = lax.broadcasted_iota(jnp.int32, (3 * LANES, LANES), 0)
    cc = lax.broadcasted_iota(jnp.int32, (3 * LANES, LANES), 1)
    piece, head = rr // LANES, rr % LANES
    lane = _lane_iota((1, LANES))
    k_const = jnp.where((lane < 6) | (lane == SHIFT_LANE), 1.0, 0.0)
    q_const = jnp.where((lane >= 6) & (lane < 12), -1.0, 0.0)
    knb = kn.astype(bf16)
    for p in range(n_heads // 2):
        place = lambda base: jnp.where(((head == 2 * p) & (cc == base + piece))
                                       | ((head == 2 * p + 1) & (cc == base + 3 + piece)), 1.0, 0.0).astype(bf16)
        both = jnp.dot(pieces, jnp.concatenate([place(6), place(0)], axis=-1), preferred_element_type=f32)
        k_aug = both[:, :LANES] + k_const
        ka_ref[p] = jnp.concatenate([knb[:, p * LANES:(p + 1) * LANES], k_aug.astype(bf16)], axis=-1)
        qa_ref[p] = (both[:, LANES:] + q_const).astype(bf16)


def fox_seq_in(x, g, w, wf, b_f, q_gain, k_gain, n_seq, tm):
    m, d = x.shape
    width = w.shape[1] // 4
    n_heads = width // HEAD
    n_pairs = n_heads // 2
    seq = m // n_seq
    tps = seq // tm
    specs, args = _fox_in_args(x, g, w, wf, b_f, q_gain, k_gain)
    row = lambda i: (i, 0)
    chan = lambda i: (i // tps, 0, i % tps)
    return pl.pallas_call(
        functools.partial(_fox_seq_in_kernel, tiles_per_seq=tps, width=width, n_heads=n_heads),
        grid=(m // tm,),
        in_specs=[pl.BlockSpec((tm, d), row)] + specs,
        out_specs=[pl.BlockSpec((tm, width), row),
                   pl.BlockSpec((1, width, tm), chan),
                   pl.BlockSpec((n_pairs, tm, AUG), lambda i: (0, i, 0)),
                   pl.BlockSpec((n_pairs, tm, LANES), lambda i: (0, i, 0)),
                   pl.BlockSpec((1, width, tm), chan),
                   pl.BlockSpec((1, 1, width, tm), lambda i: (i // tps, i % tps, 0, 0)),
                   pl.BlockSpec((tm, width), row),
                   pl.BlockSpec((1, n_heads, tm), chan),
                   pl.BlockSpec((tm, LANES), row)],
        out_shape=[jax.ShapeDtypeStruct((m, width), bf16),
                   jax.ShapeDtypeStruct((n_seq, width, seq), f32),
                   jax.ShapeDtypeStruct((n_pairs, m, AUG), bf16),
                   jax.ShapeDtypeStruct((n_pairs, m, LANES), bf16),
                   jax.ShapeDtypeStruct((n_seq, width, seq), f32),
                   jax.ShapeDtypeStruct((n_seq, tps, width, tm), bf16),
                   jax.ShapeDtypeStruct((m, width), f32),
                   jax.ShapeDtypeStruct((n_seq, n_heads, seq), f32),
                   jax.ShapeDtypeStruct((m, LANES), f32)],
        scratch_shapes=[pltpu.VMEM((1, LANES), f32)],
        compiler_params=_params(("arbitrary",)),
        name="fox_seq_in",
    )(x, *args)


def _fox_flash_kernel(lo_ref, qk_ref, q_ref, qa_ref, ka_ref, vt_ref, sg_ref, o_ref, *, tq, tv, bounded):
    b, p, qi = pl.program_id(0), pl.program_id(1), pl.program_id(2)
    nsub = tq // tv
    q2 = q_ref[0]
    lo = _lane_iota((tq, LANES)) < HEAD
    zero = jnp.zeros_like(q2)
    bias2 = qa_ref[0]
    lane = _lane_iota((tq, LANES))
    shift = jnp.where(lane == SHIFT_LANE, -qk_ref[0], 0.0).astype(bf16)
    qa = []
    for j in (0, 1):
        own = ((lane >= 3 * j) & (lane < 3 * j + 3)) | ((lane >= 6 + 3 * j) & (lane < 9 + 3 * j))
        bias = jnp.where(own, bias2, zero)
        if bounded:
            bias = jnp.where(lane == SHIFT_LANE, shift, bias)
        qh = jnp.where(lo, q2, zero) if j == 0 else jnp.where(lo, zero, q2)
        qa.append(jnp.concatenate([qh, bias], axis=-1))
    row = lax.broadcasted_iota(jnp.int32, (tq, tq), 0)
    col = lax.broadcasted_iota(jnp.int32, (tq, tq), 1)

    def scores(kb, j):
        ka = ka_ref[0, 0, pl.ds(pl.multiple_of(kb * tq, tq), tq), :]
        return _bdot_nt(ka, qa[j])

    def softmax_pv(kb, j, t, stats, masked):
        m_old, l_old, acc = stats
        if masked:
            t = jnp.where(row <= col, t, -jnp.inf)
        m_new = jnp.maximum(m_old, jnp.max(t, axis=0, keepdims=True))
        pe = jnp.exp2(t - m_new).astype(bf16)
        alpha = jnp.exp2(m_old - m_new)
        l_new = alpha * l_old + jnp.sum(pe.astype(f32), axis=0, keepdims=True)
        pv = sum(jnp.dot(vt_ref[0, kb * nsub + s, j * HEAD:(j + 1) * HEAD, :], pe[s * tv:(s + 1) * tv],
                         preferred_element_type=f32) for s in range(nsub))
        return m_new, l_new, acc * alpha + pv

    def block(kb, t0, st0, st1, masked, last):
        t1 = scores(kb, 1)
        st0 = softmax_pv(kb, 0, t0, st0, masked)
        t0_next = t0 if last else scores(kb + 1, 0)
        st1 = softmax_pv(kb, 1, t1, st1, masked)
        return t0_next, st0, st1

    def pv(kb, j, pe):
        return sum(jnp.dot(vt_ref[0, kb * nsub + s, j * HEAD:(j + 1) * HEAD, :], pe[s * tv:(s + 1) * tv],
                   preferred_element_type=f32) for s in range(nsub))

    def free_blocks(kbs, sts):
        units = [(kb, j) for kb in kbs for j in (0, 1)]
        l = [sts[0][1], sts[1][1]]
        acc = [sts[0][2], sts[1][2]]
        t_next = scores(*units[0])
        for n, (kb, j) in enumerate(units):
            t = t_next
            if n + 1 < len(units):
                t_next = scores(*units[n + 1])
            pe = jnp.exp2(jnp.where((row <= col) | (kb < qi), t, -jnp.inf))
            l[j] = l[j] + jnp.sum(pe, axis=0, keepdims=True)
            acc[j] = acc[j] + pv(kb, j, pe.astype(bf16))
        return (sts[0][0], l[0], acc[0]), (sts[1][0], l[1], acc[1])

    init = (jnp.full((1, tq), -jnp.inf, f32), jnp.zeros((1, tq), f32), jnp.zeros((HEAD, tq), f32))
    first = lo_ref[b, p, qi]
    if bounded:
        odd = (qi - first + 1) % 2
        sts = lax.cond(odd == 1, lambda s: free_blocks((first,), s), lambda s: s, (init, init))
        (m0, l0, a0), (m1, l1, a1) = lax.fori_loop(
            0, (qi - first + 1) // 2,
            lambda i, s: free_blocks((first + odd + 2 * i, first + odd + 2 * i + 1), s), sts)
    else:
        state = lax.fori_loop(first, qi, lambda kb, s: block(kb, *s, False, False),
                              (scores(first, 0), init, init))
        _, (m0, l0, a0), (m1, l1, a1) = block(qi, *state, True, True)
    o = jnp.concatenate([a0 / l0, a1 / l1], axis=0).T
    o_ref[0] = (o * sg_ref[0]).astype(o_ref.dtype)


QK_BOUNDED_MAX = 40.0


def fox_flash(q, qa, ka, vtb, sg, first_block, qk, n_seq, tq):
    m, width = q.shape
    seq = m // n_seq
    n_pairs = width // LANES
    tv = vtb.shape[3]
    r3 = lambda a: a.reshape(n_seq, seq, a.shape[1])
    tile = pl.BlockSpec((1, tq, LANES), lambda b, p, i, lo, qk_: (b, i, p))
    grid_spec = pltpu.PrefetchScalarGridSpec(
        num_scalar_prefetch=2,
        grid=(n_seq, n_pairs, seq // tq),
        in_specs=[tile,
                  pl.BlockSpec((1, tq, LANES), lambda b, p, i, lo, qk_: (p, b * (seq // tq) + i, 0)),
                  pl.BlockSpec((1, 1, seq, AUG), lambda b, p, i, lo, qk_: (p, b, 0, 0)),
                  pl.BlockSpec((1, seq // tv, LANES, tv), lambda b, p, i, lo, qk_: (b, 0, p, 0)),
                  tile],
        out_specs=tile,
    )
    args = (first_block, qk.reshape(1).astype(f32), r3(q), qa, ka.reshape(n_pairs, n_seq, seq, AUG), vtb,
            r3(sg))
    call = lambda bounded: pl.pallas_call(
        functools.partial(_fox_flash_kernel, tq=tq, tv=tv, bounded=bounded),
        grid_spec=grid_spec,
        out_shape=jax.ShapeDtypeStruct((n_seq, seq, width), bf16),
        compiler_params=_params(("parallel", "parallel", "arbitrary")),
        name="fox_flash_bounded" if bounded else "fox_flash",
    )
    out = lax.cond(qk <= QK_BOUNDED_MAX, lambda a: call(True)(*a), lambda a: call(False)(*a), args)
    return out.reshape(m, width)


FOX_PAGES_PER_STEP = 16
EXP_UNDERFLOW = 104.0


def _fox_scores_kernel(pt_ref, qc_ref, knt_ref, lfn_ref, *rest, n_heads, pps):
    kt_refs, lf_refs = rest[:pps], rest[pps:2 * pps]
    t_ref, town_ref, m_out, l_out, m_ref, l_ref, carry_ref = rest[2 * pps:]
    pg = pl.program_id(1)
    page = lf_refs[0].shape[2]

    def score(kt_ref, bias):
        rows = [jnp.sum(kt_ref[0, h] * qc_ref[0, h], axis=0, keepdims=True) for h in range(n_heads)]
        t = jnp.concatenate(rows, axis=0) + bias
        m_old = m_ref[...]
        m_new = jnp.maximum(m_old, jnp.max(t, axis=-1, keepdims=True))
        l_ref[...] = jnp.exp(m_old - m_new) * l_ref[...] + jnp.sum(jnp.exp(t - m_new), axis=-1, keepdims=True)
        m_ref[...] = m_new
        return t

    @pl.when(pg == 0)
    def _():
        m_ref[...] = jnp.full(m_ref.shape, -jnp.inf, f32)
        l_ref[...] = jnp.zeros_like(l_ref)
        carry_ref[...] = lfn_ref[0]
        town_ref[0] = score(knt_ref, jnp.where(_lane_iota((n_heads, page)) == 0, 0.0, -jnp.inf))

    r = lax.broadcasted_iota(jnp.int32, (page, page), 0)
    cidx = lax.broadcasted_iota(jnp.int32, (page, page), 1)
    later = jnp.where(r > cidx, 1.0, 0.0).astype(bf16)
    for j in range(pps):
        lf = lf_refs[j][0]
        t_ref[0, j] = score(kt_refs[j], _dot01_right(lf, later) + carry_ref[...])
        carry_ref[...] = carry_ref[...] + jnp.sum(lf, axis=-1, keepdims=True)

    @pl.when(pg == pl.num_programs(1) - 1)
    def _():
        m_out[0] = m_ref[...]
        l_out[0] = l_ref[...]


def _fox_values_kernel(pt_ref, idx_ref, cnt_ref, town_ref, vnt_ref, t_ref, m_ref, l_ref, sg_ref, vt_hbm,
                       o_ref, acc_ref, vbuf, sem, *, n_heads, n_pages):
    i = pl.program_id(0)
    n_live = cnt_ref[i]
    m = m_ref[0]

    def page_copy(u, buf):
        page = pt_ref[i, n_pages - 1 - idx_ref[i, u]]
        return pltpu.make_async_copy(vt_hbm.at[page], vbuf.at[buf], sem.at[buf])

    @pl.when(n_live > 0)
    def _():
        page_copy(0, 0).start()

    pe = jnp.exp(town_ref[0] - m)
    for h in range(n_heads):
        acc_ref[h] = pe[h:h + 1, :] * vnt_ref[0, h]

    def live_page(u, _):
        buf = u % 2

        @pl.when(u + 1 < n_live)
        def _():
            page_copy(u + 1, 1 - buf).start()

        page_copy(u, buf).wait()
        pe = jnp.exp(t_ref[0, idx_ref[i, u]] - m)
        for h in range(n_heads):
            acc_ref[h] = acc_ref[h] + pe[h:h + 1, :] * vbuf[buf, h]
        return 0

    lax.fori_loop(0, n_live, live_page, 0)
    l = l_ref[0]
    for h in range(n_heads):
        o_ref[0, h] = jnp.sum(acc_ref[h], axis=-1, keepdims=True) / l[h:h + 1, :] * sg_ref[0, h]


def fox_step(q, k_new, v_new, lf_new, sg, cache_k, cache_v, cache_lf, page_table):
    b, width = q.shape
    pool, page, n_heads = cache_lf.shape
    n_pages = page_table.shape[1]
    pps = math.gcd(FOX_PAGES_PER_STEP, n_pages)
    kt = jnp.transpose(cache_k, (0, 2, 3, 1))
    vt = jnp.transpose(cache_v, (0, 2, 3, 1))
    lft = jnp.transpose(cache_lf, (0, 2, 1))
    col = lambda a: a.reshape(b, n_heads, HEAD, 1)
    own_page = lambda a: jnp.zeros((b, n_heads, HEAD, page), f32).at[..., 0].set(a.reshape(b, n_heads, HEAD))
    qc = jnp.broadcast_to(col(q), (b, n_heads, HEAD, page))
    per_seq = lambda shape: pl.BlockSpec((1,) + shape, lambda i, g, *_: (i,) + (0,) * len(shape))
    page_spec = lambda shape, j: pl.BlockSpec(
        (1,) + shape, lambda i, g, pt: (pt[i, n_pages - 1 - (g * pps + j)],) + (0,) * len(shape))
    stat = jax.ShapeDtypeStruct((b, n_heads, 1), f32)
    t_all, t_own, m, l = pl.pallas_call(
        functools.partial(_fox_scores_kernel, n_heads=n_heads, pps=pps),
        grid_spec=pltpu.PrefetchScalarGridSpec(
            num_scalar_prefetch=1,
            grid=(b, n_pages // pps),
            in_specs=[per_seq((n_heads, HEAD, page))] * 2 + [per_seq((n_heads, 1))]
                     + [page_spec((n_heads, HEAD, page), j) for j in range(pps)]
                     + [page_spec((n_heads, page), j) for j in range(pps)],
            out_specs=[pl.BlockSpec((1, pps, n_heads, page), lambda i, g, pt: (i, g, 0, 0)),
                       per_seq((n_heads, page)), per_seq((n_heads, 1)), per_seq((n_heads, 1))],
            scratch_shapes=[pltpu.VMEM((n_heads, 1), f32)] * 3),
        out_shape=[jax.ShapeDtypeStruct((b, n_pages, n_heads, page), f32),
                   jax.ShapeDtypeStruct((b, n_heads, page), f32), stat, stat],
        compiler_params=_params(("parallel", "arbitrary")),
        name="fox_step_scores",
    )(page_table, qc, own_page(k_new), lf_new.reshape(b, n_heads, 1), *([kt] * pps), *([lft] * pps))

    live = jnp.any(jnp.max(t_all, axis=-1) - m[:, None, :, 0] > -EXP_UNDERFLOW, axis=-1)
    live_idx = jnp.argsort(jnp.logical_not(live), axis=1, stable=True).astype(jnp.int32)
    count = jnp.sum(live, axis=1).astype(jnp.int32)

    one = lambda shape: pl.BlockSpec((1,) + shape, lambda i, *_: (i,) + (0,) * len(shape))
    out = pl.pallas_call(
        functools.partial(_fox_values_kernel, n_heads=n_heads, n_pages=n_pages),
        grid_spec=pltpu.PrefetchScalarGridSpec(
            num_scalar_prefetch=3,
            grid=(b,),
            in_specs=[one((n_heads, page)), one((n_heads, HEAD, page)), one((n_pages, n_heads, page)),
                      one((n_heads, 1)), one((n_heads, 1)), one((n_heads, HEAD, 1)),
                      pl.BlockSpec(memory_space=pl.ANY)],
            out_specs=one((n_heads, HEAD, 1)),
            scratch_shapes=[pltpu.VMEM((n_heads, HEAD, page), f32), pltpu.VMEM((2, n_heads, HEAD, page), f32),
                            pltpu.SemaphoreType.DMA((2,))]),
        out_shape=jax.ShapeDtypeStruct((b, n_heads, HEAD, 1), f32),
        compiler_params=_params(("arbitrary",)),
        name="fox_step_values",
    )(page_table, live_idx, count, t_own, own_page(v_new), t_all, m, l, col(sg), vt)
    return out.reshape(b, width).astype(bf16)


def _mem_attn_step_kernel(q_ref, qg_ref, k_ref, v_ref, o_ref):
    q = q_ref[0]
    qn = q * lax.rsqrt(jnp.mean(q * q, axis=-1, keepdims=True) + RMS_EPS) * (qg_ref[...] * MEM_HEAD ** -0.5)
    s = jnp.sum(k_ref[0, 0] * qn, axis=-1, keepdims=True)
    p = jnp.exp(s - jnp.max(s, axis=0, keepdims=True))
    l = jnp.sum(p, axis=0, keepdims=True)
    o_ref[0] = jnp.sum(p * v_ref[0, 0], axis=0, keepdims=True) / l


def mem_attn_step(q, q_gain, cache_k, cache_v, layer):
    b, d = q.shape
    _, _, n_mem, heads, hd = cache_k.shape
    mem_spec = pl.BlockSpec((1, 1, n_mem, heads, hd), lambda i: (layer, i, 0, 0, 0))
    out = pl.pallas_call(
        _mem_attn_step_kernel,
        grid=(b,),
        in_specs=[pl.BlockSpec((1, 1, heads, hd), lambda i: (i, 0, 0, 0)),
                  pl.BlockSpec((1, 1, hd), lambda i: (0, 0, 0)),
                  mem_spec, mem_spec],
        out_specs=pl.BlockSpec((1, 1, heads, hd), lambda i: (i, 0, 0, 0)),
        out_shape=jax.ShapeDtypeStruct((b, 1, heads, hd), f32),
        compiler_params=_params(("parallel",)),
        name="mem_attn_step",
    )(q.reshape(b, 1, heads, hd), q_gain.reshape(1, 1, hd), cache_k, cache_v)
    return out.reshape(b, d)


def _softplus(x):
    return jnp.maximum(x, 0.0) + jnp.log(1.0 + jnp.exp(-jnp.abs(x)))


def _rwkv_kernel(zr_ref, zk_ref, zv_ref, zg_ref, zwa_ref, sh0_ref, s0_ref,
                 mu_ref, w0_ref, a0_ref, kk_ref, ka_ref, rk_ref, lnw_ref, lnb_ref, wup_ref, aup_ref,
                 y_ref, sout_ref, shout_ref,
                 s_ref, carry_ref, r_s, k_s, v_s, al_s, be_s, ld_s, yo_s, bo_s, sg_s,
                 *, chunk, n_valid, width, padded):
    i = pl.program_id(1)
    nblk = pl.num_programs(1)
    nsq, cg = zr_ref.shape[0], zr_ref.shape[1]
    n_pairs = width // LANES

    @pl.when(i == 0)
    def _():
        s_ref[...] = s0_ref[...]
        carry_ref[...] = sh0_ref[...]

    rows = lax.broadcasted_iota(jnp.int32, (cg, 1), 0)
    valid = (rows + i * cg) < n_valid
    last = jnp.minimum(n_valid - 1 - i * cg, cg - 1)

    for s in range(nsq):
        def shifted(z, lane0, n):
            prev = jnp.where(rows == 0, carry_ref[s, :, lane0:lane0 + n], pltpu.roll(z, 1, 0))
            return z + (prev - z) * mu_ref[:, lane0:lane0 + n]

        zr, zk, zv, zg, zwa = zr_ref[s], zk_ref[s], zv_ref[s], zg_ref[s], zwa_ref[s]
        r = shifted(zr, 0, width)
        k = shifted(zk, width, width)
        v = shifted(zv, 2 * width, width)
        gp = shifted(zg, 3 * width, width)
        wa = shifted(zwa, 4 * width, LANES)

        pieces = ((zr, 0, width), (zk, width, width), (zv, 2 * width, width), (zg, 3 * width, width),
                  (zwa, 4 * width, LANES))
        if padded:
            @pl.when(last >= 0)
            def _():
                for z, lane0, n in pieces:
                    carry_ref[s, :, lane0:lane0 + n] = jnp.sum(jnp.where(rows == last, z, 0.0), axis=0,
                                                               keepdims=True)
        else:
            for z, lane0, n in pieces:
                carry_ref[s, :, lane0:lane0 + n] = z[cg - 1:cg, :]

        w = -_softplus(-(w0_ref[...] + _bdot(jnp.tanh(wa), wup_ref[...]))) - 0.5
        logd = -jnp.exp(w)
        a = _sigmoid(a0_ref[...] + _bdot(wa, aup_ref[...]))
        kk = k * kk_ref[...]
        kk = kk * lax.rsqrt(_per_head_sums(kk * kk) + L2_EPS)
        k2 = k * (1.0 + (a - 1.0) * ka_ref[...])
        bo_s[s] = _per_head_sums(r * k2 * rk_ref[...]) * v
        sg_s[s] = _sigmoid(gp)
        r_s[s] = r
        keep = (lambda t: jnp.where(valid, t, 0.0)) if padded else (lambda t: t)
        k_s[s] = keep(k2)
        v_s[s] = keep(v)
        al_s[s] = -kk
        be_s[s] = keep(kk * a)
        ld_s[s] = keep(logd)

    ri = lax.broadcasted_iota(jnp.int32, (chunk, chunk), 0)
    ci = lax.broadcasted_iota(jnp.int32, (chunk, chunk), 1)
    tri = jnp.where(ri >= ci, 1.0, 0.0).astype(bf16)
    lower, strict = ri >= ci, ri > ci
    eye = jnp.where(ri == ci, 1.0, 0.0)
    lo = _lane_iota((1, LANES)) < HEAD
    sr = lax.broadcasted_iota(jnp.int32, (LANES, LANES), 0) < HEAD
    sc = lax.broadcasted_iota(jnp.int32, (LANES, LANES), 1) < HEAD
    same_head = sr == sc
    n_sq = max(int(math.log2(chunk)) - 1, 0)

    def chunk_body(c, _):
        rs = pl.ds(pl.multiple_of(c * chunk, chunk), chunk)
        pairs = [(s, p) for s in range(nsq) for p in range(n_pairs)]
        heads = [(s, p, j) for s, p in pairs for j in (0, 1)]
        lsl = [slice(p * LANES, (p + 1) * LANES) for p in range(n_pairs)]
        xs, vv, vp, yb, yk, ybe, yke, d_end = {}, {}, {}, {}, {}, {}, {}, {}
        for s in range(nsq):
            ld = ld_s[s, rs, :]
            lcum = _dot01_left(tri, ld)
            lend = lcum[chunk - 1:chunk, :]
            e_neg, e_end = jnp.exp(-lcum), jnp.exp(lend - lcum)
            xa_all, xr_all = al_s[s, rs, :] * jnp.exp(lcum - ld), r_s[s, rs, :] * jnp.exp(lcum)
            be, kc, v_all = be_s[s, rs, :], k_s[s, rs, :], v_s[s, rs, :]
            dec = jnp.exp(lend)
            for p in range(n_pairs):
                xs[s, p] = jnp.concatenate([xa_all[:, lsl[p]], xr_all[:, lsl[p]]], axis=0)
                vv[s, p] = v_all[:, lsl[p]]
                vp[s, p] = vv[s, p].astype(bf16)
                yb[s, p], yk[s, p] = (be * e_neg)[:, lsl[p]], (kc * e_neg)[:, lsl[p]]
                ybe[s, p], yke[s, p] = (be * e_end)[:, lsl[p]], (kc * e_end)[:, lsl[p]]
                d_end[s, p] = dec[:, lsl[p]]
        s_bd = {sp: s_ref[sp[0], sp[1]] for sp in pairs}
        a_ab, a_ak, a_rb, a_rk = {}, {}, {}, {}
        for s, p, j in heads:
            xm = jnp.where(lo if j == 0 else jnp.logical_not(lo), xs[s, p], 0.0).astype(bf16)
            gb = _bdot_nt(xm, yb[s, p])
            gk = _bdot_nt(xm, yk[s, p])
            a_ab[s, p, j] = jnp.where(strict, gb[:chunk], 0.0)
            a_ak[s, p, j] = jnp.where(strict, gk[:chunk], 0.0)
            a_rb[s, p, j] = jnp.where(lower, gb[chunk:], 0.0)
            a_rk[s, p, j] = jnp.where(lower, gk[chunk:], 0.0)
        tinv = {h: eye + a_ab[h] for h in heads}
        pw = dict(a_ab)
        for _ in range(n_sq):
            for h in heads:
                pw[h] = _bdot(pw[h], pw[h])
            for h in heads:
                tinv[h] = tinv[h] + _bdot(tinv[h], pw[h])
        xh = {sp: _bdot_nt(xs[sp], s_bd[sp]) for sp in pairs}
        av = {(s, p, j): _bdot(a_ak[s, p, j], vp[s, p]) for s, p, j in heads}
        uh = {(s, p, j): _bdot(tinv[s, p, j], xh[s, p][:chunk] + av[s, p, j]) for s, p, j in heads}
        u = {(s, p): jnp.where(lo, uh[s, p, 0], uh[s, p, 1]) for s, p in pairs}
        yh = {(s, p, j): _bdot(a_rb[s, p, j], u[s, p]) + _bdot(a_rk[s, p, j], vp[s, p]) for s, p, j in heads}
        for s, p in pairs:
            yo_s[s, rs, lsl[p]] = xh[s, p][chunk:] + jnp.where(lo, yh[s, p, 0], yh[s, p, 1])
            uv_t = jnp.concatenate([u[s, p], vv[s, p]], axis=0).T
            upd = _bdot(uv_t, jnp.concatenate([ybe[s, p], yke[s, p]], axis=0))
            s_ref[s, p] = s_bd[s, p] * d_end[s, p] + jnp.where(same_head, upd, 0.0)
        return 0

    lax.fori_loop(0, cg // chunk, chunk_body, 0)

    for s in range(nsq):
        y = yo_s[s]
        mean = _per_head_sums(y) * (1.0 / HEAD)
        yc = y - mean
        var = _per_head_sums(yc * yc) * (1.0 / HEAD)
        y = yc * lax.rsqrt(var + GN_EPS) * lnw_ref[...] + lnb_ref[...]
        y_ref[s] = ((y + bo_s[s]) * sg_s[s]).astype(y_ref.dtype)

    @pl.when(i == nblk - 1)
    def _():
        sout_ref[...] = s_ref[...]
        shout_ref[...] = carry_ref[...]


def rwkv7(z, col0, shift0, s0, prm, n_seq, n_valid, chunk, block_rows):
    m = z.shape[0]
    rows = m // n_seq
    nblk = rows // block_rows
    width = prm["b_w0"].shape[0]
    n_heads = width // HEAD
    n_pairs = width // LANES
    bcols = 4 * width + LANES
    s0p = s0.reshape(n_seq, n_pairs, 2, HEAD, HEAD)
    zeros = jnp.zeros_like(s0p[:, :, 0])
    s0bd = jnp.concatenate([jnp.concatenate([s0p[:, :, 0], zeros], axis=-1),
                            jnp.concatenate([zeros, s0p[:, :, 1]], axis=-1)], axis=-2)
    lora = lambda w_, off: jnp.zeros((LANES, width), bf16).at[off:off + w_.shape[0]].set(w_.astype(bf16))
    row1 = lambda a: a.reshape(1, -1).astype(f32)
    cb = col0 // width
    nsq = math.gcd(RWKV_SEQS_PER_STEP, n_seq)
    z3 = z.reshape(n_seq, rows, z.shape[1])
    zspec = lambda j: pl.BlockSpec((nsq, block_rows, width), lambda g, i: (g, i, cb + j))
    const = lambda shape: pl.BlockSpec(shape, lambda g, i: (0,) * len(shape))
    per_seq = lambda shape: pl.BlockSpec((nsq,) + shape, lambda g, i: (g,) + (0,) * len(shape))
    y, sbd, shift = pl.pallas_call(
        functools.partial(_rwkv_kernel, chunk=chunk, n_valid=n_valid, width=width, padded=n_valid < rows),
        grid=(n_seq // nsq, nblk),
        in_specs=[zspec(0), zspec(1), zspec(2), zspec(3),
                  pl.BlockSpec((nsq, block_rows, LANES), lambda g, i: (g, i, (col0 + 4 * width) // LANES)),
                  per_seq((1, bcols)), per_seq((n_pairs, LANES, LANES)),
                  const((1, bcols))] + [const((1, width))] * 7 + [const((LANES, width))] * 2,
        out_specs=[pl.BlockSpec((nsq, block_rows, width), lambda g, i: (g, i, 0)),
                   per_seq((n_pairs, LANES, LANES)), per_seq((1, bcols))],
        out_shape=[jax.ShapeDtypeStruct((n_seq, rows, width), bf16),
                   jax.ShapeDtypeStruct((n_seq, n_pairs, LANES, LANES), f32),
                   jax.ShapeDtypeStruct((n_seq, 1, bcols), f32)],
        scratch_shapes=[pltpu.VMEM((nsq, n_pairs, LANES, LANES), f32), pltpu.VMEM((nsq, 1, bcols), f32)]
                       + [pltpu.VMEM((nsq, block_rows, width), f32)] * 9,
        compiler_params=_params(("parallel", "arbitrary")),
        name="rwkv7",
    )(z3, z3, z3, z3, z3, shift0.reshape(n_seq, 1, bcols), s0bd,
      row1(prm["b_mu"]), row1(prm["b_w0"]), row1(prm["b_a0"]), row1(prm["b_k_k"]), row1(prm["b_k_a"]),
      row1(prm["b_r_k"]), row1(prm["b_ln_w"]), row1(prm["b_ln_b"]),
      lora(prm["b_w_up"], 0), lora(prm["b_a_up"], HEAD))
    sp = sbd.reshape(n_seq, n_pairs, 2, HEAD, 2, HEAD)
    s_fin = jnp.stack([sp[:, :, 0, :, 0, :], sp[:, :, 1, :, 1, :]], axis=2).reshape(n_seq, n_heads, HEAD, HEAD)
    return y.reshape(m, width), s_fin, shift.reshape(n_seq, bcols)


A_WIDTH = 512
STEP_ROWS = 64
RWKV_CHUNK = 64
RWKV_BLOCK = 128
RWKV_SEQS_PER_STEP = 4
FLASH_TILE = 512
DENSE_TILE = 512
FOX_IN_TILE = 256


UNDERFLOW_LOG2 = 160.0


def _qk_bound(q_gain, k_gain):
    return 1.02 * HEAD ** 0.5 * LOG2E * jnp.max(jnp.abs(q_gain)) * jnp.max(jnp.abs(k_gain))


def _first_key_block(c, qk, n_seq, n_heads, tile):
    c3 =c.reshape(n_seq, -1, c.shape[1])[:, :, :n_heads]
    c_first = c3[:, ::tile]
    c_last = c3[:, tile - 1::tile]
    n = c_first.shape[1]
    dead = (2.0 * qk + c_first[:, :, None, :] - c_last[:, None, :, :]) <= -UNDERFLOW_LOG2
    dead = dead & (jnp.arange(n)[None, None, :, None] < jnp.arange(n)[None, :, None, None])
    dead = dead.reshape(n_seq, n, n, n_heads // 2, 2).all(axis=-1)
    return jnp.transpose(jnp.sum(dead, axis=2), (0, 2, 1)).astype(jnp.int32)


def _trunk(x3, mem_k, mem_v, rwkv_s0, rwkv_shift0, fox_past, p):
    n_seq, seq, d = x3.shape
    m = n_seq * seq
    x = x3.reshape(m, d)
    step = seq == 1
    tm = m if step else DENSE_TILE
    depth = p["norm_mix"].shape[0]
    chunk_v, s_out, shift_out, fk, fv, fl = [], [], [], [], [], []
    ie = io = 0
    for l in range(depth):
        if l % 2 == 0:
            z = norm_matmul(x, p["norm_mix"][l], p["ab_w_in"][ie], tm)
            prm = {k_: p[k_][ie] for k_ in ("b_mu", "b_w0", "b_w_up", "b_a0", "b_a_up", "b_k_k", "b_k_a",
                                           "b_r_k", "b_ln_w", "b_ln_b")}
            if step:
                ya, va = mixer_a_step(z, p["a_v_norm"][ie], p["a_w_s"][ie], p["a_b_s"][ie], A_WIDTH)
                zp = jnp.zeros((n_seq, STEP_ROWS, z.shape[1]), f32).at[:, 0].set(z)
                yb, s_new, sh_new = rwkv7(zp.reshape(n_seq * STEP_ROWS, -1), 2 * A_WIDTH, rwkv_shift0[ie],
                                          rwkv_s0[ie], prm, n_seq, 1, RWKV_CHUNK, STEP_ROWS)
                yb = yb.reshape(n_seq, STEP_ROWS, -1)[:, 0]
                chunk_v.append(va.reshape(n_seq, seq, A_WIDTH))
            else:
                ya = mixer_a(z, p["a_v_norm"][ie], p["a_w_s"][ie], p["a_b_s"][ie], A_WIDTH, tm)
                yb, s_new, sh_new = rwkv7(z, 2 * A_WIDTH, rwkv_shift0[ie], rwkv_s0[ie], prm, n_seq, seq,
                                          RWKV_CHUNK, RWKV_BLOCK)
            w_out = p["ab_w_out"][ie]
            mixed = [(ya, w_out[:A_WIDTH]), (yb, w_out[A_WIDTH:])]
            s_out.append(s_new)
            shift_out.append(sh_new)
            ie += 1
        else:
            w_in = p["c_w_in"][io]
            width = (w_in.shape[1] // 4) // HEAD * HEAD
            n_heads = width // HEAD
            front = (x, p["norm_mix"][l], w_in[:, :4 * width], w_in[:, 4 * width:], p["c_b_f"][io],
                     p["c_q_norm"][io], p["c_k_norm"][io])
            if step:
                q, k, v, sg, lf = fox_step_in(*front)
                ck, cv, cl, pt = fox_past
                og = fox_step(q, k, v, lf, sg, ck[io], cv[io], cl[io], pt)
                fk.append(k.reshape(n_seq, seq, n_heads, HEAD))
                fv.append(v.reshape(n_seq, seq, n_heads, HEAD))
                fl.append(lf.reshape(n_seq, seq, n_heads))
            else:
                q, kt, ka, qa, vt, vtb, sg, lft, c = fox_seq_in(*front, n_seq, FOX_IN_TILE)
                qk = _qk_bound(p["c_q_norm"][io], p["c_k_norm"][io])
                first = _first_key_block(c, qk, n_seq, n_heads, FLASH_TILE)
                og = fox_flash(q, qa, ka, vtb, sg, first, qk, n_seq, FLASH_TILE)
                tokens_first = lambda t: jnp.transpose(t.reshape(n_seq, n_heads, HEAD, seq), (0, 3, 1, 2))
                fk.append(tokens_first(kt))
                fv.append(tokens_first(vt))
                fl.append(jnp.transpose(lft, (0, 2, 1)))
            mixed = [(og, p["c_w_out"][io])]
            io += 1
        if step:
            x = res_matmul(x, mixed, tm)
            q_raw = norm_matmul(x, p["norm_mem"][l], p["m_wq"][l], tm)
            o = mem_attn_step(q_raw, p["m_q_norm"][l], mem_k, mem_v, l)
            x = res_matmul(x, [(o, p["m_wo"][l])], tm)
        else:
            x = mem_attn(x, mixed, p["norm_mem"][l], p["m_wq"][l], p["m_q_norm"][l], mem_k[l].astype(bf16),
                         mem_v[l].astype(bf16), p["m_wo"][l], seq, tm)
        x = ffn(x, p["norm_ffn"][l], p["f_w_in"][l], p["f_w_out"][l], tm)
    stack = lambda xs: jnp.stack(xs)
    return (x.reshape(n_seq, seq, d), (stack(chunk_v) if chunk_v else None), stack(s_out), stack(shift_out),
            stack(fk), stack(fv), stack(fl))


def kernel(x_prompt, x_sample, mem_prompt, cache_mem_k, cache_mem_v, state_rwkv_S, state_rwkv_shift, cache_fox_k, cache_fox_v, cache_fox_logf, page_table, norm_mix, norm_mem, norm_ffn, ab_w_in, ab_w_out, a_v_norm, a_w_s, a_b_s, b_mu, b_w0, b_w_up, b_a0, b_a_up, b_k_k, b_k_a, b_r_k, b_ln_w, b_ln_b, c_w_in, c_b_f, c_q_norm, c_k_norm, c_w_out, m_mem_norm, m_wq, m_wk, m_wv, m_q_norm, m_k_norm, m_wo, f_w_in, f_w_out):
    cast = lambda w: w.astype(bf16)
    p = dict(norm_mix=norm_mix, norm_mem=norm_mem, norm_ffn=norm_ffn, ab_w_in=cast(ab_w_in),
             ab_w_out=cast(ab_w_out), a_v_norm=a_v_norm, a_w_s=a_w_s, a_b_s=a_b_s, b_mu=b_mu, b_w0=b_w0,
             b_w_up=b_w_up, b_a0=b_a0, b_a_up=b_a_up, b_k_k=b_k_k, b_k_a=b_k_a, b_r_k=b_r_k, b_ln_w=b_ln_w,
             b_ln_b=b_ln_b, c_w_in=cast(c_w_in), c_b_f=c_b_f, c_q_norm=c_q_norm, c_k_norm=c_k_norm,
             c_w_out=cast(c_w_out), m_wq=cast(m_wq), m_q_norm=m_q_norm, m_wo=cast(m_wo),
             f_w_in=cast(f_w_in), f_w_out=cast(f_w_out))
    depth, d = norm_mix.shape
    nb, n_mem = mem_prompt.shape[:2]
    n_even, _, n_heads_b, hd, _ = state_rwkv_S.shape
    mem_heads = d // MEM_HEAD

    mem2 = mem_prompt.reshape(nb * n_mem, d)
    kvs = [mem_kv(mem2, m_mem_norm[l], cast(m_wk[l]), cast(m_wv[l]), m_k_norm[l], 256) for l in range(depth)]
    p_mem_k = jnp.stack([kv[0] for kv in kvs]).reshape(depth, nb, n_mem, d)
    p_mem_v = jnp.stack([kv[1] for kv in kvs]).reshape(depth, nb, n_mem, d)
    s0 = jnp.zeros((n_even, nb, n_heads_b, hd, hd), f32)
    shift0 = jnp.zeros((n_even, nb, state_rwkv_shift.shape[2]), f32)
    y_prompt, _, p_rwkv_s, p_rwkv_shift, p_fox_k, p_fox_v, p_fox_logf = _trunk(
        x_prompt, p_mem_k, p_mem_v, s0, shift0, None, p)

    nd = x_sample.shape[0]
    y_sample, s_chunk_v, s_rwkv_s, s_rwkv_shift, s_fox_k, s_fox_v, s_fox_logf = _trunk(
        x_sample, cache_mem_k, cache_mem_v, state_rwkv_S, state_rwkv_shift, (cache_fox_k, cache_fox_v, cache_fox_logf, page_table), p)
    heads5 = lambda a: a.reshape(depth, nb, n_mem, mem_heads, MEM_HEAD)
    return (y_prompt, y_sample, heads5(p_mem_k), heads5(p_mem_v), p_rwkv_s, p_rwkv_shift, p_fox_k, p_fox_v,
            p_fox_logf, s_chunk_v, s_rwkv_s, s_rwkv_shift, s_fox_k, s_fox_v, s_fox_logf)
```

```python
import functools
import math

import jax
import jax.numpy as jnp
from jax import lax
from jax.experimental import pallas as pl
from jax.experimental.pallas import tpu as pltpu

f32 = jnp.float32
bf16 = jnp.bfloat16

RMS_EPS = 1e-6
GN_EPS = 64e-5
L2_EPS = 1e-12
LOG2E = 1.4426950408889634
LANES = 128
HEAD = 64
A_GROUP = 128
MEM_HEAD = 256
VMEM_LIMIT = 56 * 1024 * 1024


def _params(sem):
    return pltpu.CompilerParams(dimension_semantics=sem, vmem_limit_bytes=VMEM_LIMIT)


def _bdot(a, b):
    return jnp.dot(a.astype(bf16), b.astype(bf16), preferred_element_type=f32)


def _bdot_nt(a, b):
    return lax.dot_general(a.astype(bf16), b.astype(bf16), (((1,), (1,)), ((), ())),
                           preferred_element_type=f32)


def _rms_rows(x, g):
    return x * lax.rsqrt(jnp.mean(x * x, axis=-1, keepdims=True) + RMS_EPS) * g


def _sigmoid(x):
    return 1.0 / (1.0 + jnp.exp(-x))


def _lane_iota(shape):
    return lax.broadcasted_iota(jnp.int32, shape, len(shape) - 1)


def _half_sums(x):
    lo = _lane_iota(x.shape) < HEAD
    s_lo = jnp.sum(jnp.where(lo, x, 0.0), axis=-1, keepdims=True)
    s_hi = jnp.sum(jnp.where(lo, 0.0, x), axis=-1, keepdims=True)
    return jnp.where(lo, s_lo, s_hi)


def _per_head_sums(x):
    n = x.shape[-1] // LANES
    return jnp.concatenate([_half_sums(x[:, i * LANES:(i + 1) * LANES]) for i in range(n)], axis=-1)


def _norm_matmul_kernel(x_ref, g_ref, w_ref, o_ref):
    h = _rms_rows(x_ref[...], g_ref[...])
    o_ref[...] = _bdot(h, w_ref[...])


def norm_matmul(x, g, w, tm):
    m, d = x.shape
    n = w.shape[1]
    return pl.pallas_call(
        _norm_matmul_kernel,
        grid=(m // tm,),
        in_specs=[pl.BlockSpec((tm, d), lambda i: (i, 0)),
                  pl.BlockSpec((1, d), lambda i: (0, 0)),
                  pl.BlockSpec((d, n), lambda i: (0, 0))],
        out_specs=pl.BlockSpec((tm, n), lambda i: (i, 0)),
        out_shape=jax.ShapeDtypeStruct((m, n), f32),
        compiler_params=_params(("parallel",)),
        name="norm_matmul",
    )(x, g.reshape(1, d), w)


def _res_matmul_kernel(*refs):
    x_ref, o_ref = refs[0], refs[-1]
    acc = x_ref[...]
    for a_ref, w_ref in zip(refs[1:-1:2], refs[2:-1:2]):
        acc = acc + _bdot(a_ref[...], w_ref[...])
    o_ref[...] = acc


def res_matmul(x, pairs, tm):
    m, d = x.shape
    in_specs = [pl.BlockSpec((tm, d), lambda i: (i, 0))]
    args = [x]
    for a, w in pairs:
        in_specs += [pl.BlockSpec((tm, a.shape[1]), lambda i: (i, 0)),
                     pl.BlockSpec(w.shape, lambda i: (0, 0))]
        args += [a, w]
    return pl.pallas_call(
        _res_matmul_kernel,
        grid=(m // tm,),
        in_specs=in_specs,
        out_specs=pl.BlockSpec((tm, d), lambda i: (i, 0)),
        out_shape=jax.ShapeDtypeStruct((m, d), f32),
        compiler_params=_params(("parallel",)),
        name="res_matmul",
    )(*args)


def _ffn_kernel(x_ref, g_ref, wi_ref, wo_ref, o_ref, *, d_ff, tf):
    x = x_ref[...]
    h = _rms_rows(x, g_ref[...]).astype(bf16)
    acc = x
    for c in range(d_ff // tf):
        gate = jnp.dot(h, wi_ref[:, c * tf:(c + 1) * tf], preferred_element_type=f32)
        up = jnp.dot(h, wi_ref[:, d_ff + c * tf:d_ff + (c + 1) * tf], preferred_element_type=f32)
        act = gate * _sigmoid(gate) * up
        acc = acc + _bdot(act, wo_ref[c * tf:(c + 1) * tf, :])
    o_ref[...] = acc


def ffn(x, g, w_in, w_out, tm):
    m, d = x.shape
    d_ff = w_out.shape[0]
    tf = 256
    return pl.pallas_call(
        functools.partial(_ffn_kernel, d_ff=d_ff, tf=tf),
        grid=(m // tm,),
        in_specs=[pl.BlockSpec((tm, d), lambda i: (i, 0)),
                  pl.BlockSpec((1, d), lambda i: (0, 0)),
                  pl.BlockSpec(w_in.shape, lambda i: (0, 0)),
                  pl.BlockSpec(w_out.shape, lambda i: (0, 0))],
        out_specs=pl.BlockSpec((tm, d), lambda i: (i, 0)),
        out_shape=jax.ShapeDtypeStruct((m, d), f32),
        compiler_params=_params(("parallel",)),
        name="ffn",
    )(x, g.reshape(1, d), w_in, w_out)


def _mem_kv_kernel(m_ref, g_ref, wk_ref, wv_ref, kg_ref, k_ref, v_ref):
    h = _rms_rows(m_ref[...], g_ref[...]).astype(bf16)
    k = jnp.dot(h, wk_ref[...], preferred_element_type=f32)
    kg = kg_ref[...]
    for hd in range(k.shape[1] // MEM_HEAD):
        sl = slice(hd * MEM_HEAD, (hd + 1) * MEM_HEAD)
        k_ref[:, sl] = _rms_rows(k[:, sl], kg)
    v_ref[...] = jnp.dot(h, wv_ref[...], preferred_element_type=f32)


def mem_kv(mem, g, wk, wv, k_gain, tm):
    m, d = mem.shape
    return pl.pallas_call(
        _mem_kv_kernel,
        grid=(m // tm,),
        in_specs=[pl.BlockSpec((tm, d), lambda i: (i, 0)),
                  pl.BlockSpec((1, d), lambda i: (0, 0)),
                  pl.BlockSpec((d, d), lambda i: (0, 0)),
                  pl.BlockSpec((d, d), lambda i: (0, 0)),
                  pl.BlockSpec((1, MEM_HEAD), lambda i: (0, 0))],
        out_specs=[pl.BlockSpec((tm, d), lambda i: (i, 0))] * 2,
        out_shape=[jax.ShapeDtypeStruct((m, d), f32)] * 2,
        compiler_params=_params(("parallel",)),
        name="mem_kv",
    )(mem, g.reshape(1, d), wk, wv, k_gain.reshape(1, MEM_HEAD))


def _mem_attn_kernel(x_ref, g_ref, wq_ref, qg_ref, k_ref, v_ref, wo_ref, *rest):
    o_ref = rest[-1]
    x = x_ref[...]
    for a_ref, w_ref in zip(rest[:-1:2], rest[1:-1:2]):
        x = x + _bdot(a_ref[...], w_ref[...])
    h = _rms_rows(x, g_ref[...])
    q = _bdot(h, wq_ref[...])
    qg = qg_ref[...] * (MEM_HEAD ** -0.5)
    outs = []
    for hd in range(q.shape[1] // MEM_HEAD):
        sl = slice(hd * MEM_HEAD, (hd + 1) * MEM_HEAD)
        qn = _rms_rows(q[:, sl], qg)
        s = _bdot_nt(qn, k_ref[0, :, sl])
        p = jnp.exp(s - jnp.max(s, axis=-1, keepdims=True))
        l = jnp.sum(p, axis=-1, keepdims=True)
        outs.append(_bdot(p, v_ref[0, :, sl]) / l)
    o = jnp.concatenate(outs, axis=-1)
    o_ref[...] = x + _bdot(o, wo_ref[...])


def mem_attn(x, pairs, g, wq, q_gain, k, v, wo, rows_per_seq, tm):
    m, d = x.shape
    n_mem = k.shape[1]
    per = rows_per_seq // tm
    extra_specs, extra = [], []
    for a, w in pairs:
        extra_specs += [pl.BlockSpec((tm, a.shape[1]), lambda i: (i, 0)), pl.BlockSpec(w.shape, lambda i: (0, 0))]
        extra += [a, w]
    return pl.pallas_call(
        _mem_attn_kernel,
        grid=(m // tm,),
        in_specs=[pl.BlockSpec((tm, d), lambda i: (i, 0)),
                  pl.BlockSpec((1, d), lambda i: (0, 0)),
                  pl.BlockSpec((d, d), lambda i: (0, 0)),
                  pl.BlockSpec((1, MEM_HEAD), lambda i: (0, 0)),
                  pl.BlockSpec((1, n_mem, d), lambda i: (i // per, 0, 0)),
                  pl.BlockSpec((1, n_mem, d), lambda i: (i // per, 0, 0)),
                  pl.BlockSpec((d, d), lambda i: (0, 0))] + extra_specs,
        out_specs=pl.BlockSpec((tm, d), lambda i: (i, 0)),
        out_shape=jax.ShapeDtypeStruct((m, d), f32),
        compiler_params=_params(("parallel",)),
        name="mem_attn",
    )(x, g.reshape(1, d), wq, q_gain.reshape(1, MEM_HEAD), k, v, wo, *extra)


def _gelu(x):
    return 0.5 * x * (1.0 + jnp.tanh(0.7978845608028654 * (x + 0.044715 * x * x * x)))


def _mixer_a_kernel(z_ref, vg_ref, w_ref, b_ref, ya_ref, *, width):
    ge = _gelu(z_ref[...])
    tm = ge.shape[0]
    row = lax.broadcasted_iota(jnp.int32, (A_GROUP, A_GROUP), 0)
    col = lax.broadcasted_iota(jnp.int32, (A_GROUP, A_GROUP), 1)
    for g in range(width // A_GROUP):
        u = ge[:, g * A_GROUP:(g + 1) * A_GROUP]
        v = ge[:, width + g * A_GROUP:width + (g + 1) * A_GROUP]
        va = _rms_rows(v, vg_ref[g:g + 1, :]).astype(bf16)
        w = jnp.where(row >= col, w_ref[g], 0.0).astype(bf16)
        for c in range(tm // A_GROUP):
            rs = slice(c * A_GROUP, (c + 1) * A_GROUP)
            s = jnp.dot(w, va[rs], preferred_element_type=f32) + b_ref[g]
            ya_ref[rs, g * A_GROUP:(g + 1) * A_GROUP] = (u[rs] * s).astype(ya_ref.dtype)


def mixer_a(z, v_gain, w_s, b_s, width, tm):
    m = z.shape[0]
    ng = width // A_GROUP
    b_rows = jnp.broadcast_to(b_s[:, :, None], (ng, A_GROUP, A_GROUP))
    return pl.pallas_call(
        functools.partial(_mixer_a_kernel, width=width),
        grid=(m // tm,),
        in_specs=[pl.BlockSpec((tm, 2 * width), lambda i: (i, 0)),
                  pl.BlockSpec((ng, A_GROUP), lambda i: (0, 0)),
                  pl.BlockSpec((ng, A_GROUP, A_GROUP), lambda i: (0, 0, 0)),
                  pl.BlockSpec((ng, A_GROUP, A_GROUP), lambda i: (0, 0, 0))],
        out_specs=pl.BlockSpec((tm, width), lambda i: (i, 0)),
        out_shape=jax.ShapeDtypeStruct((m, width), bf16),
        compiler_params=_params(("parallel",)),
        name="mixer_a",
    )(z, v_gain, w_s, b_rows)


def _mixer_a_step_kernel(z_ref, vg_ref, w0_ref, b0_ref, ya_ref, va_ref, *, width):
    ge = _gelu(z_ref[...])
    for g in range(width // A_GROUP):
        sl = slice(g * A_GROUP, (g + 1) * A_GROUP)
        u = ge[:, sl]
        v = ge[:, width + g * A_GROUP:width + (g + 1) * A_GROUP]
        va = _rms_rows(v, vg_ref[g:g + 1, :])
        va_ref[:, sl] = va
        ya_ref[:, sl] = (u * (w0_ref[:, sl] * va + b0_ref[:, sl])).astype(ya_ref.dtype)


def mixer_a_step(z, v_gain, w_s, b_s, width):
    m = z.shape[0]
    w0 = jnp.repeat(w_s[:, 0, 0], A_GROUP).reshape(1, width)
    b0 = jnp.repeat(b_s[:, 0], A_GROUP).reshape(1, width)
    return pl.pallas_call(
        functools.partial(_mixer_a_step_kernel, width=width),
        grid=(1,),
        in_specs=[pl.BlockSpec((m, 2 * width), lambda i: (0, 0)),
                  pl.BlockSpec(v_gain.shape, lambda i: (0, 0)),
                  pl.BlockSpec((1, width), lambda i: (0, 0)),
                  pl.BlockSpec((1, width), lambda i: (0, 0))],
        out_specs=[pl.BlockSpec((m, width), lambda i: (0, 0))] * 2,
        out_shape=[jax.ShapeDtypeStruct((m, width), bf16), jax.ShapeDtypeStruct((m, width), f32)],
        compiler_params=_params(("arbitrary",)),
        name="mixer_a_step",
    )(z, v_gain, w0, b0)


def _split3(x):
    hi = x.astype(bf16)
    r1 = x - hi.astype(f32)
    mid = r1.astype(bf16)
    lo = (r1 - mid.astype(f32)).astype(bf16)
    return hi, mid, lo


def _dot01_left(sel, x):
    hi, mid, lo = _split3(x)
    d = lambda p: jnp.dot(sel, p, preferred_element_type=f32)
    return d(hi) + d(mid) + d(lo)


def _dot01_right(x, sel):
    hi, mid, lo = _split3(x)
    d = lambda p: jnp.dot(p, sel, preferred_element_type=f32)
    return d(hi) + d(mid) + d(lo)


def _log_sigmoid(x):
    return jnp.minimum(x, 0.0) - jnp.log(1.0 + jnp.exp(-jnp.abs(x)))


AUG = 2 * LANES
SHIFT_LANE = 12


def _fox_project(x_ref, g_ref, w_ref, wf_ref, bf_ref, qg_ref, kg_ref, width, unit):
    h = _rms_rows(x_ref[...], g_ref[...]).astype(bf16)
    z = jnp.dot(h, w_ref[...], preferred_element_type=f32)
    q, k = z[:, :width], z[:, width:2 * width]
    v, g = z[:, 2 * width:3 * width], z[:, 3 * width:4 * width]
    qn = q * lax.rsqrt(_per_head_sums(q * q) * (1.0 / HEAD) + RMS_EPS) * (qg_ref[...] * (HEAD ** -0.5 * unit))
    kn = k * lax.rsqrt(_per_head_sums(k * k) * (1.0 / HEAD) + RMS_EPS) * kg_ref[...]
    lf = _log_sigmoid(jnp.dot(h, wf_ref[...], preferred_element_type=f32) + bf_ref[...])
    return qn, kn, v, g, lf


def _fox_in_args(x, g, w, wf, b_f, q_gain, k_gain):
    d = x.shape[1]
    width = w.shape[1] // 4
    n_heads = width // HEAD
    wf_pad = jnp.zeros((d, LANES), bf16).at[:, :n_heads].set(wf)
    bf_pad = jnp.zeros((1, LANES), f32).at[0, :n_heads].set(b_f)
    qg = jnp.tile(q_gain, n_heads).reshape(1, width)
    kg = jnp.tile(k_gain, n_heads).reshape(1, width)
    const = lambda i: (0, 0)
    specs = [pl.BlockSpec((1, d), const), pl.BlockSpec(w.shape, const), pl.BlockSpec((d, LANES), const),
             pl.BlockSpec((1, LANES), const), pl.BlockSpec((1, width), const), pl.BlockSpec((1, width), const)]
    return specs, (g.reshape(1, d), w, wf_pad, bf_pad, qg, kg)


def _fox_step_in_kernel(x_ref, g_ref, w_ref, wf_ref, bf_ref, qg_ref, kg_ref,
                        q_ref, k_ref, v_ref, sg_ref, lf_ref, *, width, n_heads):
    qn, kn, v, g, lf = _fox_project(x_ref, g_ref, w_ref, wf_ref, bf_ref, qg_ref, kg_ref, width, 1.0)
    q_ref[...] = qn.astype(bf16).astype(f32)
    k_ref[...] = kn
    v_ref[...] = v
    sg_ref[...] = _sigmoid(g)
    lf_ref[...] = lf[:, :n_heads]


def fox_step_in(x, g, w, wf, b_f, q_gain, k_gain):
    m, d = x.shape
    width = w.shape[1] // 4
    n_heads = width // HEAD
    specs, args = _fox_in_args(x, g, w, wf, b_f, q_gain, k_gain)
    whole = lambda n: pl.BlockSpec((m, n), lambda i: (0, 0))
    return pl.pallas_call(
        functools.partial(_fox_step_in_kernel, width=width, n_heads=n_heads),
        grid=(1,),
        in_specs=[whole(d)] + specs,
        out_specs=[whole(width)] * 4 + [whole(n_heads)],
        out_shape=[jax.ShapeDtypeStruct((m, width), f32)] * 4 + [jax.ShapeDtypeStruct((m, n_heads), f32)],
        compiler_params=_params(("arbitrary",)),
        name="fox_step_in",
    )(x, *args)


def _fox_seq_in_kernel(x_ref, g_ref, w_ref, wf_ref, bf_ref, qg_ref, kg_ref,
                       q_ref, kt_ref, ka_ref, qa_ref, vt_ref, vtb_ref, sg_ref, lft_ref, c_ref, carry_ref,
                       *, tiles_per_seq, width, n_heads):
    i = pl.program_id(0)
    qn, kn, v, g, lf = _fox_project(x_ref, g_ref, w_ref, wf_ref, bf_ref, qg_ref, kg_ref, width, LOG2E)
    tm = lf.shape[0]
    q_ref[...] = qn.astype(bf16)
    sg_ref[...] = _sigmoid(g)
    for blk in range(width // LANES):
        sl = slice(blk * LANES, (blk + 1) * LANES)
        kt_ref[0, sl, :] = kn[:, sl].T
        vt = v[:, sl].T
        vt_ref[0, sl, :] = vt
        vtb_ref[0, 0, sl, :] = vt.astype(bf16)
    lft_ref[0] = lf.T[:n_heads, :]

    @pl.when(i % tiles_per_seq == 0)
    def _():
        carry_ref[...] = jnp.zeros_like(carry_ref)

    row = lax.broadcasted_iota(jnp.int32, (tm, tm), 0)
    col = lax.broadcasted_iota(jnp.int32, (tm, tm), 1)
    tri = jnp.where(row >= col, 1.0, 0.0).astype(bf16)
    c = _dot01_left(tri, lf) + carry_ref[...]
    carry_ref[...] = c[tm - 1:tm, :]
    c = c * LOG2E
    c_ref[...] = c
    pieces = jnp.concatenate(_split3(c), axis=-1)
    rr = lax.broadcasted_iota(jnp.int32, (3 * LANES, LANES), 0)
    cc = lax.broadcasted_iota(jnp.int32, (3 * LANES, LANES), 1)
    piece, head = rr // LANES, rr % LANES
    lane = _lane_iota((1, LANES))
    k_const = jnp.where((lane < 6) | (lane == SHIFT_LANE), 1.0, 0.0)
    q_const = jnp.where((lane >= 6) & (lane < 12), -1.0, 0.0)
    knb = kn.astype(bf16)
    for p in range(n_heads // 2):
        place = lambda base: jnp.where(((head == 2 * p) & (cc == base + piece))
                                       | ((head == 2 * p + 1) & (cc == base + 3 + piece)), 1.0, 0.0).astype(bf16)
        both = jnp.dot(pieces, jnp.concatenate([place(6), place(0)], axis=-1), preferred_element_type=f32)
        k_aug = both[:, :LANES] + k_const
        ka_ref[p] = jnp.concatenate([knb[:, p * LANES:(p + 1) * LANES], k_aug.astype(bf16)], axis=-1)
        qa_ref[p] = (both[:, LANES:] + q_const).astype(bf16)


def fox_seq_in(x, g, w, wf, b_f, q_gain, k_gain, n_seq, tm):
    m, d = x.shape
    width = w.shape[1] // 4
    n_heads = width // HEAD
    n_pairs = n_heads // 2
    seq = m // n_seq
    tps = seq // tm
    specs, args = _fox_in_args(x, g, w, wf, b_f, q_gain, k_gain)
    row = lambda i: (i, 0)
    chan = lambda i: (i // tps, 0, i % tps)
    return pl.pallas_call(
        functools.partial(_fox_seq_in_kernel, tiles_per_seq=tps, width=width, n_heads=n_heads),
        grid=(m // tm,),
        in_specs=[pl.BlockSpec((tm, d), row)] + specs,
        out_specs=[pl.BlockSpec((tm, width), row),
                   pl.BlockSpec((1, width, tm), chan),
                   pl.BlockSpec((n_pairs, tm, AUG), lambda i: (0, i, 0)),
                   pl.BlockSpec((n_pairs, tm, LANES), lambda i: (0, i, 0)),
                   pl.BlockSpec((1, width, tm), chan),
                   pl.BlockSpec((1, 1, width, tm), lambda i: (i // tps, i % tps, 0, 0)),
                   pl.BlockSpec((tm, width), row),
                   pl.BlockSpec((1, n_heads, tm), chan),
                   pl.BlockSpec((tm, LANES), row)],
        out_shape=[jax.ShapeDtypeStruct((m, width), bf16),
                   jax.ShapeDtypeStruct((n_seq, width, seq), f32),
                   jax.ShapeDtypeStruct((n_pairs, m, AUG), bf16),
                   jax.ShapeDtypeStruct((n_pairs, m, LANES), bf16),
                   jax.ShapeDtypeStruct((n_seq, width, seq), f32),
                   jax.ShapeDtypeStruct((n_seq, tps, width, tm), bf16),
                   jax.ShapeDtypeStruct((m, width), f32),
                   jax.ShapeDtypeStruct((n_seq, n_heads, seq), f32),
                   jax.ShapeDtypeStruct((m, LANES), f32)],
        scratch_shapes=[pltpu.VMEM((1, LANES), f32)],
        compiler_params=_params(("arbitrary",)),
        name="fox_seq_in",
    )(x, *args)


def _fox_flash_kernel(lo_ref, qk_ref, q_ref, qa_ref, ka_ref, vt_ref, sg_ref, o_ref, *, tq, tv, bounded):
    b, p, qi = pl.program_id(0), pl.program_id(1), pl.program_id(2)
    nsub = tq // tv
    q2 = q_ref[0]
    lo = _lane_iota((tq, LANES)) < HEAD
    zero = jnp.zeros_like(q2)
    bias2 = qa_ref[0]
    lane = _lane_iota((tq, LANES))
    shift = jnp.where(lane == SHIFT_LANE, -qk_ref[0], 0.0).astype(bf16)
    qa = []
    for j in (0, 1):
        own = ((lane >= 3 * j) & (lane < 3 * j + 3)) | ((lane >= 6 + 3 * j) & (lane < 9 + 3 * j))
        bias = jnp.where(own, bias2, zero)
        if bounded:
            bias = jnp.where(lane == SHIFT_LANE, shift, bias)
        qh = jnp.where(lo, q2, zero) if j == 0 else jnp.where(lo, zero, q2)
        qa.append(jnp.concatenate([qh, bias], axis=-1))
    row = lax.broadcasted_iota(jnp.int32, (tq, tq), 0)
    col = lax.broadcasted_iota(jnp.int32, (tq, tq), 1)

    def scores(kb, j):
        ka = ka_ref[0, 0, pl.ds(pl.multiple_of(kb * tq, tq), tq), :]
        return _bdot_nt(ka, qa[j])

    def softmax_pv(kb, j, t, stats, masked):
        m_old, l_old, acc = stats
        if masked:
            t = jnp.where(row <= col, t, -jnp.inf)
        m_new = jnp.maximum(m_old, jnp.max(t, axis=0, keepdims=True))
        pe = jnp.exp2(t - m_new).astype(bf16)
        alpha = jnp.exp2(m_old - m_new)
        l_new = alpha * l_old + jnp.sum(pe.astype(f32), axis=0, keepdims=True)
        pv = sum(jnp.dot(vt_ref[0, kb * nsub + s, j * HEAD:(j + 1) * HEAD, :], pe[s * tv:(s + 1) * tv],
                         preferred_element_type=f32) for s in range(nsub))
        return m_new, l_new, acc * alpha + pv

    def block(kb, t0, st0, st1, masked, last):
        t1 = scores(kb, 1)
        st0 = softmax_pv(kb, 0, t0, st0, masked)
        t0_next = t0 if last else scores(kb + 1, 0)
        st1 = softmax_pv(kb, 1, t1, st1, masked)
        return t0_next, st0, st1

    def pv(kb, j, pe):
        return sum(jnp.dot(vt_ref[0, kb * nsub + s, j * HEAD:(j + 1) * HEAD, :], pe[s * tv:(s + 1) * tv],
                   preferred_element_type=f32) for s in range(nsub))

    def free_blocks(kbs, sts):
        units = [(kb, j) for kb in kbs for j in (0, 1)]
        l = [sts[0][1], sts[1][1]]
        acc = [sts[0][2], sts[1][2]]
        t_next = scores(*units[0])
        for n, (kb, j) in enumerate(units):
            t = t_next
            if n + 1 < len(units):
                t_next = scores(*units[n + 1])
            pe = jnp.exp2(jnp.where((row <= col) | (kb < qi), t, -jnp.inf))
            l[j] = l[j] + jnp.sum(pe, axis=0, keepdims=True)
            acc[j] = acc[j] + pv(kb, j, pe.astype(bf16))
        return (sts[0][0], l[0], acc[0]), (sts[1][0], l[1], acc[1])

    init = (jnp.full((1, tq), -jnp.inf, f32), jnp.zeros((1, tq), f32), jnp.zeros((HEAD, tq), f32))
    first = lo_ref[b, p, qi]
    if bounded:
        odd = (qi - first + 1) % 2
        sts = lax.cond(odd == 1, lambda s: free_blocks((first,), s), lambda s: s, (init, init))
        (m0, l0, a0), (m1, l1, a1) = lax.fori_loop(
            0, (qi - first + 1) // 2,
            lambda i, s: free_blocks((first + odd + 2 * i, first + odd + 2 * i + 1), s), sts)
    else:
        state = lax.fori_loop(first, qi, lambda kb, s: block(kb, *s, False, False),
                              (scores(first, 0), init, init))
        _, (m0, l0, a0), (m1, l1, a1) = block(qi, *state, True, True)
    o = jnp.concatenate([a0 / l0, a1 / l1], axis=0).T
    o_ref[0] = (o * sg_ref[0]).astype(o_ref.dtype)


QK_BOUNDED_MAX = 40.0


def fox_flash(q, qa, ka, vtb, sg, first_block, qk, n_seq, tq):
    m, width = q.shape
    seq = m // n_seq
    n_pairs = width // LANES
    tv = vtb.shape[3]
    r3 = lambda a: a.reshape(n_seq, seq, a.shape[1])
    tile = pl.BlockSpec((1, tq, LANES), lambda b, p, i, lo, qk_: (b, i, p))
    grid_spec = pltpu.PrefetchScalarGridSpec(
        num_scalar_prefetch=2,
        grid=(n_seq, n_pairs, seq // tq),
        in_specs=[tile,
                  pl.BlockSpec((1, tq, LANES), lambda b, p, i, lo, qk_: (p, b * (seq // tq) + i, 0)),
                  pl.BlockSpec((1, 1, seq, AUG), lambda b, p, i, lo, qk_: (p, b, 0, 0)),
                  pl.BlockSpec((1, seq // tv, LANES, tv), lambda b, p, i, lo, qk_: (b, 0, p, 0)),
                  tile],
        out_specs=tile,
    )
    args = (first_block, qk.reshape(1).astype(f32), r3(q), qa, ka.reshape(n_pairs, n_seq, seq, AUG), vtb,
            r3(sg))
    call = lambda bounded: pl.pallas_call(
        functools.partial(_fox_flash_kernel, tq=tq, tv=tv, bounded=bounded),
        grid_spec=grid_spec,
        out_shape=jax.ShapeDtypeStruct((n_seq, seq, width), bf16),
        compiler_params=_params(("parallel", "parallel", "arbitrary")),
        name="fox_flash_bounded" if bounded else "fox_flash",
    )
    out = lax.cond(qk <= QK_BOUNDED_MAX, lambda a: call(True)(*a), lambda a: call(False)(*a), args)
    return out.reshape(m, width)


FOX_PAGES_PER_STEP = 16
EXP_UNDERFLOW = 104.0


def _fox_scores_kernel(pt_ref, qc_ref, knt_ref, lfn_ref, *rest, n_heads, pps):
    kt_refs, lf_refs = rest[:pps], rest[pps:2 * pps]
    t_ref, town_ref, m_out, l_out, m_ref, l_ref, carry_ref = rest[2 * pps:]
    pg = pl.program_id(1)
    page = lf_refs[0].shape[2]

    def score(kt_ref, bias):
        rows = [jnp.sum(kt_ref[0, h] * qc_ref[0, h], axis=0, keepdims=True) for h in range(n_heads)]
        t = jnp.concatenate(rows, axis=0) + bias
        m_old = m_ref[...]
        m_new = jnp.maximum(m_old, jnp.max(t, axis=-1, keepdims=True))
        l_ref[...] = jnp.exp(m_old - m_new) * l_ref[...] + jnp.sum(jnp.exp(t - m_new), axis=-1, keepdims=True)
        m_ref[...] = m_new
        return t

    @pl.when(pg == 0)
    def _():
        m_ref[...] = jnp.full(m_ref.shape, -jnp.inf, f32)
        l_ref[...] = jnp.zeros_like(l_ref)
        carry_ref[...] = lfn_ref[0]
        town_ref[0] = score(knt_ref, jnp.where(_lane_iota((n_heads, page)) == 0, 0.0, -jnp.inf))

    r = lax.broadcasted_iota(jnp.int32, (page, page), 0)
    cidx = lax.broadcasted_iota(jnp.int32, (page, page), 1)
    later = jnp.where(r > cidx, 1.0, 0.0).astype(bf16)
    for j in range(pps):
        lf = lf_refs[j][0]
        t_ref[0, j] = score(kt_refs[j], _dot01_right(lf, later) + carry_ref[...])
        carry_ref[...] = carry_ref[...] + jnp.sum(lf, axis=-1, keepdims=True)

    @pl.when(pg == pl.num_programs(1) - 1)
    def _():
        m_out[0] = m_ref[...]
        l_out[0] = l_ref[...]


def _fox_values_kernel(pt_ref, idx_ref, cnt_ref, town_ref, vnt_ref, t_ref, m_ref, l_ref, sg_ref, vt_hbm,
                       o_ref, acc_ref, vbuf, sem, *, n_heads, n_pages):
    i = pl.program_id(0)
    n_live = cnt_ref[i]
    m = m_ref[0]

    def page_copy(u, buf):
        page = pt_ref[i, n_pages - 1 - idx_ref[i, u]]
        return pltpu.make_async_copy(vt_hbm.at[page], vbuf.at[buf], sem.at[buf])

    @pl.when(n_live > 0)
    def _():
        page_copy(0, 0).start()

    pe = jnp.exp(town_ref[0] - m)
    for h in range(n_heads):
        acc_ref[h] = pe[h:h + 1, :] * vnt_ref[0, h]

    def live_page(u, _):
        buf = u % 2

        @pl.when(u + 1 < n_live)
        def _():
            page_copy(u + 1, 1 - buf).start()

        page_copy(u, buf).wait()
        pe = jnp.exp(t_ref[0, idx_ref[i, u]] - m)
        for h in range(n_heads):
            acc_ref[h] = acc_ref[h] + pe[h:h + 1, :] * vbuf[buf, h]
        return 0

    lax.fori_loop(0, n_live, live_page, 0)
    l = l_ref[0]
    for h in range(n_heads):
        o_ref[0, h] = jnp.sum(acc_ref[h], axis=-1, keepdims=True) / l[h:h + 1, :] * sg_ref[0, h]


def fox_step(q, k_new, v_new, lf_new, sg, cache_k, cache_v, cache_lf, page_table):
    b, width = q.shape
    pool, page, n_heads = cache_lf.shape
    n_pages = page_table.shape[1]
    pps = math.gcd(FOX_PAGES_PER_STEP, n_pages)
    kt = jnp.transpose(cache_k, (0, 2, 3, 1))
    vt = jnp.transpose(cache_v, (0, 2, 3, 1))
    lft = jnp.transpose(cache_lf, (0, 2, 1))
    col = lambda a: a.reshape(b, n_heads, HEAD, 1)
    own_page = lambda a: jnp.zeros((b, n_heads, HEAD, page), f32).at[..., 0].set(a.reshape(b, n_heads, HEAD))
    qc = jnp.broadcast_to(col(q), (b, n_heads, HEAD, page))
    per_seq = lambda shape: pl.BlockSpec((1,) + shape, lambda i, g, *_: (i,) + (0,) * len(shape))
    page_spec = lambda shape, j: pl.BlockSpec(
        (1,) + shape, lambda i, g, pt: (pt[i, n_pages - 1 - (g * pps + j)],) + (0,) * len(shape))
    stat = jax.ShapeDtypeStruct((b, n_heads, 1), f32)
    t_all, t_own, m, l = pl.pallas_call(
        functools.partial(_fox_scores_kernel, n_heads=n_heads, pps=pps),
        grid_spec=pltpu.PrefetchScalarGridSpec(
            num_scalar_prefetch=1,
            grid=(b, n_pages // pps),
            in_specs=[per_seq((n_heads, HEAD, page))] * 2 + [per_seq((n_heads, 1))]
                     + [page_spec((n_heads, HEAD, page), j) for j in range(pps)]
                     + [page_spec((n_heads, page), j) for j in range(pps)],
            out_specs=[pl.BlockSpec((1, pps, n_heads, page), lambda i, g, pt: (i, g, 0, 0)),
                       per_seq((n_heads, page)), per_seq((n_heads, 1)), per_seq((n_heads, 1))],
            scratch_shapes=[pltpu.VMEM((n_heads, 1), f32)] * 3),
        out_shape=[jax.ShapeDtypeStruct((b, n_pages, n_heads, page), f32),
                   jax.ShapeDtypeStruct((b, n_heads, page), f32), stat, stat],
        compiler_params=_params(("parallel", "arbitrary")),
        name="fox_step_scores",
    )(page_table, qc, own_page(k_new), lf_new.reshape(b, n_heads, 1), *([kt] * pps), *([lft] * pps))

    live = jnp.any(jnp.max(t_all, axis=-1) - m[:, None, :, 0] > -EXP_UNDERFLOW, axis=-1)
    live_idx = jnp.argsort(jnp.logical_not(live), axis=1, stable=True).astype(jnp.int32)
    count = jnp.sum(live, axis=1).astype(jnp.int32)

    one = lambda shape: pl.BlockSpec((1,) + shape, lambda i, *_: (i,) + (0,) * len(shape))
    out = pl.pallas_call(
        functools.partial(_fox_values_kernel, n_heads=n_heads, n_pages=n_pages),
        grid_spec=pltpu.PrefetchScalarGridSpec(
            num_scalar_prefetch=3,
            grid=(b,),
            in_specs=[one((n_heads, page)), one((n_heads, HEAD, page)), one((n_pages, n_heads, page)),
                      one((n_heads, 1)), one((n_heads, 1)), one((n_heads, HEAD, 1)),
                      pl.BlockSpec(memory_space=pl.ANY)],
            out_specs=one((n_heads, HEAD, 1)),
            scratch_shapes=[pltpu.VMEM((n_heads, HEAD, page), f32), pltpu.VMEM((2, n_heads, HEAD, page), f32),
                            pltpu.SemaphoreType.DMA((2,))]),
        out_shape=jax.ShapeDtypeStruct((b, n_heads, HEAD, 1), f32),
        compiler_params=_params(("arbitrary",)),
        name="fox_step_values",
    )(page_table, live_idx, count, t_own, own_page(v_new), t_all, m, l, col(sg), vt)
    return out.reshape(b, width).astype(bf16)


def _mem_attn_step_kernel(q_ref, qg_ref, k_ref, v_ref, o_ref):
    q = q_ref[0]
    qn = q * lax.rsqrt(jnp.mean(q * q, axis=-1, keepdims=True) + RMS_EPS) * (qg_ref[...] * MEM_HEAD ** -0.5)
    s = jnp.sum(k_ref[0, 0] * qn, axis=-1, keepdims=True)
    p = jnp.exp(s - jnp.max(s, axis=0, keepdims=True))
    l = jnp.sum(p, axis=0, keepdims=True)
    o_ref[0] = jnp.sum(p * v_ref[0, 0], axis=0, keepdims=True) / l


def mem_attn_step(q, q_gain, cache_k, cache_v, layer):
    b, d = q.shape
    _, _, n_mem, heads, hd = cache_k.shape
    mem_spec = pl.BlockSpec((1, 1, n_mem, heads, hd), lambda i: (layer, i, 0, 0, 0))
    out = pl.pallas_call(
        _mem_attn_step_kernel,
        grid=(b,),
        in_specs=[pl.BlockSpec((1, 1, heads, hd), lambda i: (i, 0, 0, 0)),
                  pl.BlockSpec((1, 1, hd), lambda i: (0, 0, 0)),
                  mem_spec, mem_spec],
        out_specs=pl.BlockSpec((1, 1, heads, hd), lambda i: (i, 0, 0, 0)),
        out_shape=jax.ShapeDtypeStruct((b, 1, heads, hd), f32),
        compiler_params=_params(("parallel",)),
        name="mem_attn_step",
    )(q.reshape(b, 1, heads, hd), q_gain.reshape(1, 1, hd), cache_k, cache_v)
    return out.reshape(b, d)


def _softplus(x):
    return jnp.maximum(x, 0.0) + jnp.log(1.0 + jnp.exp(-jnp.abs(x)))


def _rwkv_kernel(zr_ref, zk_ref, zv_ref, zg_ref, zwa_ref, sh0_ref, s0_ref,
                 mu_ref, w0_ref, a0_ref, kk_ref, ka_ref, rk_ref, lnw_ref, lnb_ref, wup_ref, aup_ref,
                 y_ref, sout_ref, shout_ref,
                 s_ref, carry_ref, r_s, k_s, v_s, al_s, be_s, ld_s, yo_s, bo_s, sg_s,
                 *, chunk, n_valid, width, padded):
    i = pl.program_id(1)
    nblk = pl.num_programs(1)
    nsq, cg = zr_ref.shape[0], zr_ref.shape[1]
    n_pairs = width // LANES

    @pl.when(i == 0)
    def _():
        s_ref[...] = s0_ref[...]
        carry_ref[...] = sh0_ref[...]

    rows = lax.broadcasted_iota(jnp.int32, (cg, 1), 0)
    valid = (rows + i * cg) < n_valid
    last = jnp.minimum(n_valid - 1 - i * cg, cg - 1)

    for s in range(nsq):
        def shifted(z, lane0, n):
            prev = jnp.where(rows == 0, carry_ref[s, :, lane0:lane0 + n], pltpu.roll(z, 1, 0))
            return z + (prev - z) * mu_ref[:, lane0:lane0 + n]

        zr, zk, zv, zg, zwa = zr_ref[s], zk_ref[s], zv_ref[s], zg_ref[s], zwa_ref[s]
        r = shifted(zr, 0, width)
        k = shifted(zk, width, width)
        v = shifted(zv, 2 * width, width)
        gp = shifted(zg, 3 * width, width)
        wa = shifted(zwa, 4 * width, LANES)

        pieces = ((zr, 0, width), (zk, width, width), (zv, 2 * width, width), (zg, 3 * width, width),
                  (zwa, 4 * width, LANES))
        if padded:
            @pl.when(last >= 0)
            def _():
                for z, lane0, n in pieces:
                    carry_ref[s, :, lane0:lane0 + n] = jnp.sum(jnp.where(rows == last, z, 0.0), axis=0,
                                                               keepdims=True)
        else:
            for z, lane0, n in pieces:
                carry_ref[s, :, lane0:lane0 + n] = z[cg - 1:cg, :]

        w = -_softplus(-(w0_ref[...] + _bdot(jnp.tanh(wa), wup_ref[...]))) - 0.5
        logd = -jnp.exp(w)
        a = _sigmoid(a0_ref[...] + _bdot(wa, aup_ref[...]))
        kk = k * kk_ref[...]
        kk = kk * lax.rsqrt(_per_head_sums(kk * kk) + L2_EPS)
        k2 = k * (1.0 + (a - 1.0) * ka_ref[...])
        bo_s[s] = _per_head_sums(r * k2 * rk_ref[...]) * v
        sg_s[s] = _sigmoid(gp)
        r_s[s] = r
        keep = (lambda t: jnp.where(valid, t, 0.0)) if padded else (lambda t: t)
        k_s[s] = keep(k2)
        v_s[s] = keep(v)
        al_s[s] = -kk
        be_s[s] = keep(kk * a)
        ld_s[s] = keep(logd)

    ri = lax.broadcasted_iota(jnp.int32, (chunk, chunk), 0)
    ci = lax.broadcasted_iota(jnp.int32, (chunk, chunk), 1)
    tri = jnp.where(ri >= ci, 1.0, 0.0).astype(bf16)
    lower, strict = ri >= ci, ri > ci
    eye = jnp.where(ri == ci, 1.0, 0.0)
    lo = _lane_iota((1, LANES)) < HEAD
    sr = lax.broadcasted_iota(jnp.int32, (LANES, LANES), 0) < HEAD
    sc = lax.broadcasted_iota(jnp.int32, (LANES, LANES), 1) < HEAD
    same_head = sr == sc
    n_sq = max(int(math.log2(chunk)) - 1, 0)

    def chunk_body(c, _):
        rs = pl.ds(pl.multiple_of(c * chunk, chunk), chunk)
        pairs = [(s, p) for s in range(nsq) for p in range(n_pairs)]
        heads = [(s, p, j) for s, p in pairs for j in (0, 1)]
        lsl = [slice(p * LANES, (p + 1) * LANES) for p in range(n_pairs)]
        xs, vv, vp, yb, yk, ybe, yke, d_end = {}, {}, {}, {}, {}, {}, {}, {}
        for s in range(nsq):
            ld = ld_s[s, rs, :]
            lcum = _dot01_left(tri, ld)
            lend = lcum[chunk - 1:chunk, :]
            e_neg, e_end = jnp.exp(-lcum), jnp.exp(lend - lcum)
            xa_all, xr_all = al_s[s, rs, :] * jnp.exp(lcum - ld), r_s[s, rs, :] * jnp.exp(lcum)
            be, kc, v_all = be_s[s, rs, :], k_s[s, rs, :], v_s[s, rs, :]
            dec = jnp.exp(lend)
            for p in range(n_pairs):
                xs[s, p] = jnp.concatenate([xa_all[:, lsl[p]], xr_all[:, lsl[p]]], axis=0)
                vv[s, p] = v_all[:, lsl[p]]
                vp[s, p] = vv[s, p].astype(bf16)
                yb[s, p], yk[s, p] = (be * e_neg)[:, lsl[p]], (kc * e_neg)[:, lsl[p]]
                ybe[s, p], yke[s, p] = (be * e_end)[:, lsl[p]], (kc * e_end)[:, lsl[p]]
                d_end[s, p] = dec[:, lsl[p]]
        s_bd = {sp: s_ref[sp[0], sp[1]] for sp in pairs}
        a_ab, a_ak, a_rb, a_rk = {}, {}, {}, {}
        for s, p, j in heads:
            xm = jnp.where(lo if j == 0 else jnp.logical_not(lo), xs[s, p], 0.0).astype(bf16)
            gb = _bdot_nt(xm, yb[s, p])
            gk = _bdot_nt(xm, yk[s, p])
            a_ab[s, p, j] = jnp.where(strict, gb[:chunk], 0.0)
            a_ak[s, p, j] = jnp.where(strict, gk[:chunk], 0.0)
            a_rb[s, p, j] = jnp.where(lower, gb[chunk:], 0.0)
            a_rk[s, p, j] = jnp.where(lower, gk[chunk:], 0.0)
        tinv = {h: eye + a_ab[h] for h in heads}
        pw = dict(a_ab)
        for _ in range(n_sq):
            for h in heads:
                pw[h] = _bdot(pw[h], pw[h])
            for h in heads:
                tinv[h] = tinv[h] + _bdot(tinv[h], pw[h])
        xh = {sp: _bdot_nt(xs[sp], s_bd[sp]) for sp in pairs}
        av = {(s, p, j): _bdot(a_ak[s, p, j], vp[s, p]) for s, p, j in heads}
        uh = {(s, p, j): _bdot(tinv[s, p, j], xh[s, p][:chunk] + av[s, p, j]) for s, p, j in heads}
        u = {(s, p): jnp.where(lo, uh[s, p, 0], uh[s, p, 1]) for s, p in pairs}
        yh = {(s, p, j): _bdot(a_rb[s, p, j], u[s, p]) + _bdot(a_rk[s, p, j], vp[s, p]) for s, p, j in heads}
        for s, p in pairs:
            yo_s[s, rs, lsl[p]] = xh[s, p][chunk:] + jnp.where(lo, yh[s, p, 0], yh[s, p, 1])
            uv_t = jnp.concatenate([u[s, p], vv[s, p]], axis=0).T
            upd = _bdot(uv_t, jnp.concatenate([ybe[s, p], yke[s, p]], axis=0))
            s_ref[s, p] = s_bd[s, p] * d_end[s, p] + jnp.where(same_head, upd, 0.0)
        return 0

    lax.fori_loop(0, cg // chunk, chunk_body, 0)

    for s in range(nsq):
        y = yo_s[s]
        mean = _per_head_sums(y) * (1.0 / HEAD)
        yc = y - mean
        var = _per_head_sums(yc * yc) * (1.0 / HEAD)
        y = yc * lax.rsqrt(var + GN_EPS) * lnw_ref[...] + lnb_ref[...]
        y_ref[s] = ((y + bo_s[s]) * sg_s[s]).astype(y_ref.dtype)

    @pl.when(i == nblk - 1)
    def _():
        sout_ref[...] = s_ref[...]
        shout_ref[...] = carry_ref[...]


def rwkv7(z, col0, shift0, s0, prm, n_seq, n_valid, chunk, block_rows):
    m = z.shape[0]
    rows = m // n_seq
    nblk = rows // block_rows
    width = prm["b_w0"].shape[0]
    n_heads = width // HEAD
    n_pairs = width // LANES
    bcols = 4 * width + LANES
    s0p = s0.reshape(n_seq, n_pairs, 2, HEAD, HEAD)
    zeros = jnp.zeros_like(s0p[:, :, 0])
    s0bd = jnp.concatenate([jnp.concatenate([s0p[:, :, 0], zeros], axis=-1),
                            jnp.concatenate([zeros, s0p[:, :, 1]], axis=-1)], axis=-2)
    lora = lambda w_, off: jnp.zeros((LANES, width), bf16).at[off:off + w_.shape[0]].set(w_.astype(bf16))
    row1 = lambda a: a.reshape(1, -1).astype(f32)
    cb = col0 // width
    nsq = math.gcd(RWKV_SEQS_PER_STEP, n_seq)
    z3 = z.reshape(n_seq, rows, z.shape[1])
    zspec = lambda j: pl.BlockSpec((nsq, block_rows, width), lambda g, i: (g, i, cb + j))
    const = lambda shape: pl.BlockSpec(shape, lambda g, i: (0,) * len(shape))
    per_seq = lambda shape: pl.BlockSpec((nsq,) + shape, lambda g, i: (g,) + (0,) * len(shape))
    y, sbd, shift = pl.pallas_call(
        functools.partial(_rwkv_kernel, chunk=chunk, n_valid=n_valid, width=width, padded=n_valid < rows),
        grid=(n_seq // nsq, nblk),
        in_specs=[zspec(0), zspec(1), zspec(2), zspec(3),
                  pl.BlockSpec((nsq, block_rows, LANES), lambda g, i: (g, i, (col0 + 4 * width) // LANES)),
                  per_seq((1, bcols)), per_seq((n_pairs, LANES, LANES)),
                  const((1, bcols))] + [const((1, width))] * 7 + [const((LANES, width))] * 2,
        out_specs=[pl.BlockSpec((nsq, block_rows, width), lambda g, i: (g, i, 0)),
                   per_seq((n_pairs, LANES, LANES)), per_seq((1, bcols))],
        out_shape=[jax.ShapeDtypeStruct((n_seq, rows, width), bf16),
                   jax.ShapeDtypeStruct((n_seq, n_pairs, LANES, LANES), f32),
                   jax.ShapeDtypeStruct((n_seq, 1, bcols), f32)],
        scratch_shapes=[pltpu.VMEM((nsq, n_pairs, LANES, LANES), f32), pltpu.VMEM((nsq, 1, bcols), f32)]
                       + [pltpu.VMEM((nsq, block_rows, width), f32)] * 9,
        compiler_params=_params(("parallel", "arbitrary")),
        name="rwkv7",
    )(z3, z3, z3, z3, z3, shift0.reshape(n_seq, 1, bcols), s0bd,
      row1(prm["b_mu"]), row1(prm["b_w0"]), row1(prm["b_a0"]), row1(prm["b_k_k"]), row1(prm["b_k_a"]),
      row1(prm["b_r_k"]), row1(prm["b_ln_w"]), row1(prm["b_ln_b"]),
      lora(prm["b_w_up"], 0), lora(prm["b_a_up"], HEAD))
    sp = sbd.reshape(n_seq, n_pairs, 2, HEAD, 2, HEAD)
    s_fin = jnp.stack([sp[:, :, 0, :, 0, :], sp[:, :, 1, :, 1, :]], axis=2).reshape(n_seq, n_heads, HEAD, HEAD)
    return y.reshape(m, width), s_fin, shift.reshape(n_seq, bcols)


A_WIDTH = 512
STEP_ROWS = 64
RWKV_CHUNK = 64
RWKV_BLOCK = 128
RWKV_SEQS_PER_STEP = 4
FLASH_TILE = 512
DENSE_TILE = 512
FOX_IN_TILE = 256


UNDERFLOW_LOG2 = 160.0


def _qk_bound(q_gain, k_gain):
    return 1.02 * HEAD ** 0.5 * LOG2E * jnp.max(jnp.abs(q_gain)) * jnp.max(jnp.abs(k_gain))


def _first_key_block(c, qk, n_seq, n_heads, tile):
    c3 =c.reshape(n_seq, -1, c.shape[1])[:, :, :n_heads]
    c_first = c3[:, ::tile]
    c_last = c3[:, tile - 1::tile]
    n = c_first.shape[1]
    dead = (2.0 * qk + c_first[:, :, None, :] - c_last[:, None, :, :]) <= -UNDERFLOW_LOG2
    dead = dead & (jnp.arange(n)[None, None, :, None] < jnp.arange(n)[None, :, None, None])
    dead = dead.reshape(n_seq, n, n, n_heads // 2, 2).all(axis=-1)
    return jnp.transpose(jnp.sum(dead, axis=2), (0, 2, 1)).astype(jnp.int32)


def _trunk(x3, mem_k, mem_v, rwkv_s0, rwkv_shift0, fox_past, p):
    n_seq, seq, d = x3.shape
    m = n_seq * seq
    x = x3.reshape(m, d)
    step = seq == 1
    tm = m if step else DENSE_TILE
    depth = p["norm_mix"].shape[0]
    chunk_v, s_out, shift_out, fk, fv, fl = [], [], [], [], [], []
    ie = io = 0
    for l in range(depth):
        if l % 2 == 0:
            z = norm_matmul(x, p["norm_mix"][l], p["ab_w_in"][ie], tm)
            prm = {k_: p[k_][ie] for k_ in ("b_mu", "b_w0", "b_w_up", "b_a0", "b_a_up", "b_k_k", "b_k_a",
                                           "b_r_k", "b_ln_w", "b_ln_b")}
            if step:
                ya, va = mixer_a_step(z, p["a_v_norm"][ie], p["a_w_s"][ie], p["a_b_s"][ie], A_WIDTH)
                zp = jnp.zeros((n_seq, STEP_ROWS, z.shape[1]), f32).at[:, 0].set(z)
                yb, s_new, sh_new = rwkv7(zp.reshape(n_seq * STEP_ROWS, -1), 2 * A_WIDTH, rwkv_shift0[ie],
                                          rwkv_s0[ie], prm, n_seq, 1, RWKV_CHUNK, STEP_ROWS)
                yb = yb.reshape(n_seq, STEP_ROWS, -1)[:, 0]
                chunk_v.append(va.reshape(n_seq, seq, A_WIDTH))
            else:
                ya = mixer_a(z, p["a_v_norm"][ie], p["a_w_s"][ie], p["a_b_s"][ie], A_WIDTH, tm)
                yb, s_new, sh_new = rwkv7(z, 2 * A_WIDTH, rwkv_shift0[ie], rwkv_s0[ie], prm, n_seq, seq,
                                          RWKV_CHUNK, RWKV_BLOCK)
            w_out = p["ab_w_out"][ie]
            mixed = [(ya, w_out[:A_WIDTH]), (yb, w_out[A_WIDTH:])]
            s_out.append(s_new)
            shift_out.append(sh_new)
            ie += 1
        else:
            w_in = p["c_w_in"][io]
            width = (w_in.shape[1] // 4) // HEAD * HEAD
            n_heads = width // HEAD
            front = (x, p["norm_mix"][l], w_in[:, :4 * width], w_in[:, 4 * width:], p["c_b_f"][io],
                     p["c_q_norm"][io], p["c_k_norm"][io])
            if step:
                q, k, v, sg, lf = fox_step_in(*front)
                ck, cv, cl, pt = fox_past
                og = fox_step(q, k, v, lf, sg, ck[io], cv[io], cl[io], pt)
                fk.append(k.reshape(n_seq, seq, n_heads, HEAD))
                fv.append(v.reshape(n_seq, seq, n_heads, HEAD))
                fl.append(lf.reshape(n_seq, seq, n_heads))
            else:
                q, kt, ka, qa, vt, vtb, sg, lft, c = fox_seq_in(*front, n_seq, FOX_IN_TILE)
                qk = _qk_bound(p["c_q_norm"][io], p["c_k_norm"][io])
                first = _first_key_block(c, qk, n_seq, n_heads, FLASH_TILE)
                og = fox_flash(q, qa, ka, vtb, sg, first, qk, n_seq, FLASH_TILE)
                tokens_first = lambda t: jnp.transpose(t.reshape(n_seq, n_heads, HEAD, seq), (0, 3, 1, 2))
                fk.append(tokens_first(kt))
                fv.append(tokens_first(vt))
                fl.append(jnp.transpose(lft, (0, 2, 1)))
            mixed = [(og, p["c_w_out"][io])]
            io += 1
        if step:
            x = res_matmul(x, mixed, tm)
            q_raw = norm_matmul(x, p["norm_mem"][l], p["m_wq"][l], tm)
            o = mem_attn_step(q_raw, p["m_q_norm"][l], mem_k, mem_v, l)
            x = res_matmul(x, [(o, p["m_wo"][l])], tm)
        else:
            x = mem_attn(x, mixed, p["norm_mem"][l], p["m_wq"][l], p["m_q_norm"][l], mem_k[l].astype(bf16),
                         mem_v[l].astype(bf16), p["m_wo"][l], seq, tm)
        x = ffn(x, p["norm_ffn"][l], p["f_w_in"][l], p["f_w_out"][l], tm)
    stack = lambda xs: jnp.stack(xs)
    return (x.reshape(n_seq, seq, d), (stack(chunk_v) if chunk_v else None), stack(s_out), stack(shift_out),
            stack(fk), stack(fv), stack(fl))


def kernel(x_prompt, x_sample, mem_prompt, cache_mem_k, cache_mem_v, state_rwkv_S, state_rwkv_shift, cache_fox_k, cache_fox_v, cache_fox_logf, page_table, norm_mix, norm_mem, norm_ffn, ab_w_in, ab_w_out, a_v_norm, a_w_s, a_b_s, b_mu, b_w0, b_w_up, b_a0, b_a_up, b_k_k, b_k_a, b_r_k, b_ln_w, b_ln_b, c_w_in, c_b_f, c_q_norm, c_k_norm, c_w_out, m_mem_norm, m_wq, m_wk, m_wv, m_q_norm, m_k_norm, m_wo, f_w_in, f_w_out):
    cast = lambda w: w.astype(bf16)
    p = dict(norm_mix=norm_mix, norm_mem=norm_mem, norm_ffn=norm_ffn, ab_w_in=cast(ab_w_in),
             ab_w_out=cast(ab_w_out), a_v_norm=a_v_norm, a_w_s=a_w_s, a_b_s=a_b_s, b_mu=b_mu, b_w0=b_w0,
             b_w_up=b_w_up, b_a0=b_a0, b_a_up=b_a_up, b_k_k=b_k_k, b_k_a=b_k_a, b_r_k=b_r_k, b_ln_w=b_ln_w,
             b_ln_b=b_ln_b, c_w_in=cast(c_w_in), c_b_f=c_b_f, c_q_norm=c_q_norm, c_k_norm=c_k_norm,
             c_w_out=cast(c_w_out), m_wq=cast(m_wq), m_q_norm=m_q_norm, m_wo=cast(m_wo),
             f_w_in=cast(f_w_in), f_w_out=cast(f_w_out))
    depth, d = norm_mix.shape
    nb, n_mem = mem_prompt.shape[:2]
    n_even, _, n_heads_b, hd, _ = state_rwkv_S.shape
    mem_heads = d // MEM_HEAD

    mem2 = mem_prompt.reshape(nb * n_mem, d)
    kvs = [mem_kv(mem2, m_mem_norm[l], cast(m_wk[l]), cast(m_wv[l]), m_k_norm[l], 256) for l in range(depth)]
    p_mem_k = jnp.stack([kv[0] for kv in kvs]).reshape(depth, nb, n_mem, d)
    p_mem_v = jnp.stack([kv[1] for kv in kvs]).reshape(depth, nb, n_mem, d)
    s0 = jnp.zeros((n_even, nb, n_heads_b, hd, hd), f32)
    shift0 = jnp.zeros((n_even, nb, state_rwkv_shift.shape[2]), f32)
    y_prompt, _, p_rwkv_s, p_rwkv_shift, p_fox_k, p_fox_v, p_fox_logf = _trunk(
        x_prompt, p_mem_k, p_mem_v, s0, shift0, None, p)

    y_sample, s_chunk_v, s_rwkv_s, s_rwkv_shift, s_fox_k, s_fox_v, s_fox_logf = _trunk(
        x_sample, cache_mem_k, cache_mem_v, state_rwkv_S, state_rwkv_shift, (cache_fox_k, cache_fox_v, cache_fox_logf, page_table), p)
    heads5 = lambda a: a.reshape(depth, nb, n_mem, mem_heads, MEM_HEAD)
    return (y_prompt, y_sample, heads5(p_mem_k), heads5(p_mem_v), p_rwkv_s, p_rwkv_shift, p_fox_k, p_fox_v,
            p_fox_logf, s_chunk_v, s_rwkv_s, s_rwkv_shift, s_fox_k, s_fox_v, s_fox_logf)
```

```python
import functools
import math

import jax
import jax.numpy as jnp
from jax import lax
from jax.experimental import pallas as pl
from jax.experimental.pallas import tpu as pltpu

f32 = jnp.float32
bf16 = jnp.bfloat16

RMS_EPS = 1e-6
GN_EPS = 64e-5
L2_EPS = 1e-12
LOG2E = 1.4426950408889634
LANES = 128
HEAD = 64
A_GROUP = 128
MEM_HEAD = 256
VMEM_LIMIT = 56 * 1024 * 1024


def _params(sem):
    return pltpu.CompilerParams(dimension_semantics=sem, vmem_limit_bytes=VMEM_LIMIT)


def _bdot(a, b):
    return jnp.dot(a.astype(bf16), b.astype(bf16), preferred_element_type=f32)


def _bdot_nt(a, b):
    return lax.dot_general(a.astype(bf16), b.astype(bf16), (((1,), (1,)), ((), ())),
                           preferred_element_type=f32)


def _rms_rows(x, g):
    return x * lax.rsqrt(jnp.mean(x * x, axis=-1, keepdims=True) + RMS_EPS) * g


def _sigmoid(x):
    return 1.0 / (1.0 + jnp.exp(-x))


def _lane_iota(shape):
    return lax.broadcasted_iota(jnp.int32, shape, len(shape) - 1)


def _half_sums(x):
    lo = _lane_iota(x.shape) < HEAD
    s_lo = jnp.sum(jnp.where(lo, x, 0.0), axis=-1, keepdims=True)
    s_hi = jnp.sum(jnp.where(lo, 0.0, x), axis=-1, keepdims=True)
    return jnp.where(lo, s_lo, s_hi)


def _per_head_sums(x):
    n = x.shape[-1] // LANES
    return jnp.concatenate([_half_sums(x[:, i * LANES:(i + 1) * LANES]) for i in range(n)], axis=-1)


def _norm_matmul_kernel(x_ref, g_ref, w_ref, o_ref):
    h = _rms_rows(x_ref[...], g_ref[...])
    o_ref[...] = _bdot(h, w_ref[...])


def norm_matmul(x, g, w, tm):
    m, d = x.shape
    n = w.shape[1]
    return pl.pallas_call(
        _norm_matmul_kernel,
        grid=(m // tm,),
        in_specs=[pl.BlockSpec((tm, d), lambda i: (i, 0)),
                  pl.BlockSpec((1, d), lambda i: (0, 0)),
                  pl.BlockSpec((d, n), lambda i: (0, 0))],
        out_specs=pl.BlockSpec((tm, n), lambda i: (i, 0)),
        out_shape=jax.ShapeDtypeStruct((m, n), f32),
        compiler_params=_params(("parallel",)),
        name="norm_matmul",
    )(x, g.reshape(1, d), w)


def _res_matmul_kernel(*refs):
    x_ref, o_ref = refs[0], refs[-1]
    acc = x_ref[...]
    for a_ref, w_ref in zip(refs[1:-1:2], refs[2:-1:2]):
        acc = acc + _bdot(a_ref[...], w_ref[...])
    o_ref[...] = acc


def res_matmul(x, pairs, tm):
    m, d = x.shape
    in_specs = [pl.BlockSpec((tm, d), lambda i: (i, 0))]
    args = [x]
    for a, w in pairs:
        in_specs += [pl.BlockSpec((tm, a.shape[1]), lambda i: (i, 0)),
                     pl.BlockSpec(w.shape, lambda i: (0, 0))]
        args += [a, w]
    return pl.pallas_call(
        _res_matmul_kernel,
        grid=(m // tm,),
        in_specs=in_specs,
        out_specs=pl.BlockSpec((tm, d), lambda i: (i, 0)),
        out_shape=jax.ShapeDtypeStruct((m, d), f32),
        compiler_params=_params(("parallel",)),
        name="res_matmul",
    )(*args)


def _ffn_kernel(x_ref, g_ref, wi_ref, wo_ref, o_ref, *, d_ff, tf):
    x = x_ref[...]
    h = _rms_rows(x, g_ref[...]).astype(bf16)
    acc = x
    for c in range(d_ff // tf):
        gate = jnp.dot(h, wi_ref[:, c * tf:(c + 1) * tf], preferred_element_type=f32)
        up = jnp.dot(h, wi_ref[:, d_ff + c * tf:d_ff + (c + 1) * tf], preferred_element_type=f32)
        act = gate * _sigmoid(gate) * up
        acc = acc + _bdot(act, wo_ref[c * tf:(c + 1) * tf, :])
    o_ref[...] = acc


def ffn(x, g, w_in, w_out, tm):
    m, d = x.shape
    d_ff = w_out.shape[0]
    tf = 256
    return pl.pallas_call(
        functools.partial(_ffn_kernel, d_ff=d_ff, tf=tf),
        grid=(m // tm,),
        in_specs=[pl.BlockSpec((tm, d), lambda i: (i, 0)),
                  pl.BlockSpec((1, d), lambda i: (0, 0)),
                  pl.BlockSpec(w_in.shape, lambda i: (0, 0)),
                  pl.BlockSpec(w_out.shape, lambda i: (0, 0))],
        out_specs=pl.BlockSpec((tm, d), lambda i: (i, 0)),
        out_shape=jax.ShapeDtypeStruct((m, d), f32),
        compiler_params=_params(("parallel",)),
        name="ffn",
    )(x, g.reshape(1, d), w_in, w_out)


def _mem_kv_kernel(m_ref, g_ref, wk_ref, wv_ref, kg_ref, k_ref, v_ref):
    h = _rms_rows(m_ref[...], g_ref[...]).astype(bf16)
    k = jnp.dot(h, wk_ref[...], preferred_element_type=f32)
    kg = kg_ref[...]
    for hd in range(k.shape[1] // MEM_HEAD):
        sl = slice(hd * MEM_HEAD, (hd + 1) * MEM_HEAD)
        k_ref[:, sl] = _rms_rows(k[:, sl], kg)
    v_ref[...] = jnp.dot(h, wv_ref[...], preferred_element_type=f32)


def mem_kv(mem, g, wk, wv, k_gain, tm):
    m, d = mem.shape
    return pl.pallas_call(
        _mem_kv_kernel,
        grid=(m // tm,),
        in_specs=[pl.BlockSpec((tm, d), lambda i: (i, 0)),
                  pl.BlockSpec((1, d), lambda i: (0, 0)),
                  pl.BlockSpec((d, d), lambda i: (0, 0)),
                  pl.BlockSpec((d, d), lambda i: (0, 0)),
                  pl.BlockSpec((1, MEM_HEAD), lambda i: (0, 0))],
        out_specs=[pl.BlockSpec((tm, d), lambda i: (i, 0))] * 2,
        out_shape=[jax.ShapeDtypeStruct((m, d), f32)] * 2,
        compiler_params=_params(("parallel",)),
        name="mem_kv",
    )(mem, g.reshape(1, d), wk, wv, k_gain.reshape(1, MEM_HEAD))


def _mem_attn_kernel(x_ref, g_ref, wq_ref, qg_ref, k_ref, v_ref, wo_ref, *rest):
    o_ref = rest[-1]
    x = x_ref[...]
    for a_ref, w_ref in zip(rest[:-1:2], rest[1:-1:2]):
        x = x + _bdot(a_ref[...], w_ref[...])
    h = _rms_rows(x, g_ref[...])
    q = _bdot(h, wq_ref[...])
    qg = qg_ref[...] * (MEM_HEAD ** -0.5)
    outs = []
    for hd in range(q.shape[1] // MEM_HEAD):
        sl = slice(hd * MEM_HEAD, (hd + 1) * MEM_HEAD)
        qn = _rms_rows(q[:, sl], qg)
        s = _bdot_nt(qn, k_ref[0, :, sl])
        p = jnp.exp(s - jnp.max(s, axis=-1, keepdims=True))
        l = jnp.sum(p, axis=-1, keepdims=True)
        outs.append(_bdot(p, v_ref[0, :, sl]) / l)
    o = jnp.concatenate(outs, axis=-1)
    o_ref[...] = x + _bdot(o, wo_ref[...])


def mem_attn(x, pairs, g, wq, q_gain, k, v, wo, rows_per_seq, tm):
    m, d = x.shape
    n_mem = k.shape[1]
    per = rows_per_seq // tm
    extra_specs, extra = [], []
    for a, w in pairs:
        extra_specs += [pl.BlockSpec((tm, a.shape[1]), lambda i: (i, 0)), pl.BlockSpec(w.shape, lambda i: (0, 0))]
        extra += [a, w]
    return pl.pallas_call(
        _mem_attn_kernel,
        grid=(m // tm,),
        in_specs=[pl.BlockSpec((tm, d), lambda i: (i, 0)),
                  pl.BlockSpec((1, d), lambda i: (0, 0)),
                  pl.BlockSpec((d, d), lambda i: (0, 0)),
                  pl.BlockSpec((1, MEM_HEAD), lambda i: (0, 0)),
                  pl.BlockSpec((1, n_mem, d), lambda i: (i // per, 0, 0)),
                  pl.BlockSpec((1, n_mem, d), lambda i: (i // per, 0, 0)),
                  pl.BlockSpec((d, d), lambda i: (0, 0))] + extra_specs,
        out_specs=pl.BlockSpec((tm, d), lambda i: (i, 0)),
        out_shape=jax.ShapeDtypeStruct((m, d), f32),
        compiler_params=_params(("parallel",)),
        name="mem_attn",
    )(x, g.reshape(1, d), wq, q_gain.reshape(1, MEM_HEAD), k, v, wo, *extra)


def _gelu(x):
    return 0.5 * x * (1.0 + jnp.tanh(0.7978845608028654 * (x + 0.044715 * x * x * x)))


def _mixer_a_kernel(z_ref, vg_ref, w_ref, b_ref, ya_ref, *, width):
    ge = _gelu(z_ref[...])
    tm = ge.shape[0]
    row = lax.broadcasted_iota(jnp.int32, (A_GROUP, A_GROUP), 0)
    col = lax.broadcasted_iota(jnp.int32, (A_GROUP, A_GROUP), 1)
    for g in range(width // A_GROUP):
        u = ge[:, g * A_GROUP:(g + 1) * A_GROUP]
        v = ge[:, width + g * A_GROUP:width + (g + 1) * A_GROUP]
        va = _rms_rows(v, vg_ref[g:g + 1, :]).astype(bf16)
        w = jnp.where(row >= col, w_ref[g], 0.0).astype(bf16)
        for c in range(tm // A_GROUP):
            rs = slice(c * A_GROUP, (c + 1) * A_GROUP)
            s = jnp.dot(w, va[rs], preferred_element_type=f32) + b_ref[g]
            ya_ref[rs, g * A_GROUP:(g + 1) * A_GROUP] = (u[rs] * s).astype(ya_ref.dtype)


def mixer_a(z, v_gain, w_s, b_s, width, tm):
    m = z.shape[0]
    ng = width // A_GROUP
    b_rows = jnp.broadcast_to(b_s[:, :, None], (ng, A_GROUP, A_GROUP))
    return pl.pallas_call(
        functools.partial(_mixer_a_kernel, width=width),
        grid=(m // tm,),
        in_specs=[pl.BlockSpec((tm, 2 * width), lambda i: (i, 0)),
                  pl.BlockSpec((ng, A_GROUP), lambda i: (0, 0)),
                  pl.BlockSpec((ng, A_GROUP, A_GROUP), lambda i: (0, 0, 0)),
                  pl.BlockSpec((ng, A_GROUP, A_GROUP), lambda i: (0, 0, 0))],
        out_specs=pl.BlockSpec((tm, width), lambda i: (i, 0)),
        out_shape=jax.ShapeDtypeStruct((m, width), bf16),
        compiler_params=_params(("parallel",)),
        name="mixer_a",
    )(z, v_gain, w_s, b_rows)


def _mixer_a_step_kernel(z_ref, vg_ref, w0_ref, b0_ref, ya_ref, va_ref, *, width):
    ge = _gelu(z_ref[...])
    for g in range(width // A_GROUP):
        sl = slice(g * A_GROUP, (g + 1) * A_GROUP)
        u = ge[:, sl]
        v = ge[:, width + g * A_GROUP:width + (g + 1) * A_GROUP]
        va = _rms_rows(v, vg_ref[g:g + 1, :])
        va_ref[:, sl] = va
        ya_ref[:, sl] = (u * (w0_ref[:, sl] * va + b0_ref[:, sl])).astype(ya_ref.dtype)


def mixer_a_step(z, v_gain, w_s, b_s, width):
    m = z.shape[0]
    w0 = jnp.repeat(w_s[:, 0, 0], A_GROUP).reshape(1, width)
    b0 = jnp.repeat(b_s[:, 0], A_GROUP).reshape(1, width)
    return pl.pallas_call(
        functools.partial(_mixer_a_step_kernel, width=width),
        grid=(1,),
        in_specs=[pl.BlockSpec((m, 2 * width), lambda i: (0, 0)),
                  pl.BlockSpec(v_gain.shape, lambda i: (0, 0)),
                  pl.BlockSpec((1, width), lambda i: (0, 0)),
                  pl.BlockSpec((1, width), lambda i: (0, 0))],
        out_specs=[pl.BlockSpec((m, width), lambda i: (0, 0))] * 2,
        out_shape=[jax.ShapeDtypeStruct((m, width), bf16), jax.ShapeDtypeStruct((m, width), f32)],
        compiler_params=_params(("arbitrary",)),
        name="mixer_a_step",
    )(z, v_gain, w0, b0)


def _split3(x):
    hi = x.astype(bf16)
    r1 = x - hi.astype(f32)
    mid = r1.astype(bf16)
    lo = (r1 - mid.astype(f32)).astype(bf16)
    return hi, mid, lo


def _dot01_left(sel, x):
    hi, mid, lo = _split3(x)
    d = lambda p: jnp.dot(sel, p, preferred_element_type=f32)
    return d(hi) + d(mid) + d(lo)


def _dot01_right(x, sel):
    hi, mid, lo = _split3(x)
    d = lambda p: jnp.dot(p, sel, preferred_element_type=f32)
    return d(hi) + d(mid) + d(lo)


def _log_sigmoid(x):
    return jnp.minimum(x, 0.0) - jnp.log(1.0 + jnp.exp(-jnp.abs(x)))


AUG = 2 * LANES
SHIFT_LANE = 12


def _fox_project(x_ref, g_ref, w_ref, wf_ref, bf_ref, qg_ref, kg_ref, width, unit):
    h = _rms_rows(x_ref[...], g_ref[...]).astype(bf16)
    z = jnp.dot(h, w_ref[...], preferred_element_type=f32)
    q, k = z[:, :width], z[:, width:2 * width]
    v, g = z[:, 2 * width:3 * width], z[:, 3 * width:4 * width]
    qn = q * lax.rsqrt(_per_head_sums(q * q) * (1.0 / HEAD) + RMS_EPS) * (qg_ref[...] * (HEAD ** -0.5 * unit))
    kn = k * lax.rsqrt(_per_head_sums(k * k) * (1.0 / HEAD) + RMS_EPS) * kg_ref[...]
    lf = _log_sigmoid(jnp.dot(h, wf_ref[...], preferred_element_type=f32) + bf_ref[...])
    return qn, kn, v, g, lf


def _fox_in_args(x, g, w, wf, b_f, q_gain, k_gain):
    d = x.shape[1]
    width = w.shape[1] // 4
    n_heads = width // HEAD
    wf_pad = jnp.zeros((d, LANES), bf16).at[:, :n_heads].set(wf)
    bf_pad = jnp.zeros((1, LANES), f32).at[0, :n_heads].set(b_f)
    qg = jnp.tile(q_gain, n_heads).reshape(1, width)
    kg = jnp.tile(k_gain, n_heads).reshape(1, width)
    const = lambda i: (0, 0)
    specs = [pl.BlockSpec((1, d), const), pl.BlockSpec(w.shape, const), pl.BlockSpec((d, LANES), const),
             pl.BlockSpec((1, LANES), const), pl.BlockSpec((1, width), const), pl.BlockSpec((1, width), const)]
    return specs, (g.reshape(1, d), w, wf_pad, bf_pad, qg, kg)


def _fox_step_in_kernel(x_ref, g_ref, w_ref, wf_ref, bf_ref, qg_ref, kg_ref,
                        q_ref, k_ref, v_ref, sg_ref, lf_ref, *, width, n_heads):
    qn, kn, v, g, lf = _fox_project(x_ref, g_ref, w_ref, wf_ref, bf_ref, qg_ref, kg_ref, width, 1.0)
    q_ref[...] = qn.astype(bf16).astype(f32)
    k_ref[...] = kn
    v_ref[...] = v
    sg_ref[...] = _sigmoid(g)
    lf_ref[...] = lf[:, :n_heads]


def fox_step_in(x, g, w, wf, b_f, q_gain, k_gain):
    m, d = x.shape
    width = w.shape[1] // 4
    n_heads = width // HEAD
    specs, args = _fox_in_args(x, g, w, wf, b_f, q_gain, k_gain)
    whole = lambda n: pl.BlockSpec((m, n), lambda i: (0, 0))
    return pl.pallas_call(
        functools.partial(_fox_step_in_kernel, width=width, n_heads=n_heads),
        grid=(1,),
        in_specs=[whole(d)] + specs,
        out_specs=[whole(width)] * 4 + [whole(n_heads)],
        out_shape=[jax.ShapeDtypeStruct((m, width), f32)] * 4 + [jax.ShapeDtypeStruct((m, n_heads), f32)],
        compiler_params=_params(("arbitrary",)),
        name="fox_step_in",
    )(x, *args)


def _fox_seq_in_kernel(x_ref, g_ref, w_ref, wf_ref, bf_ref, qg_ref, kg_ref,
                       q_ref, kt_ref, ka_ref, qa_ref, vt_ref, vtb_ref, sg_ref, lft_ref, c_ref, carry_ref,
                       *, tiles_per_seq, width, n_heads):
    i = pl.program_id(0)
    qn, kn, v, g, lf = _fox_project(x_ref, g_ref, w_ref, wf_ref, bf_ref, qg_ref, kg_ref, width, LOG2E)
    tm = lf.shape[0]
    q_ref[...] = qn.astype(bf16)
    sg_ref[...] = _sigmoid(g)
    for blk in range(width // LANES):
        sl = slice(blk * LANES, (blk + 1) * LANES)
        kt_ref[0, sl, :] = kn[:, sl].T
        vt = v[:, sl].T
        vt_ref[0, sl, :] = vt
        vtb_ref[0, 0, sl, :] = vt.astype(bf16)
    lft_ref[0] = lf.T[:n_heads, :]

    @pl.when(i % tiles_per_seq == 0)
    def _():
        carry_ref[...] = jnp.zeros_like(carry_ref)

    row = lax.broadcasted_iota(jnp.int32, (tm, tm), 0)
    col = lax.broadcasted_iota(jnp.int32, (tm, tm), 1)
    tri = jnp.where(row >= col, 1.0, 0.0).astype(bf16)
    c = _dot01_left(tri, lf) + carry_ref[...]
    carry_ref[...] = c[tm - 1:tm, :]
    c = c * LOG2E
    c_ref[...] = c
    pieces = jnp.concatenate(_split3(c), axis=-1)
    rr = lax.broadcasted_iota(jnp.int32, (3 * LANES, LANES), 0)
    cc = lax.broadcasted_iota(jnp.int32, (3 * LANES, LANES), 1)
    piece, head = rr // LANES, rr % LANES
    lane = _lane_iota((1, LANES))
    k_const = jnp.where((lane < 6) | (lane == SHIFT_LANE), 1.0, 0.0)
    q_const = jnp.where((lane >= 6) & (lane < 12), -1.0, 0.0)
    knb = kn.astype(bf16)
    for p in range(n_heads // 2):
        place = lambda base: jnp.where(((head == 2 * p) & (cc == base + piece))
                                       | ((head == 2 * p + 1) & (cc == base + 3 + piece)), 1.0, 0.0).astype(bf16)
        both = jnp.dot(pieces, jnp.concatenate([place(6), place(0)], axis=-1), preferred_element_type=f32)
        k_aug = both[:, :LANES] + k_const
        ka_ref[p] = jnp.concatenate([knb[:, p * LANES:(p + 1) * LANES], k_aug.astype(bf16)], axis=-1)
        qa_ref[p] = (both[:, LANES:] + q_const).astype(bf16)


def fox_seq_in(x, g, w, wf, b_f, q_gain, k_gain, n_seq, tm):
    m, d = x.shape
    width = w.shape[1] // 4
    n_heads = width // HEAD
    n_pairs = n_heads // 2
    seq = m // n_seq
    tps = seq // tm
    specs, args = _fox_in_args(x, g, w, wf, b_f, q_gain, k_gain)
    row = lambda i: (i, 0)
    chan = lambda i: (i // tps, 0, i % tps)
    return pl.pallas_call(
        functools.partial(_fox_seq_in_kernel, tiles_per_seq=tps, width=width, n_heads=n_heads),
        grid=(m // tm,),
        in_specs=[pl.BlockSpec((tm, d), row)] + specs,
        out_specs=[pl.BlockSpec((tm, width), row),
                   pl.BlockSpec((1, width, tm), chan),
                   pl.BlockSpec((n_pairs, tm, AUG), lambda i: (0, i, 0)),
                   pl.BlockSpec((n_pairs, tm, LANES), lambda i: (0, i, 0)),
                   pl.BlockSpec((1, width, tm), chan),
                   pl.BlockSpec((1, 1, width, tm), lambda i: (i // tps, i % tps, 0, 0)),
                   pl.BlockSpec((tm, width), row),
                   pl.BlockSpec((1, n_heads, tm), chan),
                   pl.BlockSpec((tm, LANES), row)],
        out_shape=[jax.ShapeDtypeStruct((m, width), bf16),
                   jax.ShapeDtypeStruct((n_seq, width, seq), f32),
                   jax.ShapeDtypeStruct((n_pairs, m, AUG), bf16),
                   jax.ShapeDtypeStruct((n_pairs, m, LANES), bf16),
                   jax.ShapeDtypeStruct((n_seq, width, seq), f32),
                   jax.ShapeDtypeStruct((n_seq, tps, width, tm), bf16),
                   jax.ShapeDtypeStruct((m, width), f32),
                   jax.ShapeDtypeStruct((n_seq, n_heads, seq), f32),
                   jax.ShapeDtypeStruct((m, LANES), f32)],
        scratch_shapes=[pltpu.VMEM((1, LANES), f32)],
        compiler_params=_params(("arbitrary",)),
        name="fox_seq_in",
    )(x, *args)


def _fox_flash_kernel(lo_ref, qk_ref, q_ref, qa_ref, ka_ref, vt_ref, sg_ref, o_ref, *, tq, tv, bounded):
    b, p, qi = pl.program_id(0), pl.program_id(1), pl.program_id(2)
    nsub = tq // tv
    q2 = q_ref[0]
    lo = _lane_iota((tq, LANES)) < HEAD
    zero = jnp.zeros_like(q2)
    bias2 = qa_ref[0]
    lane = _lane_iota((tq, LANES))
    shift = jnp.where(lane == SHIFT_LANE, -qk_ref[0], 0.0).astype(bf16)
    qa = []
    for j in (0, 1):
        own = ((lane >= 3 * j) & (lane < 3 * j + 3)) | ((lane >= 6 + 3 * j) & (lane < 9 + 3 * j))
        bias = jnp.where(own, bias2, zero)
        if bounded:
            bias = jnp.where(lane == SHIFT_LANE, shift, bias)
        qh = jnp.where(lo, q2, zero) if j == 0 else jnp.where(lo, zero, q2)
        qa.append(jnp.concatenate([qh, bias], axis=-1))
    row = lax.broadcasted_iota(jnp.int32, (tq, tq), 0)
    col = lax.broadcasted_iota(jnp.int32, (tq, tq), 1)

    def scores(kb, j):
        ka = ka_ref[0, 0, pl.ds(pl.multiple_of(kb * tq, tq), tq), :]
        return _bdot_nt(ka, qa[j])

    def softmax_pv(kb, j, t, stats, masked):
        m_old, l_old, acc = stats
        if masked:
            t = jnp.where(row <= col, t, -jnp.inf)
        m_new = jnp.maximum(m_old, jnp.max(t, axis=0, keepdims=True))
        pe = jnp.exp2(t - m_new).astype(bf16)
        alpha = jnp.exp2(m_old - m_new)
        l_new = alpha * l_old + jnp.sum(pe.astype(f32), axis=0, keepdims=True)
        pv = sum(jnp.dot(vt_ref[0, kb * nsub + s, j * HEAD:(j + 1) * HEAD, :], pe[s * tv:(s + 1) * tv],
                         preferred_element_type=f32) for s in range(nsub))
        return m_new, l_new, acc * alpha + pv

    def block(kb, t0, st0, st1, masked, last):
        t1 = scores(kb, 1)
        st0 = softmax_pv(kb, 0, t0, st0, masked)
        t0_next = t0 if last else scores(kb + 1, 0)
        st1 = softmax_pv(kb, 1, t1, st1, masked)
        return t0_next, st0, st1

    def pv(kb, j, pe):
        return sum(jnp.dot(vt_ref[0, kb * nsub + s, j * HEAD:(j + 1) * HEAD, :], pe[s * tv:(s + 1) * tv],
                   preferred_element_type=f32) for s in range(nsub))

    def free_blocks(kbs, sts):
        units = [(kb, j) for kb in kbs for j in (0, 1)]
        l = [sts[0][1], sts[1][1]]
        acc = [sts[0][2], sts[1][2]]
        t_next = scores(*units[0])
        for n, (kb, j) in enumerate(units):
            t = t_next
            if n + 1 < len(units):
                t_next = scores(*units[n + 1])
            pe = jnp.exp2(jnp.where((row <= col) | (kb < qi), t, -jnp.inf))
            l[j] = l[j] + jnp.sum(pe, axis=0, keepdims=True)
            acc[j] = acc[j] + pv(kb, j, pe.astype(bf16))
        return (sts[0][0], l[0], acc[0]), (sts[1][0], l[1], acc[1])

    init = (jnp.full((1, tq), -jnp.inf, f32), jnp.zeros((1, tq), f32), jnp.zeros((HEAD, tq), f32))
    first = lo_ref[b, p, qi]
    if bounded:
        odd = (qi - first + 1) % 2
        sts = lax.cond(odd == 1, lambda s: free_blocks((first,), s), lambda s: s, (init, init))
        (m0, l0, a0), (m1, l1, a1) = lax.fori_loop(
            0, (qi - first + 1) // 2,
            lambda i, s: free_blocks((first + odd + 2 * i, first + odd + 2 * i + 1), s), sts)
    else:
        state = lax.fori_loop(first, qi, lambda kb, s: block(kb, *s, False, False),
                              (scores(first, 0), init, init))
        _, (m0, l0, a0), (m1, l1, a1) = block(qi, *state, True, True)
    o = jnp.concatenate([a0 / l0, a1 / l1], axis=0).T
    o_ref[0] = (o * sg_ref[0]).astype(o_ref.dtype)


QK_BOUNDED_MAX = 40.0


def fox_flash(q, qa, ka, vtb, sg, first_block, qk, n_seq, tq):
    m, width = q.shape
    seq = m // n_seq
    n_pairs = width // LANES
    tv = vtb.shape[3]
    r3 = lambda a: a.reshape(n_seq, seq, a.shape[1])
    tile = pl.BlockSpec((1, tq, LANES), lambda b, p, i, lo, qk_: (b, i, p))
    grid_spec = pltpu.PrefetchScalarGridSpec(
        num_scalar_prefetch=2,
        grid=(n_seq, n_pairs, seq // tq),
        in_specs=[tile,
                  pl.BlockSpec((1, tq, LANES), lambda b, p, i, lo, qk_: (p, b * (seq // tq) + i, 0)),
                  pl.BlockSpec((1, 1, seq, AUG), lambda b, p, i, lo, qk_: (p, b, 0, 0)),
                  pl.BlockSpec((1, seq // tv, LANES, tv), lambda b, p, i, lo, qk_: (b, 0, p, 0)),
                  tile],
        out_specs=tile,
    )
    args = (first_block, qk.reshape(1).astype(f32), r3(q), qa, ka.reshape(n_pairs, n_seq, seq, AUG), vtb,
            r3(sg))
    call = lambda bounded: pl.pallas_call(
        functools.partial(_fox_flash_kernel, tq=tq, tv=tv, bounded=bounded),
        grid_spec=grid_spec,
        out_shape=jax.ShapeDtypeStruct((n_seq, seq, width), bf16),
        compiler_params=_params(("parallel", "parallel", "arbitrary")),
        name="fox_flash_bounded" if bounded else "fox_flash",
    )
    out = lax.cond(qk <= QK_BOUNDED_MAX, lambda a: call(True)(*a), lambda a: call(False)(*a), args)
    return out.reshape(m, width)


FOX_PAGES_PER_STEP = 16
FOX_VALUE_BUFFERS = 4
EXP_UNDERFLOW = 104.0


def _fox_scores_kernel(pt_ref, qc_ref, knt_ref, lfn_ref, *rest, n_heads, pps):
    kt_refs, lf_refs = rest[:pps], rest[pps:2 * pps]
    t_ref, town_ref, m_out, l_out, m_ref, l_ref, carry_ref = rest[2 * pps:]
    pg = pl.program_id(1)
    page = lf_refs[0].shape[2]

    def score(kt_ref, bias):
        rows = [jnp.sum(kt_ref[0, h] * qc_ref[0, h], axis=0, keepdims=True) for h in range(n_heads)]
        t = jnp.concatenate(rows, axis=0) + bias
        m_old = m_ref[...]
        m_new = jnp.maximum(m_old, jnp.max(t, axis=-1, keepdims=True))
        l_ref[...] = jnp.exp(m_old - m_new) * l_ref[...] + jnp.sum(jnp.exp(t - m_new), axis=-1, keepdims=True)
        m_ref[...] = m_new
        return t

    @pl.when(pg == 0)
    def _():
        m_ref[...] = jnp.full(m_ref.shape, -jnp.inf, f32)
        l_ref[...] = jnp.zeros_like(l_ref)
        carry_ref[...] = lfn_ref[0]
        town_ref[0] = score(knt_ref, jnp.where(_lane_iota((n_heads, page)) == 0, 0.0, -jnp.inf))

    r = lax.broadcasted_iota(jnp.int32, (page, page), 0)
    cidx = lax.broadcasted_iota(jnp.int32, (page, page), 1)
    later = jnp.where(r > cidx, 1.0, 0.0).astype(bf16)
    for j in range(pps):
        lf = lf_refs[j][0]
        t_ref[0, j] = score(kt_refs[j], _dot01_right(lf, later) + carry_ref[...])
        carry_ref[...] = carry_ref[...] + jnp.sum(lf, axis=-1, keepdims=True)

    @pl.when(pg == pl.num_programs(1) - 1)
    def _():
        m_out[0] = m_ref[...]
        l_out[0] = l_ref[...]


def _fox_values_kernel(pt_ref, idx_ref, cnt_ref, town_ref, vnt_ref, t_ref, m_ref, l_ref, sg_ref, vt_hbm,
                       o_ref, acc_ref, vbuf, sem, *, n_heads, n_pages):
    i = pl.program_id(0)
    n_live = cnt_ref[i]
    m = m_ref[0]

    def page_copy(u, buf):
        page = pt_ref[i, n_pages - 1 - idx_ref[i, u]]
        return pltpu.make_async_copy(vt_hbm.at[page], vbuf.at[buf], sem.at[buf])

    n_buf = vbuf.shape[0]
    for u0 in range(n_buf):
        @pl.when(u0 < n_live)
        def _():
            page_copy(u0, u0).start()

    pe = jnp.exp(town_ref[0] - m)
    for h in range(n_heads):
        acc_ref[h] = pe[h:h + 1, :] * vnt_ref[0, h]

    def live_page(u, _):
        buf = u % n_buf
        page_copy(u, buf).wait()
        pe = jnp.exp(t_ref[0, idx_ref[i, u]] - m)
        for h in range(n_heads):
            acc_ref[h] = acc_ref[h] + pe[h:h + 1, :] * vbuf[buf, h]

        @pl.when(u + n_buf < n_live)
        def _():
            page_copy(u + n_buf, buf).start()
        return 0

    lax.fori_loop(0, n_live, live_page, 0)
    l = l_ref[0]
    for h in range(n_heads):
        o_ref[0, h] = jnp.sum(acc_ref[h], axis=-1, keepdims=True) / l[h:h + 1, :] * sg_ref[0, h]


def fox_step(q, k_new, v_new, lf_new, sg, cache_k, cache_v, cache_lf, page_table):
    b, width = q.shape
    pool, page, n_heads = cache_lf.shape
    n_pages = page_table.shape[1]
    pps = math.gcd(FOX_PAGES_PER_STEP, n_pages)
    kt = jnp.transpose(cache_k, (0, 2, 3, 1))
    vt = jnp.transpose(cache_v, (0, 2, 3, 1))
    lft = jnp.transpose(cache_lf, (0, 2, 1))
    col = lambda a: a.reshape(b, n_heads, HEAD, 1)
    own_page = lambda a: jnp.zeros((b, n_heads, HEAD, page), f32).at[..., 0].set(a.reshape(b, n_heads, HEAD))
    qc = jnp.broadcast_to(col(q), (b, n_heads, HEAD, page))
    per_seq = lambda shape: pl.BlockSpec((1,) + shape, lambda i, g, *_: (i,) + (0,) * len(shape))
    page_spec = lambda shape, j: pl.BlockSpec(
        (1,) + shape, lambda i, g, pt: (pt[i, n_pages - 1 - (g * pps + j)],) + (0,) * len(shape))
    stat = jax.ShapeDtypeStruct((b, n_heads, 1), f32)
    t_all, t_own, m, l = pl.pallas_call(
        functools.partial(_fox_scores_kernel, n_heads=n_heads, pps=pps),
        grid_spec=pltpu.PrefetchScalarGridSpec(
            num_scalar_prefetch=1,
            grid=(b, n_pages // pps),
            in_specs=[per_seq((n_heads, HEAD, page))] * 2 + [per_seq((n_heads, 1))]
                     + [page_spec((n_heads, HEAD, page), j) for j in range(pps)]
                     + [page_spec((n_heads, page), j) for j in range(pps)],
            out_specs=[pl.BlockSpec((1, pps, n_heads, page), lambda i, g, pt: (i, g, 0, 0)),
                       per_seq((n_heads, page)), per_seq((n_heads, 1)), per_seq((n_heads, 1))],
            scratch_shapes=[pltpu.VMEM((n_heads, 1), f32)] * 3),
        out_shape=[jax.ShapeDtypeStruct((b, n_pages, n_heads, page), f32),
                   jax.ShapeDtypeStruct((b, n_heads, page), f32), stat, stat],
        compiler_params=_params(("parallel", "arbitrary")),
        name="fox_step_scores",
    )(page_table, qc, own_page(k_new), lf_new.reshape(b, n_heads, 1), *([kt] * pps), *([lft] * pps))

    live = jnp.any(jnp.max(t_all, axis=-1) - m[:, None, :, 0] > -EXP_UNDERFLOW, axis=-1)
    live_idx = jnp.argsort(jnp.logical_not(live), axis=1, stable=True).astype(jnp.int32)
    count = jnp.sum(live, axis=1).astype(jnp.int32)

    one = lambda shape: pl.BlockSpec((1,) + shape, lambda i, *_: (i,) + (0,) * len(shape))
    out = pl.pallas_call(
        functools.partial(_fox_values_kernel, n_heads=n_heads, n_pages=n_pages),
        grid_spec=pltpu.PrefetchScalarGridSpec(
            num_scalar_prefetch=3,
            grid=(b,),
            in_specs=[one((n_heads, page)), one((n_heads, HEAD, page)), one((n_pages, n_heads, page)),
                      one((n_heads, 1)), one((n_heads, 1)), one((n_heads, HEAD, 1)),
                      pl.BlockSpec(memory_space=pl.ANY)],
            out_specs=one((n_heads, HEAD, 1)),
            scratch_shapes=[pltpu.VMEM((n_heads, HEAD, page), f32),
                            pltpu.VMEM((FOX_VALUE_BUFFERS, n_heads, HEAD, page), f32),
                            pltpu.SemaphoreType.DMA((FOX_VALUE_BUFFERS,))]),
        out_shape=jax.ShapeDtypeStruct((b, n_heads, HEAD, 1), f32),
        compiler_params=_params(("arbitrary",)),
        name="fox_step_values",
    )(page_table, live_idx, count, t_own, own_page(v_new), t_all, m, l, col(sg), vt)
    return out.reshape(b, width).astype(bf16)


def _mem_attn_step_kernel(q_ref, qg_ref, k_ref, v_ref, o_ref):
    q = q_ref[0]
    qn = q * lax.rsqrt(jnp.mean(q * q, axis=-1, keepdims=True) + RMS_EPS) * (qg_ref[...] * MEM_HEAD ** -0.5)
    s = jnp.sum(k_ref[0, 0] * qn, axis=-1, keepdims=True)
    p = jnp.exp(s - jnp.max(s, axis=0, keepdims=True))
    l = jnp.sum(p, axis=0, keepdims=True)
    o_ref[0] = jnp.sum(p * v_ref[0, 0], axis=0, keepdims=True) / l


def mem_attn_step(q, q_gain, cache_k, cache_v, layer):
    b, d = q.shape
    _, _, n_mem, heads, hd = cache_k.shape
    mem_spec = pl.BlockSpec((1, 1, n_mem, heads, hd), lambda i: (layer, i, 0, 0, 0))
    out = pl.pallas_call(
        _mem_attn_step_kernel,
        grid=(b,),
        in_specs=[pl.BlockSpec((1, 1, heads, hd), lambda i: (i, 0, 0, 0)),
                  pl.BlockSpec((1, 1, hd), lambda i: (0, 0, 0)),
                  mem_spec, mem_spec],
        out_specs=pl.BlockSpec((1, 1, heads, hd), lambda i: (i, 0, 0, 0)),
        out_shape=jax.ShapeDtypeStruct((b, 1, heads, hd), f32),
        compiler_params=_params(("parallel",)),
        name="mem_attn_step",
    )(q.reshape(b, 1, heads, hd), q_gain.reshape(1, 1, hd), cache_k, cache_v)
    return out.reshape(b, d)


def _softplus(x):
    return jnp.maximum(x, 0.0) + jnp.log(1.0 + jnp.exp(-jnp.abs(x)))


def _rwkv_kernel(zr_ref, zk_ref, zv_ref, zg_ref, zwa_ref, sh0_ref, s0_ref,
                 mu_ref, w0_ref, a0_ref, kk_ref, ka_ref, rk_ref, lnw_ref, lnb_ref, wup_ref, aup_ref,
                 y_ref, sout_ref, shout_ref,
                 s_ref, carry_ref, r_s, k_s, v_s, al_s, be_s, ld_s, yo_s, bo_s, sg_s,
                 *, chunk, n_valid, width, padded):
    i = pl.program_id(1)
    nblk = pl.num_programs(1)
    nsq, cg = zr_ref.shape[0], zr_ref.shape[1]
    n_pairs = width // LANES

    @pl.when(i == 0)
    def _():
        s_ref[...] = s0_ref[...]
        carry_ref[...] = sh0_ref[...]

    rows = lax.broadcasted_iota(jnp.int32, (cg, 1), 0)
    valid = (rows + i * cg) < n_valid
    last = jnp.minimum(n_valid - 1 - i * cg, cg - 1)

    for s in range(nsq):
        def shifted(z, lane0, n):
            prev = jnp.where(rows == 0, carry_ref[s, :, lane0:lane0 + n], pltpu.roll(z, 1, 0))
            return z + (prev - z) * mu_ref[:, lane0:lane0 + n]

        zr, zk, zv, zg, zwa = zr_ref[s], zk_ref[s], zv_ref[s], zg_ref[s], zwa_ref[s]
        r = shifted(zr, 0, width)
        k = shifted(zk, width, width)
        v = shifted(zv, 2 * width, width)
        gp = shifted(zg, 3 * width, width)
        wa = shifted(zwa, 4 * width, LANES)

        pieces = ((zr, 0, width), (zk, width, width), (zv, 2 * width, width), (zg, 3 * width, width),
                  (zwa, 4 * width, LANES))
        if padded:
            @pl.when(last >= 0)
            def _():
                for z, lane0, n in pieces:
                    carry_ref[s, :, lane0:lane0 + n] = jnp.sum(jnp.where(rows == last, z, 0.0), axis=0,
                                                               keepdims=True)
        else:
            for z, lane0, n in pieces:
                carry_ref[s, :, lane0:lane0 + n] = z[cg - 1:cg, :]

        w = -_softplus(-(w0_ref[...] + _bdot(jnp.tanh(wa), wup_ref[...]))) - 0.5
        logd = -jnp.exp(w)
        a = _sigmoid(a0_ref[...] + _bdot(wa, aup_ref[...]))
        kk = k * kk_ref[...]
        kk = kk * lax.rsqrt(_per_head_sums(kk * kk) + L2_EPS)
        k2 = k * (1.0 + (a - 1.0) * ka_ref[...])
        bo_s[s] = _per_head_sums(r * k2 * rk_ref[...]) * v
        sg_s[s] = _sigmoid(gp)
        r_s[s] = r
        keep = (lambda t: jnp.where(valid, t, 0.0)) if padded else (lambda t: t)
        k_s[s] = keep(k2)
        v_s[s] = keep(v)
        al_s[s] = -kk
        be_s[s] = keep(kk * a)
        ld_s[s] = keep(logd)

    ri = lax.broadcasted_iota(jnp.int32, (chunk, chunk), 0)
    ci = lax.broadcasted_iota(jnp.int32, (chunk, chunk), 1)
    tri = jnp.where(ri >= ci, 1.0, 0.0).astype(bf16)
    lower, strict = ri >= ci, ri > ci
    eye = jnp.where(ri == ci, 1.0, 0.0)
    lo = _lane_iota((1, LANES)) < HEAD
    sr = lax.broadcasted_iota(jnp.int32, (LANES, LANES), 0) < HEAD
    sc = lax.broadcasted_iota(jnp.int32, (LANES, LANES), 1) < HEAD
    same_head = sr == sc
    n_sq = max(int(math.log2(chunk)) - 1, 0)

    def chunk_body(c, _):
        rs = pl.ds(pl.multiple_of(c * chunk, chunk), chunk)
        pairs = [(s, p) for s in range(nsq) for p in range(n_pairs)]
        heads = [(s, p, j) for s, p in pairs for j in (0, 1)]
        lsl = [slice(p * LANES, (p + 1) * LANES) for p in range(n_pairs)]
        xs, vv, vp, yb, yk, ybe, yke, d_end = {}, {}, {}, {}, {}, {}, {}, {}
        for s in range(nsq):
            ld = ld_s[s, rs, :]
            lcum = _dot01_left(tri, ld)
            lend = lcum[chunk - 1:chunk, :]
            e_neg, e_end = jnp.exp(-lcum), jnp.exp(lend - lcum)
            xa_all, xr_all = al_s[s, rs, :] * jnp.exp(lcum - ld), r_s[s, rs, :] * jnp.exp(lcum)
            be, kc, v_all = be_s[s, rs, :], k_s[s, rs, :], v_s[s, rs, :]
            dec = jnp.exp(lend)
            for p in range(n_pairs):
                xs[s, p] = jnp.concatenate([xa_all[:, lsl[p]], xr_all[:, lsl[p]]], axis=0)
                vv[s, p] = v_all[:, lsl[p]]
                vp[s, p] = vv[s, p].astype(bf16)
                yb[s, p], yk[s, p] = (be * e_neg)[:, lsl[p]], (kc * e_neg)[:, lsl[p]]
                ybe[s, p], yke[s, p] = (be * e_end)[:, lsl[p]], (kc * e_end)[:, lsl[p]]
                d_end[s, p] = dec[:, lsl[p]]
        s_bd = {sp: s_ref[sp[0], sp[1]] for sp in pairs}
        a_ab, a_ak, a_rb, a_rk = {}, {}, {}, {}
        for s, p, j in heads:
            xm = jnp.where(lo if j == 0 else jnp.logical_not(lo), xs[s, p], 0.0).astype(bf16)
            gb = _bdot_nt(xm, yb[s, p])
            gk = _bdot_nt(xm, yk[s, p])
            a_ab[s, p, j] = jnp.where(strict, gb[:chunk], 0.0)
            a_ak[s, p, j] = jnp.where(strict, gk[:chunk], 0.0)
            a_rb[s, p, j] = jnp.where(lower, gb[chunk:], 0.0)
            a_rk[s, p, j] = jnp.where(lower, gk[chunk:], 0.0)
        tinv = {h: eye + a_ab[h] for h in heads}
        pw = dict(a_ab)
        for _ in range(n_sq):
            for h in heads:
                pw[h] = _bdot(pw[h], pw[h])
            for h in heads:
                tinv[h] = tinv[h] + _bdot(tinv[h], pw[h])
        xh = {sp: _bdot_nt(xs[sp], s_bd[sp]) for sp in pairs}
        av = {(s, p, j): _bdot(a_ak[s, p, j], vp[s, p]) for s, p, j in heads}
        uh = {(s, p, j): _bdot(tinv[s, p, j], xh[s, p][:chunk] + av[s, p, j]) for s, p, j in heads}
        u = {(s, p): jnp.where(lo, uh[s, p, 0], uh[s, p, 1]) for s, p in pairs}
        yh = {(s, p, j): _bdot(a_rb[s, p, j], u[s, p]) + _bdot(a_rk[s, p, j], vp[s, p]) for s, p, j in heads}
        for s, p in pairs:
            yo_s[s, rs, lsl[p]] = xh[s, p][chunk:] + jnp.where(lo, yh[s, p, 0], yh[s, p, 1])
            uv_t = jnp.concatenate([u[s, p], vv[s, p]], axis=0).T
            upd = _bdot(uv_t, jnp.concatenate([ybe[s, p], yke[s, p]], axis=0))
            s_ref[s, p] = s_bd[s, p] * d_end[s, p] + jnp.where(same_head, upd, 0.0)
        return 0

    lax.fori_loop(0, cg // chunk, chunk_body, 0)

    for s in range(nsq):
        y = yo_s[s]
        mean = _per_head_sums(y) * (1.0 / HEAD)
        yc = y - mean
        var = _per_head_sums(yc * yc) * (1.0 / HEAD)
        y = yc * lax.rsqrt(var + GN_EPS) * lnw_ref[...] + lnb_ref[...]
        y_ref[s] = ((y + bo_s[s]) * sg_s[s]).astype(y_ref.dtype)

    @pl.when(i == nblk - 1)
    def _():
        sout_ref[...] = s_ref[...]
        shout_ref[...] = carry_ref[...]


def rwkv7(z, col0, shift0, s0, prm, n_seq, n_valid, chunk, block_rows):
    m = z.shape[0]
    rows = m // n_seq
    nblk = rows // block_rows
    width = prm["b_w0"].shape[0]
    n_heads = width // HEAD
    n_pairs = width // LANES
    bcols = 4 * width + LANES
    s0p = s0.reshape(n_seq, n_pairs, 2, HEAD, HEAD)
    zeros = jnp.zeros_like(s0p[:, :, 0])
    s0bd = jnp.concatenate([jnp.concatenate([s0p[:, :, 0], zeros], axis=-1),
                            jnp.concatenate([zeros, s0p[:, :, 1]], axis=-1)], axis=-2)
    lora = lambda w_, off: jnp.zeros((LANES, width), bf16).at[off:off + w_.shape[0]].set(w_.astype(bf16))
    row1 = lambda a: a.reshape(1, -1).astype(f32)
    cb = col0 // width
    nsq = math.gcd(RWKV_SEQS_PER_STEP, n_seq)
    z3 = z.reshape(n_seq, rows, z.shape[1])
    zspec = lambda j: pl.BlockSpec((nsq, block_rows, width), lambda g, i: (g, i, cb + j))
    const = lambda shape: pl.BlockSpec(shape, lambda g, i: (0,) * len(shape))
    per_seq = lambda shape: pl.BlockSpec((nsq,) + shape, lambda g, i: (g,) + (0,) * len(shape))
    y, sbd, shift = pl.pallas_call(
        functools.partial(_rwkv_kernel, chunk=chunk, n_valid=n_valid, width=width, padded=n_valid < rows),
        grid=(n_seq // nsq, nblk),
        in_specs=[zspec(0), zspec(1), zspec(2), zspec(3),
                  pl.BlockSpec((nsq, block_rows, LANES), lambda g, i: (g, i, (col0 + 4 * width) // LANES)),
                  per_seq((1, bcols)), per_seq((n_pairs, LANES, LANES)),
                  const((1, bcols))] + [const((1, width))] * 7 + [const((LANES, width))] * 2,
        out_specs=[pl.BlockSpec((nsq, block_rows, width), lambda g, i: (g, i, 0)),
                   per_seq((n_pairs, LANES, LANES)), per_seq((1, bcols))],
        out_shape=[jax.ShapeDtypeStruct((n_seq, rows, width), bf16),
                   jax.ShapeDtypeStruct((n_seq, n_pairs, LANES, LANES), f32),
                   jax.ShapeDtypeStruct((n_seq, 1, bcols), f32)],
        scratch_shapes=[pltpu.VMEM((nsq, n_pairs, LANES, LANES), f32), pltpu.VMEM((nsq, 1, bcols), f32)]
                       + [pltpu.VMEM((nsq, block_rows, width), f32)] * 9,
        compiler_params=_params(("parallel", "arbitrary")),
        name="rwkv7",
    )(z3, z3, z3, z3, z3, shift0.reshape(n_seq, 1, bcols), s0bd,
      row1(prm["b_mu"]), row1(prm["b_w0"]), row1(prm["b_a0"]), row1(prm["b_k_k"]), row1(prm["b_k_a"]),
      row1(prm["b_r_k"]), row1(prm["b_ln_w"]), row1(prm["b_ln_b"]),
      lora(prm["b_w_up"], 0), lora(prm["b_a_up"], HEAD))
    sp = sbd.reshape(n_seq, n_pairs, 2, HEAD, 2, HEAD)
    s_fin = jnp.stack([sp[:, :, 0, :, 0, :], sp[:, :, 1, :, 1, :]], axis=2).reshape(n_seq, n_heads, HEAD, HEAD)
    return y.reshape(m, width), s_fin, shift.reshape(n_seq, bcols)


A_WIDTH = 512
STEP_ROWS = 64
RWKV_CHUNK = 64
RWKV_BLOCK = 128
RWKV_SEQS_PER_STEP = 4
FLASH_TILE = 512
DENSE_TILE = 512
FOX_IN_TILE = 256


UNDERFLOW_LOG2 = 160.0


def _qk_bound(q_gain, k_gain):
    return 1.02 * HEAD ** 0.5 * LOG2E * jnp.max(jnp.abs(q_gain)) * jnp.max(jnp.abs(k_gain))


def _first_key_block(c, qk, n_seq, n_heads, tile):
    c3 =c.reshape(n_seq, -1, c.shape[1])[:, :, :n_heads]
    c_first = c3[:, ::tile]
    c_last = c3[:, tile - 1::tile]
    n = c_first.shape[1]
    dead = (2.0 * qk + c_first[:, :, None, :] - c_last[:, None, :, :]) <= -UNDERFLOW_LOG2
    dead = dead & (jnp.arange(n)[None, None, :, None] < jnp.arange(n)[None, :, None, None])
    dead = dead.reshape(n_seq, n, n, n_heads // 2, 2).all(axis=-1)
    return jnp.transpose(jnp.sum(dead, axis=2), (0, 2, 1)).astype(jnp.int32)


def _trunk(x3, mem_k, mem_v, rwkv_s0, rwkv_shift0, fox_past, p):
    n_seq, seq, d = x3.shape
    m = n_seq * seq
    x = x3.reshape(m, d)
    step = seq == 1
    tm = m if step else DENSE_TILE
    depth = p["norm_mix"].shape[0]
    chunk_v, s_out, shift_out, fk, fv, fl = [], [], [], [], [], []
    ie = io = 0
    for l in range(depth):
        if l % 2 == 0:
            z = norm_matmul(x, p["norm_mix"][l], p["ab_w_in"][ie], tm)
            prm = {k_: p[k_][ie] for k_ in ("b_mu", "b_w0", "b_w_up", "b_a0", "b_a_up", "b_k_k", "b_k_a",
                                           "b_r_k", "b_ln_w", "b_ln_b")}
            if step:
                ya, va = mixer_a_step(z, p["a_v_norm"][ie], p["a_w_s"][ie], p["a_b_s"][ie], A_WIDTH)
                zp = jnp.zeros((n_seq, STEP_ROWS, z.shape[1]), f32).at[:, 0].set(z)
                yb, s_new, sh_new = rwkv7(zp.reshape(n_seq * STEP_ROWS, -1), 2 * A_WIDTH, rwkv_shift0[ie],
                                          rwkv_s0[ie], prm, n_seq, 1, RWKV_CHUNK, STEP_ROWS)
                yb = yb.reshape(n_seq, STEP_ROWS, -1)[:, 0]
                chunk_v.append(va.reshape(n_seq, seq, A_WIDTH))
            else:
                ya = mixer_a(z, p["a_v_norm"][ie], p["a_w_s"][ie], p["a_b_s"][ie], A_WIDTH, tm)
                yb, s_new, sh_new = rwkv7(z, 2 * A_WIDTH, rwkv_shift0[ie], rwkv_s0[ie], prm, n_seq, seq,
                                          RWKV_CHUNK, RWKV_BLOCK)
            w_out = p["ab_w_out"][ie]
            mixed = [(ya, w_out[:A_WIDTH]), (yb, w_out[A_WIDTH:])]
            s_out.append(s_new)
            shift_out.append(sh_new)
            ie += 1
        else:
            w_in = p["c_w_in"][io]
            width = (w_in.shape[1] // 4) // HEAD * HEAD
            n_heads = width // HEAD
            front = (x, p["norm_mix"][l], w_in[:, :4 * width], w_in[:, 4 * width:], p["c_b_f"][io],
                     p["c_q_norm"][io], p["c_k_norm"][io])
            if step:
                q, k, v, sg, lf = fox_step_in(*front)
                ck, cv, cl, pt = fox_past
                og = fox_step(q, k, v, lf, sg, ck[io], cv[io], cl[io], pt)
                fk.append(k.reshape(n_seq, seq, n_heads, HEAD))
                fv.append(v.reshape(n_seq, seq, n_heads, HEAD))
                fl.append(lf.reshape(n_seq, seq, n_heads))
            else:
                q, kt, ka, qa, vt, vtb, sg, lft, c = fox_seq_in(*front, n_seq, FOX_IN_TILE)
                qk = _qk_bound(p["c_q_norm"][io], p["c_k_norm"][io])
                first = _first_key_block(c, qk, n_seq, n_heads, FLASH_TILE)
                og = fox_flash(q, qa, ka, vtb, sg, first, qk, n_seq, FLASH_TILE)
                tokens_first = lambda t: jnp.transpose(t.reshape(n_seq, n_heads, HEAD, seq), (0, 3, 1, 2))
                fk.append(tokens_first(kt))
                fv.append(tokens_first(vt))
                fl.append(jnp.transpose(lft, (0, 2, 1)))
            mixed = [(og, p["c_w_out"][io])]
            io += 1
        if step:
            x = res_matmul(x, mixed, tm)
            q_raw = norm_matmul(x, p["norm_mem"][l], p["m_wq"][l], tm)
            o = mem_attn_step(q_raw, p["m_q_norm"][l], mem_k, mem_v, l)
            x = res_matmul(x, [(o, p["m_wo"][l])], tm)
        else:
            x = mem_attn(x, mixed, p["norm_mem"][l], p["m_wq"][l], p["m_q_norm"][l], mem_k[l].astype(bf16),
                         mem_v[l].astype(bf16), p["m_wo"][l], seq, tm)
        x = ffn(x, p["norm_ffn"][l], p["f_w_in"][l], p["f_w_out"][l], tm)
    stack = lambda xs: jnp.stack(xs)
    return (x.reshape(n_seq, seq, d), (stack(chunk_v) if chunk_v else None), stack(s_out), stack(shift_out),
            stack(fk), stack(fv), stack(fl))


def kernel(x_prompt, x_sample, mem_prompt, cache_mem_k, cache_mem_v, state_rwkv_S, state_rwkv_shift, cache_fox_k, cache_fox_v, cache_fox_logf, page_table, norm_mix, norm_mem, norm_ffn, ab_w_in, ab_w_out, a_v_norm, a_w_s, a_b_s, b_mu, b_w0, b_w_up, b_a0, b_a_up, b_k_k, b_k_a, b_r_k, b_ln_w, b_ln_b, c_w_in, c_b_f, c_q_norm, c_k_norm, c_w_out, m_mem_norm, m_wq, m_wk, m_wv, m_q_norm, m_k_norm, m_wo, f_w_in, f_w_out):
    cast = lambda w: w.astype(bf16)
    p = dict(norm_mix=norm_mix, norm_mem=norm_mem, norm_ffn=norm_ffn, ab_w_in=cast(ab_w_in),
             ab_w_out=cast(ab_w_out), a_v_norm=a_v_norm, a_w_s=a_w_s, a_b_s=a_b_s, b_mu=b_mu, b_w0=b_w0,
             b_w_up=b_w_up, b_a0=b_a0, b_a_up=b_a_up, b_k_k=b_k_k, b_k_a=b_k_a, b_r_k=b_r_k, b_ln_w=b_ln_w,
             b_ln_b=b_ln_b, c_w_in=cast(c_w_in), c_b_f=c_b_f, c_q_norm=c_q_norm, c_k_norm=c_k_norm,
             c_w_out=cast(c_w_out), m_wq=cast(m_wq), m_q_norm=m_q_norm, m_wo=cast(m_wo),
             f_w_in=cast(f_w_in), f_w_out=cast(f_w_out))
    depth, d = norm_mix.shape
    nb, n_mem = mem_prompt.shape[:2]
    n_even, _, n_heads_b, hd, _ = state_rwkv_S.shape
    mem_heads = d // MEM_HEAD

    mem2 = mem_prompt.reshape(nb * n_mem, d)
    kvs = [mem_kv(mem2, m_mem_norm[l], cast(m_wk[l]), cast(m_wv[l]), m_k_norm[l], 256) for l in range(depth)]
    p_mem_k = jnp.stack([kv[0] for kv in kvs]).reshape(depth, nb, n_mem, d)
    p_mem_v = jnp.stack([kv[1] for kv in kvs]).reshape(depth, nb, n_mem, d)
    s0 = jnp.zeros((n_even, nb, n_heads_b, hd, hd), f32)
    shift0 = jnp.zeros((n_even, nb, state_rwkv_shift.shape[2]), f32)
    y_prompt, _, p_rwkv_s, p_rwkv_shift, p_fox_k, p_fox_v, p_fox_logf = _trunk(
        x_prompt, p_mem_k, p_mem_v, s0, shift0, None, p)

    y_sample, s_chunk_v, s_rwkv_s, s_rwkv_shift, s_fox_k, s_fox_v, s_fox_logf = _trunk(
        x_sample, cache_mem_k, cache_mem_v, state_rwkv_S, state_rwkv_shift, (cache_fox_k, cache_fox_v, cache_fox_logf, page_table), p)
    heads5 = lambda a: a.reshape(depth, nb, n_mem, mem_heads, MEM_HEAD)
    return (y_prompt, y_sample, heads5(p_mem_k), heads5(p_mem_v), p_rwkv_s, p_rwkv_shift, p_fox_k, p_fox_v,
            p_fox_logf, s_chunk_v, s_rwkv_s, s_rwkv_shift, s_fox_k, s_fox_v, s_fox_logf)
```
